```python
import jax, jax.numpy as jnp
from jax import lax
import numpy as np

D_MODEL = 4096
BATCH = 4
SEQ = 2048
DEPTH = 2
DEC_BATCH = 32
DEC_SEQ = 4
PAST_LEN = 16384
PAGE_SIZE = 128

N_A_LAYERS = (DEPTH + 1) // 2
N_C_LAYERS = DEPTH // 2
LRU_W = D_MODEL // 2
LRU_BLOCKS = 16
LRU_BW = LRU_W // LRU_BLOCKS
LRU_C = 8.0
CONV_A = 4
SWA_HD = 64
SWA_NQ = (D_MODEL // 2) // SWA_HD
SWA_NKV = 4
SWA_G = SWA_NQ // SWA_NKV
WINDOW = 128
A_SPLITS = (LRU_W, 2 * LRU_W, 2 * LRU_W + SWA_NQ * SWA_HD, 2 * LRU_W + (SWA_NQ + SWA_NKV) * SWA_HD)
A_IN_W = 2 * LRU_W + (SWA_NQ + 2 * SWA_NKV) * SWA_HD
A_MIX_W = LRU_W + SWA_NQ * SWA_HD
RW_HS = 64
RW_H = D_MODEL // RW_HS
RW_DECAY_LORA = 128
RW_A_LORA = 128
RW_GATE_LORA = 480
RW_GN_EPS = 64e-5
MEM_LEN = 256
M_HEADS = 4
M_HD = 128
M_W = M_HEADS * M_HD
D_FF = 3 * D_MODEL
CONV_F = 3
NORM_EPS = 1e-6

kernel_name = 'hawk_swa_rwkv7_convglu_memxattn_step'


def rmsnorm(x, g):
    xf = x.astype(jnp.float32)
    y = xf * lax.rsqrt(jnp.mean(xf * xf, axis=-1, keepdims=True) + NORM_EPS)
    return (y * g.astype(jnp.float32)).astype(x.dtype)


def causal_dwconv(x, buf, w, b):
    K, T = w.shape[0], x.shape[1]
    xp = jnp.concatenate([buf.astype(x.dtype), x], axis=1)
    y = b + xp[:, 0:T] * w[0]
    for j in range(1, K):
        y = y + xp[:, j:j + T] * w[j]
    return y, xp[:, T:]


def rg_lru(xc, h0, ga_w, ga_b, gx_w, gx_b, lam):
    B, T, W = xc.shape
    xf = xc.astype(jnp.float32)
    xb = xf.reshape(B, T, LRU_BLOCKS, LRU_BW)
    r = jax.nn.sigmoid(jnp.einsum('btnc,ncd->btnd', xb, ga_w.astype(jnp.float32)).reshape(B, T, W) + ga_b)
    i = jax.nn.sigmoid(jnp.einsum('btnc,ncd->btnd', xb, gx_w.astype(jnp.float32)).reshape(B, T, W) + gx_b)
    log_a = -LRU_C * r * jax.nn.softplus(-lam.astype(jnp.float32))
    a = jnp.exp(log_a)
    u = jnp.sqrt(jnp.maximum(-jnp.expm1(2.0 * log_a), 0.0)) * (i * xf)
    u = u.at[:, 0].add(a[:, 0] * h0.astype(jnp.float32))

    def combine(lhs, rhs):
        return lhs[0] * rhs[0], rhs[0] * lhs[1] + rhs[1]

    _, h = lax.associative_scan(combine, (a, u), axis=1)
    return h, h[:, -1]


def sink_attend(q, k, v, mask, sink):
    s = jnp.einsum('...qhgd,...khd->...hgqk', q, k).astype(jnp.float32) * (SWA_HD ** -0.5)
    s = jnp.where(mask, s, -jnp.inf)
    sk = jnp.broadcast_to(sink.astype(jnp.float32).reshape(SWA_NKV, SWA_G, 1, 1), s.shape[:-1] + (1,))
    p = jax.nn.softmax(jnp.concatenate([s, sk], axis=-1), axis=-1)[..., :-1]
    return jnp.einsum('...hgqk,...khd->...qhgd', p.astype(v.dtype), v)


def swa_prompt(q, k, v, sink):
    B, S = q.shape[0], q.shape[1]
    nb = S // WINDOW
    qb = q.reshape(B, nb, WINDOW, SWA_NKV, SWA_G, SWA_HD)
    kb = k.reshape(B, nb, WINDOW, SWA_NKV, SWA_HD)
    vb = v.reshape(B, nb, WINDOW, SWA_NKV, SWA_HD)
    kw = jnp.concatenate([jnp.concatenate([jnp.zeros_like(kb[:, :1]), kb[:, :-1]], axis=1), kb], axis=2)
    vw = jnp.concatenate([jnp.concatenate([jnp.zeros_like(vb[:, :1]), vb[:, :-1]], axis=1), vb], axis=2)
    qi = jnp.arange(WINDOW)[:, None]
    kj = jnp.arange(2 * WINDOW)[None, :]
    dist = qi + WINDOW - kj
    band = (dist >= 0) & (dist < WINDOW)
    valid = (jnp.arange(nb)[:, None, None] > 0) | (kj[None] >= WINDOW)
    mask = (band[None] & valid)[:, None, None]
    o = sink_attend(qb, kw, vw, mask, sink)
    return o.reshape(B, S, SWA_NKV, SWA_G, SWA_HD), k[:, S - WINDOW:], v[:, S - WINDOW:]


def swa_decode(q, k, v, k_cache, v_cache, sink):
    T, Wc = q.shape[1], k_cache.shape[1]
    kk = jnp.concatenate([k_cache.astype(k.dtype), k], axis=1)
    vv = jnp.concatenate([v_cache.astype(v.dtype), v], axis=1)
    kpos = jnp.concatenate([jnp.arange(Wc) - Wc, jnp.arange(T)])
    dist = jnp.arange(T)[:, None] - kpos[None, :]
    mask = (dist >= 0) & (dist < WINDOW)
    o = sink_attend(q, kk, vv, mask, sink)
    return o, kk[:, -Wc:], vv[:, -Wc:]


def lru_swa_mixer(h, conv_buf, h0, k_cache, v_cache, w_in, conv_w, conv_b,
                  ga_w, ga_b, gx_w, gx_b, lam, qn_g, kn_g, sink, w_out):
    B, T, _ = h.shape
    x_rnn, y_gate, q, k, v = jnp.split(h @ w_in, A_SPLITS, axis=-1)
    xc, conv_new = causal_dwconv(x_rnn, conv_buf, conv_w, conv_b)
    hs, h_last = rg_lru(xc, h0, ga_w, ga_b, gx_w, gx_b, lam)
    out_a = hs.astype(h.dtype) * jax.nn.gelu(y_gate)
    q = rmsnorm(q.reshape(B, T, SWA_NKV, SWA_G, SWA_HD), qn_g)
    k = rmsnorm(k.reshape(B, T, SWA_NKV, SWA_HD), kn_g)
    v = v.reshape(B, T, SWA_NKV, SWA_HD)
    if k_cache is None:
        o, k_new, v_new = swa_prompt(q, k, v, sink)
    else:
        o, k_new, v_new = swa_decode(q, k, v, k_cache, v_cache, sink)
    out_b = o.reshape(B, T, SWA_NQ * SWA_HD)
    y = jnp.concatenate([out_a, out_b], axis=-1) @ w_out
    return y, conv_new, h_last, k_new, v_new


def rwkv7_mixer(h, shift0, S0, mu, w_r, w_k, w_v, w_o, w0, w1, w2, a0, a1, a2,
                g1, g2, k_k, k_a, r_k, ln_g, ln_b):
    B, T, D = h.shape
    f32 = jnp.float32
    xx = jnp.concatenate([shift0[:, None].astype(h.dtype), h[:, :-1]], axis=1) - h
    xr, xw, xk, xv, xa, xg = (h + xx * mu[j] for j in range(6))
    heads = lambda t: t.reshape(B, T, RW_H, RW_HS)
    r = heads((xr @ w_r).astype(f32))
    k = heads((xk @ w_k).astype(f32))
    v = heads((xv @ w_v).astype(f32))
    w = -jax.nn.softplus(-(w0 + jnp.tanh(xw @ w1) @ w2).astype(f32)) - 0.5
    decay = jnp.exp(-jnp.exp(heads(w)))
    a = heads(jax.nn.sigmoid((a0 + (xa @ a1) @ a2).astype(f32)))
    g = (jax.nn.sigmoid(xg @ g1) @ g2).astype(f32)
    kk = k * k_k.astype(f32).reshape(RW_H, RW_HS)
    kk = kk / jnp.maximum(jnp.sqrt(jnp.sum(kk * kk, axis=-1, keepdims=True)), 1e-12)
    k = k * (1.0 + (a - 1.0) * k_a.astype(f32).reshape(RW_H, RW_HS))

    def step(S, inp):
        r_t, w_t, k_t, v_t, kk_t, b_t = inp
        sa = jnp.einsum('bhij,bhj->bhi', S, -kk_t)
        S = S * w_t[:, :, None, :] + sa[..., None] * b_t[:, :, None, :] + v_t[..., None] * k_t[:, :, None, :]
        return S, jnp.einsum('bhij,bhj->bhi', S, r_t)

    tm = lambda t: jnp.moveaxis(t, 1, 0)
    S_T, ys = lax.scan(step, S0.astype(f32), (tm(r), tm(decay), tm(k), tm(v), tm(kk), tm(kk * a)))
    y = jnp.moveaxis(ys, 0, 1)
    mean = jnp.mean(y, axis=-1, keepdims=True)
    var = jnp.mean(jnp.square(y - mean), axis=-1, keepdims=True)
    yn = ((y - mean) * lax.rsqrt(var + RW_GN_EPS)).reshape(B, T, D) * ln_g + ln_b
    bonus = (jnp.sum(r * k * r_k.astype(f32), axis=-1, keepdims=True) * v).reshape(B, T, D)
    out = ((yn + bonus) * g).astype(h.dtype) @ w_o
    return out, h[:, -1], S_T


def mem_project(mem, g, w_kv, kn_g):
    B, M, _ = mem.shape
    mk, mv = jnp.split(rmsnorm(mem, g) @ w_kv, 2, axis=-1)
    return rmsnorm(mk.reshape(B, M, M_HEADS, M_HD), kn_g), mv.reshape(B, M, M_HEADS, M_HD)


def mem_attend(h, w_q, qn_g, mk, mv, w_o):
    B, T, _ = h.shape
    q = rmsnorm((h @ w_q).reshape(B, T, M_HEADS, M_HD), qn_g)
    s = jnp.einsum('bqhd,bkhd->bhqk', q, mk.astype(q.dtype)).astype(jnp.float32) * (M_HD ** -0.5)
    p = jax.nn.softmax(s, axis=-1).astype(h.dtype)
    o = jnp.einsum('bhqk,bkhd->bqhd', p, mv.astype(h.dtype))
    return o.reshape(B, T, M_W) @ w_o


def conv_ffn(h, buf, w_in, conv_w, conv_b, w_out):
    gate, up = jnp.split(h @ w_in, 2, axis=-1)
    gc, buf_new = causal_dwconv(gate, buf, conv_w, conv_b)
    return (jax.nn.gelu(gc) * up) @ w_out, buf_new


def setup_inputs(seed: int = 0) -> dict:
    key = jax.random.key(seed)
    keys = iter(jax.random.split(key, 80))
    f32 = jnp.float32
    D, NA, NC = D_MODEL, N_A_LAYERS, N_C_LAYERS
    win_rows = min(WINDOW, PAST_LEN)

    def nrm(shape, scale):
        return jax.random.normal(next(keys), shape, f32) * scale

    def gain(shape):
        return 1.0 + 0.1 * jax.random.normal(next(keys), shape, f32)

    def unif(shape, lo, hi):
        return jax.random.uniform(next(keys), shape, f32, lo, hi)

    a_pow = unif((NA, LRU_W), 0.9, 0.999) ** (1.0 / LRU_C)
    return {
        'x_prompt': nrm((BATCH, SEQ, D), 1.0),
        'x_sample': nrm((DEC_BATCH, DEC_SEQ, D), 1.0),
        'mem_prompt': nrm((BATCH, MEM_LEN, D), 1.0),
        'state_lru_conv': nrm((NA, DEC_BATCH, CONV_A - 1, LRU_W), 1.0),
        'state_lru_h': nrm((NA, DEC_BATCH, LRU_W), 0.5),
        'cache_swa_k': nrm((NA, DEC_BATCH, win_rows, SWA_NKV, SWA_HD), 1.0),
        'cache_swa_v': nrm((NA, DEC_BATCH, win_rows, SWA_NKV, SWA_HD), 1.0),
        'state_rwkv_shift': nrm((NC, DEC_BATCH, D), 1.0),
        'state_rwkv_wkv': nrm((NC, DEC_BATCH, RW_H, RW_HS, RW_HS), 0.5),
        'cache_mem_k': nrm((DEPTH, DEC_BATCH, MEM_LEN, M_HEADS, M_HD), 1.0),
        'cache_mem_v': nrm((DEPTH, DEC_BATCH, MEM_LEN, M_HEADS, M_HD), 1.0),
        'state_ffn_conv': nrm((DEPTH, DEC_BATCH, CONV_F - 1, D_FF), 1.0),
        'a_norm_g': gain((NA, D)),
        'a_w_in': nrm((NA, D, A_IN_W), D ** -0.5),
        'a_conv_w': nrm((NA, CONV_A, LRU_W), CONV_A ** -0.5),
        'a_conv_b': nrm((NA, LRU_W), 0.02),
        'a_gate_a_w': nrm((NA, LRU_BLOCKS, LRU_BW, LRU_BW), LRU_BW ** -0.5),
        'a_gate_a_b': nrm((NA, LRU_W), 0.1),
        'a_gate_x_w': nrm((NA, LRU_BLOCKS, LRU_BW, LRU_BW), LRU_BW ** -0.5),
        'a_gate_x_b': nrm((NA, LRU_W), 0.1),
        'a_lambda': jnp.log(a_pow) - jnp.log1p(-a_pow),
        'b_q_norm_g': gain((NA, SWA_HD)),
        'b_k_norm_g': gain((NA, SWA_HD)),
        'b_sink': nrm((NA, SWA_NQ), 0.5),
        'a_w_out': nrm((NA, A_MIX_W, D), A_MIX_W ** -0.5),
        'c_norm_g': gain((NC, D)),
        'c_mu': unif((NC, 6, D), 0.0, 1.0),
        'c_w_r': nrm((NC, D, D), D ** -0.5),
        'c_w_k': nrm((NC, D, D), D ** -0.5),
        'c_w_v': nrm((NC, D, D), D ** -0.5),
        'c_w_o': nrm((NC, D, D), D ** -0.5),
        'c_w0': unif((NC, D), -6.5, -1.5),
        'c_w1': nrm((NC, D, RW_DECAY_LORA), D ** -0.5),
        'c_w2': nrm((NC, RW_DECAY_LORA, D), 0.1 * RW_DECAY_LORA ** -0.5),
        'c_a0': nrm((NC, D), 0.1),
        'c_a1': nrm((NC, D, RW_A_LORA), D ** -0.5),
        'c_a2': nrm((NC, RW_A_LORA, D), 0.1 * RW_A_LORA ** -0.5),
        'c_g1': nrm((NC, D, RW_GATE_LORA), D ** -0.5),
        'c_g2': nrm((NC, RW_GATE_LORA, D), RW_GATE_LORA ** -0.5),
        'c_k_k': 0.85 + nrm((NC, D), 0.05),
        'c_k_a': 1.0 + nrm((NC, D), 0.05),
        'c_r_k': nrm((NC, RW_H, RW_HS), 0.1),
        'c_ln_g': gain((NC, D)),
        'c_ln_b': nrm((NC, D), 0.02),
        'm_norm_g': gain((DEPTH, D)),
        'm_mem_norm_g': gain((DEPTH, D)),
        'm_w_q': nrm((DEPTH, D, M_W), D ** -0.5),
        'm_w_kv': nrm((DEPTH, D, 2 * M_W), D ** -0.5),
        'm_q_norm_g': gain((DEPTH, M_HD)),
        'm_k_norm_g': gain((DEPTH, M_HD)),
        'm_w_o': nrm((DEPTH, M_W, D), M_W ** -0.5),
        'f_norm_g': gain((DEPTH, D)),
        'f_w_in': nrm((DEPTH, D, 2 * D_FF), D ** -0.5),
        'f_conv_w': nrm((DEPTH, CONV_F, D_FF), CONV_F ** -0.5),
        'f_conv_b': nrm((DEPTH, D_FF), 0.02),
        'f_w_out': nrm((DEPTH, D_FF, D), D_FF ** -0.5),
    }


def reference(x_prompt, x_sample, mem_prompt,
              state_lru_conv, state_lru_h, cache_swa_k, cache_swa_v,
              state_rwkv_shift, state_rwkv_wkv, cache_mem_k, cache_mem_v, state_ffn_conv,
              a_norm_g, a_w_in, a_conv_w, a_conv_b, a_gate_a_w, a_gate_a_b, a_gate_x_w, a_gate_x_b,
              a_lambda, b_q_norm_g, b_k_norm_g, b_sink, a_w_out,
              c_norm_g, c_mu, c_w_r, c_w_k, c_w_v, c_w_o, c_w0, c_w1, c_w2, c_a0, c_a1, c_a2,
              c_g1, c_g2, c_k_k, c_k_a, c_r_k, c_ln_g, c_ln_b,
              m_norm_g, m_mem_norm_g, m_w_q, m_w_kv, m_q_norm_g, m_k_norm_g, m_w_o,
              f_norm_g, f_w_in, f_conv_w, f_conv_b, f_w_out):
    dt = x_prompt.dtype
    B = x_prompt.shape[0]

    def run(x, lru_conv, lru_h, swa_k, swa_v, rw_shift, rw_wkv, mem_k, mem_v, ffn_conv):
        lc, lh, sk, sv, rs, rw, fc = [], [], [], [], [], [], []
        for l in range(DEPTH):
            i = l // 2
            if l % 2 == 0:
                kc = None if swa_k is None else swa_k[i]
                vc = None if swa_v is None else swa_v[i]
                y, c_new, h_new, k_new, v_new = lru_swa_mixer(
                    rmsnorm(x, a_norm_g[i]), lru_conv[i], lru_h[i], kc, vc,
                    a_w_in[i], a_conv_w[i], a_conv_b[i], a_gate_a_w[i], a_gate_a_b[i],
                    a_gate_x_w[i], a_gate_x_b[i], a_lambda[i], b_q_norm_g[i], b_k_norm_g[i],
                    b_sink[i], a_w_out[i])
                lc.append(c_new)
                lh.append(h_new.astype(x.dtype))
                sk.append(k_new)
                sv.append(v_new)
            else:
                y, sh_new, wkv_new = rwkv7_mixer(
                    rmsnorm(x, c_norm_g[i]), rw_shift[i], rw_wkv[i], c_mu[i], c_w_r[i], c_w_k[i],
                    c_w_v[i], c_w_o[i], c_w0[i], c_w1[i], c_w2[i], c_a0[i], c_a1[i], c_a2[i],
                    c_g1[i], c_g2[i], c_k_k[i], c_k_a[i], c_r_k[i], c_ln_g[i], c_ln_b[i])
                rs.append(sh_new)
                rw.append(wkv_new.astype(x.dtype))
            x = x + y
            x = x + mem_attend(rmsnorm(x, m_norm_g[l]), m_w_q[l], m_q_norm_g[l], mem_k[l], mem_v[l], m_w_o[l])
            y, f_new = conv_ffn(rmsnorm(x, f_norm_g[l]), ffn_conv[l], f_w_in[l], f_conv_w[l], f_conv_b[l], f_w_out[l])
            x = x + y
            fc.append(f_new)
        return (x, jnp.stack(lc), jnp.stack(lh), jnp.stack(sk), jnp.stack(sv),
                jnp.stack(rs), jnp.stack(rw), jnp.stack(fc))

    mks, mvs = [], []
    for l in range(DEPTH):
        mk, mv = mem_project(mem_prompt, m_mem_norm_g[l], m_w_kv[l], m_k_norm_g[l])
        mks.append(mk)
        mvs.append(mv)
    p_mem_k = jnp.stack(mks)
    p_mem_v = jnp.stack(mvs)

    z_lc = jnp.zeros((N_A_LAYERS, B, CONV_A - 1, LRU_W), dt)
    z_lh = jnp.zeros((N_A_LAYERS, B, LRU_W), dt)
    z_rs = jnp.zeros((N_C_LAYERS, B, D_MODEL), dt)
    z_rw = jnp.zeros((N_C_LAYERS, B, RW_H, RW_HS, RW_HS), dt)
    z_fc = jnp.zeros((DEPTH, B, CONV_F - 1, D_FF), dt)
    y_prompt, p_lc, p_lh, p_sk, p_sv, p_rs, p_rw, p_fc = run(
        x_prompt, z_lc, z_lh, None, None, z_rs, z_rw, p_mem_k, p_mem_v, z_fc)

    y_sample, s_lc, s_lh, s_sk, s_sv, s_rs, s_rw, s_fc = run(
        x_sample, state_lru_conv, state_lru_h, cache_swa_k, cache_swa_v,
        state_rwkv_shift, state_rwkv_wkv, cache_mem_k, cache_mem_v, state_ffn_conv)

    return (y_prompt, y_sample, p_lc, p_lh, p_sk, p_sv, p_rs, p_rw, p_mem_k, p_mem_v, p_fc,
            s_lc, s_lh, s_sk, s_sv, s_rs, s_rw, s_fc)
```

```python
import functools
import math

import jax
import jax.numpy as jnp
from jax import lax
from jax.experimental import pallas as pl
from jax.experimental.pallas import tpu as pltpu

F32 = jnp.float32
BF16 = jnp.bfloat16

NORM_EPS = 1e-6
RW_GN_EPS = 64e-5
LRU_C = 8.0
LANES = 128
SUBLANES = 8
VMEM_LIMIT_BYTES = 56 * 1024 * 1024


def _params(*sem):
    return pltpu.CompilerParams(dimension_semantics=sem, vmem_limit_bytes=VMEM_LIMIT_BYTES)


def _nt(a, b):
    return lax.dot_general(a, b, (((1,), (1,)), ((), ())), preferred_element_type=F32)


def _tn(a, b):
    return lax.dot_general(a, b, (((0,), (0,)), ((), ())), preferred_element_type=F32)


def _dot(a, b):
    return jnp.dot(a, b, preferred_element_type=F32)


def _split3(x):
    hi = x.astype(BF16)
    r1 = x - hi.astype(F32)
    mid = r1.astype(BF16)
    lo = (r1 - mid.astype(F32)).astype(BF16)
    return hi, mid, lo


def _dot_exact_rhs(x, m_bf16):
    hi, mid, lo = _split3(x)
    return _dot(hi, m_bf16) + _dot(mid, m_bf16) + _dot(lo, m_bf16)


def _group_sum_bcast(x, width):
    L = x.shape[-1]
    if width == LANES:
        parts = []
        for c in range(L // LANES):
            s = jnp.sum(x[:, c * LANES:(c + 1) * LANES], axis=-1, keepdims=True)
            parts.append(jnp.broadcast_to(s, (x.shape[0], LANES)))
        return parts[0] if len(parts) == 1 else jnp.concatenate(parts, axis=-1)
    li = lax.broadcasted_iota(jnp.int32, (LANES, LANES), 0) // width
    lj = lax.broadcasted_iota(jnp.int32, (LANES, LANES), 1) // width
    e = jnp.where(li == lj, 1.0, 0.0).astype(BF16)
    parts = [_dot_exact_rhs(x[:, c * LANES:(c + 1) * LANES], e) for c in range(L // LANES)]
    return parts[0] if len(parts) == 1 else jnp.concatenate(parts, axis=-1)


def _shift_rows(x, prev8, s):
    rolled = pltpu.roll(x, s, 0)
    top = jnp.where(lax.broadcasted_iota(jnp.int32, (SUBLANES, 1), 0) < s,
                    pltpu.roll(prev8, s, 0), rolled[0:SUBLANES])
    if x.shape[0] == SUBLANES:
        return top
    return jnp.concatenate([top, rolled[SUBLANES:]], axis=0)


def _rmsnorm_kernel(x_ref, g_ref, o_ref):
    x = x_ref[...]
    ms = jnp.mean(x * x, axis=-1, keepdims=True)
    o_ref[...] = (x * lax.rsqrt(ms + NORM_EPS) * g_ref[...]).astype(o_ref.dtype)


def rmsnorm_rows(x, g, out_dtype=BF16, tm=256):
    M, D = x.shape
    tm = min(tm, M)
    return pl.pallas_call(
        _rmsnorm_kernel,
        grid=(M // tm,),
        in_specs=[pl.BlockSpec((tm, D), lambda i: (i, 0)), pl.BlockSpec((1, D), lambda i: (0, 0))],
        out_specs=pl.BlockSpec((tm, D), lambda i: (i, 0)),
        out_shape=jax.ShapeDtypeStruct((M, D), out_dtype),
        compiler_params=_params("parallel"),
        name="rmsnorm_rows",
    )(x, g.reshape(1, D))


def _mm_kernel(*refs, nk, has_bias, has_res, act):
    x_ref, w_ref = refs[0], refs[1]
    pos = 2
    b_ref = r_ref = None
    if has_bias:
        b_ref = refs[pos]
        pos += 1
    if has_res:
        r_ref = refs[pos]
        pos += 1
    o_ref = refs[pos]
    acc_ref = refs[pos + 1] if nk > 1 else None

    def epilogue(y):
        if has_bias:
            y = y + b_ref[...]
        if act == "tanh":
            y = jnp.tanh(y)
        elif act == "sigmoid":
            y = jax.nn.sigmoid(y)
        if has_res:
            y = y + r_ref[...]
        o_ref[...] = y.astype(o_ref.dtype)

    part = _dot(x_ref[...], w_ref[...])
    if nk == 1:
        epilogue(part)
        return
    k = pl.program_id(2)

    @pl.when(k == 0)
    def _():
        acc_ref[...] = part

    @pl.when(k > 0)
    def _():
        acc_ref[...] += part

    @pl.when(k == nk - 1)
    def _():
        epilogue(acc_ref[...])


def _pick(n, prefs):
    for p in prefs:
        if n % p == 0:
            return p
    return n


def matmul(x, w, *, bias=None, res=None, act=None, out_dtype=F32, tm=1024, tn=512, tk=4096):
    M, K = x.shape
    N = w.shape[1]
    tm = min(tm, M)
    tn = _pick(N, (tn, 512, 256, 128))
    tk = _pick(K, (tk, 2048, 1024, 512))
    nk = K // tk
    in_specs = [pl.BlockSpec((tm, tk), lambda i, j, k: (i, k)),
                pl.BlockSpec((tk, tn), lambda i, j, k: (k, j))]
    args = [x, w]
    if bias is not None:
        in_specs.append(pl.BlockSpec((1, tn), lambda i, j, k: (0, j)))
        args.append(bias.reshape(1, N).astype(F32))
    if res is not None:
        in_specs.append(pl.BlockSpec((tm, tn), lambda i, j, k: (i, j)))
        args.append(res)
    kern = functools.partial(_mm_kernel, nk=nk, has_bias=bias is not None, has_res=res is not None, act=act)
    return pl.pallas_call(
        kern,
        grid=(M // tm, N // tn, nk),
        in_specs=in_specs,
        out_specs=pl.BlockSpec((tm, tn), lambda i, j, k: (i, j)),
        out_shape=jax.ShapeDtypeStruct((M, N), out_dtype),
        scratch_shapes=[pltpu.VMEM((tm, tn), F32)] if nk > 1 else [],
        compiler_params=_params("parallel", "parallel", "arbitrary"),
        name="matmul",
    )(*args)


def _headnorm_kernel(x_ref, g_ref, o_ref, *, hd):
    x = x_ref[...]
    ms = _group_sum_bcast(x * x, hd) * (1.0 / hd)
    o_ref[...] = (x * lax.rsqrt(ms + NORM_EPS) * g_ref[...]).astype(o_ref.dtype)


def headnorm(x, col_block, width, g, hd, out_dtype, tm=256):
    M = x.shape[0]
    tm = min(tm, M)
    g_row = jnp.tile(g.astype(F32), width // hd).reshape(1, width)
    return pl.pallas_call(
        functools.partial(_headnorm_kernel, hd=hd),
        grid=(M // tm,),
        in_specs=[pl.BlockSpec((tm, width), lambda i: (i, col_block)),
                  pl.BlockSpec((1, width), lambda i: (0, 0))],
        out_specs=pl.BlockSpec((tm, width), lambda i: (i, 0)),
        out_shape=jax.ShapeDtypeStruct((M, width), out_dtype),
        compiler_params=_params("parallel"),
        name="headnorm",
    )(x, g_row)


def _lru_kernel(xr_ref, halo_ref, yg_ref, h0_ref, cw_ref, cb_ref, gaw_ref, gab_ref, gxw_ref, gxb_ref,
                lam_ref, o_ref, hl_ref, a_scr, u_scr, h_scr, *, n_pad, n_blocks, bw, scan_w):
    t = pl.program_id(1)
    tt, W = a_scr.shape

    @pl.when(t == 0)
    def _():
        h_scr[...] = h0_ref[0]

    x = xr_ref[0]
    prev = halo_ref[0] * jnp.where(t > 0, 1.0, 0.0)
    xc = cb_ref[...] + cw_ref[3:4, :] * x
    for s in (1, 2, 3):
        xc = xc + cw_ref[3 - s:4 - s, :] * _shift_rows(x, prev, s)

    nsp = -LRU_C * jax.nn.softplus(-lam_ref[...])
    if n_pad:
        live = (lax.broadcasted_iota(jnp.int32, (tt, 1), 0) >= n_pad) | (t > 0)
    for n in range(n_blocks):
        sl = slice(n * bw, (n + 1) * bw)
        xb = xc[:, sl]
        xb16 = xb.astype(BF16)
        r = jax.nn.sigmoid(_dot(xb16, gaw_ref[n]) + gab_ref[:, sl])
        i = jax.nn.sigmoid(_dot(xb16, gxw_ref[n]) + gxb_ref[:, sl])
        log_a = r * nsp[:, sl]
        a = jnp.exp(log_a)
        u = jnp.sqrt(jnp.maximum(-jnp.tanh(log_a) * (a * a + 1.0), 0.0)) * (i * xb)
        if n_pad:
            a = jnp.where(live, a, 1.0)
            u = jnp.where(live, u, 0.0)
        a_scr[:, sl] = a
        u_scr[:, sl] = u

    row8 = lax.broadcasted_iota(jnp.int32, (SUBLANES, 1), 0)
    for c in range(W // scan_w):
        cs = slice(c * scan_w, (c + 1) * scan_w)

        def body(gi, h, cs=cs):
            r0 = pl.multiple_of(gi * SUBLANES, SUBLANES)
            A = a_scr[pl.ds(r0, SUBLANES), cs]
            U = u_scr[pl.ds(r0, SUBLANES), cs]
            for s in (1, 2, 4):
                As = pltpu.roll(A, s, 0)
                Us = pltpu.roll(U, s, 0)
                m = row8 >= s
                U = jnp.where(m, A * Us + U, U)
                A = jnp.where(m, A * As, A)
            H = A * h + U
            u_scr[pl.ds(r0, SUBLANES), cs] = H
            return H[SUBLANES - 1:SUBLANES, :]

        h_end = lax.fori_loop(0, tt // SUBLANES, body, h_scr[:, cs])
        h_scr[:, cs] = h_end

    o_ref[0] = (u_scr[...] * jax.nn.gelu(yg_ref[0], approximate=True)).astype(o_ref.dtype)
    hl_ref[0] = h_scr[...]


def lru_mixer(xr, xr_cb, yg, yg_cb, h0, conv_w, conv_b, ga_w, ga_b, gx_w, gx_b, lam, *, n_pad, tt):
    B, T, _ = xr.shape
    W = conv_w.shape[1]
    nb, bw = ga_w.shape[0], ga_w.shape[1]
    tt = min(tt, T)
    hb = tt // SUBLANES
    row = lambda v: v.reshape(1, W).astype(F32)
    kern = functools.partial(_lru_kernel, n_pad=n_pad, n_blocks=nb, bw=bw, scan_w=min(W, 512))
    return pl.pallas_call(
        kern,
        grid=(B, T // tt),
        in_specs=[
            pl.BlockSpec((1, tt, W), lambda b, t: (b, t, xr_cb)),
            pl.BlockSpec((1, SUBLANES, W), lambda b, t: (b, jnp.maximum(t * hb - 1, 0), xr_cb)),
            pl.BlockSpec((1, tt, W), lambda b, t: (b, t, yg_cb)),
            pl.BlockSpec((1, 1, W), lambda b, t: (b, 0, 0)),
            pl.BlockSpec((4, W), lambda b, t: (0, 0)),
            pl.BlockSpec((1, W), lambda b, t: (0, 0)),
            pl.BlockSpec((nb, bw, bw), lambda b, t: (0, 0, 0)),
            pl.BlockSpec((1, W), lambda b, t: (0, 0)),
            pl.BlockSpec((nb, bw, bw), lambda b, t: (0, 0, 0)),
            pl.BlockSpec((1, W), lambda b, t: (0, 0)),
            pl.BlockSpec((1, W), lambda b, t: (0, 0)),
        ],
        out_specs=[pl.BlockSpec((1, tt, W), lambda b, t: (b, t, 0)),
                   pl.BlockSpec((1, 1, W), lambda b, t: (b, 0, 0))],
        out_shape=[jax.ShapeDtypeStruct((B, T, W), BF16), jax.ShapeDtypeStruct((B, 1, W), F32)],
        scratch_shapes=[pltpu.VMEM((tt, W), F32), pltpu.VMEM((tt, W), F32), pltpu.VMEM((1, W), F32)],
        compiler_params=_params("parallel", "arbitrary"),
        name="lru_mixer",
    )(xr, xr, yg, h0, conv_w.astype(F32), row(conv_b), ga_w.astype(BF16), row(ga_b),
      gx_w.astype(BF16), row(gx_b), row(lam))


def _swa_kernel(sink_ref, q_ref, kp_ref, kc_ref, vp_ref, vc_ref, o_ref, *, j0, n_kv, group, hd, win):
    j = pl.program_id(1) + j0
    q = q_ref[0]
    k2 = jnp.concatenate([kp_ref[0], kc_ref[0]], axis=0)
    v2 = jnp.concatenate([vp_ref[0], vc_ref[0]], axis=0)
    qi = lax.broadcasted_iota(jnp.int32, (win, 2 * win), 0)
    kj = lax.broadcasted_iota(jnp.int32, (win, 2 * win), 1)
    dist = qi + win - kj
    mask = (dist >= 0) & (dist < win) & ((j > 0) | (kj >= win))
    scale = hd ** -0.5
    for kvh in range(n_kv):
        kh = k2[:, kvh * hd:(kvh + 1) * hd]
        vh = v2[:, kvh * hd:(kvh + 1) * hd]
        for g in range(0, group, 2):
            outs = []
            for gg in (g, g + 1):
                h = kvh * group + gg
                qh = q[:, h * hd:(h + 1) * hd]
                s = _nt(qh, kh) * scale
                s = jnp.where(mask, s, -jnp.inf)
                sk = sink_ref[h]
                m = jnp.maximum(jnp.max(s, axis=-1, keepdims=True), sk)
                p = jnp.exp(s - m)
                den = jnp.sum(p, axis=-1, keepdims=True) + jnp.exp(sk - m)
                outs.append(_dot(p.astype(BF16), vh) / den)
            h0 = kvh * group + g
            o_ref[0, :, h0 * hd:(h0 + 2) * hd] = jnp.concatenate(outs, axis=-1).astype(o_ref.dtype)


def swa_attention(q, k, v, sink, *, j0, win, hd):
    B, Tq, QW = q.shape
    KW = k.shape[2]
    n_kv = KW // hd
    group = QW // KW
    nq = Tq // win
    kern = functools.partial(_swa_kernel, j0=j0, n_kv=n_kv, group=group, hd=hd, win=win)
    prev = lambda b, j: (b, jnp.maximum(j + j0 - 1, 0), 0)
    cur = lambda b, j: (b, j + j0, 0)
    return pl.pallas_call(
        kern,
        grid=(B, nq),
        in_specs=[pl.BlockSpec(memory_space=pltpu.SMEM),
                  pl.BlockSpec((1, win, QW), lambda b, j: (b, j, 0)),
                  pl.BlockSpec((1, win, KW), prev), pl.BlockSpec((1, win, KW), cur),
                  pl.BlockSpec((1, win, KW), prev), pl.BlockSpec((1, win, KW), cur)],
        out_specs=pl.BlockSpec((1, win, QW), lambda b, j: (b, j, 0)),
        out_shape=jax.ShapeDtypeStruct((B, Tq, QW), BF16),
        compiler_params=_params("parallel", "parallel"),
        name="swa_attention",
    )(sink.astype(F32), q, k, k, v, v)


def _mem_attn_kernel(q_ref, g_ref, mk_ref, mv_ref, o_ref, *, n_heads, hd):
    q = q_ref[0]
    scale = hd ** -0.5
    for h in range(n_heads):
        sl = slice(h * hd, (h + 1) * hd)
        qh = q[:, sl]
        qn = (qh * lax.rsqrt(jnp.mean(qh * qh, axis=-1, keepdims=True) + NORM_EPS) * g_ref[...]).astype(BF16)
        s = _nt(qn, mk_ref[0, :, sl]) * scale
        m = jnp.max(s, axis=-1, keepdims=True)
        p = jnp.exp(s - m)
        p = p / jnp.sum(p, axis=-1, keepdims=True)
        o_ref[0, :, sl] = _dot(p.astype(BF16), mv_ref[0, :, sl]).astype(o_ref.dtype)


def mem_attention(q, qn_g, mk, mv, *, n_heads, tt=256):
    B, T, MW = q.shape
    ML = mk.shape[1]
    hd = MW // n_heads
    tt = min(tt, T)
    return pl.pallas_call(
        functools.partial(_mem_attn_kernel, n_heads=n_heads, hd=hd),
        grid=(B, T // tt),
        in_specs=[pl.BlockSpec((1, tt, MW), lambda b, t: (b, t, 0)),
                  pl.BlockSpec((1, hd), lambda b, t: (0, 0)),
                  pl.BlockSpec((1, ML, MW), lambda b, t: (b, 0, 0)),
                  pl.BlockSpec((1, ML, MW), lambda b, t: (b, 0, 0))],
        out_specs=pl.BlockSpec((1, tt, MW), lambda b, t: (b, t, 0)),
        out_shape=jax.ShapeDtypeStruct((B, T, MW), BF16),
        compiler_params=_params("parallel", "parallel"),
        name="mem_attention",
    )(q, qn_g.reshape(1, hd).astype(F32), mk, mv)


def _ffn_act_kernel(g_ref, halo_ref, up_ref, cw_ref, cb_ref, o_ref):
    t = pl.program_id(2)
    g = g_ref[0]
    prev = halo_ref[0] * jnp.where(t > 0, 1.0, 0.0)
    gc = cb_ref[...] + cw_ref[2:3, :] * g
    for s in (1, 2):
        gc = gc + cw_ref[2 - s:3 - s, :] * _shift_rows(g, prev, s)
    o_ref[0] = (jax.nn.gelu(gc, approximate=True) * up_ref[0]).astype(o_ref.dtype)


def ffn_act(gate, gate_cb0, up, up_cb0, conv_w, conv_b, *, tt=256, tc=512):
    B, T, _ = gate.shape
    C = conv_w.shape[1]
    tt = min(tt, T)
    hb = tt // SUBLANES
    return pl.pallas_call(
        _ffn_act_kernel,
        grid=(B, C // tc, T // tt),
        in_specs=[pl.BlockSpec((1, tt, tc), lambda b, c, t: (b, t, gate_cb0 + c)),
                  pl.BlockSpec((1, SUBLANES, tc), lambda b, c, t: (b, jnp.maximum(t * hb - 1, 0), gate_cb0 + c)),
                  pl.BlockSpec((1, tt, tc), lambda b, c, t: (b, t, up_cb0 + c)),
                  pl.BlockSpec((3, tc), lambda b, c, t: (0, c)),
                  pl.BlockSpec((1, tc), lambda b, c, t: (0, c))],
        out_specs=pl.BlockSpec((1, tt, tc), lambda b, c, t: (b, t, c)),
        out_shape=jax.ShapeDtypeStruct((B, T, C), BF16),
        compiler_params=_params("parallel", "parallel", "parallel"),
        name="ffn_act",
    )(gate, gate, up, conv_w.astype(F32), conv_b.reshape(1, C).astype(F32))


def _rwkv_mix_kernel(x_ref, halo_ref, s0_ref, g_ref, mu_ref, *out_refs, n_valid_last):
    t = pl.program_id(1)
    nt = pl.num_programs(1)
    o_refs, hl_ref = out_refs[:6], out_refs[6]

    def norm(v):
        return v * lax.rsqrt(jnp.mean(v * v, axis=-1, keepdims=True) + NORM_EPS) * g_ref[...]

    h = norm(x_ref[0])
    hp = norm(halo_ref[0])
    first = jnp.where(t > 0, 1.0, 0.0)
    hp = hp * first + jnp.broadcast_to(s0_ref[0], hp.shape) * (1.0 - first)
    xx = _shift_rows(h, hp, 1) - h
    for j in range(6):
        o_refs[j][0] = (h + xx * mu_ref[j:j + 1, :]).astype(o_refs[j].dtype)

    @pl.when(t == nt - 1)
    def _():
        hl_ref[0] = h[n_valid_last - 1:n_valid_last, :]


def rwkv_mix(x, shift0, g, mu, *, n_valid, tt=256):
    B, T, D = x.shape
    tt = min(tt, T)
    hb = tt // SUBLANES
    n_valid_last = n_valid - (T // tt - 1) * tt
    blk = pl.BlockSpec((1, tt, D), lambda b, t: (b, t, 0))
    one = pl.BlockSpec((1, 1, D), lambda b, t: (b, 0, 0))
    return pl.pallas_call(
        functools.partial(_rwkv_mix_kernel, n_valid_last=n_valid_last),
        grid=(B, T // tt),
        in_specs=[blk,
                  pl.BlockSpec((1, SUBLANES, D), lambda b, t: (b, jnp.maximum(t * hb - 1, 0), 0)),
                  one,
                  pl.BlockSpec((1, D), lambda b, t: (0, 0)),
                  pl.BlockSpec((6, D), lambda b, t: (0, 0))],
        out_specs=[blk] * 6 + [one],
        out_shape=[jax.ShapeDtypeStruct((B, T, D), BF16)] * 6 + [jax.ShapeDtypeStruct((B, 1, D), F32)],
        compiler_params=_params("parallel", "arbitrary"),
        name="rwkv_mix",
    )(x, x, shift0, g.reshape(1, D).astype(F32), mu.astype(F32))


def _rwkv_chunk_kernel(r_ref, k_ref, v_ref, wp_ref, ap_ref, g_ref, kk_ref, ka_ref, rk_ref, lng_ref, lnb_ref,
                       s0_ref, o_ref, so_ref, s_scr, y_scr, *, hs):
    c = pl.program_id(2)
    nc = pl.num_programs(2)
    C, L = y_scr.shape
    n_pairs = L // LANES

    @pl.when(c == 0)
    def _():
        s_scr[...] = s0_ref[0]

    r = r_ref[0]
    k = k_ref[0]
    v = v_ref[0]
    logw = -jnp.exp(-jax.nn.softplus(-wp_ref[0]) - 0.5)
    a = jax.nn.sigmoid(ap_ref[0])
    kk = k * kk_ref[...]
    kk = kk / jnp.maximum(jnp.sqrt(_group_sum_bcast(kk * kk, hs)), 1e-12)
    k2 = k * (1.0 + (a - 1.0) * ka_ref[...])
    bm = kk * a

    ti = lax.broadcasted_iota(jnp.int32, (C, C), 0)
    si = lax.broadcasted_iota(jnp.int32, (C, C), 1)
    tri = jnp.where(ti >= si, 1.0, 0.0).astype(BF16)
    cum = _dot_exact_rhs_lhs(tri, logw)
    e_in = jnp.exp(cum)
    e_out = jnp.exp(-cum)
    e_end = jnp.exp(cum[C - 1:C, :] - cum)
    g_end = jnp.exp(cum[C - 1:C, :])
    rt = r * e_in
    kkt = kk * jnp.exp(cum - logw)
    bh = bm * e_out
    kh = k2 * e_out
    bbar = bm * e_end
    kbar = k2 * e_end

    lane = lax.broadcasted_iota(jnp.int32, (1, LANES), 1)
    m0 = lane < hs
    C2 = 2 * C
    ri = lax.broadcasted_iota(jnp.int32, (C2, C2), 0)
    ci = lax.broadcasted_iota(jnp.int32, (C2, C2), 1)
    same = (ri // C) == (ci // C)
    strict = same & ((ri % C) > (ci % C))
    lower = (ri % C) >= (ci % C)
    vi = lax.broadcasted_iota(jnp.int32, (LANES, LANES), 0) // hs
    vj = lax.broadcasted_iota(jnp.int32, (LANES, LANES), 1) // hs
    blockdiag = vi == vj

    def stack_heads(x):
        return jnp.concatenate([jnp.where(m0, x, 0.0), jnp.where(m0, 0.0, x)], axis=0).astype(BF16)

    n_steps = int(math.log2(C))
    for p in range(n_pairs):
        sl = slice(p * LANES, (p + 1) * LANES)
        S = s_scr[p]
        S16 = S.astype(BF16)
        V = v[:, sl]
        V16 = V.astype(BF16)
        lhk = stack_heads(kkt[:, sl])
        lhr = stack_heads(rt[:, sl])
        nmat = jnp.where(strict, -_nt(lhk, stack_heads(bh[:, sl])), 0.0)
        auk = jnp.where(strict, -_nt(lhk, stack_heads(kh[:, sl])), 0.0)
        rhf = jnp.concatenate([bh[:, sl], kh[:, sl]], axis=0).astype(BF16)
        ar = jnp.where(lower, _nt(lhr, rhf), 0.0)
        u = _dot(auk.astype(BF16), jnp.concatenate([V16, V16], axis=0)) - _nt(lhk, S16)
        npow = nmat.astype(BF16)
        for it in range(n_steps):
            u = u + _dot(npow, u.astype(BF16))
            if it + 1 < n_steps:
                npow = _dot(npow, npow).astype(BF16)
        up = jnp.where(m0, u[:C], u[C:])
        uv = jnp.concatenate([up.astype(BF16), V16], axis=0)
        tmat = _dot(ar.astype(BF16), uv)
        y_scr[:, sl] = _nt(rt[:, sl].astype(BF16), S16) + jnp.where(m0, tmat[:C], tmat[C:])
        upd = _tn(uv, jnp.concatenate([bbar[:, sl], kbar[:, sl]], axis=0).astype(BF16))
        s_scr[p] = S * g_end[:, sl] + jnp.where(blockdiag, upd, 0.0)

    y = y_scr[...]
    inv = 1.0 / hs
    mean = _group_sum_bcast(y, hs) * inv
    yc = y - mean
    var = _group_sum_bcast(yc * yc, hs) * inv
    yn = yc * lax.rsqrt(var + RW_GN_EPS) * lng_ref[...] + lnb_ref[...]
    bonus = _group_sum_bcast(r * k2 * rk_ref[...], hs) * v
    o_ref[0] = ((yn + bonus) * g_ref[0]).astype(o_ref.dtype)

    @pl.when(c == nc - 1)
    def _():
        so_ref[0] = s_scr[...]


def _dot_exact_rhs_lhs(m_bf16, x):
    hi, mid, lo = _split3(x)
    return _dot(m_bf16, hi) + _dot(m_bf16, mid) + _dot(m_bf16, lo)


def rwkv_chunked(r, k, v, wp, ap, g, k_k, k_a, r_k, ln_g, ln_b, s0, *, hs, chunk, lanes_per_step=512):
    B, T, D = r.shape
    L = min(lanes_per_step, D)
    npg = L // LANES
    seq = pl.BlockSpec((1, chunk, L), lambda b, hg, c: (b, c, hg))
    par = pl.BlockSpec((1, L), lambda b, hg, c: (0, hg))
    st = pl.BlockSpec((1, npg, LANES, LANES), lambda b, hg, c: (b, hg, 0, 0))
    row = lambda x: x.reshape(1, D).astype(F32)
    return pl.pallas_call(
        functools.partial(_rwkv_chunk_kernel, hs=hs),
        grid=(B, D // L, T // chunk),
        in_specs=[seq] * 6 + [par] * 5 + [st],
        out_specs=[seq, st],
        out_shape=[jax.ShapeDtypeStruct((B, T, D), BF16),
                   jax.ShapeDtypeStruct(s0.shape, F32)],
        scratch_shapes=[pltpu.VMEM((npg, LANES, LANES), F32), pltpu.VMEM((chunk, L), F32)],
        compiler_params=_params("parallel", "parallel", "arbitrary"),
        name="rwkv_chunked",
    )(r, k, v, wp, ap, g, row(k_k), row(k_a), row(r_k), row(ln_g), row(ln_b), s0)


def _pairs_from_heads(s, hs):
    B, H = s.shape[:2]
    s = s.reshape(B, H // 2, 2, hs, hs)
    z = jnp.zeros_like(s[:, :, 0])
    top = jnp.concatenate([s[:, :, 0], z], axis=-1)
    bot = jnp.concatenate([z, s[:, :, 1]], axis=-1)
    return jnp.concatenate([top, bot], axis=-2)


def _heads_from_pairs(sp, hs):
    B, P = sp.shape[:2]
    return jnp.stack([sp[:, :, :hs, :hs], sp[:, :, hs:, hs:]], axis=2).reshape(B, 2 * P, hs, hs)


def _pad_rows(x, front, back):
    return jnp.pad(x, ((0, 0), (front, back), (0, 0)))


def kernel(x_prompt, x_sample, mem_prompt, state_lru_conv, state_lru_h, cache_swa_k, cache_swa_v,
           state_rwkv_shift, state_rwkv_wkv, cache_mem_k, cache_mem_v, state_ffn_conv,
           a_norm_g, a_w_in, a_conv_w, a_conv_b, a_gate_a_w, a_gate_a_b, a_gate_x_w, a_gate_x_b,
           a_lambda, b_q_norm_g, b_k_norm_g, b_sink, a_w_out,
           c_norm_g, c_mu, c_w_r, c_w_k, c_w_v, c_w_o, c_w0, c_w1, c_w2, c_a0, c_a1, c_a2,
           c_g1, c_g2, c_k_k, c_k_a, c_r_k, c_ln_g, c_ln_b,
           m_norm_g, m_mem_norm_g, m_w_q, m_w_kv, m_q_norm_g, m_k_norm_g, m_w_o,
           f_norm_g, f_w_in, f_conv_w, f_conv_b, f_w_out):
    D = x_prompt.shape[-1]
    depth = m_norm_g.shape[0]
    W = a_conv_w.shape[-1]
    KA = a_conv_w.shape[1]
    hd = b_q_norm_g.shape[-1]
    n_q = b_sink.shape[-1]
    n_kv = cache_swa_k.shape[3]
    win = cache_swa_k.shape[2]
    QW, KW = n_q * hd, n_kv * hd
    hs = c_r_k.shape[-1]
    n_rw = c_r_k.shape[1]
    m_heads, m_hd = cache_mem_k.shape[3], cache_mem_k.shape[4]
    MW = m_heads * m_hd
    FF = f_conv_w.shape[-1]
    KF = f_conv_w.shape[1]
    bf = lambda w: w.astype(BF16)

    def run(x, T_real, lru_conv, lru_h, swa_k, swa_v, rw_shift, rw_wkv, mem_k, mem_v, ffn_conv, prompt):
        B = x.shape[0]
        T = T_real
        M = B * T
        xf = x.reshape(M, D)
        outs = {}
        ia = ic = 0
        for l in range(depth):
            if l % 2 == 0:
                i = ia
                ia += 1
                h = rmsnorm_rows(xf, a_norm_g[i])
                w_in = a_w_in[i]
                zz = matmul(h, bf(w_in[:, :2 * W]))
                qkv = matmul(h, bf(w_in[:, 2 * W:]))
                qn = headnorm(qkv, 0, QW, b_q_norm_g[i], hd, BF16)
                kn = headnorm(qkv, QW // KW, KW, b_k_norm_g[i], hd, F32)
                vv = qkv[:, QW + KW:]
                zz3 = zz.reshape(B, T, 2 * W)
                if prompt:
                    out_a, h_last = lru_mixer(zz3, 0, zz3, 1, jnp.zeros((B, 1, W), F32),
                                              a_conv_w[i], a_conv_b[i], a_gate_a_w[i], a_gate_a_b[i],
                                              a_gate_x_w[i], a_gate_x_b[i], a_lambda[i], n_pad=0, tt=256)
                    conv_new = zz3[:, T - (KA - 1):, :W]
                    o = swa_attention(qn.reshape(B, T, QW), bf(kn).reshape(B, T, KW),
                                      bf(vv).reshape(B, T, KW), b_sink[i], j0=0, win=win, hd=hd)
                    k_new = kn.reshape(B, T, n_kv, hd)[:, T - win:]
                    v_new = vv.reshape(B, T, n_kv, hd)[:, T - win:]
                else:
                    n_pad = SUBLANES - T
                    xr_hist = jnp.concatenate([lru_conv[i].astype(F32), zz3[:, :, :W]], axis=1)
                    xr_p = _pad_rows(xr_hist, SUBLANES - xr_hist.shape[1], 0)
                    yg_p = _pad_rows(zz3[:, :, W:], n_pad, 0)
                    out_a, h_last = lru_mixer(xr_p, 0, yg_p, 0, lru_h[i].reshape(B, 1, W).astype(F32),
                                              a_conv_w[i], a_conv_b[i], a_gate_a_w[i], a_gate_a_b[i],
                                              a_gate_x_w[i], a_gate_x_b[i], a_lambda[i], n_pad=n_pad, tt=SUBLANES)
                    out_a = out_a[:, n_pad:]
                    conv_new = xr_hist[:, T:]
                    kc = swa_k[i].reshape(B, win, KW)
                    vc = swa_v[i].reshape(B, win, KW)
                    k_all = jnp.concatenate([kc, kn.reshape(B, T, KW)], axis=1)
                    v_all = jnp.concatenate([vc, vv.reshape(B, T, KW)], axis=1)
                    o = swa_attention(_pad_rows(qn.reshape(B, T, QW), 0, win - T),
                                      bf(_pad_rows(k_all, 0, win - T)), bf(_pad_rows(v_all, 0, win - T)),
                                      b_sink[i], j0=1, win=win, hd=hd)[:, :T]
                    k_new = k_all[:, T:].reshape(B, win, n_kv, hd)
                    v_new = v_all[:, T:].reshape(B, win, n_kv, hd)
                mix = jnp.concatenate([out_a.reshape(M, W), o.reshape(M, QW)], axis=-1)
                xf = matmul(mix, bf(a_w_out[i]), res=xf)
                outs.setdefault("lc", []).append(conv_new)
                outs.setdefault("lh", []).append(h_last.reshape(B, W))
                outs.setdefault("sk", []).append(k_new)
                outs.setdefault("sv", []).append(v_new)
            else:
                i = ic
                ic += 1
                Tp = T if prompt else SUBLANES
                chunk = 64 if prompt else 16
                x3 = xf.reshape(B, T, D)
                if not prompt:
                    x3 = _pad_rows(x3, 0, Tp - T)
                mixes = rwkv_mix(x3, rw_shift[i].reshape(B, 1, D).astype(F32), c_norm_g[i], c_mu[i], n_valid=T)
                xr, xw, xk, xv, xa, xg = [m.reshape(B * Tp, D) for m in mixes[:6]]
                sh_new = mixes[6].reshape(B, D)
                r = matmul(xr, bf(c_w_r[i]))
                k = matmul(xk, bf(c_w_k[i]))
                v = matmul(xv, bf(c_w_v[i]))
                wp = matmul(matmul(xw, bf(c_w1[i]), act="tanh", out_dtype=BF16), bf(c_w2[i]), bias=c_w0[i])
                ap = matmul(matmul(xa, bf(c_a1[i]), out_dtype=BF16), bf(c_a2[i]), bias=c_a0[i])
                gl = c_g1[i].shape[1]
                glp = -(-gl // LANES) * LANES
                g1p = jnp.pad(c_g1[i], ((0, 0), (0, glp - gl)))
                g2p = jnp.pad(c_g2[i], ((0, glp - gl), (0, 0)))
                g = matmul(matmul(xg, bf(g1p), act="sigmoid", out_dtype=BF16), bf(g2p))
                if not prompt:
                    sq = lambda t, fill=0.0: jnp.pad(t.reshape(B, Tp, D)[:, :T], ((0, 0), (0, chunk - T), (0, 0)),
                                                     constant_values=fill)
                    r, k, v, ap, g = sq(r), sq(k), sq(v), sq(ap), sq(g)
                    wp = sq(wp, -1e30)
                else:
                    sq = lambda t: t.reshape(B, Tp, D)
                    r, k, v, wp, ap, g = sq(r), sq(k), sq(v), sq(wp), sq(ap), sq(g)
                s0 = _pairs_from_heads(rw_wkv[i].astype(F32), hs)
                yo, s_end = rwkv_chunked(r, k, v, wp, ap, g, c_k_k[i], c_k_a[i], c_r_k[i], c_ln_g[i], c_ln_b[i],
                                         s0, hs=hs, chunk=chunk)
                yo = yo[:, :T].reshape(M, D)
                xf = matmul(yo, bf(c_w_o[i]), res=xf)
                outs.setdefault("rs", []).append(sh_new)
                outs.setdefault("rw", []).append(_heads_from_pairs(s_end, hs))
            hm = rmsnorm_rows(xf, m_norm_g[l])
            q = matmul(hm, bf(m_w_q[l]))
            Tq = T if prompt else SUBLANES
            q3 = q.reshape(B, T, MW)
            if not prompt:
                q3 = _pad_rows(q3, 0, Tq - T)
            ML = mem_k[l].shape[1]
            om = mem_attention(q3, m_q_norm_g[l], bf(mem_k[l]).reshape(B, ML, MW), bf(mem_v[l]).reshape(B, ML, MW),
                               n_heads=m_heads)
            om = om[:, :T].reshape(M, MW)
            xf = matmul(om, bf(m_w_o[l]), res=xf)
            hf = rmsnorm_rows(xf, f_norm_g[l])
            gu = matmul(hf, bf(f_w_in[l]), tn=1024)
            gu3 = gu.reshape(B, T, 2 * FF)
            tc = 512
            if prompt:
                act = ffn_act(gu3, 0, gu3, FF // tc, f_conv_w[l], f_conv_b[l], tc=tc)
                f_new = gu3[:, T - (KF - 1):, :FF]
            else:
                g_hist = jnp.concatenate([ffn_conv[l].astype(F32), gu3[:, :, :FF]], axis=1)
                g_p = _pad_rows(g_hist, SUBLANES - g_hist.shape[1], 0)
                up_p = _pad_rows(gu3[:, :, FF:], SUBLANES - T, 0)
                act = ffn_act(g_p, 0, up_p, 0, f_conv_w[l], f_conv_b[l], tt=SUBLANES, tc=tc)[:, SUBLANES - T:]
                f_new = g_hist[:, T:]
            xf = matmul(act.reshape(M, FF), bf(f_w_out[l]), res=xf)
            outs.setdefault("fc", []).append(f_new)
        st = lambda name: jnp.stack(outs[name])
        return (xf.reshape(B, T, D), st("lc"), st("lh"), st("sk"), st("sv"), st("rs"), st("rw"), st("fc"))

    Bp, S = x_prompt.shape[:2]
    ML = mem_prompt.shape[1]
    mem_flat = mem_prompt.reshape(Bp * ML, D)
    mks, mvs = [], []
    for l in range(depth):
        mn = rmsnorm_rows(mem_flat, m_mem_norm_g[l])
        kv = matmul(mn, bf(m_w_kv[l]))
        mk = headnorm(kv, 0, MW, m_k_norm_g[l], m_hd, F32)
        mks.append(mk.reshape(Bp, ML, m_heads, m_hd))
        mvs.append(kv[:, MW:].reshape(Bp, ML, m_heads, m_hd))
    p_mem_k = jnp.stack(mks)
    p_mem_v = jnp.stack(mvs)

    n_a = a_norm_g.shape[0]
    n_c = c_norm_g.shape[0]
    zeros = lambda *s: jnp.zeros(s, F32)
    y_p, p_lc, p_lh, p_sk, p_sv, p_rs, p_rw, p_fc = run(
        x_prompt, S, None, None, None, None, zeros(n_c, Bp, D), zeros(n_c, Bp, n_rw, hs, hs),
        p_mem_k, p_mem_v, None, True)
    Bs, Ts = x_sample.shape[:2]
    y_s, s_lc, s_lh, s_sk, s_sv, s_rs, s_rw, s_fc = run(
        x_sample, Ts, state_lru_conv, state_lru_h, cache_swa_k, cache_swa_v,
        state_rwkv_shift, state_rwkv_wkv, cache_mem_k, cache_mem_v, state_ffn_conv, False)
    return (y_p, y_s, p_lc, p_lh, p_sk, p_sv, p_rs, p_rw, p_mem_k, p_mem_v, p_fc,
            s_lc, s_lh, s_sk, s_sv, s_rs, s_rw, s_fc)
```

```python
import functools
import math

import jax
import jax.numpy as jnp
from jax import lax
from jax.experimental import pallas as pl
from jax.experimental.pallas import tpu as pltpu

F32 = jnp.float32
BF16 = jnp.bfloat16

NORM_EPS = 1e-6
RW_GN_EPS = 64e-5
LRU_C = 8.0
LANES = 128
SUBLANES = 8
VMEM_LIMIT_BYTES = 56 * 1024 * 1024


def _params(*sem):
    return pltpu.CompilerParams(dimension_semantics=sem, vmem_limit_bytes=VMEM_LIMIT_BYTES)


def _nt(a, b):
    return lax.dot_general(a, b, (((1,), (1,)), ((), ())), preferred_element_type=F32)


def _tn(a, b):
    return lax.dot_general(a, b, (((0,), (0,)), ((), ())), preferred_element_type=F32)


def _dot(a, b):
    return jnp.dot(a, b, preferred_element_type=F32)


def _split3(x):
    hi = x.astype(BF16)
    r1 = x - hi.astype(F32)
    mid = r1.astype(BF16)
    lo = (r1 - mid.astype(F32)).astype(BF16)
    return hi, mid, lo


def _group_sum_bcast(x, width):
    m, L = x.shape
    n = L // LANES
    if width == LANES:
        parts = []
        for c in range(n):
            s = jnp.sum(x[:, c * LANES:(c + 1) * LANES], axis=-1, keepdims=True)
            parts.append(jnp.broadcast_to(s, (m, LANES)))
        return parts[0] if n == 1 else jnp.concatenate(parts, axis=-1)
    li = lax.broadcasted_iota(jnp.int32, (LANES, LANES), 0) // width
    lj = lax.broadcasted_iota(jnp.int32, (LANES, LANES), 1) // width
    e = jnp.where(li == lj, 1.0, 0.0).astype(BF16)
    xs = x if n == 1 else jnp.concatenate([x[:, c * LANES:(c + 1) * LANES] for c in range(n)], axis=0)
    hi = xs.astype(BF16)
    lo = (xs - hi.astype(F32)).astype(BF16)
    out = _dot(hi, e) + _dot(lo, e)
    return out if n == 1 else jnp.concatenate([out[c * m:(c + 1) * m] for c in range(n)], axis=-1)


def _shift_rows(x, prev8, s):
    rolled = pltpu.roll(x, s, 0)
    top = jnp.where(lax.broadcasted_iota(jnp.int32, (SUBLANES, 1), 0) < s,
                    pltpu.roll(prev8, s, 0), rolled[0:SUBLANES])
    if x.shape[0] == SUBLANES:
        return top
    return jnp.concatenate([top, rolled[SUBLANES:]], axis=0)


def _rmsnorm_kernel(x_ref, g_ref, o_ref):
    x = x_ref[...]
    ms = jnp.mean(x * x, axis=-1, keepdims=True)
    o_ref[...] = (x * lax.rsqrt(ms + NORM_EPS) * g_ref[...]).astype(o_ref.dtype)


def rmsnorm_rows(x, g, out_dtype=BF16, tm=256):
    M, D = x.shape
    tm = min(tm, M)
    return pl.pallas_call(
        _rmsnorm_kernel,
        grid=(M // tm,),
        in_specs=[pl.BlockSpec((tm, D), lambda i: (i, 0)), pl.BlockSpec((1, D), lambda i: (0, 0))],
        out_specs=pl.BlockSpec((tm, D), lambda i: (i, 0)),
        out_shape=jax.ShapeDtypeStruct((M, D), out_dtype),
        compiler_params=_params("parallel"),
        name="rmsnorm_rows",
    )(x, g.reshape(1, D))


def _mm_kernel(*refs, nk, has_bias, has_res, act):
    x_ref, w_ref = refs[0], refs[1]
    pos = 2
    b_ref = r_ref = None
    if has_bias:
        b_ref = refs[pos]
        pos += 1
    if has_res:
        r_ref = refs[pos]
        pos += 1
    o_ref = refs[pos]
    acc_ref = refs[pos + 1] if nk > 1 else None

    def epilogue(y):
        if has_bias:
            y = y + b_ref[...]
        if act == "tanh":
            y = jnp.tanh(y)
        elif act == "sigmoid":
            y = jax.nn.sigmoid(y)
        if has_res:
            y = y + r_ref[...]
        o_ref[...] = y.astype(o_ref.dtype)

    part = _dot(x_ref[...], w_ref[...])
    if nk == 1:
        epilogue(part)
        return
    k = pl.program_id(2)

    @pl.when(k == 0)
    def _():
        acc_ref[...] = part

    @pl.when(k > 0)
    def _():
        acc_ref[...] += part

    @pl.when(k == nk - 1)
    def _():
        epilogue(acc_ref[...])


def _pick(n, prefs):
    for p in prefs:
        if n % p == 0:
            return p
    return n


def matmul(x, w, *, bias=None, res=None, act=None, out_dtype=F32, tm=1024, tn=512, tk=4096):
    M, K = x.shape
    N = w.shape[1]
    tm = min(tm, M)
    tn = _pick(N, (tn, 512, 256, 128))
    tk = _pick(K, (tk, 2048, 1024, 512))
    nk = K // tk
    in_specs = [pl.BlockSpec((tm, tk), lambda i, j, k: (i, k)),
                pl.BlockSpec((tk, tn), lambda i, j, k: (k, j))]
    args = [x, w]
    if bias is not None:
        in_specs.append(pl.BlockSpec((1, tn), lambda i, j, k: (0, j)))
        args.append(bias.reshape(1, N).astype(F32))
    if res is not None:
        in_specs.append(pl.BlockSpec((tm, tn), lambda i, j, k: (i, j)))
        args.append(res)
    kern = functools.partial(_mm_kernel, nk=nk, has_bias=bias is not None, has_res=res is not None, act=act)
    return pl.pallas_call(
        kern,
        grid=(M // tm, N // tn, nk),
        in_specs=in_specs,
        out_specs=pl.BlockSpec((tm, tn), lambda i, j, k: (i, j)),
        out_shape=jax.ShapeDtypeStruct((M, N), out_dtype),
        scratch_shapes=[pltpu.VMEM((tm, tn), F32)] if nk > 1 else [],
        compiler_params=_params("parallel", "parallel", "arbitrary"),
        name="matmul",
    )(*args)


def _headnorm_kernel(x_ref, g_ref, o_ref, *, hd):
    x = x_ref[...]
    ms = _group_sum_bcast(x * x, hd) * (1.0 / hd)
    o_ref[...] = (x * lax.rsqrt(ms + NORM_EPS) * g_ref[...]).astype(o_ref.dtype)


def headnorm(x, col_block, width, g, hd, out_dtype, tm=256):
    M = x.shape[0]
    tm = min(tm, M)
    g_row = jnp.tile(g.astype(F32), width // hd).reshape(1, width)
    return pl.pallas_call(
        functools.partial(_headnorm_kernel, hd=hd),
        grid=(M // tm,),
        in_specs=[pl.BlockSpec((tm, width), lambda i: (i, col_block)),
                  pl.BlockSpec((1, width), lambda i: (0, 0))],
        out_specs=pl.BlockSpec((tm, width), lambda i: (i, 0)),
        out_shape=jax.ShapeDtypeStruct((M, width), out_dtype),
        compiler_params=_params("parallel"),
        name="headnorm",
    )(x, g_row)


def _lru_kernel(xr_ref, halo_ref, yg_ref, h0_ref, cw_ref, cb_ref, gaw_ref, gab_ref, gxw_ref, gxb_ref,
                lam_ref, o_ref, hl_ref, a_scr, u_scr, h_scr, *, n_pad, n_blocks, bw, scan_w):
    t = pl.program_id(1)
    tt, W = a_scr.shape

    @pl.when(t == 0)
    def _():
        h_scr[...] = h0_ref[0]

    x = xr_ref[0]
    prev = halo_ref[0] * jnp.where(t > 0, 1.0, 0.0)
    xc = cb_ref[...] + cw_ref[3:4, :] * x
    for s in (1, 2, 3):
        xc = xc + cw_ref[3 - s:4 - s, :] * _shift_rows(x, prev, s)

    nsp = -LRU_C * jax.nn.softplus(-lam_ref[...])
    if n_pad:
        live = (lax.broadcasted_iota(jnp.int32, (tt, 1), 0) >= n_pad) | (t > 0)
    for n in range(n_blocks):
        sl = slice(n * bw, (n + 1) * bw)
        xb = xc[:, sl]
        xb16 = xb.astype(BF16)
        r = jax.nn.sigmoid(_dot(xb16, gaw_ref[n]) + gab_ref[:, sl])
        i = jax.nn.sigmoid(_dot(xb16, gxw_ref[n]) + gxb_ref[:, sl])
        log_a = r * nsp[:, sl]
        a = jnp.exp(log_a)
        u = jnp.sqrt(jnp.maximum(-jnp.tanh(log_a) * (a * a + 1.0), 0.0)) * (i * xb)
        if n_pad:
            a = jnp.where(live, a, 1.0)
            u = jnp.where(live, u, 0.0)
        a_scr[:, sl] = a
        u_scr[:, sl] = u

    row8 = lax.broadcasted_iota(jnp.int32, (SUBLANES, 1), 0)
    for c in range(W // scan_w):
        cs = slice(c * scan_w, (c + 1) * scan_w)

        def body(gi, h, cs=cs):
            r0 = pl.multiple_of(gi * SUBLANES, SUBLANES)
            A = a_scr[pl.ds(r0, SUBLANES), cs]
            U = u_scr[pl.ds(r0, SUBLANES), cs]
            for s in (1, 2, 4):
                As = pltpu.roll(A, s, 0)
                Us = pltpu.roll(U, s, 0)
                m = row8 >= s
                U = jnp.where(m, A * Us + U, U)
                A = jnp.where(m, A * As, A)
            H = A * h + U
            u_scr[pl.ds(r0, SUBLANES), cs] = H
            return H[SUBLANES - 1:SUBLANES, :]

        h_end = lax.fori_loop(0, tt // SUBLANES, body, h_scr[:, cs])
        h_scr[:, cs] = h_end

    o_ref[0] = (u_scr[...] * jax.nn.gelu(yg_ref[0], approximate=True)).astype(o_ref.dtype)
    hl_ref[0] = h_scr[...]


def lru_mixer(xr, xr_cb, yg, yg_cb, h0, conv_w, conv_b, ga_w, ga_b, gx_w, gx_b, lam, *, n_pad, tt):
    B, T, _ = xr.shape
    W = conv_w.shape[1]
    nb, bw = ga_w.shape[0], ga_w.shape[1]
    tt = min(tt, T)
    hb = tt // SUBLANES
    row = lambda v: v.reshape(1, W).astype(F32)
    kern = functools.partial(_lru_kernel, n_pad=n_pad, n_blocks=nb, bw=bw, scan_w=min(W, 512))
    return pl.pallas_call(
        kern,
        grid=(B, T // tt),
        in_specs=[
            pl.BlockSpec((1, tt, W), lambda b, t: (b, t, xr_cb)),
            pl.BlockSpec((1, SUBLANES, W), lambda b, t: (b, jnp.maximum(t * hb - 1, 0), xr_cb)),
            pl.BlockSpec((1, tt, W), lambda b, t: (b, t, yg_cb)),
            pl.BlockSpec((1, 1, W), lambda b, t: (b, 0, 0)),
            pl.BlockSpec((4, W), lambda b, t: (0, 0)),
            pl.BlockSpec((1, W), lambda b, t: (0, 0)),
            pl.BlockSpec((nb, bw, bw), lambda b, t: (0, 0, 0)),
            pl.BlockSpec((1, W), lambda b, t: (0, 0)),
            pl.BlockSpec((nb, bw, bw), lambda b, t: (0, 0, 0)),
            pl.BlockSpec((1, W), lambda b, t: (0, 0)),
            pl.BlockSpec((1, W), lambda b, t: (0, 0)),
        ],
        out_specs=[pl.BlockSpec((1, tt, W), lambda b, t: (b, t, 0)),
                   pl.BlockSpec((1, 1, W), lambda b, t: (b, 0, 0))],
        out_shape=[jax.ShapeDtypeStruct((B, T, W), BF16), jax.ShapeDtypeStruct((B, 1, W), F32)],
        scratch_shapes=[pltpu.VMEM((tt, W), F32), pltpu.VMEM((tt, W), F32), pltpu.VMEM((1, W), F32)],
        compiler_params=_params("parallel", "arbitrary"),
        name="lru_mixer",
    )(xr, xr, yg, h0, conv_w.astype(F32), row(conv_b), ga_w.astype(BF16), row(ga_b),
      gx_w.astype(BF16), row(gx_b), row(lam))


def _swa_kernel(sink_ref, q_ref, kp_ref, kc_ref, vp_ref, vc_ref, o_ref, *, j0, n_kv, group, hd, win):
    j = pl.program_id(1) + j0
    q = q_ref[0]
    k2 = jnp.concatenate([kp_ref[0], kc_ref[0]], axis=0)
    v2 = jnp.concatenate([vp_ref[0], vc_ref[0]], axis=0)
    qi = lax.broadcasted_iota(jnp.int32, (win, 2 * win), 0)
    kj = lax.broadcasted_iota(jnp.int32, (win, 2 * win), 1)
    dist = qi + win - kj
    mask = (dist >= 0) & (dist < win) & ((j > 0) | (kj >= win))
    scale = hd ** -0.5
    for kvh in range(n_kv):
        kh = k2[:, kvh * hd:(kvh + 1) * hd]
        vh = v2[:, kvh * hd:(kvh + 1) * hd]
        for g in range(0, group, 2):
            outs = []
            for gg in (g, g + 1):
                h = kvh * group + gg
                qh = q[:, h * hd:(h + 1) * hd]
                s = _nt(qh, kh) * scale
                s = jnp.where(mask, s, -jnp.inf)
                sk = sink_ref[h]
                m = jnp.maximum(jnp.max(s, axis=-1, keepdims=True), sk)
                p = jnp.exp(s - m)
                den = jnp.sum(p, axis=-1, keepdims=True) + jnp.exp(sk - m)
                outs.append(_dot(p.astype(BF16), vh) / den)
            h0 = kvh * group + g
            o_ref[0, :, h0 * hd:(h0 + 2) * hd] = jnp.concatenate(outs, axis=-1).astype(o_ref.dtype)


def swa_attention(q, k, v, sink, *, j0, win, hd):
    B, Tq, QW = q.shape
    KW = k.shape[2]
    n_kv = KW // hd
    group = QW // KW
    nq = Tq // win
    kern = functools.partial(_swa_kernel, j0=j0, n_kv=n_kv, group=group, hd=hd, win=win)
    prev = lambda b, j: (b, jnp.maximum(j + j0 - 1, 0), 0)
    cur = lambda b, j: (b, j + j0, 0)
    return pl.pallas_call(
        kern,
        grid=(B, nq),
        in_specs=[pl.BlockSpec(memory_space=pltpu.SMEM),
                  pl.BlockSpec((1, win, QW), lambda b, j: (b, j, 0)),
                  pl.BlockSpec((1, win, KW), prev), pl.BlockSpec((1, win, KW), cur),
                  pl.BlockSpec((1, win, KW), prev), pl.BlockSpec((1, win, KW), cur)],
        out_specs=pl.BlockSpec((1, win, QW), lambda b, j: (b, j, 0)),
        out_shape=jax.ShapeDtypeStruct((B, Tq, QW), BF16),
        compiler_params=_params("parallel", "parallel"),
        name="swa_attention",
    )(sink.astype(F32), q, k, k, v, v)


def _mem_attn_kernel(q_ref, g_ref, mk_ref, mv_ref, o_ref, *, n_heads, hd):
    q = q_ref[0]
    scale = hd ** -0.5
    for h in range(n_heads):
        sl = slice(h * hd, (h + 1) * hd)
        qh = q[:, sl]
        qn = (qh * lax.rsqrt(jnp.mean(qh * qh, axis=-1, keepdims=True) + NORM_EPS) * g_ref[...]).astype(BF16)
        s = _nt(qn, mk_ref[0, :, sl]) * scale
        m = jnp.max(s, axis=-1, keepdims=True)
        p = jnp.exp(s - m)
        p = p / jnp.sum(p, axis=-1, keepdims=True)
        o_ref[0, :, sl] = _dot(p.astype(BF16), mv_ref[0, :, sl]).astype(o_ref.dtype)


def mem_attention(q, qn_g, mk, mv, *, n_heads, tt=256):
    B, T, MW = q.shape
    ML = mk.shape[1]
    hd = MW // n_heads
    tt = min(tt, T)
    return pl.pallas_call(
        functools.partial(_mem_attn_kernel, n_heads=n_heads, hd=hd),
        grid=(B, T // tt),
        in_specs=[pl.BlockSpec((1, tt, MW), lambda b, t: (b, t, 0)),
                  pl.BlockSpec((1, hd), lambda b, t: (0, 0)),
                  pl.BlockSpec((1, ML, MW), lambda b, t: (b, 0, 0)),
                  pl.BlockSpec((1, ML, MW), lambda b, t: (b, 0, 0))],
        out_specs=pl.BlockSpec((1, tt, MW), lambda b, t: (b, t, 0)),
        out_shape=jax.ShapeDtypeStruct((B, T, MW), BF16),
        compiler_params=_params("parallel", "parallel"),
        name="mem_attention",
    )(q, qn_g.reshape(1, hd).astype(F32), mk, mv)


def _ffn_act_kernel(g_ref, halo_ref, up_ref, cw_ref, cb_ref, o_ref):
    t = pl.program_id(2)
    g = g_ref[0]
    prev = halo_ref[0] * jnp.where(t > 0, 1.0, 0.0)
    gc = cb_ref[...] + cw_ref[2:3, :] * g
    for s in (1, 2):
        gc = gc + cw_ref[2 - s:3 - s, :] * _shift_rows(g, prev, s)
    o_ref[0] = (jax.nn.gelu(gc, approximate=True) * up_ref[0]).astype(o_ref.dtype)


def ffn_act(gate, gate_cb0, up, up_cb0, conv_w, conv_b, *, tt=256, tc=512):
    B, T, _ = gate.shape
    C = conv_w.shape[1]
    tt = min(tt, T)
    hb = tt // SUBLANES
    return pl.pallas_call(
        _ffn_act_kernel,
        grid=(B, C // tc, T // tt),
        in_specs=[pl.BlockSpec((1, tt, tc), lambda b, c, t: (b, t, gate_cb0 + c)),
                  pl.BlockSpec((1, SUBLANES, tc), lambda b, c, t: (b, jnp.maximum(t * hb - 1, 0), gate_cb0 + c)),
                  pl.BlockSpec((1, tt, tc), lambda b, c, t: (b, t, up_cb0 + c)),
                  pl.BlockSpec((3, tc), lambda b, c, t: (0, c)),
                  pl.BlockSpec((1, tc), lambda b, c, t: (0, c))],
        out_specs=pl.BlockSpec((1, tt, tc), lambda b, c, t: (b, t, c)),
        out_shape=jax.ShapeDtypeStruct((B, T, C), BF16),
        compiler_params=_params("parallel", "parallel", "parallel"),
        name="ffn_act",
    )(gate, gate, up, conv_w.astype(F32), conv_b.reshape(1, C).astype(F32))


def _ffn_in_kernel(x_ref, wg_ref, wu_ref, cw_ref, cb_ref, o_ref, tail_ref, carry_scr, *, blocks_per_seq, sub):
    i = pl.program_id(0)
    j = pl.program_id(1)
    tm = x_ref.shape[0]
    @pl.when(i % blocks_per_seq == 0)
    def _():
        carry_scr[j] = jnp.zeros(carry_scr.shape[1:], F32)

    prev = carry_scr[j]
    for c in range(tm // sub):
        rows = slice(c * sub, (c + 1) * sub)
        x = x_ref[rows, :]
        gate = _dot(x, wg_ref[...])
        up = _dot(x, wu_ref[...])
        gc = cb_ref[...] + cw_ref[2:3, :] * gate
        for s in (1, 2):
            gc = gc + cw_ref[2 - s:3 - s, :] * _shift_rows(gate, prev, s)
        o_ref[rows, :] = (jax.nn.gelu(gc, approximate=True) * up).astype(o_ref.dtype)
        prev = gate[sub - SUBLANES:, :]
    carry_scr[j] = prev
    tail_ref[0] = prev


def ffn_in_fused(x, w_in, conv_w, conv_b, *, seq_len, tm=1024, tn=512, sub=256):
    M, D = x.shape
    FF = conv_w.shape[1]
    tm = min(tm, seq_len)
    bps = seq_len // tm
    nj = FF // tn
    act, tails = pl.pallas_call(
        functools.partial(_ffn_in_kernel, blocks_per_seq=bps, sub=min(sub, tm)),
        grid=(M // tm, nj),
        in_specs=[pl.BlockSpec((tm, D), lambda i, j: (i, 0)),
                  pl.BlockSpec((D, tn), lambda i, j: (0, j)),
                  pl.BlockSpec((D, tn), lambda i, j: (0, j + nj)),
                  pl.BlockSpec((3, tn), lambda i, j: (0, j)),
                  pl.BlockSpec((1, tn), lambda i, j: (0, j))],
        out_specs=[pl.BlockSpec((tm, tn), lambda i, j: (i, j)),
                   pl.BlockSpec((1, SUBLANES, tn), lambda i, j: (i, 0, j))],
        out_shape=[jax.ShapeDtypeStruct((M, FF), BF16),
                   jax.ShapeDtypeStruct((M // tm, SUBLANES, FF), F32)],
        scratch_shapes=[pltpu.VMEM((nj, SUBLANES, tn), F32)],
        compiler_params=_params("arbitrary", "arbitrary"),
        name="ffn_in_fused",
    )(x, w_in, w_in, conv_w.astype(F32), conv_b.reshape(1, FF).astype(F32))
    return act, tails[bps - 1::bps]


def _rwkv_mix_kernel(x_ref, halo_ref, s0_ref, g_ref, mu_ref, *out_refs, n_valid_last):
    t = pl.program_id(1)
    nt = pl.num_programs(1)
    o_refs, hl_ref = out_refs[:6], out_refs[6]

    def norm(v):
        return v * lax.rsqrt(jnp.mean(v * v, axis=-1, keepdims=True) + NORM_EPS) * g_ref[...]

    h = norm(x_ref[0])
    hp = norm(halo_ref[0])
    first = jnp.where(t > 0, 1.0, 0.0)
    hp = hp * first + jnp.broadcast_to(s0_ref[0], hp.shape) * (1.0 - first)
    xx = _shift_rows(h, hp, 1) - h
    for j in range(6):
        o_refs[j][0] = (h + xx * mu_ref[j:j + 1, :]).astype(o_refs[j].dtype)

    @pl.when(t == nt - 1)
    def _():
        hl_ref[0] = h[n_valid_last - 1:n_valid_last, :]


def rwkv_mix(x, shift0, g, mu, *, n_valid, tt=256):
    B, T, D = x.shape
    tt = min(tt, T)
    hb = tt // SUBLANES
    n_valid_last = n_valid - (T // tt - 1) * tt
    blk = pl.BlockSpec((1, tt, D), lambda b, t: (b, t, 0))
    one = pl.BlockSpec((1, 1, D), lambda b, t: (b, 0, 0))
    return pl.pallas_call(
        functools.partial(_rwkv_mix_kernel, n_valid_last=n_valid_last),
        grid=(B, T // tt),
        in_specs=[blk,
                  pl.BlockSpec((1, SUBLANES, D), lambda b, t: (b, jnp.maximum(t * hb - 1, 0), 0)),
                  one,
                  pl.BlockSpec((1, D), lambda b, t: (0, 0)),
                  pl.BlockSpec((6, D), lambda b, t: (0, 0))],
        out_specs=[blk] * 6 + [one],
        out_shape=[jax.ShapeDtypeStruct((B, T, D), BF16)] * 6 + [jax.ShapeDtypeStruct((B, 1, D), F32)],
        compiler_params=_params("parallel", "arbitrary"),
        name="rwkv_mix",
    )(x, x, shift0, g.reshape(1, D).astype(F32), mu.astype(F32))


def _rwkv_chunk_kernel(r_ref, k_ref, v_ref, wp_ref, ap_ref, g_ref, kk_ref, ka_ref, rk_ref, lng_ref, lnb_ref,
                       s0_ref, o_ref, so_ref, s_scr, *, hs):
    c = pl.program_id(2)
    nc = pl.num_programs(2)
    _, C, L = r_ref.shape
    n_pairs = L // LANES

    @pl.when(c == 0)
    def _():
        s_scr[...] = s0_ref[0]

    r = r_ref[0]
    k = k_ref[0]
    v = v_ref[0]
    logw = -jnp.exp(-jax.nn.softplus(-wp_ref[0]) - 0.5)
    a = jax.nn.sigmoid(ap_ref[0])
    kk = k * kk_ref[...]
    kk = kk / jnp.maximum(jnp.sqrt(_group_sum_bcast(kk * kk, hs)), 1e-12)
    k2 = k * (1.0 + (a - 1.0) * ka_ref[...])
    bm = kk * a

    ti = lax.broadcasted_iota(jnp.int32, (C, C), 0)
    si = lax.broadcasted_iota(jnp.int32, (C, C), 1)
    tri = jnp.where(ti >= si, 1.0, 0.0).astype(BF16)
    cum = _dot_exact_rhs_lhs(tri, logw)
    e_in = jnp.exp(cum)
    e_out = jnp.exp(-cum)
    e_end = jnp.exp(cum[C - 1:C, :] - cum)
    g_end = jnp.exp(cum[C - 1:C, :])
    rt = r * e_in
    kkt = kk * jnp.exp(cum - logw)
    bh = bm * e_out
    kh = k2 * e_out
    bbar = bm * e_end
    kbar = k2 * e_end

    lane = lax.broadcasted_iota(jnp.int32, (1, LANES), 1)
    m0 = lane < hs
    C2 = 2 * C
    ri = lax.broadcasted_iota(jnp.int32, (C2, C2), 0)
    ci = lax.broadcasted_iota(jnp.int32, (C2, C2), 1)
    same = (ri // C) == (ci // C)
    strict = same & ((ri % C) > (ci % C))
    lower = (ri % C) >= (ci % C)
    vi = lax.broadcasted_iota(jnp.int32, (LANES, LANES), 0) // hs
    vj = lax.broadcasted_iota(jnp.int32, (LANES, LANES), 1) // hs
    blockdiag = vi == vj

    def stack_heads(x):
        return jnp.concatenate([jnp.where(m0, x, 0.0), jnp.where(m0, 0.0, x)], axis=0).astype(BF16)

    n_steps = int(math.log2(C))
    P = range(n_pairs)
    sls = [slice(p * LANES, (p + 1) * LANES) for p in P]
    S = [s_scr[p] for p in P]
    S16 = [s.astype(BF16) for s in S]
    V16 = [v[:, sl].astype(BF16) for sl in sls]
    lhk = [stack_heads(kkt[:, sl]) for sl in sls]
    wide = C2 % LANES == 0
    if wide:
        gbk = [_nt(lhk[p], jnp.concatenate([stack_heads(bh[:, sls[p]]), stack_heads(kh[:, sls[p]])], axis=0))
               for p in P]
        nmat = [jnp.where(strict, -g[:, :C2], 0.0) for g in gbk]
        auk = [jnp.where(strict, -g[:, C2:], 0.0).astype(BF16) for g in gbk]
    else:
        nmat = [jnp.where(strict, -_nt(lhk[p], stack_heads(bh[:, sls[p]])), 0.0) for p in P]
        auk = [jnp.where(strict, -_nt(lhk[p], stack_heads(kh[:, sls[p]])), 0.0).astype(BF16) for p in P]
    sprod = [_nt(jnp.concatenate([lhk[p], rt[:, sls[p]].astype(BF16)], axis=0), S16[p]) for p in P]
    u = [_dot(auk[p], jnp.concatenate([V16[p], V16[p]], axis=0)) - sprod[p][:C2] for p in P]
    npow = [n.astype(BF16) for n in nmat]
    for it in range(n_steps):
        last = it + 1 == n_steps
        if wide and not last:
            res = [_dot(npow[p], jnp.concatenate([u[p].astype(BF16), npow[p]], axis=1)) for p in P]
            u = [u[p] + res[p][:, :LANES] for p in P]
            npow = [res[p][:, LANES:].astype(BF16) for p in P]
        else:
            u = [u[p] + _dot(npow[p], u[p].astype(BF16)) for p in P]
            if not last:
                npow = [_dot(npow[p], npow[p]).astype(BF16) for p in P]
    uv = [jnp.concatenate([jnp.where(m0, u[p][:C], u[p][C:]).astype(BF16), V16[p]], axis=0) for p in P]
    ar = [jnp.where(lower, _nt(stack_heads(rt[:, sls[p]]),
                               jnp.concatenate([bh[:, sls[p]], kh[:, sls[p]]], axis=0).astype(BF16)), 0.0).astype(BF16)
          for p in P]
    tmat = [_dot(ar[p], uv[p]) for p in P]
    ys = [sprod[p][C2:] + jnp.where(m0, tmat[p][:C], tmat[p][C:]) for p in P]
    upd = [_tn(uv[p], jnp.concatenate([bbar[:, sls[p]], kbar[:, sls[p]]], axis=0).astype(BF16)) for p in P]
    for p in P:
        s_scr[p] = S[p] * g_end[:, sls[p]] + jnp.where(blockdiag, upd[p], 0.0)

    y = ys[0] if n_pairs == 1 else jnp.concatenate(ys, axis=-1)
    inv = 1.0 / hs
    mean = _group_sum_bcast(y, hs) * inv
    yc = y - mean
    var = _group_sum_bcast(yc * yc, hs) * inv
    yn = yc * lax.rsqrt(var + RW_GN_EPS) * lng_ref[...] + lnb_ref[...]
    bonus = _group_sum_bcast(r * k2 * rk_ref[...], hs) * v
    o_ref[0] = ((yn + bonus) * g_ref[0]).astype(o_ref.dtype)

    @pl.when(c == nc - 1)
    def _():
        so_ref[0] = s_scr[...]


def _dot_exact_rhs_lhs(m_bf16, x):
    hi, mid, lo = _split3(x)
    return _dot(m_bf16, hi) + _dot(m_bf16, mid) + _dot(m_bf16, lo)


def rwkv_chunked(r, k, v, wp, ap, g, k_k, k_a, r_k, ln_g, ln_b, s0, *, hs, chunk, lanes_per_step=1024):
    B, T, D = r.shape
    L = min(lanes_per_step, D)
    npg = L // LANES
    seq = pl.BlockSpec((1, chunk, L), lambda b, hg, c: (b, c, hg))
    par = pl.BlockSpec((1, L), lambda b, hg, c: (0, hg))
    st = pl.BlockSpec((1, npg, LANES, LANES), lambda b, hg, c: (b, hg, 0, 0))
    row = lambda x: x.reshape(1, D).astype(F32)
    return pl.pallas_call(
        functools.partial(_rwkv_chunk_kernel, hs=hs),
        grid=(B, D // L, T // chunk),
        in_specs=[seq] * 6 + [par] * 5 + [st],
        out_specs=[seq, st],
        out_shape=[jax.ShapeDtypeStruct((B, T, D), BF16),
                   jax.ShapeDtypeStruct(s0.shape, F32)],
        scratch_shapes=[pltpu.VMEM((npg, LANES, LANES), F32)],
        compiler_params=_params("parallel", "parallel", "arbitrary"),
        name="rwkv_chunked",
    )(r, k, v, wp, ap, g, row(k_k), row(k_a), row(r_k), row(ln_g), row(ln_b), s0)


def _pairs_from_heads(s, hs):
    B, H = s.shape[:2]
    s = s.reshape(B, H // 2, 2, hs, hs)
    z = jnp.zeros_like(s[:, :, 0])
    top = jnp.concatenate([s[:, :, 0], z], axis=-1)
    bot = jnp.concatenate([z, s[:, :, 1]], axis=-1)
    return jnp.concatenate([top, bot], axis=-2)


def _heads_from_pairs(sp, hs):
    B, P = sp.shape[:2]
    return jnp.stack([sp[:, :, :hs, :hs], sp[:, :, hs:, hs:]], axis=2).reshape(B, 2 * P, hs, hs)


def _pad_rows(x, front, back):
    return jnp.pad(x, ((0, 0), (front, back), (0, 0)))


def kernel(x_prompt, x_sample, mem_prompt, state_lru_conv, state_lru_h, cache_swa_k, cache_swa_v,
           state_rwkv_shift, state_rwkv_wkv, cache_mem_k, cache_mem_v, state_ffn_conv,
           a_norm_g, a_w_in, a_conv_w, a_conv_b, a_gate_a_w, a_gate_a_b, a_gate_x_w, a_gate_x_b,
           a_lambda, b_q_norm_g, b_k_norm_g, b_sink, a_w_out,
           c_norm_g, c_mu, c_w_r, c_w_k, c_w_v, c_w_o, c_w0, c_w1, c_w2, c_a0, c_a1, c_a2,
           c_g1, c_g2, c_k_k, c_k_a, c_r_k, c_ln_g, c_ln_b,
           m_norm_g, m_mem_norm_g, m_w_q, m_w_kv, m_q_norm_g, m_k_norm_g, m_w_o,
           f_norm_g, f_w_in, f_conv_w, f_conv_b, f_w_out):
    D = x_prompt.shape[-1]
    depth = m_norm_g.shape[0]
    W = a_conv_w.shape[-1]
    KA = a_conv_w.shape[1]
    hd = b_q_norm_g.shape[-1]
    n_q = b_sink.shape[-1]
    n_kv = cache_swa_k.shape[3]
    win = cache_swa_k.shape[2]
    QW, KW = n_q * hd, n_kv * hd
    hs = c_r_k.shape[-1]
    n_rw = c_r_k.shape[1]
    m_heads, m_hd = cache_mem_k.shape[3], cache_mem_k.shape[4]
    MW = m_heads * m_hd
    FF = f_conv_w.shape[-1]
    KF = f_conv_w.shape[1]
    bf = lambda w: w.astype(BF16)

    def run(x, T_real, lru_conv, lru_h, swa_k, swa_v, rw_shift, rw_wkv, mem_k, mem_v, ffn_conv, prompt):
        B = x.shape[0]
        T = T_real
        M = B * T
        xf = x.reshape(M, D)
        outs = {}
        ia = ic = 0
        for l in range(depth):
            if l % 2 == 0:
                i = ia
                ia += 1
                h = rmsnorm_rows(xf, a_norm_g[i])
                w_in = a_w_in[i]
                zz = matmul(h, bf(w_in[:, :2 * W]))
                qkv = matmul(h, bf(w_in[:, 2 * W:]))
                qn = headnorm(qkv, 0, QW, b_q_norm_g[i], hd, BF16)
                kn = headnorm(qkv, QW // KW, KW, b_k_norm_g[i], hd, F32)
                vv = qkv[:, QW + KW:]
                zz3 = zz.reshape(B, T, 2 * W)
                if prompt:
                    out_a, h_last = lru_mixer(zz3, 0, zz3, 1, jnp.zeros((B, 1, W), F32),
                                              a_conv_w[i], a_conv_b[i], a_gate_a_w[i], a_gate_a_b[i],
                                              a_gate_x_w[i], a_gate_x_b[i], a_lambda[i], n_pad=0, tt=256)
                    conv_new = zz3[:, T - (KA - 1):, :W]
                    o = swa_attention(qn.reshape(B, T, QW), bf(kn).reshape(B, T, KW),
                                      bf(vv).reshape(B, T, KW), b_sink[i], j0=0, win=win, hd=hd)
                    k_new = kn.reshape(B, T, n_kv, hd)[:, T - win:]
                    v_new = vv.reshape(B, T, n_kv, hd)[:, T - win:]
                else:
                    n_pad = SUBLANES - T
                    xr_hist = jnp.concatenate([lru_conv[i].astype(F32), zz3[:, :, :W]], axis=1)
                    xr_p = _pad_rows(xr_hist, SUBLANES - xr_hist.shape[1], 0)
                    yg_p = _pad_rows(zz3[:, :, W:], n_pad, 0)
                    out_a, h_last = lru_mixer(xr_p, 0, yg_p, 0, lru_h[i].reshape(B, 1, W).astype(F32),
                                              a_conv_w[i], a_conv_b[i], a_gate_a_w[i], a_gate_a_b[i],
                                              a_gate_x_w[i], a_gate_x_b[i], a_lambda[i], n_pad=n_pad, tt=SUBLANES)
                    out_a = out_a[:, n_pad:]
                    conv_new = xr_hist[:, T:]
                    kc = swa_k[i].reshape(B, win, KW)
                    vc = swa_v[i].reshape(B, win, KW)
                    k_all = jnp.concatenate([kc, kn.reshape(B, T, KW)], axis=1)
                    v_all = jnp.concatenate([vc, vv.reshape(B, T, KW)], axis=1)
                    o = swa_attention(_pad_rows(qn.reshape(B, T, QW), 0, win - T),
                                      bf(_pad_rows(k_all, 0, win - T)), bf(_pad_rows(v_all, 0, win - T)),
                                      b_sink[i], j0=1, win=win, hd=hd)[:, :T]
                    k_new = k_all[:, T:].reshape(B, win, n_kv, hd)
                    v_new = v_all[:, T:].reshape(B, win, n_kv, hd)
                mix = jnp.concatenate([out_a.reshape(M, W), o.reshape(M, QW)], axis=-1)
                xf = matmul(mix, bf(a_w_out[i]), res=xf)
                outs.setdefault("lc", []).append(conv_new)
                outs.setdefault("lh", []).append(h_last.reshape(B, W))
                outs.setdefault("sk", []).append(k_new)
                outs.setdefault("sv", []).append(v_new)
            else:
                i = ic
                ic += 1
                Tp = T if prompt else SUBLANES
                chunk = 64 if prompt else 16
                x3 = xf.reshape(B, T, D)
                if not prompt:
                    x3 = _pad_rows(x3, 0, Tp - T)
                mixes = rwkv_mix(x3, rw_shift[i].reshape(B, 1, D).astype(F32), c_norm_g[i], c_mu[i], n_valid=T)
                xr, xw, xk, xv, xa, xg = [m.reshape(B * Tp, D) for m in mixes[:6]]
                sh_new = mixes[6].reshape(B, D)
                r = matmul(xr, bf(c_w_r[i]))
                k = matmul(xk, bf(c_w_k[i]))
                v = matmul(xv, bf(c_w_v[i]))
                wp = matmul(matmul(xw, bf(c_w1[i]), act="tanh", out_dtype=BF16), bf(c_w2[i]), bias=c_w0[i])
                ap = matmul(matmul(xa, bf(c_a1[i]), out_dtype=BF16), bf(c_a2[i]), bias=c_a0[i])
                gl = c_g1[i].shape[1]
                glp = -(-gl // LANES) * LANES
                g1p = jnp.pad(c_g1[i], ((0, 0), (0, glp - gl)))
                g2p = jnp.pad(c_g2[i], ((0, glp - gl), (0, 0)))
                g = matmul(matmul(xg, bf(g1p), act="sigmoid", out_dtype=BF16), bf(g2p))
                if not prompt:
                    sq = lambda t, fill=0.0: jnp.pad(t.reshape(B, Tp, D)[:, :T], ((0, 0), (0, chunk - T), (0, 0)),
                                                     constant_values=fill)
                    r, k, v, ap, g = sq(r), sq(k), sq(v), sq(ap), sq(g)
                    wp = sq(wp, -1e30)
                else:
                    sq = lambda t: t.reshape(B, Tp, D)
                    r, k, v, wp, ap, g = sq(r), sq(k), sq(v), sq(wp), sq(ap), sq(g)
                s0 = _pairs_from_heads(rw_wkv[i].astype(F32), hs)
                yo, s_end = rwkv_chunked(r, k, v, wp, ap, g, c_k_k[i], c_k_a[i], c_r_k[i], c_ln_g[i], c_ln_b[i],
                                         s0, hs=hs, chunk=chunk)
                yo = yo[:, :T].reshape(M, D)
                xf = matmul(yo, bf(c_w_o[i]), res=xf)
                outs.setdefault("rs", []).append(sh_new)
                outs.setdefault("rw", []).append(_heads_from_pairs(s_end, hs))
            hm = rmsnorm_rows(xf, m_norm_g[l])
            q = matmul(hm, bf(m_w_q[l]))
            Tq = T if prompt else SUBLANES
            q3 = q.reshape(B, T, MW)
            if not prompt:
                q3 = _pad_rows(q3, 0, Tq - T)
            ML = mem_k[l].shape[1]
            om = mem_attention(q3, m_q_norm_g[l], bf(mem_k[l]).reshape(B, ML, MW), bf(mem_v[l]).reshape(B, ML, MW),
                               n_heads=m_heads)
            om = om[:, :T].reshape(M, MW)
            xf = matmul(om, bf(m_w_o[l]), res=xf)
            hf = rmsnorm_rows(xf, f_norm_g[l])
            if prompt:
                act, tail = ffn_in_fused(hf, bf(f_w_in[l]), f_conv_w[l], f_conv_b[l], seq_len=T)
                f_new = tail[:, SUBLANES - (KF - 1):, :]
            else:
                gu3 = matmul(hf, bf(f_w_in[l]), tn=1024).reshape(B, T, 2 * FF)
                g_hist = jnp.concatenate([ffn_conv[l].astype(F32), gu3[:, :, :FF]], axis=1)
                g_p = _pad_rows(g_hist, SUBLANES - g_hist.shape[1], 0)
                up_p = _pad_rows(gu3[:, :, FF:], SUBLANES - T, 0)
                act = ffn_act(g_p, 0, up_p, 0, f_conv_w[l], f_conv_b[l], tt=SUBLANES, tc=FF)[:, SUBLANES - T:]
                f_new = g_hist[:, T:]
            xf = matmul(act.reshape(M, FF), bf(f_w_out[l]), res=xf)
            outs.setdefault("fc", []).append(f_new)
        st = lambda name: jnp.stack(outs[name])
        return (xf.reshape(B, T, D), st("lc"), st("lh"), st("sk"), st("sv"), st("rs"), st("rw"), st("fc"))

    Bp, S = x_prompt.shape[:2]
    ML = mem_prompt.shape[1]
    mem_flat = mem_prompt.reshape(Bp * ML, D)
    mks, mvs = [], []
    for l in range(depth):
        mn = rmsnorm_rows(mem_flat, m_mem_norm_g[l])
        kv = matmul(mn, bf(m_w_kv[l]))
        mk = headnorm(kv, 0, MW, m_k_norm_g[l], m_hd, F32)
        mks.append(mk.reshape(Bp, ML, m_heads, m_hd))
        mvs.append(kv[:, MW:].reshape(Bp, ML, m_heads, m_hd))
    p_mem_k = jnp.stack(mks)
    p_mem_v = jnp.stack(mvs)

    n_a = a_norm_g.shape[0]
    n_c = c_norm_g.shape[0]
    zeros = lambda *s: jnp.zeros(s, F32)
    y_p, p_lc, p_lh, p_sk, p_sv, p_rs, p_rw, p_fc = run(
        x_prompt, S, None, None, None, None, zeros(n_c, Bp, D), zeros(n_c, Bp, n_rw, hs, hs),
        p_mem_k, p_mem_v, None, True)
    Bs, Ts = x_sample.shape[:2]
    y_s, s_lc, s_lh, s_sk, s_sv, s_rs, s_rw, s_fc = run(
        x_sample, Ts, state_lru_conv, state_lru_h, cache_swa_k, cache_swa_v,
        state_rwkv_shift, state_rwkv_wkv, cache_mem_k, cache_mem_v, state_ffn_conv, False)
    return (y_p, y_s, p_lc, p_lh, p_sk, p_sv, p_rs, p_rw, p_mem_k, p_mem_v, p_fc,
            s_lc, s_lh, s_sk, s_sv, s_rs, s_rw, s_fc)
```

```python
import functools
import math

import jax
import jax.numpy as jnp
from jax import lax
from jax.experimental import pallas as pl
from jax.experimental.pallas import tpu as pltpu

F32 = jnp.float32
BF16 = jnp.bfloat16

NORM_EPS = 1e-6
RW_GN_EPS = 64e-5
LRU_C = 8.0
LANES = 128
SUBLANES = 8
VMEM_LIMIT_BYTES = 56 * 1024 * 1024


def _params(*sem):
    return pltpu.CompilerParams(dimension_semantics=sem, vmem_limit_bytes=VMEM_LIMIT_BYTES)


def _nt(a, b):
    return lax.dot_general(a, b, (((1,), (1,)), ((), ())), preferred_element_type=F32)


def _tn(a, b):
    return lax.dot_general(a, b, (((0,), (0,)), ((), ())), preferred_element_type=F32)


def _dot(a, b):
    return jnp.dot(a, b, preferred_element_type=F32)


def _split3(x):
    hi = x.astype(BF16)
    r1 = x - hi.astype(F32)
    mid = r1.astype(BF16)
    lo = (r1 - mid.astype(F32)).astype(BF16)
    return hi, mid, lo


def _group_sum_bcast(x, width):
    m, L = x.shape
    n = L // LANES
    if width == LANES:
        parts = []
        for c in range(n):
            s = jnp.sum(x[:, c * LANES:(c + 1) * LANES], axis=-1, keepdims=True)
            parts.append(jnp.broadcast_to(s, (m, LANES)))
        return parts[0] if n == 1 else jnp.concatenate(parts, axis=-1)
    li = lax.broadcasted_iota(jnp.int32, (LANES, LANES), 0) // width
    lj = lax.broadcasted_iota(jnp.int32, (LANES, LANES), 1) // width
    e = jnp.where(li == lj, 1.0, 0.0).astype(BF16)
    xs = x if n == 1 else jnp.concatenate([x[:, c * LANES:(c + 1) * LANES] for c in range(n)], axis=0)
    hi = xs.astype(BF16)
    lo = (xs - hi.astype(F32)).astype(BF16)
    out = _dot(hi, e) + _dot(lo, e)
    return out if n == 1 else jnp.concatenate([out[c * m:(c + 1) * m] for c in range(n)], axis=-1)


def _shift_rows(x, prev8, s):
    rolled = pltpu.roll(x, s, 0)
    top = jnp.where(lax.broadcasted_iota(jnp.int32, (SUBLANES, 1), 0) < s,
                    pltpu.roll(prev8, s, 0), rolled[0:SUBLANES])
    if x.shape[0] == SUBLANES:
        return top
    return jnp.concatenate([top, rolled[SUBLANES:]], axis=0)


def _rmsnorm_kernel(x_ref, g_ref, o_ref):
    x = x_ref[...]
    ms = jnp.mean(x * x, axis=-1, keepdims=True)
    o_ref[...] = (x * lax.rsqrt(ms + NORM_EPS) * g_ref[...]).astype(o_ref.dtype)


def rmsnorm_rows(x, g, out_dtype=BF16, tm=256):
    M, D = x.shape
    tm = min(tm, M)
    return pl.pallas_call(
        _rmsnorm_kernel,
        grid=(M // tm,),
        in_specs=[pl.BlockSpec((tm, D), lambda i: (i, 0)), pl.BlockSpec((1, D), lambda i: (0, 0))],
        out_specs=pl.BlockSpec((tm, D), lambda i: (i, 0)),
        out_shape=jax.ShapeDtypeStruct((M, D), out_dtype),
        compiler_params=_params("parallel"),
        name="rmsnorm_rows",
    )(x, g.reshape(1, D))


def _mm_kernel(*refs, nk, has_bias, has_res, act):
    x_ref, w_ref = refs[0], refs[1]
    pos = 2
    b_ref = r_ref = None
    if has_bias:
        b_ref = refs[pos]
        pos += 1
    if has_res:
        r_ref = refs[pos]
        pos += 1
    o_ref = refs[pos]
    acc_ref = refs[pos + 1] if nk > 1 else None

    def epilogue(y):
        if has_bias:
            y = y + b_ref[...]
        if act == "tanh":
            y = jnp.tanh(y)
        elif act == "sigmoid":
            y = jax.nn.sigmoid(y)
        if has_res:
            y = y + r_ref[...]
        o_ref[...] = y.astype(o_ref.dtype)

    part = _dot(x_ref[...], w_ref[...].astype(BF16))
    if nk == 1:
        epilogue(part)
        return
    k = pl.program_id(2)

    @pl.when(k == 0)
    def _():
        acc_ref[...] = part

    @pl.when(k > 0)
    def _():
        acc_ref[...] += part

    @pl.when(k == nk - 1)
    def _():
        epilogue(acc_ref[...])


def _pick(n, prefs):
    for p in prefs:
        if n % p == 0:
            return p
    return n


def matmul(x, w, *, layer=None, n0=0, n=None, bias=None, res=None, act=None, out_dtype=F32,
           tm=1024, tn=512, tk=4096):
    M, K = x.shape
    N = w.shape[-1] - n0 if n is None else n
    tm = min(tm, M)
    tn = _pick(math.gcd(N, n0) if n0 else N, (tn, 512, 256, 128))
    tk = _pick(K, (tk, 2048, 1024, 512))
    nk = K // tk
    j0 = n0 // tn
    x_mode = dict(pipeline_mode=pl.Buffered(1)) if (nk == 1 and N // tn > 1) else {}
    in_specs = [pl.BlockSpec((tm, tk), lambda i, j, k: (i, k), **x_mode),
                pl.BlockSpec((tk, tn), lambda i, j, k: (k, j + j0)) if layer is None else
                pl.BlockSpec((None, tk, tn), lambda i, j, k: (layer, k, j + j0))]
    args = [x, w]
    if bias is not None:
        in_specs.append(pl.BlockSpec((1, tn), lambda i, j, k: (0, j)))
        args.append(bias.reshape(1, N).astype(F32))
    if res is not None:
        in_specs.append(pl.BlockSpec((tm, tn), lambda i, j, k: (i, j)))
        args.append(res)
    kern = functools.partial(_mm_kernel, nk=nk, has_bias=bias is not None, has_res=res is not None, act=act)
    return pl.pallas_call(
        kern,
        grid=(M // tm, N // tn, nk),
        in_specs=in_specs,
        out_specs=pl.BlockSpec((tm, tn), lambda i, j, k: (i, j)),
        out_shape=jax.ShapeDtypeStruct((M, N), out_dtype),
        scratch_shapes=[pltpu.VMEM((tm, tn), F32)] if nk > 1 else [],
        compiler_params=_params("parallel", "parallel", "arbitrary"),
        name="matmul",
    )(*args)


def _headnorm_kernel(x_ref, g_ref, o_ref, *, hd):
    x = x_ref[...]
    ms = _group_sum_bcast(x * x, hd) * (1.0 / hd)
    o_ref[...] = (x * lax.rsqrt(ms + NORM_EPS) * g_ref[...]).astype(o_ref.dtype)


def headnorm(x, col_block, width, g, hd, out_dtype, tm=256):
    M = x.shape[0]
    tm = min(tm, M)
    g_row = jnp.tile(g.astype(F32), width // hd).reshape(1, width)
    return pl.pallas_call(
        functools.partial(_headnorm_kernel, hd=hd),
        grid=(M // tm,),
        in_specs=[pl.BlockSpec((tm, width), lambda i: (i, col_block)),
                  pl.BlockSpec((1, width), lambda i: (0, 0))],
        out_specs=pl.BlockSpec((tm, width), lambda i: (i, 0)),
        out_shape=jax.ShapeDtypeStruct((M, width), out_dtype),
        compiler_params=_params("parallel"),
        name="headnorm",
    )(x, g_row)


def _lru_kernel(xr_ref, halo_ref, yg_ref, h0_ref, cw_ref, cb_ref, gaw_ref, gab_ref, gxw_ref, gxb_ref,
                lam_ref, o_ref, hl_ref, a_scr, u_scr, h_scr, *, n_pad, n_blocks, bw, scan_w):
    t = pl.program_id(1)
    tt, W = a_scr.shape

    @pl.when(t == 0)
    def _():
        h_scr[...] = h0_ref[0]

    x = xr_ref[0]
    prev = halo_ref[0] * jnp.where(t > 0, 1.0, 0.0)
    xc = cb_ref[...] + cw_ref[3:4, :] * x
    for s in (1, 2, 3):
        xc = xc + cw_ref[3 - s:4 - s, :] * _shift_rows(x, prev, s)

    nsp = -LRU_C * jax.nn.softplus(-lam_ref[...])
    if n_pad:
        live = (lax.broadcasted_iota(jnp.int32, (tt, 1), 0) >= n_pad) | (t > 0)
    for n in range(n_blocks):
        sl = slice(n * bw, (n + 1) * bw)
        xb = xc[:, sl]
        xb16 = xb.astype(BF16)
        r = jax.nn.sigmoid(_dot(xb16, gaw_ref[n]) + gab_ref[:, sl])
        i = jax.nn.sigmoid(_dot(xb16, gxw_ref[n]) + gxb_ref[:, sl])
        log_a = r * nsp[:, sl]
        a = jnp.exp(log_a)
        u = jnp.sqrt(jnp.maximum(-jnp.tanh(log_a) * (a * a + 1.0), 0.0)) * (i * xb)
        if n_pad:
            a = jnp.where(live, a, 1.0)
            u = jnp.where(live, u, 0.0)
        a_scr[:, sl] = a
        u_scr[:, sl] = u

    row8 = lax.broadcasted_iota(jnp.int32, (SUBLANES, 1), 0)
    for c in range(W // scan_w):
        cs = slice(c * scan_w, (c + 1) * scan_w)

        def body(gi, h, cs=cs):
            r0 = pl.multiple_of(gi * SUBLANES, SUBLANES)
            A = a_scr[pl.ds(r0, SUBLANES), cs]
            U = u_scr[pl.ds(r0, SUBLANES), cs]
            for s in (1, 2, 4):
                As = pltpu.roll(A, s, 0)
                Us = pltpu.roll(U, s, 0)
                m = row8 >= s
                U = jnp.where(m, A * Us + U, U)
                A = jnp.where(m, A * As, A)
            H = A * h + U
            u_scr[pl.ds(r0, SUBLANES), cs] = H
            return H[SUBLANES - 1:SUBLANES, :]

        h_end = lax.fori_loop(0, tt // SUBLANES, body, h_scr[:, cs])
        h_scr[:, cs] = h_end

    o_ref[0] = (u_scr[...] * jax.nn.gelu(yg_ref[0], approximate=True)).astype(o_ref.dtype)
    hl_ref[0] = h_scr[...]


def lru_mixer(xr, xr_cb, yg, yg_cb, h0, conv_w, conv_b, ga_w, ga_b, gx_w, gx_b, lam, *, n_pad, tt):
    B, T, _ = xr.shape
    W = conv_w.shape[1]
    nb, bw = ga_w.shape[0], ga_w.shape[1]
    tt = min(tt, T)
    hb = tt // SUBLANES
    row = lambda v: v.reshape(1, W).astype(F32)
    kern = functools.partial(_lru_kernel, n_pad=n_pad, n_blocks=nb, bw=bw, scan_w=min(W, 512))
    return pl.pallas_call(
        kern,
        grid=(B, T // tt),
        in_specs=[
            pl.BlockSpec((1, tt, W), lambda b, t: (b, t, xr_cb)),
            pl.BlockSpec((1, SUBLANES, W), lambda b, t: (b, jnp.maximum(t * hb - 1, 0), xr_cb)),
            pl.BlockSpec((1, tt, W), lambda b, t: (b, t, yg_cb)),
            pl.BlockSpec((1, 1, W), lambda b, t: (b, 0, 0)),
            pl.BlockSpec((4, W), lambda b, t: (0, 0)),
            pl.BlockSpec((1, W), lambda b, t: (0, 0)),
            pl.BlockSpec((nb, bw, bw), lambda b, t: (0, 0, 0)),
            pl.BlockSpec((1, W), lambda b, t: (0, 0)),
            pl.BlockSpec((nb, bw, bw), lambda b, t: (0, 0, 0)),
            pl.BlockSpec((1, W), lambda b, t: (0, 0)),
            pl.BlockSpec((1, W), lambda b, t: (0, 0)),
        ],
        out_specs=[pl.BlockSpec((1, tt, W), lambda b, t: (b, t, 0)),
                   pl.BlockSpec((1, 1, W), lambda b, t: (b, 0, 0))],
        out_shape=[jax.ShapeDtypeStruct((B, T, W), BF16), jax.ShapeDtypeStruct((B, 1, W), F32)],
        scratch_shapes=[pltpu.VMEM((tt, W), F32), pltpu.VMEM((tt, W), F32), pltpu.VMEM((1, W), F32)],
        compiler_params=_params("parallel", "arbitrary"),
        name="lru_mixer",
    )(xr, xr, yg, h0, conv_w.astype(F32), row(conv_b), ga_w.astype(BF16), row(ga_b),
      gx_w.astype(BF16), row(gx_b), row(lam))


def _swa_kernel(sink_ref, q_ref, kp_ref, kc_ref, vp_ref, vc_ref, o_ref, *, j0, n_kv, group, hd, win):
    j = pl.program_id(1) + j0
    q = q_ref[0]
    k2 = jnp.concatenate([kp_ref[0], kc_ref[0]], axis=0)
    v2 = jnp.concatenate([vp_ref[0], vc_ref[0]], axis=0)
    tq = q.shape[0]
    qi = lax.broadcasted_iota(jnp.int32, (tq, 2 * win), 0)
    kj = lax.broadcasted_iota(jnp.int32, (tq, 2 * win), 1)
    dist = qi + win - kj
    mask = (dist >= 0) & (dist < win) & ((j > 0) | (kj >= win))
    scale = hd ** -0.5
    for kvh in range(n_kv):
        kh = k2[:, kvh * hd:(kvh + 1) * hd]
        vh = v2[:, kvh * hd:(kvh + 1) * hd]
        for g in range(0, group, 2):
            outs = []
            for gg in (g, g + 1):
                h = kvh * group + gg
                qh = q[:, h * hd:(h + 1) * hd]
                s = _nt(qh, kh) * scale
                s = jnp.where(mask, s, -jnp.inf)
                sk = sink_ref[h]
                m = jnp.maximum(jnp.max(s, axis=-1, keepdims=True), sk)
                p = jnp.exp(s - m)
                den = jnp.sum(p, axis=-1, keepdims=True) + jnp.exp(sk - m)
                outs.append(_dot(p.astype(BF16), vh) / den)
            h0 = kvh * group + g
            o_ref[0, :, h0 * hd:(h0 + 2) * hd] = jnp.concatenate(outs, axis=-1).astype(o_ref.dtype)


def swa_attention(q, k, v, sink, *, j0, win, hd, tq=None):
    B, Tq, QW = q.shape
    KW = k.shape[2]
    n_kv = KW // hd
    group = QW // KW
    tq = win if tq is None else tq
    nq = Tq // tq
    kern = functools.partial(_swa_kernel, j0=j0, n_kv=n_kv, group=group, hd=hd, win=win)
    prev = lambda b, j: (b, jnp.maximum(j + j0 - 1, 0), 0)
    cur = lambda b, j: (b, j + j0, 0)
    return pl.pallas_call(
        kern,
        grid=(B, nq),
        in_specs=[pl.BlockSpec(memory_space=pltpu.SMEM),
                  pl.BlockSpec((1, tq, QW), lambda b, j: (b, j, 0)),
                  pl.BlockSpec((1, win, KW), prev), pl.BlockSpec((1, win, KW), cur),
                  pl.BlockSpec((1, win, KW), prev), pl.BlockSpec((1, win, KW), cur)],
        out_specs=pl.BlockSpec((1, tq, QW), lambda b, j: (b, j, 0)),
        out_shape=jax.ShapeDtypeStruct((B, Tq, QW), BF16),
        compiler_params=_params("parallel", "parallel"),
        name="swa_attention",
    )(sink.astype(F32), q, k, k, v, v)


def _mem_attn_kernel(q_ref, g_ref, mk_ref, mv_ref, o_ref, *, n_heads, hd):
    q = q_ref[0]
    scale = hd ** -0.5
    for h in range(n_heads):
        sl = slice(h * hd, (h + 1) * hd)
        qh = q[:, sl]
        qn = (qh * lax.rsqrt(jnp.mean(qh * qh, axis=-1, keepdims=True) + NORM_EPS) * g_ref[...]).astype(BF16)
        s = _nt(qn, mk_ref[0, :, sl]) * scale
        m = jnp.max(s, axis=-1, keepdims=True)
        p = jnp.exp(s - m)
        p = p / jnp.sum(p, axis=-1, keepdims=True)
        o_ref[0, :, sl] = _dot(p.astype(BF16), mv_ref[0, :, sl]).astype(o_ref.dtype)


def mem_attention(q, qn_g, mk, mv, *, n_heads, tt=256):
    B, T, MW = q.shape
    ML = mk.shape[1]
    hd = MW // n_heads
    tt = min(tt, T)
    return pl.pallas_call(
        functools.partial(_mem_attn_kernel, n_heads=n_heads, hd=hd),
        grid=(B, T // tt),
        in_specs=[pl.BlockSpec((1, tt, MW), lambda b, t: (b, t, 0)),
                  pl.BlockSpec((1, hd), lambda b, t: (0, 0)),
                  pl.BlockSpec((1, ML, MW), lambda b, t: (b, 0, 0)),
                  pl.BlockSpec((1, ML, MW), lambda b, t: (b, 0, 0))],
        out_specs=pl.BlockSpec((1, tt, MW), lambda b, t: (b, t, 0)),
        out_shape=jax.ShapeDtypeStruct((B, T, MW), BF16),
        compiler_params=_params("parallel", "parallel"),
        name="mem_attention",
    )(q, qn_g.reshape(1, hd).astype(F32), mk, mv)


def _ffn_act_kernel(g_ref, halo_ref, up_ref, cw_ref, cb_ref, o_ref):
    t = pl.program_id(2)
    g = g_ref[0]
    prev = halo_ref[0] * jnp.where(t > 0, 1.0, 0.0)
    gc = cb_ref[...] + cw_ref[2:3, :] * g
    for s in (1, 2):
        gc = gc + cw_ref[2 - s:3 - s, :] * _shift_rows(g, prev, s)
    o_ref[0] = (jax.nn.gelu(gc, approximate=True) * up_ref[0]).astype(o_ref.dtype)


def ffn_act(gate, gate_cb0, up, up_cb0, conv_w, conv_b, *, tt=256, tc=512):
    B, T, _ = gate.shape
    C = conv_w.shape[1]
    tt = min(tt, T)
    hb = tt // SUBLANES
    return pl.pallas_call(
        _ffn_act_kernel,
        grid=(B, C // tc, T // tt),
        in_specs=[pl.BlockSpec((1, tt, tc), lambda b, c, t: (b, t, gate_cb0 + c)),
                  pl.BlockSpec((1, SUBLANES, tc), lambda b, c, t: (b, jnp.maximum(t * hb - 1, 0), gate_cb0 + c)),
                  pl.BlockSpec((1, tt, tc), lambda b, c, t: (b, t, up_cb0 + c)),
                  pl.BlockSpec((3, tc), lambda b, c, t: (0, c)),
                  pl.BlockSpec((1, tc), lambda b, c, t: (0, c))],
        out_specs=pl.BlockSpec((1, tt, tc), lambda b, c, t: (b, t, c)),
        out_shape=jax.ShapeDtypeStruct((B, T, C), BF16),
        compiler_params=_params("parallel", "parallel", "parallel"),
        name="ffn_act",
    )(gate, gate, up, conv_w.astype(F32), conv_b.reshape(1, C).astype(F32))


def _ffn_in_kernel(x_ref, wg_ref, wu_ref, cw_ref, cb_ref, o_ref, tail_ref, carry_scr, *, blocks_per_seq, sub):
    i = pl.program_id(0)
    j = pl.program_id(1)
    tm = x_ref.shape[0]

    @pl.when(i % blocks_per_seq == 0)
    def _():
        carry_scr[j] = jnp.zeros(carry_scr.shape[1:], F32)

    prev = carry_scr[j]
    wg = wg_ref[...].astype(BF16)
    wu = wu_ref[...].astype(BF16)
    for c in range(tm // sub):
        rows = slice(c * sub, (c + 1) * sub)
        x = x_ref[rows, :]
        gate = _dot(x, wg)
        up = _dot(x, wu)
        gc = cb_ref[...] + cw_ref[2:3, :] * gate
        for s in (1, 2):
            gc = gc + cw_ref[2 - s:3 - s, :] * _shift_rows(gate, prev, s)
        o_ref[rows, :] = (jax.nn.gelu(gc, approximate=True) * up).astype(o_ref.dtype)
        prev = gate[sub - SUBLANES:, :]
    carry_scr[j] = prev
    tail_ref[0] = prev


def ffn_in_fused(x, w_in, layer, conv_w, conv_b, *, seq_len, tm=2048, tn=256, sub=256):
    M, D = x.shape
    FF = conv_w.shape[1]
    tm = min(tm, seq_len)
    bps = seq_len // tm
    nj = FF // tn
    act, tails = pl.pallas_call(
        functools.partial(_ffn_in_kernel, blocks_per_seq=bps, sub=min(sub, tm)),
        grid=(M // tm, nj),
        in_specs=[pl.BlockSpec((tm, D), lambda i, j: (i, 0), pipeline_mode=pl.Buffered(1)),
                  pl.BlockSpec((None, D, tn), lambda i, j: (layer, 0, j)),
                  pl.BlockSpec((None, D, tn), lambda i, j: (layer, 0, j + nj)),
                  pl.BlockSpec((3, tn), lambda i, j: (0, j)),
                  pl.BlockSpec((1, tn), lambda i, j: (0, j))],
        out_specs=[pl.BlockSpec((tm, tn), lambda i, j: (i, j)),
                   pl.BlockSpec((1, SUBLANES, tn), lambda i, j: (i, 0, j))],
        out_shape=[jax.ShapeDtypeStruct((M, FF), BF16),
                   jax.ShapeDtypeStruct((M // tm, SUBLANES, FF), F32)],
        scratch_shapes=[pltpu.VMEM((nj, SUBLANES, tn), F32)],
        compiler_params=_params("arbitrary", "arbitrary"),
        name="ffn_in_fused",
    )(x, w_in, w_in, conv_w.astype(F32), conv_b.reshape(1, FF).astype(F32))
    return act, tails[bps - 1::bps]


def _rwkv_mix_kernel(x_ref, halo_ref, s0_ref, g_ref, mu_ref, *out_refs, n_valid_last):
    t = pl.program_id(1)
    nt = pl.num_programs(1)
    o_refs, hl_ref = out_refs[:6], out_refs[6]

    def norm(v):
        return v * lax.rsqrt(jnp.mean(v * v, axis=-1, keepdims=True) + NORM_EPS) * g_ref[...]

    h = norm(x_ref[0])
    hp = norm(halo_ref[0])
    first = jnp.where(t > 0, 1.0, 0.0)
    hp = hp * first + jnp.broadcast_to(s0_ref[0], hp.shape) * (1.0 - first)
    xx = _shift_rows(h, hp, 1) - h
    for j in range(6):
        o_refs[j][0] = (h + xx * mu_ref[j:j + 1, :]).astype(o_refs[j].dtype)

    @pl.when(t == nt - 1)
    def _():
        hl_ref[0] = h[n_valid_last - 1:n_valid_last, :]


def rwkv_mix(x, shift0, g, mu, *, n_valid, tt=256):
    B, T, D = x.shape
    tt = min(tt, T)
    hb = tt // SUBLANES
    n_valid_last = n_valid - (T // tt - 1) * tt
    blk = pl.BlockSpec((1, tt, D), lambda b, t: (b, t, 0))
    one = pl.BlockSpec((1, 1, D), lambda b, t: (b, 0, 0))
    return pl.pallas_call(
        functools.partial(_rwkv_mix_kernel, n_valid_last=n_valid_last),
        grid=(B, T // tt),
        in_specs=[blk,
                  pl.BlockSpec((1, SUBLANES, D), lambda b, t: (b, jnp.maximum(t * hb - 1, 0), 0)),
                  one,
                  pl.BlockSpec((1, D), lambda b, t: (0, 0)),
                  pl.BlockSpec((6, D), lambda b, t: (0, 0))],
        out_specs=[blk] * 6 + [one],
        out_shape=[jax.ShapeDtypeStruct((B, T, D), BF16)] * 6 + [jax.ShapeDtypeStruct((B, 1, D), F32)],
        compiler_params=_params("parallel", "arbitrary"),
        name="rwkv_mix",
    )(x, x, shift0, g.reshape(1, D).astype(F32), mu.astype(F32))


def _rwkv_chunk_kernel(r_ref, k_ref, v_ref, wp_ref, ap_ref, g_ref, kk_ref, ka_ref, rk_ref, lng_ref, lnb_ref,
                       s0_ref, o_ref, so_ref, s_scr, *, hs):
    c = pl.program_id(2)
    nc = pl.num_programs(2)
    _, C, L = r_ref.shape
    n_pairs = L // LANES

    @pl.when(c == 0)
    def _():
        s_scr[...] = s0_ref[0]

    r = r_ref[0]
    k = k_ref[0]
    v = v_ref[0]
    logw = -jnp.exp(-jax.nn.softplus(-wp_ref[0]) - 0.5)
    a = jax.nn.sigmoid(ap_ref[0])
    kk = k * kk_ref[...]
    kk = kk / jnp.maximum(jnp.sqrt(_group_sum_bcast(kk * kk, hs)), 1e-12)
    k2 = k * (1.0 + (a - 1.0) * ka_ref[...])
    bm = kk * a

    ti = lax.broadcasted_iota(jnp.int32, (C, C), 0)
    si = lax.broadcasted_iota(jnp.int32, (C, C), 1)
    tri = jnp.where(ti >= si, 1.0, 0.0).astype(BF16)
    cum = _dot_exact_rhs_lhs(tri, logw)
    e_in = jnp.exp(cum)
    e_out = jnp.exp(-cum)
    e_end = jnp.exp(cum[C - 1:C, :] - cum)
    g_end = jnp.exp(cum[C - 1:C, :])
    rt = r * e_in
    kkt = kk * jnp.exp(cum - logw)
    bh = bm * e_out
    kh = k2 * e_out
    bbar = bm * e_end
    kbar = k2 * e_end

    lane = lax.broadcasted_iota(jnp.int32, (1, LANES), 1)
    m0 = lane < hs
    C2 = 2 * C
    ri = lax.broadcasted_iota(jnp.int32, (C2, C2), 0)
    ci = lax.broadcasted_iota(jnp.int32, (C2, C2), 1)
    same = (ri // C) == (ci // C)
    strict = same & ((ri % C) > (ci % C))
    lower = (ri % C) >= (ci % C)
    vi = lax.broadcasted_iota(jnp.int32, (LANES, LANES), 0) // hs
    vj = lax.broadcasted_iota(jnp.int32, (LANES, LANES), 1) // hs
    blockdiag = vi == vj

    def stack_heads(x):
        return jnp.concatenate([jnp.where(m0, x, 0.0), jnp.where(m0, 0.0, x)], axis=0).astype(BF16)

    n_steps = int(math.log2(C))
    P = range(n_pairs)
    sls = [slice(p * LANES, (p + 1) * LANES) for p in P]
    S = [s_scr[p] for p in P]
    S16 = [s.astype(BF16) for s in S]
    V16 = [v[:, sl].astype(BF16) for sl in sls]
    lhk = [stack_heads(kkt[:, sl]) for sl in sls]
    wide = C2 % LANES == 0
    if wide:
        gbk = [_nt(lhk[p], jnp.concatenate([stack_heads(bh[:, sls[p]]), stack_heads(kh[:, sls[p]])], axis=0))
               for p in P]
        nmat = [jnp.where(strict, -g[:, :C2], 0.0) for g in gbk]
        auk = [jnp.where(strict, -g[:, C2:], 0.0).astype(BF16) for g in gbk]
    else:
        nmat = [jnp.where(strict, -_nt(lhk[p], stack_heads(bh[:, sls[p]])), 0.0) for p in P]
        auk = [jnp.where(strict, -_nt(lhk[p], stack_heads(kh[:, sls[p]])), 0.0).astype(BF16) for p in P]
    sprod = [_nt(jnp.concatenate([lhk[p], rt[:, sls[p]].astype(BF16)], axis=0), S16[p]) for p in P]
    u = [_dot(auk[p], jnp.concatenate([V16[p], V16[p]], axis=0)) - sprod[p][:C2] for p in P]
    npow = [n.astype(BF16) for n in nmat]
    for it in range(n_steps):
        last = it + 1 == n_steps
        if wide and not last:
            res = [_dot(npow[p], jnp.concatenate([u[p].astype(BF16), npow[p]], axis=1)) for p in P]
            u = [u[p] + res[p][:, :LANES] for p in P]
            npow = [res[p][:, LANES:].astype(BF16) for p in P]
        else:
            u = [u[p] + _dot(npow[p], u[p].astype(BF16)) for p in P]
            if not last:
                npow = [_dot(npow[p], npow[p]).astype(BF16) for p in P]
    uv = [jnp.concatenate([jnp.where(m0, u[p][:C], u[p][C:]).astype(BF16), V16[p]], axis=0) for p in P]
    ar = [jnp.where(lower, _nt(stack_heads(rt[:, sls[p]]),
                               jnp.concatenate([bh[:, sls[p]], kh[:, sls[p]]], axis=0).astype(BF16)), 0.0).astype(BF16)
          for p in P]
    tmat = [_dot(ar[p], uv[p]) for p in P]
    ys = [sprod[p][C2:] + jnp.where(m0, tmat[p][:C], tmat[p][C:]) for p in P]
    upd = [_tn(uv[p], jnp.concatenate([bbar[:, sls[p]], kbar[:, sls[p]]], axis=0).astype(BF16)) for p in P]
    for p in P:
        s_scr[p] = S[p] * g_end[:, sls[p]] + jnp.where(blockdiag, upd[p], 0.0)

    y = ys[0] if n_pairs == 1 else jnp.concatenate(ys, axis=-1)
    inv = 1.0 / hs
    mean = _group_sum_bcast(y, hs) * inv
    yc = y - mean
    var = _group_sum_bcast(yc * yc, hs) * inv
    yn = yc * lax.rsqrt(var + RW_GN_EPS) * lng_ref[...] + lnb_ref[...]
    bonus = _group_sum_bcast(r * k2 * rk_ref[...], hs) * v
    o_ref[0] = ((yn + bonus) * g_ref[0]).astype(o_ref.dtype)

    @pl.when(c == nc - 1)
    def _():
        so_ref[0] = s_scr[...]


def _dot_exact_rhs_lhs(m_bf16, x):
    hi, mid, lo = _split3(x)
    return _dot(m_bf16, hi) + _dot(m_bf16, mid) + _dot(m_bf16, lo)


def rwkv_chunked(r, k, v, wp, ap, g, k_k, k_a, r_k, ln_g, ln_b, s0, *, hs, chunk, lanes_per_step=1024):
    B, T, D = r.shape
    L = min(lanes_per_step, D)
    npg = L // LANES
    seq = pl.BlockSpec((1, chunk, L), lambda b, hg, c: (b, c, hg))
    par = pl.BlockSpec((1, L), lambda b, hg, c: (0, hg))
    st = pl.BlockSpec((1, npg, LANES, LANES), lambda b, hg, c: (b, hg, 0, 0))
    row = lambda x: x.reshape(1, D).astype(F32)
    return pl.pallas_call(
        functools.partial(_rwkv_chunk_kernel, hs=hs),
        grid=(B, D // L, T // chunk),
        in_specs=[seq] * 6 + [par] * 5 + [st],
        out_specs=[seq, st],
        out_shape=[jax.ShapeDtypeStruct((B, T, D), BF16),
                   jax.ShapeDtypeStruct(s0.shape, F32)],
        scratch_shapes=[pltpu.VMEM((npg, LANES, LANES), F32)],
        compiler_params=_params("parallel", "parallel", "arbitrary"),
        name="rwkv_chunked",
    )(r, k, v, wp, ap, g, row(k_k), row(k_a), row(r_k), row(ln_g), row(ln_b), s0)


def _pairs_from_heads(s, hs):
    B, H = s.shape[:2]
    s = s.reshape(B, H // 2, 2, hs, hs)
    z = jnp.zeros_like(s[:, :, 0])
    top = jnp.concatenate([s[:, :, 0], z], axis=-1)
    bot = jnp.concatenate([z, s[:, :, 1]], axis=-1)
    return jnp.concatenate([top, bot], axis=-2)


def _heads_from_pairs(sp, hs):
    B, P = sp.shape[:2]
    return jnp.stack([sp[:, :, :hs, :hs], sp[:, :, hs:, hs:]], axis=2).reshape(B, 2 * P, hs, hs)


def _pad_rows(x, front, back):
    return jnp.pad(x, ((0, 0), (front, back), (0, 0)))


def kernel(x_prompt, x_sample, mem_prompt, state_lru_conv, state_lru_h, cache_swa_k, cache_swa_v,
           state_rwkv_shift, state_rwkv_wkv, cache_mem_k, cache_mem_v, state_ffn_conv,
           a_norm_g, a_w_in, a_conv_w, a_conv_b, a_gate_a_w, a_gate_a_b, a_gate_x_w, a_gate_x_b,
           a_lambda, b_q_norm_g, b_k_norm_g, b_sink, a_w_out,
           c_norm_g, c_mu, c_w_r, c_w_k, c_w_v, c_w_o, c_w0, c_w1, c_w2, c_a0, c_a1, c_a2,
           c_g1, c_g2, c_k_k, c_k_a, c_r_k, c_ln_g, c_ln_b,
           m_norm_g, m_mem_norm_g, m_w_q, m_w_kv, m_q_norm_g, m_k_norm_g, m_w_o,
           f_norm_g, f_w_in, f_conv_w, f_conv_b, f_w_out):
    D = x_prompt.shape[-1]
    depth = m_norm_g.shape[0]
    W = a_conv_w.shape[-1]
    KA = a_conv_w.shape[1]
    hd = b_q_norm_g.shape[-1]
    n_q = b_sink.shape[-1]
    n_kv = cache_swa_k.shape[3]
    win = cache_swa_k.shape[2]
    QW, KW = n_q * hd, n_kv * hd
    hs = c_r_k.shape[-1]
    n_rw = c_r_k.shape[1]
    m_heads, m_hd = cache_mem_k.shape[3], cache_mem_k.shape[4]
    MW = m_heads * m_hd
    FF = f_conv_w.shape[-1]
    KF = f_conv_w.shape[1]
    bf = lambda w: w.astype(BF16)

    def run(x, T_real, lru_conv, lru_h, swa_k, swa_v, rw_shift, rw_wkv, mem_k, mem_v, ffn_conv, prompt):
        B = x.shape[0]
        T = T_real
        M = B * T
        xf = x.reshape(M, D)
        outs = {}
        ia = ic = 0
        for l in range(depth):
            if l % 2 == 0:
                i = ia
                ia += 1
                h = rmsnorm_rows(xf, a_norm_g[i])
                zz = matmul(h, a_w_in, layer=i, n=2 * W)
                qkv = matmul(h, a_w_in, layer=i, n0=2 * W)
                qn = headnorm(qkv, 0, QW, b_q_norm_g[i], hd, BF16)
                kn = headnorm(qkv, QW // KW, KW, b_k_norm_g[i], hd, F32)
                vv = qkv[:, QW + KW:]
                zz3 = zz.reshape(B, T, 2 * W)
                if prompt:
                    out_a, h_last = lru_mixer(zz3, 0, zz3, 1, jnp.zeros((B, 1, W), F32),
                                              a_conv_w[i], a_conv_b[i], a_gate_a_w[i], a_gate_a_b[i],
                                              a_gate_x_w[i], a_gate_x_b[i], a_lambda[i], n_pad=0, tt=256)
                    conv_new = zz3[:, T - (KA - 1):, :W]
                    o = swa_attention(qn.reshape(B, T, QW), bf(kn).reshape(B, T, KW),
                                      bf(vv).reshape(B, T, KW), b_sink[i], j0=0, win=win, hd=hd)
                    k_new = kn.reshape(B, T, n_kv, hd)[:, T - win:]
                    v_new = vv.reshape(B, T, n_kv, hd)[:, T - win:]
                else:
                    n_pad = SUBLANES - T
                    xr_hist = jnp.concatenate([lru_conv[i].astype(F32), zz3[:, :, :W]], axis=1)
                    xr_p = _pad_rows(xr_hist, SUBLANES - xr_hist.shape[1], 0)
                    yg_p = _pad_rows(zz3[:, :, W:], n_pad, 0)
                    out_a, h_last = lru_mixer(xr_p, 0, yg_p, 0, lru_h[i].reshape(B, 1, W).astype(F32),
                                              a_conv_w[i], a_conv_b[i], a_gate_a_w[i], a_gate_a_b[i],
                                              a_gate_x_w[i], a_gate_x_b[i], a_lambda[i], n_pad=n_pad, tt=SUBLANES)
                    out_a = out_a[:, n_pad:]
                    conv_new = xr_hist[:, T:]
                    kc = swa_k[i].reshape(B, win, KW)
                    vc = swa_v[i].reshape(B, win, KW)
                    k_all = jnp.concatenate([kc, kn.reshape(B, T, KW)], axis=1)
                    v_all = jnp.concatenate([vc, vv.reshape(B, T, KW)], axis=1)
                    o = swa_attention(_pad_rows(qn.reshape(B, T, QW), 0, 2 * SUBLANES - T),
                                      bf(_pad_rows(k_all, 0, win - T)), bf(_pad_rows(v_all, 0, win - T)),
                                      b_sink[i], j0=1, win=win, hd=hd, tq=2 * SUBLANES)[:, :T]
                    k_new = k_all[:, T:].reshape(B, win, n_kv, hd)
                    v_new = v_all[:, T:].reshape(B, win, n_kv, hd)
                mix = jnp.concatenate([out_a.reshape(M, W), o.reshape(M, QW)], axis=-1)
                xf = matmul(mix, a_w_out, layer=i, res=xf)
                outs.setdefault("lc", []).append(conv_new)
                outs.setdefault("lh", []).append(h_last.reshape(B, W))
                outs.setdefault("sk", []).append(k_new)
                outs.setdefault("sv", []).append(v_new)
            else:
                i = ic
                ic += 1
                Tp = T if prompt else SUBLANES
                chunk = 64 if prompt else 16
                x3 = xf.reshape(B, T, D)
                if not prompt:
                    x3 = _pad_rows(x3, 0, Tp - T)
                mixes = rwkv_mix(x3, rw_shift[i].reshape(B, 1, D).astype(F32), c_norm_g[i], c_mu[i], n_valid=T)
                xr, xw, xk, xv, xa, xg = [m.reshape(B * Tp, D) for m in mixes[:6]]
                sh_new = mixes[6].reshape(B, D)
                r = matmul(xr, c_w_r, layer=i)
                k = matmul(xk, c_w_k, layer=i)
                v = matmul(xv, c_w_v, layer=i)
                wp = matmul(matmul(xw, c_w1, layer=i, act="tanh", out_dtype=BF16), c_w2, layer=i, bias=c_w0[i])
                ap = matmul(matmul(xa, c_a1, layer=i, out_dtype=BF16), c_a2, layer=i, bias=c_a0[i])
                gl = c_g1[i].shape[1]
                glp = -(-gl // LANES) * LANES
                g1p = jnp.pad(c_g1[i], ((0, 0), (0, glp - gl)))
                g2p = jnp.pad(c_g2[i], ((0, glp - gl), (0, 0)))
                g = matmul(matmul(xg, g1p, act="sigmoid", out_dtype=BF16), g2p)
                if not prompt:
                    sq = lambda t, fill=0.0: jnp.pad(t.reshape(B, Tp, D)[:, :T], ((0, 0), (0, chunk - T), (0, 0)),
                                                     constant_values=fill)
                    r, k, v, ap, g = sq(r), sq(k), sq(v), sq(ap), sq(g)
                    wp = sq(wp, -1e30)
                else:
                    sq = lambda t: t.reshape(B, Tp, D)
                    r, k, v, wp, ap, g = sq(r), sq(k), sq(v), sq(wp), sq(ap), sq(g)
                s0 = _pairs_from_heads(rw_wkv[i].astype(F32), hs)
                yo, s_end = rwkv_chunked(r, k, v, wp, ap, g, c_k_k[i], c_k_a[i], c_r_k[i], c_ln_g[i], c_ln_b[i],
                                         s0, hs=hs, chunk=chunk)
                yo = yo[:, :T].reshape(M, D)
                xf = matmul(yo, c_w_o, layer=i, res=xf)
                outs.setdefault("rs", []).append(sh_new)
                outs.setdefault("rw", []).append(_heads_from_pairs(s_end, hs))
            hm = rmsnorm_rows(xf, m_norm_g[l])
            q = matmul(hm, m_w_q, layer=l)
            Tq = T if prompt else SUBLANES
            q3 = q.reshape(B, T, MW)
            if not prompt:
                q3 = _pad_rows(q3, 0, Tq - T)
            ML = mem_k[l].shape[1]
            om = mem_attention(q3, m_q_norm_g[l], bf(mem_k[l]).reshape(B, ML, MW), bf(mem_v[l]).reshape(B, ML, MW),
                               n_heads=m_heads)
            om = om[:, :T].reshape(M, MW)
            xf = matmul(om, m_w_o, layer=l, res=xf)
            hf = rmsnorm_rows(xf, f_norm_g[l])
            if prompt:
                act, tail = ffn_in_fused(hf, f_w_in, l, f_conv_w[l], f_conv_b[l], seq_len=T)
                f_new = tail[:, SUBLANES - (KF - 1):, :]
            else:
                gu3 = matmul(hf, f_w_in, layer=l, tn=1024).reshape(B, T, 2 * FF)
                g_hist = jnp.concatenate([ffn_conv[l].astype(F32), gu3[:, :, :FF]], axis=1)
                g_p = _pad_rows(g_hist, SUBLANES - g_hist.shape[1], 0)
                up_p = _pad_rows(gu3[:, :, FF:], SUBLANES - T, 0)
                act = ffn_act(g_p, 0, up_p, 0, f_conv_w[l], f_conv_b[l], tt=SUBLANES, tc=FF)[:, SUBLANES - T:]
                f_new = g_hist[:, T:]
            xf = matmul(act.reshape(M, FF), bf(f_w_out[l]), res=xf)
            outs.setdefault("fc", []).append(f_new)
        st = lambda name: jnp.stack(outs[name])
        return (xf.reshape(B, T, D), st("lc"), st("lh"), st("sk"), st("sv"), st("rs"), st("rw"), st("fc"))

    Bp, S = x_prompt.shape[:2]
    ML = mem_prompt.shape[1]
    mem_flat = mem_prompt.reshape(Bp * ML, D)
    mks, mvs = [], []
    for l in range(depth):
        mn = rmsnorm_rows(mem_flat, m_mem_norm_g[l])
        kv = matmul(mn, m_w_kv, layer=l)
        mk = headnorm(kv, 0, MW, m_k_norm_g[l], m_hd, F32)
        mks.append(mk.reshape(Bp, ML, m_heads, m_hd))
        mvs.append(kv[:, MW:].reshape(Bp, ML, m_heads, m_hd))
    p_mem_k = jnp.stack(mks)
    p_mem_v = jnp.stack(mvs)

    n_a = a_norm_g.shape[0]
    n_c = c_norm_g.shape[0]
    zeros = lambda *s: jnp.zeros(s, F32)
    y_p, p_lc, p_lh, p_sk, p_sv, p_rs, p_rw, p_fc = run(
        x_prompt, S, None, None, None, None, zeros(n_c, Bp, D), zeros(n_c, Bp, n_rw, hs, hs),
        p_mem_k, p_mem_v, None, True)
    Bs, Ts = x_sample.shape[:2]
    y_s, s_lc, s_lh, s_sk, s_sv, s_rs, s_rw, s_fc = run(
        x_sample, Ts, state_lru_conv, state_lru_h, cache_swa_k, cache_swa_v,
        state_rwkv_shift, state_rwkv_wkv, cache_mem_k, cache_mem_v, state_ffn_conv, False)
    return (y_p, y_s, p_lc, p_lh, p_sk, p_sv, p_rs, p_rw, p_mem_k, p_mem_v, p_fc,
            s_lc, s_lh, s_sk, s_sv, s_rs, s_rw, s_fc)
```

```python
import functools
import math

import jax
import jax.numpy as jnp
from jax import lax
from jax.experimental import pallas as pl
from jax.experimental.pallas import tpu as pltpu

F32 = jnp.float32
BF16 = jnp.bfloat16

NORM_EPS = 1e-6
RW_GN_EPS = 64e-5
LRU_C = 8.0
LANES = 128
SUBLANES = 8
VMEM_LIMIT_BYTES = 56 * 1024 * 1024


def _params(*sem):
    return pltpu.CompilerParams(dimension_semantics=sem, vmem_limit_bytes=VMEM_LIMIT_BYTES)


def _nt(a, b):
    return lax.dot_general(a, b, (((1,), (1,)), ((), ())), preferred_element_type=F32)


def _tn(a, b):
    return lax.dot_general(a, b, (((0,), (0,)), ((), ())), preferred_element_type=F32)


def _dot(a, b):
    return jnp.dot(a, b, preferred_element_type=F32)


def _split3(x):
    hi = x.astype(BF16)
    r1 = x - hi.astype(F32)
    mid = r1.astype(BF16)
    lo = (r1 - mid.astype(F32)).astype(BF16)
    return hi, mid, lo


def _group_sum_bcast(x, width):
    m, L = x.shape
    n = L // LANES
    if width == LANES:
        parts = []
        for c in range(n):
            s = jnp.sum(x[:, c * LANES:(c + 1) * LANES], axis=-1, keepdims=True)
            parts.append(jnp.broadcast_to(s, (m, LANES)))
        return parts[0] if n == 1 else jnp.concatenate(parts, axis=-1)
    li = lax.broadcasted_iota(jnp.int32, (LANES, LANES), 0) // width
    lj = lax.broadcasted_iota(jnp.int32, (LANES, LANES), 1) // width
    e = jnp.where(li == lj, 1.0, 0.0).astype(BF16)
    xs = x if n == 1 else jnp.concatenate([x[:, c * LANES:(c + 1) * LANES] for c in range(n)], axis=0)
    hi = xs.astype(BF16)
    lo = (xs - hi.astype(F32)).astype(BF16)
    out = _dot(hi, e) + _dot(lo, e)
    return out if n == 1 else jnp.concatenate([out[c * m:(c + 1) * m] for c in range(n)], axis=-1)


def _shift_rows(x, prev8, s):
    rolled = pltpu.roll(x, s, 0)
    top = jnp.where(lax.broadcasted_iota(jnp.int32, (SUBLANES, 1), 0) < s,
                    pltpu.roll(prev8, s, 0), rolled[0:SUBLANES])
    if x.shape[0] == SUBLANES:
        return top
    return jnp.concatenate([top, rolled[SUBLANES:]], axis=0)


def _rmsnorm_kernel(x_ref, g_ref, o_ref):
    x = x_ref[...]
    ms = jnp.mean(x * x, axis=-1, keepdims=True)
    o_ref[...] = (x * lax.rsqrt(ms + NORM_EPS) * g_ref[...]).astype(o_ref.dtype)


def rmsnorm_rows(x, g, out_dtype=BF16, tm=256):
    M, D = x.shape
    tm = min(tm, M)
    return pl.pallas_call(
        _rmsnorm_kernel,
        grid=(M // tm,),
        in_specs=[pl.BlockSpec((tm, D), lambda i: (i, 0)), pl.BlockSpec((1, D), lambda i: (0, 0))],
        out_specs=pl.BlockSpec((tm, D), lambda i: (i, 0)),
        out_shape=jax.ShapeDtypeStruct((M, D), out_dtype),
        compiler_params=_params("parallel"),
        name="rmsnorm_rows",
    )(x, g.reshape(1, D))


def _mm_kernel(*refs, nk, has_bias, has_res, act):
    x_ref, w_ref = refs[0], refs[1]
    pos = 2
    b_ref = r_ref = None
    if has_bias:
        b_ref = refs[pos]
        pos += 1
    if has_res:
        r_ref = refs[pos]
        pos += 1
    o_ref = refs[pos]
    acc_ref = refs[pos + 1] if len(refs) > pos + 1 else None

    def epilogue(y):
        if has_bias:
            y = y + b_ref[...]
        if act == "tanh":
            y = jnp.tanh(y)
        elif act == "sigmoid":
            y = jax.nn.sigmoid(y)
        if has_res:
            y = y + r_ref[...]
        o_ref[...] = y.astype(o_ref.dtype)

    if nk > 1 and acc_ref is None:
        k = pl.program_id(2)
        tn = o_ref.shape[1]
        cw = min(tn, 2 * LANES)

        def sweep(first):
            x = x_ref[...]
            for c in range(tn // cw):
                cs = slice(c * cw, (c + 1) * cw)
                part = _dot(x, w_ref[:, cs].astype(BF16))
                if not first:
                    o_ref[:, cs] += part
                elif has_res:
                    o_ref[:, cs] = part + r_ref[:, cs]
                else:
                    o_ref[:, cs] = part

        pl.when(k == 0)(functools.partial(sweep, True))
        pl.when(k > 0)(functools.partial(sweep, False))
        return

    part = _dot(x_ref[...], w_ref[...].astype(BF16))
    if nk == 1:
        epilogue(part)
        return
    k = pl.program_id(2)

    @pl.when(k == 0)
    def _():
        acc_ref[...] = part

    @pl.when(k > 0)
    def _():
        acc_ref[...] += part

    @pl.when(k == nk - 1)
    def _():
        epilogue(acc_ref[...])


def _pick(n, prefs):
    for p in prefs:
        if n % p == 0:
            return p
    return n


def matmul(x, w, *, layer=None, n0=0, n=None, bias=None, res=None, act=None, out_dtype=F32,
           tm=1024, tn=512, tk=4096):
    M, K = x.shape
    N = w.shape[-1] - n0 if n is None else n
    tm = min(tm, M)
    tn = _pick(math.gcd(N, n0) if n0 else N, (tn, 512, 256, 128))
    tk = _pick(K, (tk, 2048, 1024, 512))
    nk = K // tk
    j0 = n0 // tn
    x_mode = dict(pipeline_mode=pl.Buffered(1)) if (nk == 1 and N // tn > 1) else {}
    in_specs = [pl.BlockSpec((tm, tk), lambda i, j, k: (i, k), **x_mode),
                pl.BlockSpec((tk, tn), lambda i, j, k: (k, j + j0)) if layer is None else
                pl.BlockSpec((None, tk, tn), lambda i, j, k: (layer, k, j + j0))]
    args = [x, w]
    if bias is not None:
        in_specs.append(pl.BlockSpec((1, tn), lambda i, j, k: (0, j)))
        args.append(bias.reshape(1, N).astype(F32))
    if res is not None:
        in_specs.append(pl.BlockSpec((tm, tn), lambda i, j, k: (i, j)))
        args.append(res)
    in_place = bias is None and act is None and out_dtype == F32
    kern = functools.partial(_mm_kernel, nk=nk, has_bias=bias is not None, has_res=res is not None, act=act)
    return pl.pallas_call(
        kern,
        grid=(M // tm, N // tn, nk),
        in_specs=in_specs,
        out_specs=pl.BlockSpec((tm, tn), lambda i, j, k: (i, j)),
        out_shape=jax.ShapeDtypeStruct((M, N), out_dtype),
        scratch_shapes=[pltpu.VMEM((tm, tn), F32)] if (nk > 1 and not in_place) else [],
        compiler_params=_params("parallel", "parallel", "arbitrary"),
        name="matmul",
    )(*args)


def _headnorm_kernel(x_ref, g_ref, o_ref, *, hd):
    x = x_ref[...]
    ms = _group_sum_bcast(x * x, hd) * (1.0 / hd)
    o_ref[...] = (x * lax.rsqrt(ms + NORM_EPS) * g_ref[...]).astype(o_ref.dtype)


def headnorm(x, col_block, width, g, hd, out_dtype, tm=256):
    M = x.shape[0]
    tm = min(tm, M)
    g_row = jnp.tile(g.astype(F32), width // hd).reshape(1, width)
    return pl.pallas_call(
        functools.partial(_headnorm_kernel, hd=hd),
        grid=(M // tm,),
        in_specs=[pl.BlockSpec((tm, width), lambda i: (i, col_block)),
                  pl.BlockSpec((1, width), lambda i: (0, 0))],
        out_specs=pl.BlockSpec((tm, width), lambda i: (i, 0)),
        out_shape=jax.ShapeDtypeStruct((M, width), out_dtype),
        compiler_params=_params("parallel"),
        name="headnorm",
    )(x, g_row)


def _lru_kernel(xr_ref, halo_ref, yg_ref, h0_ref, cw_ref, cb_ref, gaw_ref, gab_ref, gxw_ref, gxb_ref,
                lam_ref, o_ref, hl_ref, a_scr, u_scr, h_scr, *, n_pad, n_blocks, bw, scan_w):
    t = pl.program_id(1)
    tt, W = a_scr.shape

    @pl.when(t == 0)
    def _():
        h_scr[...] = h0_ref[0]

    x = xr_ref[0]
    prev = halo_ref[0] * jnp.where(t > 0, 1.0, 0.0)
    xc = cb_ref[...] + cw_ref[3:4, :] * x
    for s in (1, 2, 3):
        xc = xc + cw_ref[3 - s:4 - s, :] * _shift_rows(x, prev, s)

    nsp = -LRU_C * jax.nn.softplus(-lam_ref[...])
    if n_pad:
        live = (lax.broadcasted_iota(jnp.int32, (tt, 1), 0) >= n_pad) | (t > 0)
    for n in range(n_blocks):
        sl = slice(n * bw, (n + 1) * bw)
        xb = xc[:, sl]
        xb16 = xb.astype(BF16)
        r = jax.nn.sigmoid(_dot(xb16, gaw_ref[n]) + gab_ref[:, sl])
        i = jax.nn.sigmoid(_dot(xb16, gxw_ref[n]) + gxb_ref[:, sl])
        log_a = r * nsp[:, sl]
        a = jnp.exp(log_a)
        u = jnp.sqrt(jnp.maximum(-jnp.tanh(log_a) * (a * a + 1.0), 0.0)) * (i * xb)
        if n_pad:
            a = jnp.where(live, a, 1.0)
            u = jnp.where(live, u, 0.0)
        a_scr[:, sl] = a
        u_scr[:, sl] = u

    row8 = lax.broadcasted_iota(jnp.int32, (SUBLANES, 1), 0)
    for c in range(W // scan_w):
        cs = slice(c * scan_w, (c + 1) * scan_w)

        def body(gi, h, cs=cs):
            r0 = pl.multiple_of(gi * SUBLANES, SUBLANES)
            A = a_scr[pl.ds(r0, SUBLANES), cs]
            U = u_scr[pl.ds(r0, SUBLANES), cs]
            for s in (1, 2, 4):
                As = pltpu.roll(A, s, 0)
                Us = pltpu.roll(U, s, 0)
                m = row8 >= s
                U = jnp.where(m, A * Us + U, U)
                A = jnp.where(m, A * As, A)
            H = A * h + U
            u_scr[pl.ds(r0, SUBLANES), cs] = H
            return H[SUBLANES - 1:SUBLANES, :]

        h_end = lax.fori_loop(0, tt // SUBLANES, body, h_scr[:, cs])
        h_scr[:, cs] = h_end

    o_ref[0] = (u_scr[...] * jax.nn.gelu(yg_ref[0], approximate=True)).astype(o_ref.dtype)
    hl_ref[0] = h_scr[...]


def lru_mixer(xr, xr_cb, yg, yg_cb, h0, conv_w, conv_b, ga_w, ga_b, gx_w, gx_b, lam, *, n_pad, tt):
    B, T, _ = xr.shape
    W = conv_w.shape[1]
    nb, bw = ga_w.shape[0], ga_w.shape[1]
    tt = min(tt, T)
    hb = tt // SUBLANES
    row = lambda v: v.reshape(1, W).astype(F32)
    kern = functools.partial(_lru_kernel, n_pad=n_pad, n_blocks=nb, bw=bw, scan_w=min(W, 512))
    return pl.pallas_call(
        kern,
        grid=(B, T // tt),
        in_specs=[
            pl.BlockSpec((1, tt, W), lambda b, t: (b, t, xr_cb)),
            pl.BlockSpec((1, SUBLANES, W), lambda b, t: (b, jnp.maximum(t * hb - 1, 0), xr_cb)),
            pl.BlockSpec((1, tt, W), lambda b, t: (b, t, yg_cb)),
            pl.BlockSpec((1, 1, W), lambda b, t: (b, 0, 0)),
            pl.BlockSpec((4, W), lambda b, t: (0, 0)),
            pl.BlockSpec((1, W), lambda b, t: (0, 0)),
            pl.BlockSpec((nb, bw, bw), lambda b, t: (0, 0, 0)),
            pl.BlockSpec((1, W), lambda b, t: (0, 0)),
            pl.BlockSpec((nb, bw, bw), lambda b, t: (0, 0, 0)),
            pl.BlockSpec((1, W), lambda b, t: (0, 0)),
            pl.BlockSpec((1, W), lambda b, t: (0, 0)),
        ],
        out_specs=[pl.BlockSpec((1, tt, W), lambda b, t: (b, t, 0)),
                   pl.BlockSpec((1, 1, W), lambda b, t: (b, 0, 0))],
        out_shape=[jax.ShapeDtypeStruct((B, T, W), BF16), jax.ShapeDtypeStruct((B, 1, W), F32)],
        scratch_shapes=[pltpu.VMEM((tt, W), F32), pltpu.VMEM((tt, W), F32), pltpu.VMEM((1, W), F32)],
        compiler_params=_params("parallel", "arbitrary"),
        name="lru_mixer",
    )(xr, xr, yg, h0, conv_w.astype(F32), row(conv_b), ga_w.astype(BF16), row(ga_b),
      gx_w.astype(BF16), row(gx_b), row(lam))


def _swa_kernel(sink_ref, q_ref, kp_ref, kc_ref, vp_ref, vc_ref, o_ref, *, j0, n_kv, group, hd, win):
    j = pl.program_id(1) + j0
    q = q_ref[0]
    k2 = jnp.concatenate([kp_ref[0], kc_ref[0]], axis=0)
    v2 = jnp.concatenate([vp_ref[0], vc_ref[0]], axis=0)
    tq = q.shape[0]
    qi = lax.broadcasted_iota(jnp.int32, (tq, 2 * win), 0)
    kj = lax.broadcasted_iota(jnp.int32, (tq, 2 * win), 1)
    dist = qi + win - kj
    mask = (dist >= 0) & (dist < win) & ((j > 0) | (kj >= win))
    scale = hd ** -0.5
    for kvh in range(n_kv):
        kh = k2[:, kvh * hd:(kvh + 1) * hd]
        vh = v2[:, kvh * hd:(kvh + 1) * hd]
        hs_ = [kvh * group + g for g in range(group)]
        s = [jnp.where(mask, _nt(q[:, h * hd:(h + 1) * hd], kh) * scale, -jnp.inf) for h in hs_]
        m = [jnp.maximum(jnp.max(s[g], axis=-1, keepdims=True), sink_ref[hs_[g]]) for g in range(group)]
        p = [jnp.exp(s[g] - m[g]) for g in range(group)]
        den = [jnp.sum(p[g], axis=-1, keepdims=True) + jnp.exp(sink_ref[hs_[g]] - m[g]) for g in range(group)]
        outs = [_dot(p[g].astype(BF16), vh) / den[g] for g in range(group)]
        o_ref[0, :, hs_[0] * hd:(hs_[-1] + 1) * hd] = jnp.concatenate(outs, axis=-1).astype(o_ref.dtype)


def swa_attention(q, k, v, sink, *, j0, win, hd, tq=None):
    B, Tq, QW = q.shape
    KW = k.shape[2]
    n_kv = KW // hd
    group = QW // KW
    tq = win if tq is None else tq
    nq = Tq // tq
    kern = functools.partial(_swa_kernel, j0=j0, n_kv=n_kv, group=group, hd=hd, win=win)
    prev = lambda b, j: (b, jnp.maximum(j + j0 - 1, 0), 0)
    cur = lambda b, j: (b, j + j0, 0)
    return pl.pallas_call(
        kern,
        grid=(B, nq),
        in_specs=[pl.BlockSpec(memory_space=pltpu.SMEM),
                  pl.BlockSpec((1, tq, QW), lambda b, j: (b, j, 0)),
                  pl.BlockSpec((1, win, KW), prev), pl.BlockSpec((1, win, KW), cur),
                  pl.BlockSpec((1, win, KW), prev), pl.BlockSpec((1, win, KW), cur)],
        out_specs=pl.BlockSpec((1, tq, QW), lambda b, j: (b, j, 0)),
        out_shape=jax.ShapeDtypeStruct((B, Tq, QW), BF16),
        compiler_params=_params("parallel", "parallel"),
        name="swa_attention",
    )(sink.astype(F32), q, k, k, v, v)


def _mem_attn_kernel(q_ref, g_ref, mk_ref, mv_ref, o_ref, *, n_heads, hd):
    q = q_ref[0]
    scale = hd ** -0.5
    for h in range(n_heads):
        sl = slice(h * hd, (h + 1) * hd)
        qh = q[:, sl]
        qn = (qh * lax.rsqrt(jnp.mean(qh * qh, axis=-1, keepdims=True) + NORM_EPS) * g_ref[...]).astype(BF16)
        s = _nt(qn, mk_ref[0, :, sl]) * scale
        m = jnp.max(s, axis=-1, keepdims=True)
        p = jnp.exp(s - m)
        p = p / jnp.sum(p, axis=-1, keepdims=True)
        o_ref[0, :, sl] = _dot(p.astype(BF16), mv_ref[0, :, sl]).astype(o_ref.dtype)


def mem_attention(q, qn_g, mk, mv, *, n_heads, tt=256):
    B, T, MW = q.shape
    ML = mk.shape[1]
    hd = MW // n_heads
    tt = min(tt, T)
    return pl.pallas_call(
        functools.partial(_mem_attn_kernel, n_heads=n_heads, hd=hd),
        grid=(B, T // tt),
        in_specs=[pl.BlockSpec((1, tt, MW), lambda b, t: (b, t, 0)),
                  pl.BlockSpec((1, hd), lambda b, t: (0, 0)),
                  pl.BlockSpec((1, ML, MW), lambda b, t: (b, 0, 0)),
                  pl.BlockSpec((1, ML, MW), lambda b, t: (b, 0, 0))],
        out_specs=pl.BlockSpec((1, tt, MW), lambda b, t: (b, t, 0)),
        out_shape=jax.ShapeDtypeStruct((B, T, MW), BF16),
        compiler_params=_params("parallel", "parallel"),
        name="mem_attention",
    )(q, qn_g.reshape(1, hd).astype(F32), mk, mv)


def _ffn_act_kernel(g_ref, halo_ref, up_ref, cw_ref, cb_ref, o_ref):
    t = pl.program_id(2)
    g = g_ref[0]
    prev = halo_ref[0] * jnp.where(t > 0, 1.0, 0.0)
    gc = cb_ref[...] + cw_ref[2:3, :] * g
    for s in (1, 2):
        gc = gc + cw_ref[2 - s:3 - s, :] * _shift_rows(g, prev, s)
    o_ref[0] = (jax.nn.gelu(gc, approximate=True) * up_ref[0]).astype(o_ref.dtype)


def ffn_act(gate, gate_cb0, up, up_cb0, conv_w, conv_b, *, tt=256, tc=512):
    B, T, _ = gate.shape
    C = conv_w.shape[1]
    tt = min(tt, T)
    hb = tt // SUBLANES
    return pl.pallas_call(
        _ffn_act_kernel,
        grid=(B, C // tc, T // tt),
        in_specs=[pl.BlockSpec((1, tt, tc), lambda b, c, t: (b, t, gate_cb0 + c)),
                  pl.BlockSpec((1, SUBLANES, tc), lambda b, c, t: (b, jnp.maximum(t * hb - 1, 0), gate_cb0 + c)),
                  pl.BlockSpec((1, tt, tc), lambda b, c, t: (b, t, up_cb0 + c)),
                  pl.BlockSpec((3, tc), lambda b, c, t: (0, c)),
                  pl.BlockSpec((1, tc), lambda b, c, t: (0, c))],
        out_specs=pl.BlockSpec((1, tt, tc), lambda b, c, t: (b, t, c)),
        out_shape=jax.ShapeDtypeStruct((B, T, C), BF16),
        compiler_params=_params("parallel", "parallel", "parallel"),
        name="ffn_act",
    )(gate, gate, up, conv_w.astype(F32), conv_b.reshape(1, C).astype(F32))


def _ffn_in_kernel(x_ref, wg_ref, wu_ref, cw_ref, cb_ref, o_ref, tail_ref, carry_scr, *, blocks_per_seq, sub):
    i = pl.program_id(0)
    j = pl.program_id(1)
    tm = x_ref.shape[0]

    @pl.when(i % blocks_per_seq == 0)
    def _():
        carry_scr[j] = jnp.zeros(carry_scr.shape[1:], F32)

    prev = carry_scr[j]
    wg = wg_ref[...].astype(BF16)
    wu = wu_ref[...].astype(BF16)
    for c in range(tm // sub):
        rows = slice(c * sub, (c + 1) * sub)
        x = x_ref[rows, :]
        gate = _dot(x, wg)
        up = _dot(x, wu)
        gc = cb_ref[...] + cw_ref[2:3, :] * gate
        for s in (1, 2):
            gc = gc + cw_ref[2 - s:3 - s, :] * _shift_rows(gate, prev, s)
        o_ref[rows, :] = (jax.nn.gelu(gc, approximate=True) * up).astype(o_ref.dtype)
        prev = gate[sub - SUBLANES:, :]
    carry_scr[j] = prev
    tail_ref[0] = prev


def ffn_in_fused(x, w_in, layer, conv_w, conv_b, *, seq_len, tm=1024, tn=512, sub=256):
    M, D = x.shape
    FF = conv_w.shape[1]
    tm = min(tm, seq_len)
    bps = seq_len // tm
    nj = FF // tn
    act, tails = pl.pallas_call(
        functools.partial(_ffn_in_kernel, blocks_per_seq=bps, sub=min(sub, tm)),
        grid=(M // tm, nj),
        in_specs=[pl.BlockSpec((tm, D), lambda i, j: (i, 0), pipeline_mode=pl.Buffered(1)),
                  pl.BlockSpec((None, D, tn), lambda i, j: (layer, 0, j)),
                  pl.BlockSpec((None, D, tn), lambda i, j: (layer, 0, j + nj)),
                  pl.BlockSpec((3, tn), lambda i, j: (0, j)),
                  pl.BlockSpec((1, tn), lambda i, j: (0, j))],
        out_specs=[pl.BlockSpec((tm, tn), lambda i, j: (i, j)),
                   pl.BlockSpec((1, SUBLANES, tn), lambda i, j: (i, 0, j))],
        out_shape=[jax.ShapeDtypeStruct((M, FF), BF16),
                   jax.ShapeDtypeStruct((M // tm, SUBLANES, FF), F32)],
        scratch_shapes=[pltpu.VMEM((nj, SUBLANES, tn), F32)],
        compiler_params=_params("arbitrary", "arbitrary"),
        name="ffn_in_fused",
    )(x, w_in, w_in, conv_w.astype(F32), conv_b.reshape(1, FF).astype(F32))
    return act, tails[bps - 1::bps]


def _rwkv_mix_kernel(x_ref, halo_ref, s0_ref, g_ref, mu_ref, *out_refs, n_valid_last):
    t = pl.program_id(1)
    nt = pl.num_programs(1)
    o_refs, hl_ref = out_refs[:6], out_refs[6]

    def norm(v):
        return v * lax.rsqrt(jnp.mean(v * v, axis=-1, keepdims=True) + NORM_EPS) * g_ref[...]

    h = norm(x_ref[0])
    hp = norm(halo_ref[0])
    first = jnp.where(t > 0, 1.0, 0.0)
    hp = hp * first + jnp.broadcast_to(s0_ref[0], hp.shape) * (1.0 - first)
    xx = _shift_rows(h, hp, 1) - h
    for j in range(6):
        o_refs[j][0] = (h + xx * mu_ref[j:j + 1, :]).astype(o_refs[j].dtype)

    @pl.when(t == nt - 1)
    def _():
        hl_ref[0] = h[n_valid_last - 1:n_valid_last, :]


def rwkv_mix(x, shift0, g, mu, *, n_valid, tt=256):
    B, T, D = x.shape
    tt = min(tt, T)
    hb = tt // SUBLANES
    n_valid_last = n_valid - (T // tt - 1) * tt
    blk = pl.BlockSpec((1, tt, D), lambda b, t: (b, t, 0))
    one = pl.BlockSpec((1, 1, D), lambda b, t: (b, 0, 0))
    return pl.pallas_call(
        functools.partial(_rwkv_mix_kernel, n_valid_last=n_valid_last),
        grid=(B, T // tt),
        in_specs=[blk,
                  pl.BlockSpec((1, SUBLANES, D), lambda b, t: (b, jnp.maximum(t * hb - 1, 0), 0)),
                  one,
                  pl.BlockSpec((1, D), lambda b, t: (0, 0)),
                  pl.BlockSpec((6, D), lambda b, t: (0, 0))],
        out_specs=[blk] * 6 + [one],
        out_shape=[jax.ShapeDtypeStruct((B, T, D), BF16)] * 6 + [jax.ShapeDtypeStruct((B, 1, D), F32)],
        compiler_params=_params("parallel", "arbitrary"),
        name="rwkv_mix",
    )(x, x, shift0, g.reshape(1, D).astype(F32), mu.astype(F32))


def _rwkv_chunk_kernel(r_ref, k_ref, v_ref, wp_ref, ap_ref, g_ref, kk_ref, ka_ref, rk_ref, lng_ref, lnb_ref,
                       s0_ref, o_ref, so_ref, s_scr, *, hs):
    c = pl.program_id(2)
    nc = pl.num_programs(2)
    _, C, L = r_ref.shape
    n_pairs = L // LANES

    @pl.when(c == 0)
    def _():
        s_scr[...] = s0_ref[0]

    r = r_ref[0]
    k = k_ref[0]
    v = v_ref[0]
    logw = -jnp.exp(-jax.nn.softplus(-wp_ref[0]) - 0.5)
    a = jax.nn.sigmoid(ap_ref[0])
    kk = k * kk_ref[...]
    kk = kk / jnp.maximum(jnp.sqrt(_group_sum_bcast(kk * kk, hs)), 1e-12)
    k2 = k * (1.0 + (a - 1.0) * ka_ref[...])
    bm = kk * a

    ti = lax.broadcasted_iota(jnp.int32, (C, C), 0)
    si = lax.broadcasted_iota(jnp.int32, (C, C), 1)
    tri = jnp.where(ti >= si, 1.0, 0.0).astype(BF16)
    cum = _dot_exact_rhs_lhs(tri, logw)
    e_in = jnp.exp(cum)
    e_out = jnp.exp(-cum)
    e_end = jnp.exp(cum[C - 1:C, :] - cum)
    g_end = jnp.exp(cum[C - 1:C, :])
    rt = r * e_in
    kkt = kk * jnp.exp(cum - logw)
    bh = bm * e_out
    kh = k2 * e_out
    bbar = bm * e_end
    kbar = k2 * e_end

    lane = lax.broadcasted_iota(jnp.int32, (1, LANES), 1)
    m0 = lane < hs
    C2 = 2 * C
    ri = lax.broadcasted_iota(jnp.int32, (C2, C2), 0)
    ci = lax.broadcasted_iota(jnp.int32, (C2, C2), 1)
    same = (ri // C) == (ci // C)
    strict = same & ((ri % C) > (ci % C))
    lower = (ri % C) >= (ci % C)
    vi = lax.broadcasted_iota(jnp.int32, (LANES, LANES), 0) // hs
    vj = lax.broadcasted_iota(jnp.int32, (LANES, LANES), 1) // hs
    blockdiag = vi == vj

    def stack_heads(x):
        return jnp.concatenate([jnp.where(m0, x, 0.0), jnp.where(m0, 0.0, x)], axis=0).astype(BF16)

    n_steps = int(math.log2(C))
    P = range(n_pairs)
    sls = [slice(p * LANES, (p + 1) * LANES) for p in P]
    S = [s_scr[p] for p in P]
    S16 = [s.astype(BF16) for s in S]
    V16 = [v[:, sl].astype(BF16) for sl in sls]
    lhk = [stack_heads(kkt[:, sl]) for sl in sls]
    wide = C2 % LANES == 0
    if wide:
        gbk = [_nt(lhk[p], jnp.concatenate([stack_heads(bh[:, sls[p]]), stack_heads(kh[:, sls[p]])], axis=0))
               for p in P]
        nmat = [jnp.where(strict, -g[:, :C2], 0.0) for g in gbk]
        auk = [jnp.where(strict, -g[:, C2:], 0.0).astype(BF16) for g in gbk]
    else:
        nmat = [jnp.where(strict, -_nt(lhk[p], stack_heads(bh[:, sls[p]])), 0.0) for p in P]
        auk = [jnp.where(strict, -_nt(lhk[p], stack_heads(kh[:, sls[p]])), 0.0).astype(BF16) for p in P]
    sprod = [_nt(jnp.concatenate([lhk[p], rt[:, sls[p]].astype(BF16)], axis=0), S16[p]) for p in P]
    u = [_dot(auk[p], jnp.concatenate([V16[p], V16[p]], axis=0)) - sprod[p][:C2] for p in P]
    npow = [n.astype(BF16) for n in nmat]
    for it in range(n_steps):
        last = it + 1 == n_steps
        if wide and not last:
            res = [_dot(npow[p], jnp.concatenate([u[p].astype(BF16), npow[p]], axis=1)) for p in P]
            u = [u[p] + res[p][:, :LANES] for p in P]
            npow = [res[p][:, LANES:].astype(BF16) for p in P]
        else:
            u = [u[p] + _dot(npow[p], u[p].astype(BF16)) for p in P]
            if not last:
                npow = [_dot(npow[p], npow[p]).astype(BF16) for p in P]
    uv = [jnp.concatenate([jnp.where(m0, u[p][:C], u[p][C:]).astype(BF16), V16[p]], axis=0) for p in P]
    ar = [jnp.where(lower, _nt(stack_heads(rt[:, sls[p]]),
                               jnp.concatenate([bh[:, sls[p]], kh[:, sls[p]]], axis=0).astype(BF16)), 0.0).astype(BF16)
          for p in P]
    tmat = [_dot(ar[p], uv[p]) for p in P]
    ys = [sprod[p][C2:] + jnp.where(m0, tmat[p][:C], tmat[p][C:]) for p in P]
    upd = [_tn(uv[p], jnp.concatenate([bbar[:, sls[p]], kbar[:, sls[p]]], axis=0).astype(BF16)) for p in P]
    for p in P:
        s_scr[p] = S[p] * g_end[:, sls[p]] + jnp.where(blockdiag, upd[p], 0.0)

    y = ys[0] if n_pairs == 1 else jnp.concatenate(ys, axis=-1)
    inv = 1.0 / hs
    mean = _group_sum_bcast(y, hs) * inv
    yc = y - mean
    var = _group_sum_bcast(yc * yc, hs) * inv
    yn = yc * lax.rsqrt(var + RW_GN_EPS) * lng_ref[...] + lnb_ref[...]
    bonus = _group_sum_bcast(r * k2 * rk_ref[...], hs) * v
    o_ref[0] = ((yn + bonus) * g_ref[0]).astype(o_ref.dtype)

    @pl.when(c == nc - 1)
    def _():
        so_ref[0] = s_scr[...]


def _dot_exact_rhs_lhs(m_bf16, x):
    hi, mid, lo = _split3(x)
    return _dot(m_bf16, hi) + _dot(m_bf16, mid) + _dot(m_bf16, lo)


def rwkv_chunked(r, k, v, wp, ap, g, k_k, k_a, r_k, ln_g, ln_b, s0, *, hs, chunk, lanes_per_step=1024):
    B, T, D = r.shape
    L = min(lanes_per_step, D)
    npg = L // LANES
    seq = pl.BlockSpec((1, chunk, L), lambda b, hg, c: (b, c, hg))
    par = pl.BlockSpec((1, L), lambda b, hg, c: (0, hg))
    st = pl.BlockSpec((1, npg, LANES, LANES), lambda b, hg, c: (b, hg, 0, 0))
    row = lambda x: x.reshape(1, D).astype(F32)
    return pl.pallas_call(
        functools.partial(_rwkv_chunk_kernel, hs=hs),
        grid=(B, D // L, T // chunk),
        in_specs=[seq] * 6 + [par] * 5 + [st],
        out_specs=[seq, st],
        out_shape=[jax.ShapeDtypeStruct((B, T, D), BF16),
                   jax.ShapeDtypeStruct(s0.shape, F32)],
        scratch_shapes=[pltpu.VMEM((npg, LANES, LANES), F32)],
        compiler_params=_params("parallel", "parallel", "arbitrary"),
        name="rwkv_chunked",
    )(r, k, v, wp, ap, g, row(k_k), row(k_a), row(r_k), row(ln_g), row(ln_b), s0)


def _pairs_from_heads(s, hs):
    B, H = s.shape[:2]
    s = s.reshape(B, H // 2, 2, hs, hs)
    z = jnp.zeros_like(s[:, :, 0])
    top = jnp.concatenate([s[:, :, 0], z], axis=-1)
    bot = jnp.concatenate([z, s[:, :, 1]], axis=-1)
    return jnp.concatenate([top, bot], axis=-2)


def _heads_from_pairs(sp, hs):
    B, P = sp.shape[:2]
    return jnp.stack([sp[:, :, :hs, :hs], sp[:, :, hs:, hs:]], axis=2).reshape(B, 2 * P, hs, hs)


def _pad_rows(x, front, back):
    return jnp.pad(x, ((0, 0), (front, back), (0, 0)))


def kernel(x_prompt, x_sample, mem_prompt, state_lru_conv, state_lru_h, cache_swa_k, cache_swa_v,
           state_rwkv_shift, state_rwkv_wkv, cache_mem_k, cache_mem_v, state_ffn_conv,
           a_norm_g, a_w_in, a_conv_w, a_conv_b, a_gate_a_w, a_gate_a_b, a_gate_x_w, a_gate_x_b,
           a_lambda, b_q_norm_g, b_k_norm_g, b_sink, a_w_out,
           c_norm_g, c_mu, c_w_r, c_w_k, c_w_v, c_w_o, c_w0, c_w1, c_w2, c_a0, c_a1, c_a2,
           c_g1, c_g2, c_k_k, c_k_a, c_r_k, c_ln_g, c_ln_b,
           m_norm_g, m_mem_norm_g, m_w_q, m_w_kv, m_q_norm_g, m_k_norm_g, m_w_o,
           f_norm_g, f_w_in, f_conv_w, f_conv_b, f_w_out):
    D = x_prompt.shape[-1]
    depth = m_norm_g.shape[0]
    W = a_conv_w.shape[-1]
    KA = a_conv_w.shape[1]
    hd = b_q_norm_g.shape[-1]
    n_q = b_sink.shape[-1]
    n_kv = cache_swa_k.shape[3]
    win = cache_swa_k.shape[2]
    QW, KW = n_q * hd, n_kv * hd
    hs = c_r_k.shape[-1]
    n_rw = c_r_k.shape[1]
    m_heads, m_hd = cache_mem_k.shape[3], cache_mem_k.shape[4]
    MW = m_heads * m_hd
    FF = f_conv_w.shape[-1]
    KF = f_conv_w.shape[1]
    bf = lambda w: w.astype(BF16)
    wb = {name: bf(w) for name, w in dict(
        a_w_in=a_w_in, a_w_out=a_w_out, c_w_r=c_w_r, c_w_k=c_w_k, c_w_v=c_w_v, c_w_o=c_w_o,
        m_w_q=m_w_q, m_w_kv=m_w_kv, m_w_o=m_w_o, f_w_in=f_w_in, f_w_out=f_w_out).items()}

    def run(x, T_real, lru_conv, lru_h, swa_k, swa_v, rw_shift, rw_wkv, mem_k, mem_v, ffn_conv, prompt):
        B = x.shape[0]
        T = T_real
        M = B * T
        xf = x.reshape(M, D)
        outs = {}
        ia = ic = 0
        for l in range(depth):
            if l % 2 == 0:
                i = ia
                ia += 1
                h = rmsnorm_rows(xf, a_norm_g[i])
                zz = matmul(h, wb["a_w_in"], layer=i, n=2 * W)
                qkv = matmul(h, wb["a_w_in"], layer=i, n0=2 * W)
                qn = headnorm(qkv, 0, QW, b_q_norm_g[i], hd, BF16)
                kn = headnorm(qkv, QW // KW, KW, b_k_norm_g[i], hd, F32)
                vv = qkv[:, QW + KW:]
                zz3 = zz.reshape(B, T, 2 * W)
                if prompt:
                    out_a, h_last = lru_mixer(zz3, 0, zz3, 1, jnp.zeros((B, 1, W), F32),
                                              a_conv_w[i], a_conv_b[i], a_gate_a_w[i], a_gate_a_b[i],
                                              a_gate_x_w[i], a_gate_x_b[i], a_lambda[i], n_pad=0, tt=256)
                    conv_new = zz3[:, T - (KA - 1):, :W]
                    o = swa_attention(qn.reshape(B, T, QW), bf(kn).reshape(B, T, KW),
                                      bf(vv).reshape(B, T, KW), b_sink[i], j0=0, win=win, hd=hd)
                    k_new = kn.reshape(B, T, n_kv, hd)[:, T - win:]
                    v_new = vv.reshape(B, T, n_kv, hd)[:, T - win:]
                else:
                    n_pad = SUBLANES - T
                    xr_hist = jnp.concatenate([lru_conv[i].astype(F32), zz3[:, :, :W]], axis=1)
                    xr_p = _pad_rows(xr_hist, SUBLANES - xr_hist.shape[1], 0)
                    yg_p = _pad_rows(zz3[:, :, W:], n_pad, 0)
                    out_a, h_last = lru_mixer(xr_p, 0, yg_p, 0, lru_h[i].reshape(B, 1, W).astype(F32),
                                              a_conv_w[i], a_conv_b[i], a_gate_a_w[i], a_gate_a_b[i],
                                              a_gate_x_w[i], a_gate_x_b[i], a_lambda[i], n_pad=n_pad, tt=SUBLANES)
                    out_a = out_a[:, n_pad:]
                    conv_new = xr_hist[:, T:]
                    kc = swa_k[i].reshape(B, win, KW)
                    vc = swa_v[i].reshape(B, win, KW)
                    k_all = jnp.concatenate([kc, kn.reshape(B, T, KW)], axis=1)
                    v_all = jnp.concatenate([vc, vv.reshape(B, T, KW)], axis=1)
                    o = swa_attention(_pad_rows(qn.reshape(B, T, QW), 0, 2 * SUBLANES - T),
                                      bf(_pad_rows(k_all, 0, win - T)), bf(_pad_rows(v_all, 0, win - T)),
                                      b_sink[i], j0=1, win=win, hd=hd, tq=2 * SUBLANES)[:, :T]
                    k_new = k_all[:, T:].reshape(B, win, n_kv, hd)
                    v_new = v_all[:, T:].reshape(B, win, n_kv, hd)
                mix = jnp.concatenate([out_a.reshape(M, W), o.reshape(M, QW)], axis=-1)
                xf = matmul(mix, wb["a_w_out"], layer=i, res=xf)
                outs.setdefault("lc", []).append(conv_new)
                outs.setdefault("lh", []).append(h_last.reshape(B, W))
                outs.setdefault("sk", []).append(k_new)
                outs.setdefault("sv", []).append(v_new)
            else:
                i = ic
                ic += 1
                Tp = T if prompt else SUBLANES
                chunk = 64 if prompt else 16
                x3 = xf.reshape(B, T, D)
                if not prompt:
                    x3 = _pad_rows(x3, 0, Tp - T)
                mixes = rwkv_mix(x3, rw_shift[i].reshape(B, 1, D).astype(F32), c_norm_g[i], c_mu[i], n_valid=T)
                xr, xw, xk, xv, xa, xg = [m.reshape(B * Tp, D) for m in mixes[:6]]
                sh_new = mixes[6].reshape(B, D)
                r = matmul(xr, wb["c_w_r"], layer=i)
                k = matmul(xk, wb["c_w_k"], layer=i)
                v = matmul(xv, wb["c_w_v"], layer=i)
                wp = matmul(matmul(xw, c_w1, layer=i, act="tanh", out_dtype=BF16), c_w2, layer=i, bias=c_w0[i])
                ap = matmul(matmul(xa, c_a1, layer=i, out_dtype=BF16), c_a2, layer=i, bias=c_a0[i])
                gl = c_g1[i].shape[1]
                glp = -(-gl // LANES) * LANES
                g1p = jnp.pad(c_g1[i], ((0, 0), (0, glp - gl)))
                g2p = jnp.pad(c_g2[i], ((0, glp - gl), (0, 0)))
                g = matmul(matmul(xg, g1p, act="sigmoid", out_dtype=BF16), g2p)
                if not prompt:
                    sq = lambda t, fill=0.0: jnp.pad(t.reshape(B, Tp, D)[:, :T], ((0, 0), (0, chunk - T), (0, 0)),
                                                     constant_values=fill)
                    r, k, v, ap, g = sq(r), sq(k), sq(v), sq(ap), sq(g)
                    wp = sq(wp, -1e30)
                else:
                    sq = lambda t: t.reshape(B, Tp, D)
                    r, k, v, wp, ap, g = sq(r), sq(k), sq(v), sq(wp), sq(ap), sq(g)
                s0 = _pairs_from_heads(rw_wkv[i].astype(F32), hs)
                yo, s_end = rwkv_chunked(r, k, v, wp, ap, g, c_k_k[i], c_k_a[i], c_r_k[i], c_ln_g[i], c_ln_b[i],
                                         s0, hs=hs, chunk=chunk)
                yo = yo[:, :T].reshape(M, D)
                xf = matmul(yo, wb["c_w_o"], layer=i, res=xf)
                outs.setdefault("rs", []).append(sh_new)
                outs.setdefault("rw", []).append(_heads_from_pairs(s_end, hs))
            hm = rmsnorm_rows(xf, m_norm_g[l])
            q = matmul(hm, wb["m_w_q"], layer=l)
            Tq = T if prompt else SUBLANES
            q3 = q.reshape(B, T, MW)
            if not prompt:
                q3 = _pad_rows(q3, 0, Tq - T)
            ML = mem_k[l].shape[1]
            om = mem_attention(q3, m_q_norm_g[l], bf(mem_k[l]).reshape(B, ML, MW), bf(mem_v[l]).reshape(B, ML, MW),
                               n_heads=m_heads)
            om = om[:, :T].reshape(M, MW)
            xf = matmul(om, wb["m_w_o"], layer=l, res=xf)
            hf = rmsnorm_rows(xf, f_norm_g[l])
            if prompt:
                act, tail = ffn_in_fused(hf, wb["f_w_in"], l, f_conv_w[l], f_conv_b[l], seq_len=T)
                f_new = tail[:, SUBLANES - (KF - 1):, :]
            else:
                gu3 = matmul(hf, wb["f_w_in"], layer=l, tn=1024).reshape(B, T, 2 * FF)
                g_hist = jnp.concatenate([ffn_conv[l].astype(F32), gu3[:, :, :FF]], axis=1)
                g_p = _pad_rows(g_hist, SUBLANES - g_hist.shape[1], 0)
                up_p = _pad_rows(gu3[:, :, FF:], SUBLANES - T, 0)
                act = ffn_act(g_p, 0, up_p, 0, f_conv_w[l], f_conv_b[l], tt=SUBLANES, tc=FF)[:, SUBLANES - T:]
                f_new = g_hist[:, T:]
            xf = matmul(act.reshape(M, FF), wb["f_w_out"], layer=l, res=xf, tn=1024, tk=2048)
            outs.setdefault("fc", []).append(f_new)
        st = lambda name: jnp.stack(outs[name])
        return (xf.reshape(B, T, D), st("lc"), st("lh"), st("sk"), st("sv"), st("rs"), st("rw"), st("fc"))

    Bp, S = x_prompt.shape[:2]
    ML = mem_prompt.shape[1]
    mem_flat = mem_prompt.reshape(Bp * ML, D)
    mks, mvs = [], []
    for l in range(depth):
        mn = rmsnorm_rows(mem_flat, m_mem_norm_g[l])
        kv = matmul(mn, wb["m_w_kv"], layer=l)
        mk = headnorm(kv, 0, MW, m_k_norm_g[l], m_hd, F32)
        mks.append(mk.reshape(Bp, ML, m_heads, m_hd))
        mvs.append(kv[:, MW:].reshape(Bp, ML, m_heads, m_hd))
    p_mem_k = jnp.stack(mks)
    p_mem_v = jnp.stack(mvs)

    n_a = a_norm_g.shape[0]
    n_c = c_norm_g.shape[0]
    zeros = lambda *s: jnp.zeros(s, F32)
    y_p, p_lc, p_lh, p_sk, p_sv, p_rs, p_rw, p_fc = run(
        x_prompt, S, None, None, None, None, zeros(n_c, Bp, D), zeros(n_c, Bp, n_rw, hs, hs),
        p_mem_k, p_mem_v, None, True)
    Bs, Ts = x_sample.shape[:2]
    y_s, s_lc, s_lh, s_sk, s_sv, s_rs, s_rw, s_fc = run(
        x_sample, Ts, state_lru_conv, state_lru_h, cache_swa_k, cache_swa_v,
        state_rwkv_shift, state_rwkv_wkv, cache_mem_k, cache_mem_v, state_ffn_conv, False)
    return (y_p, y_s, p_lc, p_lh, p_sk, p_sv, p_rs, p_rw, p_mem_k, p_mem_v, p_fc,
            s_lc, s_lh, s_sk, s_sv, s_rs, s_rw, s_fc)
```

```python
import functools
import math

import jax
import jax.numpy as jnp
from jax import lax
from jax.experimental import pallas as pl
from jax.experimental.pallas import tpu as pltpu

F32 = jnp.float32
BF16 = jnp.bfloat16

NORM_EPS = 1e-6
RW_GN_EPS = 64e-5
LRU_C = 8.0
LANES = 128
SUBLANES = 8
VMEM_LIMIT_BYTES = 56 * 1024 * 1024


def _params(*sem):
    return pltpu.CompilerParams(dimension_semantics=sem, vmem_limit_bytes=VMEM_LIMIT_BYTES)


def _nt(a, b):
    return lax.dot_general(a, b, (((1,), (1,)), ((), ())), preferred_element_type=F32)


def _tn(a, b):
    return lax.dot_general(a, b, (((0,), (0,)), ((), ())), preferred_element_type=F32)


def _dot(a, b):
    return jnp.dot(a, b, preferred_element_type=F32)


def _split3(x):
    hi = x.astype(BF16)
    r1 = x - hi.astype(F32)
    mid = r1.astype(BF16)
    lo = (r1 - mid.astype(F32)).astype(BF16)
    return hi, mid, lo


def _group_sum_bcast(x, width):
    m, L = x.shape
    n = L // LANES
    if width == LANES:
        parts = []
        for c in range(n):
            s = jnp.sum(x[:, c * LANES:(c + 1) * LANES], axis=-1, keepdims=True)
            parts.append(jnp.broadcast_to(s, (m, LANES)))
        return parts[0] if n == 1 else jnp.concatenate(parts, axis=-1)
    li = lax.broadcasted_iota(jnp.int32, (LANES, LANES), 0) // width
    lj = lax.broadcasted_iota(jnp.int32, (LANES, LANES), 1) // width
    e = jnp.where(li == lj, 1.0, 0.0).astype(BF16)
    xs = x if n == 1 else jnp.concatenate([x[:, c * LANES:(c + 1) * LANES] for c in range(n)], axis=0)
    hi = xs.astype(BF16)
    lo = (xs - hi.astype(F32)).astype(BF16)
    out = _dot(hi, e) + _dot(lo, e)
    return out if n == 1 else jnp.concatenate([out[c * m:(c + 1) * m] for c in range(n)], axis=-1)


def _shift_rows(x, prev8, s):
    rolled = pltpu.roll(x, s, 0)
    top = jnp.where(lax.broadcasted_iota(jnp.int32, (SUBLANES, 1), 0) < s,
                    pltpu.roll(prev8, s, 0), rolled[0:SUBLANES])
    if x.shape[0] == SUBLANES:
        return top
    return jnp.concatenate([top, rolled[SUBLANES:]], axis=0)


def _rmsnorm_kernel(x_ref, g_ref, o_ref):
    x = x_ref[...]
    ms = jnp.mean(x * x, axis=-1, keepdims=True)
    o_ref[...] = (x * lax.rsqrt(ms + NORM_EPS) * g_ref[...]).astype(o_ref.dtype)


def rmsnorm_rows(x, g, out_dtype=BF16, tm=256):
    M, D = x.shape
    tm = min(tm, M)
    return pl.pallas_call(
        _rmsnorm_kernel,
        grid=(M // tm,),
        in_specs=[pl.BlockSpec((tm, D), lambda i: (i, 0)), pl.BlockSpec((1, D), lambda i: (0, 0))],
        out_specs=pl.BlockSpec((tm, D), lambda i: (i, 0)),
        out_shape=jax.ShapeDtypeStruct((M, D), out_dtype),
        compiler_params=_params("parallel"),
        name="rmsnorm_rows",
    )(x, g.reshape(1, D))


def _mm_kernel(*refs, nk, has_bias, has_res, has_into, emit_w, in_place, act):
    x_ref, w_ref = refs[0], refs[1]
    pos = 2
    b_ref = r_ref = wb_ref = acc_ref = None
    if has_bias:
        b_ref = refs[pos]
        pos += 1
    if has_res:
        r_ref = refs[pos]
        pos += 1
    if has_into:
        pos += 1
    o_ref = refs[pos]
    pos += 1
    if emit_w:
        wb_ref = refs[pos]
        pos += 1
    if nk > 1 and not in_place:
        acc_ref = refs[pos]

    def weights(cs=slice(None)):
        w16 = w_ref[:, cs].astype(BF16)
        if emit_w:
            wb_ref[:, cs] = w16
        return w16

    def epilogue(y):
        if has_bias:
            y = y + b_ref[...]
        if act == "tanh":
            y = jnp.tanh(y)
        elif act == "sigmoid":
            y = jax.nn.sigmoid(y)
        if has_res:
            y = y + r_ref[...]
        o_ref[...] = y.astype(o_ref.dtype)

    if nk > 1 and in_place:
        k = pl.program_id(2)
        tn = o_ref.shape[1]
        cw = min(tn, 2 * LANES)

        def sweep(first):
            x = x_ref[...]
            for c in range(tn // cw):
                cs = slice(c * cw, (c + 1) * cw)
                part = _dot(x, weights(cs))
                if not first:
                    o_ref[:, cs] += part
                elif has_res:
                    o_ref[:, cs] = part + r_ref[:, cs]
                else:
                    o_ref[:, cs] = part

        pl.when(k == 0)(functools.partial(sweep, True))
        pl.when(k > 0)(functools.partial(sweep, False))
        return

    part = _dot(x_ref[...], weights())
    if nk == 1:
        epilogue(part)
        return
    k = pl.program_id(2)

    @pl.when(k == 0)
    def _():
        acc_ref[...] = part

    @pl.when(k > 0)
    def _():
        acc_ref[...] += part

    @pl.when(k == nk - 1)
    def _():
        epilogue(acc_ref[...])


def _pick(n, prefs):
    for p in prefs:
        if n % p == 0:
            return p
    return n


def matmul(x, w, *, layer=None, n0=0, n=None, bias=None, res=None, act=None, out_dtype=F32,
           tm=1024, tn=512, tk=4096, row0=0, rows=None, into=None, emit_w=False):
    M, K = x.shape
    N = w.shape[-1] - n0 if n is None else n
    rows = M - row0 if rows is None else rows
    tm = min(tm, rows)
    tn = _pick(math.gcd(N, n0) if n0 else N, (tn, 512, 256, 128))
    tk = _pick(K, (tk, 2048, 1024, 512))
    nk = K // tk
    j0 = n0 // tn
    i0 = row0 // tm
    assert row0 % tm == 0 and rows % tm == 0 and (not emit_w or rows == tm)
    in_specs = [pl.BlockSpec((tm, tk), lambda i, j, k: (i + i0, k)),
                pl.BlockSpec((tk, tn), lambda i, j, k: (k, j + j0)) if layer is None else
                pl.BlockSpec((None, tk, tn), lambda i, j, k: (layer, k, j + j0))]
    args = [x, w]
    if bias is not None:
        in_specs.append(pl.BlockSpec((1, tn), lambda i, j, k: (0, j)))
        args.append(bias.reshape(1, N).astype(F32))
    if res is not None:
        in_specs.append(pl.BlockSpec((tm, tn), lambda i, j, k: (i + i0, j)))
        args.append(res)
    aliases = {}
    if into is not None:
        aliases = {len(args): 0}
        in_specs.append(pl.BlockSpec(memory_space=pl.ANY))
        args.append(into)
    in_place = bias is None and act is None and out_dtype == F32
    kern = functools.partial(_mm_kernel, nk=nk, has_bias=bias is not None, has_res=res is not None,
                             has_into=into is not None, emit_w=emit_w, in_place=in_place, act=act)
    out_specs = [pl.BlockSpec((tm, tn), lambda i, j, k: (i + i0, j))]
    out_shape = [jax.ShapeDtypeStruct((M, N), out_dtype)]
    if emit_w:
        out_specs.append(pl.BlockSpec((tk, tn), lambda i, j, k: (k, j)))
        out_shape.append(jax.ShapeDtypeStruct((K, N), BF16))
    outs = pl.pallas_call(
        kern,
        grid=(rows // tm, N // tn, nk),
        in_specs=in_specs,
        out_specs=out_specs,
        out_shape=out_shape,
        input_output_aliases=aliases,
        scratch_shapes=[pltpu.VMEM((tm, tn), F32)] if (nk > 1 and not in_place) else [],
        compiler_params=_params("parallel", "parallel", "arbitrary"),
        name="matmul",
    )(*args)
    return outs if emit_w else outs[0]


def dense(x, w, layer, wcache, key, **kw):
    tm = min(kw.get("tm", 1024), x.shape[0])
    first, wb = matmul(x, w, layer=layer, rows=tm, emit_w=True, **kw)
    wcache[key] = wb
    if tm == x.shape[0]:
        return first
    kw = {a: b for a, b in kw.items() if a not in ("n0", "n")}
    return matmul(x, wb, row0=tm, into=first, **kw)


def _headnorm_kernel(x_ref, g_ref, o_ref, *, hd):
    x = x_ref[...]
    ms = _group_sum_bcast(x * x, hd) * (1.0 / hd)
    o_ref[...] = (x * lax.rsqrt(ms + NORM_EPS) * g_ref[...]).astype(o_ref.dtype)


def headnorm(x, col_block, width, g, hd, out_dtype, tm=256):
    M = x.shape[0]
    tm = min(tm, M)
    g_row = jnp.tile(g.astype(F32), width // hd).reshape(1, width)
    return pl.pallas_call(
        functools.partial(_headnorm_kernel, hd=hd),
        grid=(M // tm,),
        in_specs=[pl.BlockSpec((tm, width), lambda i: (i, col_block)),
                  pl.BlockSpec((1, width), lambda i: (0, 0))],
        out_specs=pl.BlockSpec((tm, width), lambda i: (i, 0)),
        out_shape=jax.ShapeDtypeStruct((M, width), out_dtype),
        compiler_params=_params("parallel"),
        name="headnorm",
    )(x, g_row)


def _lru_kernel(xr_ref, halo_ref, yg_ref, h0_ref, cw_ref, cb_ref, gaw_ref, gab_ref, gxw_ref, gxb_ref,
                lam_ref, o_ref, hl_ref, a_scr, u_scr, h_scr, *, n_pad, n_blocks, bw, scan_w):
    t = pl.program_id(1)
    tt, W = a_scr.shape

    @pl.when(t == 0)
    def _():
        h_scr[...] = h0_ref[0]

    x = xr_ref[0]
    prev = halo_ref[0] * jnp.where(t > 0, 1.0, 0.0)
    xc = cb_ref[...] + cw_ref[3:4, :] * x
    for s in (1, 2, 3):
        xc = xc + cw_ref[3 - s:4 - s, :] * _shift_rows(x, prev, s)

    nsp = -LRU_C * jax.nn.softplus(-lam_ref[...])
    if n_pad:
        live = (lax.broadcasted_iota(jnp.int32, (tt, 1), 0) >= n_pad) | (t > 0)
    for n in range(n_blocks):
        sl = slice(n * bw, (n + 1) * bw)
        xb = xc[:, sl]
        xb16 = xb.astype(BF16)
        r = jax.nn.sigmoid(_dot(xb16, gaw_ref[n]) + gab_ref[:, sl])
        i = jax.nn.sigmoid(_dot(xb16, gxw_ref[n]) + gxb_ref[:, sl])
        log_a = r * nsp[:, sl]
        a = jnp.exp(log_a)
        u = jnp.sqrt(jnp.maximum(-jnp.tanh(log_a) * (a * a + 1.0), 0.0)) * (i * xb)
        if n_pad:
            a = jnp.where(live, a, 1.0)
            u = jnp.where(live, u, 0.0)
        a_scr[:, sl] = a
        u_scr[:, sl] = u

    row8 = lax.broadcasted_iota(jnp.int32, (SUBLANES, 1), 0)
    for c in range(W // scan_w):
        cs = slice(c * scan_w, (c + 1) * scan_w)

        def body(gi, h, cs=cs):
            r0 = pl.multiple_of(gi * SUBLANES, SUBLANES)
            A = a_scr[pl.ds(r0, SUBLANES), cs]
            U = u_scr[pl.ds(r0, SUBLANES), cs]
            for s in (1, 2, 4):
                As = pltpu.roll(A, s, 0)
                Us = pltpu.roll(U, s, 0)
                m = row8 >= s
                U = jnp.where(m, A * Us + U, U)
                A = jnp.where(m, A * As, A)
            H = A * h + U
            u_scr[pl.ds(r0, SUBLANES), cs] = H
            return H[SUBLANES - 1:SUBLANES, :]

        h_end = lax.fori_loop(0, tt // SUBLANES, body, h_scr[:, cs])
        h_scr[:, cs] = h_end

    o_ref[0] = (u_scr[...] * jax.nn.gelu(yg_ref[0], approximate=True)).astype(o_ref.dtype)
    hl_ref[0] = h_scr[...]


def lru_mixer(xr, xr_cb, yg, yg_cb, h0, conv_w, conv_b, ga_w, ga_b, gx_w, gx_b, lam, *, n_pad, tt):
    B, T, _ = xr.shape
    W = conv_w.shape[1]
    nb, bw = ga_w.shape[0], ga_w.shape[1]
    tt = min(tt, T)
    hb = tt // SUBLANES
    row = lambda v: v.reshape(1, W).astype(F32)
    kern = functools.partial(_lru_kernel, n_pad=n_pad, n_blocks=nb, bw=bw, scan_w=min(W, 512))
    return pl.pallas_call(
        kern,
        grid=(B, T // tt),
        in_specs=[
            pl.BlockSpec((1, tt, W), lambda b, t: (b, t, xr_cb)),
            pl.BlockSpec((1, SUBLANES, W), lambda b, t: (b, jnp.maximum(t * hb - 1, 0), xr_cb)),
            pl.BlockSpec((1, tt, W), lambda b, t: (b, t, yg_cb)),
            pl.BlockSpec((1, 1, W), lambda b, t: (b, 0, 0)),
            pl.BlockSpec((4, W), lambda b, t: (0, 0)),
            pl.BlockSpec((1, W), lambda b, t: (0, 0)),
            pl.BlockSpec((nb, bw, bw), lambda b, t: (0, 0, 0)),
            pl.BlockSpec((1, W), lambda b, t: (0, 0)),
            pl.BlockSpec((nb, bw, bw), lambda b, t: (0, 0, 0)),
            pl.BlockSpec((1, W), lambda b, t: (0, 0)),
            pl.BlockSpec((1, W), lambda b, t: (0, 0)),
        ],
        out_specs=[pl.BlockSpec((1, tt, W), lambda b, t: (b, t, 0)),
                   pl.BlockSpec((1, 1, W), lambda b, t: (b, 0, 0))],
        out_shape=[jax.ShapeDtypeStruct((B, T, W), BF16), jax.ShapeDtypeStruct((B, 1, W), F32)],
        scratch_shapes=[pltpu.VMEM((tt, W), F32), pltpu.VMEM((tt, W), F32), pltpu.VMEM((1, W), F32)],
        compiler_params=_params("parallel", "arbitrary"),
        name="lru_mixer",
    )(xr, xr, yg, h0, conv_w.astype(F32), row(conv_b), ga_w.astype(BF16), row(ga_b),
      gx_w.astype(BF16), row(gx_b), row(lam))


def _swa_kernel(sink_ref, q_ref, kp_ref, kc_ref, vp_ref, vc_ref, o_ref, *, j0, n_kv, group, hd, win):
    j = pl.program_id(1) + j0
    q = q_ref[0]
    k2 = jnp.concatenate([kp_ref[0], kc_ref[0]], axis=0)
    v2 = jnp.concatenate([vp_ref[0], vc_ref[0]], axis=0)
    tq = q.shape[0]
    qi = lax.broadcasted_iota(jnp.int32, (tq, 2 * win), 0)
    kj = lax.broadcasted_iota(jnp.int32, (tq, 2 * win), 1)
    dist = qi + win - kj
    mask = (dist >= 0) & (dist < win) & ((j > 0) | (kj >= win))
    scale = hd ** -0.5
    for kvh in range(n_kv):
        kh = k2[:, kvh * hd:(kvh + 1) * hd]
        vh = v2[:, kvh * hd:(kvh + 1) * hd]
        hs_ = [kvh * group + g for g in range(group)]
        s = [jnp.where(mask, _nt(q[:, h * hd:(h + 1) * hd], kh) * scale, -jnp.inf) for h in hs_]
        m = [jnp.maximum(jnp.max(s[g], axis=-1, keepdims=True), sink_ref[hs_[g]]) for g in range(group)]
        p = [jnp.exp(s[g] - m[g]) for g in range(group)]
        den = [jnp.sum(p[g], axis=-1, keepdims=True) + jnp.exp(sink_ref[hs_[g]] - m[g]) for g in range(group)]
        outs = [_dot(p[g].astype(BF16), vh) / den[g] for g in range(group)]
        o_ref[0, :, hs_[0] * hd:(hs_[-1] + 1) * hd] = jnp.concatenate(outs, axis=-1).astype(o_ref.dtype)


def swa_attention(q, k, v, sink, *, j0, win, hd, tq=None):
    B, Tq, QW = q.shape
    KW = k.shape[2]
    n_kv = KW // hd
    group = QW // KW
    tq = win if tq is None else tq
    nq = Tq // tq
    kern = functools.partial(_swa_kernel, j0=j0, n_kv=n_kv, group=group, hd=hd, win=win)
    prev = lambda b, j: (b, jnp.maximum(j + j0 - 1, 0), 0)
    cur = lambda b, j: (b, j + j0, 0)
    return pl.pallas_call(
        kern,
        grid=(B, nq),
        in_specs=[pl.BlockSpec(memory_space=pltpu.SMEM),
                  pl.BlockSpec((1, tq, QW), lambda b, j: (b, j, 0)),
                  pl.BlockSpec((1, win, KW), prev), pl.BlockSpec((1, win, KW), cur),
                  pl.BlockSpec((1, win, KW), prev), pl.BlockSpec((1, win, KW), cur)],
        out_specs=pl.BlockSpec((1, tq, QW), lambda b, j: (b, j, 0)),
        out_shape=jax.ShapeDtypeStruct((B, Tq, QW), BF16),
        compiler_params=_params("parallel", "parallel"),
        name="swa_attention",
    )(sink.astype(F32), q, k, k, v, v)


def _mem_attn_kernel(q_ref, g_ref, mk_ref, mv_ref, o_ref, *, n_heads, hd):
    q = q_ref[0]
    scale = hd ** -0.5
    for h in range(n_heads):
        sl = slice(h * hd, (h + 1) * hd)
        qh = q[:, sl]
        qn = (qh * lax.rsqrt(jnp.mean(qh * qh, axis=-1, keepdims=True) + NORM_EPS) * g_ref[...]).astype(BF16)
        s = _nt(qn, mk_ref[0, :, sl]) * scale
        m = jnp.max(s, axis=-1, keepdims=True)
        p = jnp.exp(s - m)
        p = p / jnp.sum(p, axis=-1, keepdims=True)
        o_ref[0, :, sl] = _dot(p.astype(BF16), mv_ref[0, :, sl]).astype(o_ref.dtype)


def mem_attention(q, qn_g, mk, mv, *, n_heads, tt=256):
    B, T, MW = q.shape
    ML = mk.shape[1]
    hd = MW // n_heads
    tt = min(tt, T)
    return pl.pallas_call(
        functools.partial(_mem_attn_kernel, n_heads=n_heads, hd=hd),
        grid=(B, T // tt),
        in_specs=[pl.BlockSpec((1, tt, MW), lambda b, t: (b, t, 0)),
                  pl.BlockSpec((1, hd), lambda b, t: (0, 0)),
                  pl.BlockSpec((1, ML, MW), lambda b, t: (b, 0, 0)),
                  pl.BlockSpec((1, ML, MW), lambda b, t: (b, 0, 0))],
        out_specs=pl.BlockSpec((1, tt, MW), lambda b, t: (b, t, 0)),
        out_shape=jax.ShapeDtypeStruct((B, T, MW), BF16),
        compiler_params=_params("parallel", "parallel"),
        name="mem_attention",
    )(q, qn_g.reshape(1, hd).astype(F32), mk, mv)


def _ffn_act_kernel(g_ref, halo_ref, up_ref, cw_ref, cb_ref, o_ref):
    t = pl.program_id(2)
    g = g_ref[0]
    prev = halo_ref[0] * jnp.where(t > 0, 1.0, 0.0)
    gc = cb_ref[...] + cw_ref[2:3, :] * g
    for s in (1, 2):
        gc = gc + cw_ref[2 - s:3 - s, :] * _shift_rows(g, prev, s)
    o_ref[0] = (jax.nn.gelu(gc, approximate=True) * up_ref[0]).astype(o_ref.dtype)


def ffn_act(gate, gate_cb0, up, up_cb0, conv_w, conv_b, *, tt=256, tc=512):
    B, T, _ = gate.shape
    C = conv_w.shape[1]
    tt = min(tt, T)
    hb = tt // SUBLANES
    return pl.pallas_call(
        _ffn_act_kernel,
        grid=(B, C // tc, T // tt),
        in_specs=[pl.BlockSpec((1, tt, tc), lambda b, c, t: (b, t, gate_cb0 + c)),
                  pl.BlockSpec((1, SUBLANES, tc), lambda b, c, t: (b, jnp.maximum(t * hb - 1, 0), gate_cb0 + c)),
                  pl.BlockSpec((1, tt, tc), lambda b, c, t: (b, t, up_cb0 + c)),
                  pl.BlockSpec((3, tc), lambda b, c, t: (0, c)),
                  pl.BlockSpec((1, tc), lambda b, c, t: (0, c))],
        out_specs=pl.BlockSpec((1, tt, tc), lambda b, c, t: (b, t, c)),
        out_shape=jax.ShapeDtypeStruct((B, T, C), BF16),
        compiler_params=_params("parallel", "parallel", "parallel"),
        name="ffn_act",
    )(gate, gate, up, conv_w.astype(F32), conv_b.reshape(1, C).astype(F32))


def _ffn_in_kernel(*refs, blocks_per_seq, sub, has_into, emit_w):
    x_ref, wg_ref, wu_ref, cw_ref, cb_ref = refs[:5]
    pos = 6 if has_into else 5
    o_ref, tail_ref = refs[pos], refs[pos + 1]
    carry_scr = refs[-1]
    i = pl.program_id(0)
    j = pl.program_id(1)
    tm = x_ref.shape[0]

    @pl.when(i % blocks_per_seq == 0)
    def _():
        carry_scr[j] = jnp.zeros(carry_scr.shape[1:], F32)

    prev = carry_scr[j]
    wg = wg_ref[...].astype(BF16)
    wu = wu_ref[...].astype(BF16)
    if emit_w:
        refs[pos + 2][...] = wg
        refs[pos + 3][...] = wu
    for c in range(tm // sub):
        rows = slice(c * sub, (c + 1) * sub)
        x = x_ref[rows, :]
        gate = _dot(x, wg)
        up = _dot(x, wu)
        gc = cb_ref[...] + cw_ref[2:3, :] * gate
        for s in (1, 2):
            gc = gc + cw_ref[2 - s:3 - s, :] * _shift_rows(gate, prev, s)
        o_ref[rows, :] = (jax.nn.gelu(gc, approximate=True) * up).astype(o_ref.dtype)
        prev = gate[sub - SUBLANES:, :]
    carry_scr[j] = prev
    tail_ref[0] = prev


def _ffn_in_call(x, wg, wu, layer, up_off, conv_w, conv_b, *, seq_len, row0, rows, tm, tn, sub, into, emit_w):
    M, D = x.shape
    FF = conv_w.shape[1]
    tm = min(tm, seq_len)
    bps = seq_len // tm
    nj = FF // tn
    i0 = row0 // tm
    uo = up_off // tn
    wspec = lambda off: (pl.BlockSpec((D, tn), lambda i, j: (0, j + off)) if layer is None else
                         pl.BlockSpec((None, D, tn), lambda i, j: (layer, 0, j + off)))
    in_specs = [pl.BlockSpec((tm, D), lambda i, j: (i + i0, 0), pipeline_mode=pl.Buffered(1)),
                wspec(0), wspec(uo),
                pl.BlockSpec((3, tn), lambda i, j: (0, j)),
                pl.BlockSpec((1, tn), lambda i, j: (0, j))]
    args = [x, wg, wu, conv_w.astype(F32), conv_b.reshape(1, FF).astype(F32)]
    aliases = {}
    if into is not None:
        aliases = {len(args): 0}
        in_specs.append(pl.BlockSpec(memory_space=pl.ANY))
        args.append(into)
    out_specs = [pl.BlockSpec((tm, tn), lambda i, j: (i + i0, j)),
                 pl.BlockSpec((1, SUBLANES, tn), lambda i, j: (i, 0, j))]
    out_shape = [jax.ShapeDtypeStruct((M, FF), BF16),
                 jax.ShapeDtypeStruct((rows // tm, SUBLANES, FF), F32)]
    if emit_w:
        out_specs += [pl.BlockSpec((D, tn), lambda i, j: (0, j))] * 2
        out_shape += [jax.ShapeDtypeStruct((D, FF), BF16)] * 2
    outs = pl.pallas_call(
        functools.partial(_ffn_in_kernel, blocks_per_seq=bps, sub=min(sub, tm), has_into=into is not None,
                          emit_w=emit_w),
        grid=(rows // tm, nj),
        in_specs=in_specs,
        out_specs=out_specs,
        out_shape=out_shape,
        input_output_aliases=aliases,
        scratch_shapes=[pltpu.VMEM((nj, SUBLANES, tn), F32)],
        compiler_params=_params("arbitrary", "arbitrary"),
        name="ffn_in_fused",
    )(*args)
    return (outs[0], outs[1][bps - 1::bps]) + tuple(outs[2:])


def ffn_in_fused(x, w_in, layer, conv_w, conv_b, wcache, key, *, seq_len, sub=256):
    M = x.shape[0]
    FF = conv_w.shape[1]
    act, tail0, wg, wu = _ffn_in_call(x, w_in, w_in, layer, FF, conv_w, conv_b, seq_len=seq_len, row0=0,
                                      rows=seq_len, tm=2048, tn=256, sub=sub, into=None, emit_w=True)
    wcache[key] = (wg, wu)
    if M == seq_len:
        return act, tail0
    act, tails = _ffn_in_call(x, wg, wu, None, 0, conv_w, conv_b, seq_len=seq_len, row0=seq_len,
                              rows=M - seq_len, tm=1024, tn=512, sub=sub, into=act, emit_w=False)
    return act, jnp.concatenate([tail0, tails], axis=0)


def _rwkv_mix_kernel(x_ref, halo_ref, s0_ref, g_ref, mu_ref, *out_refs, n_valid_last):
    t = pl.program_id(1)
    nt = pl.num_programs(1)
    o_refs, hl_ref = out_refs[:6], out_refs[6]

    def norm(v):
        return v * lax.rsqrt(jnp.mean(v * v, axis=-1, keepdims=True) + NORM_EPS) * g_ref[...]

    h = norm(x_ref[0])
    hp = norm(halo_ref[0])
    first = jnp.where(t > 0, 1.0, 0.0)
    hp = hp * first + jnp.broadcast_to(s0_ref[0], hp.shape) * (1.0 - first)
    xx = _shift_rows(h, hp, 1) - h
    for j in range(6):
        o_refs[j][0] = (h + xx * mu_ref[j:j + 1, :]).astype(o_refs[j].dtype)

    @pl.when(t == nt - 1)
    def _():
        hl_ref[0] = h[n_valid_last - 1:n_valid_last, :]


def rwkv_mix(x, shift0, g, mu, *, n_valid, tt=256):
    B, T, D = x.shape
    tt = min(tt, T)
    hb = tt // SUBLANES
    n_valid_last = n_valid - (T // tt - 1) * tt
    blk = pl.BlockSpec((1, tt, D), lambda b, t: (b, t, 0))
    one = pl.BlockSpec((1, 1, D), lambda b, t: (b, 0, 0))
    return pl.pallas_call(
        functools.partial(_rwkv_mix_kernel, n_valid_last=n_valid_last),
        grid=(B, T // tt),
        in_specs=[blk,
                  pl.BlockSpec((1, SUBLANES, D), lambda b, t: (b, jnp.maximum(t * hb - 1, 0), 0)),
                  one,
                  pl.BlockSpec((1, D), lambda b, t: (0, 0)),
                  pl.BlockSpec((6, D), lambda b, t: (0, 0))],
        out_specs=[blk] * 6 + [one],
        out_shape=[jax.ShapeDtypeStruct((B, T, D), BF16)] * 6 + [jax.ShapeDtypeStruct((B, 1, D), F32)],
        compiler_params=_params("parallel", "arbitrary"),
        name="rwkv_mix",
    )(x, x, shift0, g.reshape(1, D).astype(F32), mu.astype(F32))


def _rwkv_chunk_kernel(r_ref, k_ref, v_ref, wp_ref, ap_ref, g_ref, kk_ref, ka_ref, rk_ref, lng_ref, lnb_ref,
                       s0_ref, o_ref, so_ref, s_scr, *, hs):
    c = pl.program_id(2)
    nc = pl.num_programs(2)
    _, C, L = r_ref.shape
    n_pairs = L // LANES

    @pl.when(c == 0)
    def _():
        s_scr[...] = s0_ref[0]

    r = r_ref[0]
    k = k_ref[0]
    v = v_ref[0]
    logw = -jnp.exp(-jax.nn.softplus(-wp_ref[0]) - 0.5)
    a = jax.nn.sigmoid(ap_ref[0])
    kk = k * kk_ref[...]
    kk = kk / jnp.maximum(jnp.sqrt(_group_sum_bcast(kk * kk, hs)), 1e-12)
    k2 = k * (1.0 + (a - 1.0) * ka_ref[...])
    bm = kk * a

    ti = lax.broadcasted_iota(jnp.int32, (C, C), 0)
    si = lax.broadcasted_iota(jnp.int32, (C, C), 1)
    tri = jnp.where(ti >= si, 1.0, 0.0).astype(BF16)
    cum = _dot_exact_rhs_lhs(tri, logw)
    e_in = jnp.exp(cum)
    e_out = jnp.exp(-cum)
    e_end = jnp.exp(cum[C - 1:C, :] - cum)
    g_end = jnp.exp(cum[C - 1:C, :])
    rt = r * e_in
    kkt = kk * jnp.exp(cum - logw)
    bh = bm * e_out
    kh = k2 * e_out
    bbar = bm * e_end
    kbar = k2 * e_end

    lane = lax.broadcasted_iota(jnp.int32, (1, LANES), 1)
    m0 = lane < hs
    C2 = 2 * C
    ri = lax.broadcasted_iota(jnp.int32, (C2, C2), 0)
    ci = lax.broadcasted_iota(jnp.int32, (C2, C2), 1)
    same = (ri // C) == (ci // C)
    strict = same & ((ri % C) > (ci % C))
    lower = (ri % C) >= (ci % C)
    vi = lax.broadcasted_iota(jnp.int32, (LANES, LANES), 0) // hs
    vj = lax.broadcasted_iota(jnp.int32, (LANES, LANES), 1) // hs
    blockdiag = vi == vj

    def stack_heads(x):
        return jnp.concatenate([jnp.where(m0, x, 0.0), jnp.where(m0, 0.0, x)], axis=0).astype(BF16)

    n_steps = int(math.log2(C))
    P = range(n_pairs)
    sls = [slice(p * LANES, (p + 1) * LANES) for p in P]
    S = [s_scr[p] for p in P]
    S16 = [s.astype(BF16) for s in S]
    V16 = [v[:, sl].astype(BF16) for sl in sls]
    lhk = [stack_heads(kkt[:, sl]) for sl in sls]
    wide = C2 % LANES == 0
    if wide:
        gbk = [_nt(lhk[p], jnp.concatenate([stack_heads(bh[:, sls[p]]), stack_heads(kh[:, sls[p]])], axis=0))
               for p in P]
        nmat = [jnp.where(strict, -g[:, :C2], 0.0) for g in gbk]
        auk = [jnp.where(strict, -g[:, C2:], 0.0).astype(BF16) for g in gbk]
    else:
        nmat = [jnp.where(strict, -_nt(lhk[p], stack_heads(bh[:, sls[p]])), 0.0) for p in P]
        auk = [jnp.where(strict, -_nt(lhk[p], stack_heads(kh[:, sls[p]])), 0.0).astype(BF16) for p in P]
    sprod = [_nt(jnp.concatenate([lhk[p], rt[:, sls[p]].astype(BF16)], axis=0), S16[p]) for p in P]
    u = [_dot(auk[p], jnp.concatenate([V16[p], V16[p]], axis=0)) - sprod[p][:C2] for p in P]
    npow = [n.astype(BF16) for n in nmat]
    for it in range(n_steps):
        last = it + 1 == n_steps
        if wide and not last:
            res = [_dot(npow[p], jnp.concatenate([u[p].astype(BF16), npow[p]], axis=1)) for p in P]
            u = [u[p] + res[p][:, :LANES] for p in P]
            npow = [res[p][:, LANES:].astype(BF16) for p in P]
        else:
            u = [u[p] + _dot(npow[p], u[p].astype(BF16)) for p in P]
            if not last:
                npow = [_dot(npow[p], npow[p]).astype(BF16) for p in P]
    uv = [jnp.concatenate([jnp.where(m0, u[p][:C], u[p][C:]).astype(BF16), V16[p]], axis=0) for p in P]
    ar = [jnp.where(lower, _nt(stack_heads(rt[:, sls[p]]),
                               jnp.concatenate([bh[:, sls[p]], kh[:, sls[p]]], axis=0).astype(BF16)), 0.0).astype(BF16)
          for p in P]
    tmat = [_dot(ar[p], uv[p]) for p in P]
    ys = [sprod[p][C2:] + jnp.where(m0, tmat[p][:C], tmat[p][C:]) for p in P]
    upd = [_tn(uv[p], jnp.concatenate([bbar[:, sls[p]], kbar[:, sls[p]]], axis=0).astype(BF16)) for p in P]
    for p in P:
        s_scr[p] = S[p] * g_end[:, sls[p]] + jnp.where(blockdiag, upd[p], 0.0)

    y = ys[0] if n_pairs == 1 else jnp.concatenate(ys, axis=-1)
    inv = 1.0 / hs
    mean = _group_sum_bcast(y, hs) * inv
    yc = y - mean
    var = _group_sum_bcast(yc * yc, hs) * inv
    yn = yc * lax.rsqrt(var + RW_GN_EPS) * lng_ref[...] + lnb_ref[...]
    bonus = _group_sum_bcast(r * k2 * rk_ref[...], hs) * v
    o_ref[0] = ((yn + bonus) * g_ref[0]).astype(o_ref.dtype)

    @pl.when(c == nc - 1)
    def _():
        so_ref[0] = s_scr[...]


def _dot_exact_rhs_lhs(m_bf16, x):
    hi, mid, lo = _split3(x)
    return _dot(m_bf16, hi) + _dot(m_bf16, mid) + _dot(m_bf16, lo)


def rwkv_chunked(r, k, v, wp, ap, g, k_k, k_a, r_k, ln_g, ln_b, s0, *, hs, chunk, lanes_per_step=1024):
    B, T, D = r.shape
    L = min(lanes_per_step, D)
    npg = L // LANES
    seq = pl.BlockSpec((1, chunk, L), lambda b, hg, c: (b, c, hg))
    par = pl.BlockSpec((1, L), lambda b, hg, c: (0, hg))
    st = pl.BlockSpec((1, npg, LANES, LANES), lambda b, hg, c: (b, hg, 0, 0))
    row = lambda x: x.reshape(1, D).astype(F32)
    return pl.pallas_call(
        functools.partial(_rwkv_chunk_kernel, hs=hs),
        grid=(B, D // L, T // chunk),
        in_specs=[seq] * 6 + [par] * 5 + [st],
        out_specs=[seq, st],
        out_shape=[jax.ShapeDtypeStruct((B, T, D), BF16),
                   jax.ShapeDtypeStruct(s0.shape, F32)],
        scratch_shapes=[pltpu.VMEM((npg, LANES, LANES), F32)],
        compiler_params=_params("parallel", "parallel", "arbitrary"),
        name="rwkv_chunked",
    )(r, k, v, wp, ap, g, row(k_k), row(k_a), row(r_k), row(ln_g), row(ln_b), s0)


def _pairs_from_heads(s, hs):
    B, H = s.shape[:2]
    s = s.reshape(B, H // 2, 2, hs, hs)
    z = jnp.zeros_like(s[:, :, 0])
    top = jnp.concatenate([s[:, :, 0], z], axis=-1)
    bot = jnp.concatenate([z, s[:, :, 1]], axis=-1)
    return jnp.concatenate([top, bot], axis=-2)


def _heads_from_pairs(sp, hs):
    B, P = sp.shape[:2]
    return jnp.stack([sp[:, :, :hs, :hs], sp[:, :, hs:, hs:]], axis=2).reshape(B, 2 * P, hs, hs)


def _pad_rows(x, front, back):
    return jnp.pad(x, ((0, 0), (front, back), (0, 0)))


def kernel(x_prompt, x_sample, mem_prompt, state_lru_conv, state_lru_h, cache_swa_k, cache_swa_v,
           state_rwkv_shift, state_rwkv_wkv, cache_mem_k, cache_mem_v, state_ffn_conv,
           a_norm_g, a_w_in, a_conv_w, a_conv_b, a_gate_a_w, a_gate_a_b, a_gate_x_w, a_gate_x_b,
           a_lambda, b_q_norm_g, b_k_norm_g, b_sink, a_w_out,
           c_norm_g, c_mu, c_w_r, c_w_k, c_w_v, c_w_o, c_w0, c_w1, c_w2, c_a0, c_a1, c_a2,
           c_g1, c_g2, c_k_k, c_k_a, c_r_k, c_ln_g, c_ln_b,
           m_norm_g, m_mem_norm_g, m_w_q, m_w_kv, m_q_norm_g, m_k_norm_g, m_w_o,
           f_norm_g, f_w_in, f_conv_w, f_conv_b, f_w_out):
    D = x_prompt.shape[-1]
    depth = m_norm_g.shape[0]
    W = a_conv_w.shape[-1]
    KA = a_conv_w.shape[1]
    hd = b_q_norm_g.shape[-1]
    n_q = b_sink.shape[-1]
    n_kv = cache_swa_k.shape[3]
    win = cache_swa_k.shape[2]
    QW, KW = n_q * hd, n_kv * hd
    hs = c_r_k.shape[-1]
    n_rw = c_r_k.shape[1]
    m_heads, m_hd = cache_mem_k.shape[3], cache_mem_k.shape[4]
    MW = m_heads * m_hd
    FF = f_conv_w.shape[-1]
    KF = f_conv_w.shape[1]
    bf = lambda w: w.astype(BF16)
    wcache = {}

    def run(x, T_real, lru_conv, lru_h, swa_k, swa_v, rw_shift, rw_wkv, mem_k, mem_v, ffn_conv, prompt):
        B = x.shape[0]
        T = T_real
        M = B * T
        xf = x.reshape(M, D)
        outs = {}
        ia = ic = 0

        def mm(xin, w, layer, key, **kw):
            if prompt:
                return dense(xin, w, layer, wcache, (key, layer), **kw)
            return matmul(xin, wcache[(key, layer)], **{a: b for a, b in kw.items() if a not in ("n0", "n")})
        for l in range(depth):
            if l % 2 == 0:
                i = ia
                ia += 1
                h = rmsnorm_rows(xf, a_norm_g[i])
                zz = mm(h, a_w_in, i, "a_w_in_rg", n=2 * W)
                qkv = mm(h, a_w_in, i, "a_w_in_qkv", n0=2 * W)
                qn = headnorm(qkv, 0, QW, b_q_norm_g[i], hd, BF16)
                kn = headnorm(qkv, QW // KW, KW, b_k_norm_g[i], hd, F32)
                vv = qkv[:, QW + KW:]
                zz3 = zz.reshape(B, T, 2 * W)
                if prompt:
                    out_a, h_last = lru_mixer(zz3, 0, zz3, 1, jnp.zeros((B, 1, W), F32),
                                              a_conv_w[i], a_conv_b[i], a_gate_a_w[i], a_gate_a_b[i],
                                              a_gate_x_w[i], a_gate_x_b[i], a_lambda[i], n_pad=0, tt=256)
                    conv_new = zz3[:, T - (KA - 1):, :W]
                    o = swa_attention(qn.reshape(B, T, QW), bf(kn).reshape(B, T, KW),
                                      bf(vv).reshape(B, T, KW), b_sink[i], j0=0, win=win, hd=hd)
                    k_new = kn.reshape(B, T, n_kv, hd)[:, T - win:]
                    v_new = vv.reshape(B, T, n_kv, hd)[:, T - win:]
                else:
                    n_pad = SUBLANES - T
                    xr_hist = jnp.concatenate([lru_conv[i].astype(F32), zz3[:, :, :W]], axis=1)
                    xr_p = _pad_rows(xr_hist, SUBLANES - xr_hist.shape[1], 0)
                    yg_p = _pad_rows(zz3[:, :, W:], n_pad, 0)
                    out_a, h_last = lru_mixer(xr_p, 0, yg_p, 0, lru_h[i].reshape(B, 1, W).astype(F32),
                                              a_conv_w[i], a_conv_b[i], a_gate_a_w[i], a_gate_a_b[i],
                                              a_gate_x_w[i], a_gate_x_b[i], a_lambda[i], n_pad=n_pad, tt=SUBLANES)
                    out_a = out_a[:, n_pad:]
                    conv_new = xr_hist[:, T:]
                    kc = swa_k[i].reshape(B, win, KW)
                    vc = swa_v[i].reshape(B, win, KW)
                    k_all = jnp.concatenate([kc, kn.reshape(B, T, KW)], axis=1)
                    v_all = jnp.concatenate([vc, vv.reshape(B, T, KW)], axis=1)
                    o = swa_attention(_pad_rows(qn.reshape(B, T, QW), 0, 2 * SUBLANES - T),
                                      bf(_pad_rows(k_all, 0, win - T)), bf(_pad_rows(v_all, 0, win - T)),
                                      b_sink[i], j0=1, win=win, hd=hd, tq=2 * SUBLANES)[:, :T]
                    k_new = k_all[:, T:].reshape(B, win, n_kv, hd)
                    v_new = v_all[:, T:].reshape(B, win, n_kv, hd)
                mix = jnp.concatenate([out_a.reshape(M, W), o.reshape(M, QW)], axis=-1)
                xf = mm(mix, a_w_out, i, "a_w_out", res=xf)
                outs.setdefault("lc", []).append(conv_new)
                outs.setdefault("lh", []).append(h_last.reshape(B, W))
                outs.setdefault("sk", []).append(k_new)
                outs.setdefault("sv", []).append(v_new)
            else:
                i = ic
                ic += 1
                Tp = T if prompt else SUBLANES
                chunk = 64 if prompt else 16
                x3 = xf.reshape(B, T, D)
                if not prompt:
                    x3 = _pad_rows(x3, 0, Tp - T)
                mixes = rwkv_mix(x3, rw_shift[i].reshape(B, 1, D).astype(F32), c_norm_g[i], c_mu[i], n_valid=T)
                xr, xw, xk, xv, xa, xg = [m.reshape(B * Tp, D) for m in mixes[:6]]
                sh_new = mixes[6].reshape(B, D)
                r = mm(xr, c_w_r, i, "c_w_r")
                k = mm(xk, c_w_k, i, "c_w_k")
                v = mm(xv, c_w_v, i, "c_w_v")
                wp = matmul(matmul(xw, c_w1, layer=i, act="tanh", out_dtype=BF16), c_w2, layer=i, bias=c_w0[i])
                ap = matmul(matmul(xa, c_a1, layer=i, out_dtype=BF16), c_a2, layer=i, bias=c_a0[i])
                gl = c_g1[i].shape[1]
                glp = -(-gl // LANES) * LANES
                g1p = jnp.pad(c_g1[i], ((0, 0), (0, glp - gl)))
                g2p = jnp.pad(c_g2[i], ((0, glp - gl), (0, 0)))
                g = matmul(matmul(xg, g1p, act="sigmoid", out_dtype=BF16), g2p)
                if not prompt:
                    sq = lambda t, fill=0.0: jnp.pad(t.reshape(B, Tp, D)[:, :T], ((0, 0), (0, chunk - T), (0, 0)),
                                                     constant_values=fill)
                    r, k, v, ap, g = sq(r), sq(k), sq(v), sq(ap), sq(g)
                    wp = sq(wp, -1e30)
                else:
                    sq = lambda t: t.reshape(B, Tp, D)
                    r, k, v, wp, ap, g = sq(r), sq(k), sq(v), sq(wp), sq(ap), sq(g)
                s0 = _pairs_from_heads(rw_wkv[i].astype(F32), hs)
                yo, s_end = rwkv_chunked(r, k, v, wp, ap, g, c_k_k[i], c_k_a[i], c_r_k[i], c_ln_g[i], c_ln_b[i],
                                         s0, hs=hs, chunk=chunk)
                yo = yo[:, :T].reshape(M, D)
                xf = mm(yo, c_w_o, i, "c_w_o", res=xf)
                outs.setdefault("rs", []).append(sh_new)
                outs.setdefault("rw", []).append(_heads_from_pairs(s_end, hs))
            hm = rmsnorm_rows(xf, m_norm_g[l])
            q = mm(hm, m_w_q, l, "m_w_q")
            Tq = T if prompt else SUBLANES
            q3 = q.reshape(B, T, MW)
            if not prompt:
                q3 = _pad_rows(q3, 0, Tq - T)
            ML = mem_k[l].shape[1]
            om = mem_attention(q3, m_q_norm_g[l], bf(mem_k[l]).reshape(B, ML, MW), bf(mem_v[l]).reshape(B, ML, MW),
                               n_heads=m_heads)
            om = om[:, :T].reshape(M, MW)
            xf = mm(om, m_w_o, l, "m_w_o", res=xf)
            hf = rmsnorm_rows(xf, f_norm_g[l])
            if prompt:
                act, tail = ffn_in_fused(hf, f_w_in, l, f_conv_w[l], f_conv_b[l], wcache, ("f_w_in", l), seq_len=T)
                f_new = tail[:, SUBLANES - (KF - 1):, :]
            else:
                wg, wu = wcache[("f_w_in", l)]
                gate3 = matmul(hf, wg, tn=1024).reshape(B, T, FF)
                up3 = matmul(hf, wu, tn=1024).reshape(B, T, FF)
                g_hist = jnp.concatenate([ffn_conv[l].astype(F32), gate3], axis=1)
                g_p = _pad_rows(g_hist, SUBLANES - g_hist.shape[1], 0)
                up_p = _pad_rows(up3, SUBLANES - T, 0)
                act = ffn_act(g_p, 0, up_p, 0, f_conv_w[l], f_conv_b[l], tt=SUBLANES, tc=FF)[:, SUBLANES - T:]
                f_new = g_hist[:, T:]
            xf = mm(act.reshape(M, FF), f_w_out, l, "f_w_out", res=xf, tn=1024, tk=2048)
            outs.setdefault("fc", []).append(f_new)
        st = lambda name: jnp.stack(outs[name])
        return (xf.reshape(B, T, D), st("lc"), st("lh"), st("sk"), st("sv"), st("rs"), st("rw"), st("fc"))

    Bp, S = x_prompt.shape[:2]
    ML = mem_prompt.shape[1]
    mem_flat = mem_prompt.reshape(Bp * ML, D)
    mks, mvs = [], []
    for l in range(depth):
        mn = rmsnorm_rows(mem_flat, m_mem_norm_g[l])
        kv = matmul(mn, m_w_kv, layer=l)
        mk = headnorm(kv, 0, MW, m_k_norm_g[l], m_hd, F32)
        mks.append(mk.reshape(Bp, ML, m_heads, m_hd))
        mvs.append(kv[:, MW:].reshape(Bp, ML, m_heads, m_hd))
    p_mem_k = jnp.stack(mks)
    p_mem_v = jnp.stack(mvs)

    n_a = a_norm_g.shape[0]
    n_c = c_norm_g.shape[0]
    zeros = lambda *s: jnp.zeros(s, F32)
    y_p, p_lc, p_lh, p_sk, p_sv, p_rs, p_rw, p_fc = run(
        x_prompt, S, None, None, None, None, zeros(n_c, Bp, D), zeros(n_c, Bp, n_rw, hs, hs),
        p_mem_k, p_mem_v, None, True)
    Bs, Ts = x_sample.shape[:2]
    y_s, s_lc, s_lh, s_sk, s_sv, s_rs, s_rw, s_fc = run(
        x_sample, Ts, state_lru_conv, state_lru_h, cache_swa_k, cache_swa_v,
        state_rwkv_shift, state_rwkv_wkv, cache_mem_k, cache_mem_v, state_ffn_conv, False)
    return (y_p, y_s, p_lc, p_lh, p_sk, p_sv, p_rs, p_rw, p_mem_k, p_mem_v, p_fc,
            s_lc, s_lh, s_sk, s_sv, s_rs, s_rw, s_fc)
```

```python
import functools
import math

import jax
import jax.numpy as jnp
from jax import lax
from jax.experimental import pallas as pl
from jax.experimental.pallas import tpu as pltpu

F32 = jnp.float32
BF16 = jnp.bfloat16

NORM_EPS = 1e-6
RW_GN_EPS = 64e-5
LRU_C = 8.0
LANES = 128
SUBLANES = 8
VMEM_LIMIT_BYTES = 56 * 1024 * 1024


def _params(*sem):
    return pltpu.CompilerParams(dimension_semantics=sem, vmem_limit_bytes=VMEM_LIMIT_BYTES)


def _nt(a, b):
    return lax.dot_general(a, b, (((1,), (1,)), ((), ())), preferred_element_type=F32)


def _tn(a, b):
    return lax.dot_general(a, b, (((0,), (0,)), ((), ())), preferred_element_type=F32)


def _dot(a, b):
    return jnp.dot(a, b, preferred_element_type=F32)


def _split3(x):
    hi = x.astype(BF16)
    r1 = x - hi.astype(F32)
    mid = r1.astype(BF16)
    lo = (r1 - mid.astype(F32)).astype(BF16)
    return hi, mid, lo


def _group_sum_bcast(x, width):
    m, L = x.shape
    n = L // LANES
    if width == LANES:
        parts = []
        for c in range(n):
            s = jnp.sum(x[:, c * LANES:(c + 1) * LANES], axis=-1, keepdims=True)
            parts.append(jnp.broadcast_to(s, (m, LANES)))
        return parts[0] if n == 1 else jnp.concatenate(parts, axis=-1)
    li = lax.broadcasted_iota(jnp.int32, (LANES, LANES), 0) // width
    lj = lax.broadcasted_iota(jnp.int32, (LANES, LANES), 1) // width
    e = jnp.where(li == lj, 1.0, 0.0).astype(BF16)
    xs = x if n == 1 else jnp.concatenate([x[:, c * LANES:(c + 1) * LANES] for c in range(n)], axis=0)
    hi = xs.astype(BF16)
    lo = (xs - hi.astype(F32)).astype(BF16)
    out = _dot(hi, e) + _dot(lo, e)
    return out if n == 1 else jnp.concatenate([out[c * m:(c + 1) * m] for c in range(n)], axis=-1)


def _shift_rows(x, prev8, s):
    rolled = pltpu.roll(x, s, 0)
    top = jnp.where(lax.broadcasted_iota(jnp.int32, (SUBLANES, 1), 0) < s,
                    pltpu.roll(prev8, s, 0), rolled[0:SUBLANES])
    if x.shape[0] == SUBLANES:
        return top
    return jnp.concatenate([top, rolled[SUBLANES:]], axis=0)


def _rmsnorm_kernel(x_ref, g_ref, o_ref):
    x = x_ref[...]
    ms = jnp.mean(x * x, axis=-1, keepdims=True)
    o_ref[...] = (x * lax.rsqrt(ms + NORM_EPS) * g_ref[...]).astype(o_ref.dtype)


def _row_tile(m, target):
    best = None
    for d in range(16, min(m, target) + 1, 16):
        if m % d == 0:
            best = d
    return best or m


def rmsnorm_rows(x, g, out_dtype=BF16, tm=512):
    M, D = x.shape
    tm = _row_tile(M, tm)
    return pl.pallas_call(
        _rmsnorm_kernel,
        grid=(M // tm,),
        in_specs=[pl.BlockSpec((tm, D), lambda i: (i, 0)), pl.BlockSpec((1, D), lambda i: (0, 0))],
        out_specs=pl.BlockSpec((tm, D), lambda i: (i, 0)),
        out_shape=jax.ShapeDtypeStruct((M, D), out_dtype),
        compiler_params=_params("parallel"),
        name="rmsnorm_rows",
    )(x, g.reshape(1, D))


def _mm_kernel(*refs, nk, has_bias, has_res, has_into, emit_w, in_place, act):
    x_ref, w_ref = refs[0], refs[1]
    pos = 2
    b_ref = r_ref = wb_ref = acc_ref = None
    if has_bias:
        b_ref = refs[pos]
        pos += 1
    if has_res:
        r_ref = refs[pos]
        pos += 1
    if has_into:
        pos += 1
    o_ref = refs[pos]
    pos += 1
    if emit_w:
        wb_ref = refs[pos]
        pos += 1
    if nk > 1 and not in_place:
        acc_ref = refs[pos]

    def weights(cs=slice(None)):
        w16 = w_ref[:, cs].astype(BF16)
        if emit_w:
            wb_ref[:, cs] = w16
        return w16

    def epilogue(y):
        if has_bias:
            y = y + b_ref[...]
        if act == "tanh":
            y = jnp.tanh(y)
        elif act == "sigmoid":
            y = jax.nn.sigmoid(y)
        if has_res:
            y = y + r_ref[...]
        o_ref[...] = y.astype(o_ref.dtype)

    if nk > 1 and in_place:
        k = pl.program_id(2)
        tn = o_ref.shape[1]
        cw = min(tn, 2 * LANES)

        def sweep(first):
            x = x_ref[...]
            for c in range(tn // cw):
                cs = slice(c * cw, (c + 1) * cw)
                part = _dot(x, weights(cs))
                if not first:
                    o_ref[:, cs] += part
                elif has_res:
                    o_ref[:, cs] = part + r_ref[:, cs]
                else:
                    o_ref[:, cs] = part

        pl.when(k == 0)(functools.partial(sweep, True))
        pl.when(k > 0)(functools.partial(sweep, False))
        return

    part = _dot(x_ref[...], weights())
    if nk == 1:
        epilogue(part)
        return
    k = pl.program_id(2)

    @pl.when(k == 0)
    def _():
        acc_ref[...] = part

    @pl.when(k > 0)
    def _():
        acc_ref[...] += part

    @pl.when(k == nk - 1)
    def _():
        epilogue(acc_ref[...])


def _pick(n, prefs):
    for p in prefs:
        if n % p == 0:
            return p
    return n


def matmul(x, w, *, layer=None, n0=0, n=None, bias=None, res=None, act=None, out_dtype=F32,
           tm=1024, tn=512, tk=4096, row0=0, rows=None, into=None, emit_w=False, out_rows=None, out_row0=None):
    M, K = x.shape
    N = w.shape[-1] - n0 if n is None else n
    rows = M - row0 if rows is None else rows
    tm = min(tm, rows)
    out_rows = M if out_rows is None else out_rows
    o0 = (row0 if out_row0 is None else out_row0) // tm
    tn = _pick(math.gcd(N, n0) if n0 else N, (tn, 512, 256, 128))
    tk = _pick(K, (tk, 2048, 1024, 512))
    nk = K // tk
    j0 = n0 // tn
    i0 = row0 // tm
    assert row0 % tm == 0 and rows % tm == 0 and (not emit_w or rows == tm)
    in_specs = [pl.BlockSpec((tm, tk), lambda i, j, k: (i + i0, k)),
                pl.BlockSpec((tk, tn), lambda i, j, k: (k, j + j0)) if layer is None else
                pl.BlockSpec((None, tk, tn), lambda i, j, k: (layer, k, j + j0))]
    args = [x, w]
    if bias is not None:
        in_specs.append(pl.BlockSpec((1, tn), lambda i, j, k: (0, j)))
        args.append(bias.reshape(1, N).astype(F32))
    if res is not None:
        in_specs.append(pl.BlockSpec((tm, tn), lambda i, j, k: (i + i0, j)))
        args.append(res)
    aliases = {}
    if into is not None:
        aliases = {len(args): 0}
        in_specs.append(pl.BlockSpec(memory_space=pl.ANY))
        args.append(into)
    in_place = bias is None and act is None and out_dtype == F32
    kern = functools.partial(_mm_kernel, nk=nk, has_bias=bias is not None, has_res=res is not None,
                             has_into=into is not None, emit_w=emit_w, in_place=in_place, act=act)
    out_specs = [pl.BlockSpec((tm, tn), lambda i, j, k: (i + o0, j))]
    out_shape = [jax.ShapeDtypeStruct((out_rows, N), out_dtype)]
    if emit_w:
        out_specs.append(pl.BlockSpec((tk, tn), lambda i, j, k: (k, j)))
        out_shape.append(jax.ShapeDtypeStruct((K, N), BF16))
    outs = pl.pallas_call(
        kern,
        grid=(rows // tm, N // tn, nk),
        in_specs=in_specs,
        out_specs=out_specs,
        out_shape=out_shape,
        input_output_aliases=aliases,
        scratch_shapes=[pltpu.VMEM((tm, tn), F32)] if (nk > 1 and not in_place) else [],
        compiler_params=_params("parallel", "parallel", "arbitrary"),
        name="matmul",
    )(*args)
    return outs if emit_w else outs[0]


def dense(x, w, layer, wcache, key, **kw):
    rows = kw.pop("rows", x.shape[0])
    tm = min(kw.get("tm", 1024), rows)
    first, wb = matmul(x, w, layer=layer, rows=tm, emit_w=True, **kw)
    wcache[key] = wb
    if tm == rows:
        return first
    kw = {a: b for a, b in kw.items() if a not in ("n0", "n")}
    return matmul(x, wb, row0=tm, rows=rows - tm, into=first, **kw)


def _headnorm_kernel(x_ref, g_ref, o_ref, *, hd):
    x = x_ref[...]
    ms = _group_sum_bcast(x * x, hd) * (1.0 / hd)
    o_ref[...] = (x * lax.rsqrt(ms + NORM_EPS) * g_ref[...]).astype(o_ref.dtype)


def headnorm(x, col_block, width, g, hd, out_dtype, tm=512):
    M = x.shape[0]
    tm = _row_tile(M, tm)
    g_row = jnp.tile(g.astype(F32), width // hd).reshape(1, width)
    return pl.pallas_call(
        functools.partial(_headnorm_kernel, hd=hd),
        grid=(M // tm,),
        in_specs=[pl.BlockSpec((tm, width), lambda i: (i, col_block)),
                  pl.BlockSpec((1, width), lambda i: (0, 0))],
        out_specs=pl.BlockSpec((tm, width), lambda i: (i, 0)),
        out_shape=jax.ShapeDtypeStruct((M, width), out_dtype),
        compiler_params=_params("parallel"),
        name="headnorm",
    )(x, g_row)


def _lru_kernel(xr_ref, halo_ref, yg_ref, h0_ref, cw_ref, cb_ref, gaw_ref, gab_ref, gxw_ref, gxb_ref,
                lam_ref, o_ref, hl_ref, a_scr, u_scr, h_scr, *, n_pad, n_blocks, bw, scan_w):
    t = pl.program_id(1)
    tt, W = a_scr.shape

    @pl.when(t == 0)
    def _():
        h_scr[...] = h0_ref[0]

    x = xr_ref[...]
    prev = halo_ref[...] * jnp.where(t > 0, 1.0, 0.0)
    xc = cb_ref[...] + cw_ref[3:4, :] * x
    for s in (1, 2, 3):
        xc = xc + cw_ref[3 - s:4 - s, :] * _shift_rows(x, prev, s)

    nsp = -LRU_C * jax.nn.softplus(-lam_ref[...])
    if n_pad:
        live = (lax.broadcasted_iota(jnp.int32, (tt, 1), 0) >= n_pad) | (t > 0)
    for n in range(n_blocks):
        sl = slice(n * bw, (n + 1) * bw)
        xb = xc[:, sl]
        xb16 = xb.astype(BF16)
        r = jax.nn.sigmoid(_dot(xb16, gaw_ref[n]) + gab_ref[:, sl])
        i = jax.nn.sigmoid(_dot(xb16, gxw_ref[n]) + gxb_ref[:, sl])
        log_a = r * nsp[:, sl]
        a = jnp.exp(log_a)
        u = jnp.sqrt(jnp.maximum(-jnp.tanh(log_a) * (a * a + 1.0), 0.0)) * (i * xb)
        if n_pad:
            a = jnp.where(live, a, 1.0)
            u = jnp.where(live, u, 0.0)
        a_scr[:, sl] = a
        u_scr[:, sl] = u

    row8 = lax.broadcasted_iota(jnp.int32, (SUBLANES, 1), 0)
    for c in range(W // scan_w):
        cs = slice(c * scan_w, (c + 1) * scan_w)

        def body(gi, h, cs=cs):
            r0 = pl.multiple_of(gi * SUBLANES, SUBLANES)
            A = a_scr[pl.ds(r0, SUBLANES), cs]
            U = u_scr[pl.ds(r0, SUBLANES), cs]
            for s in (1, 2, 4):
                As = pltpu.roll(A, s, 0)
                Us = pltpu.roll(U, s, 0)
                m = row8 >= s
                U = jnp.where(m, A * Us + U, U)
                A = jnp.where(m, A * As, A)
            H = A * h + U
            u_scr[pl.ds(r0, SUBLANES), cs] = H
            return H[SUBLANES - 1:SUBLANES, :]

        h_end = lax.fori_loop(0, tt // SUBLANES, body, h_scr[:, cs])
        h_scr[:, cs] = h_end

    o_ref[...] = (u_scr[...] * jax.nn.gelu(yg_ref[...], approximate=True)).astype(o_ref.dtype)
    hl_ref[0] = h_scr[...]


def lru_mixer(xr, xr_cb, yg, yg_cb, h0, conv_w, conv_b, ga_w, ga_b, gx_w, gx_b, lam, *, B, T, n_pad, tt,
              out_rows=None, out_cols=None):
    W = conv_w.shape[1]
    nb, bw = ga_w.shape[0], ga_w.shape[1]
    tt = min(tt, T)
    nt = T // tt
    hb = tt // SUBLANES
    out_rows = B * T if out_rows is None else out_rows
    out_cols = W if out_cols is None else out_cols
    row = lambda v: v.reshape(1, W).astype(F32)
    kern = functools.partial(_lru_kernel, n_pad=n_pad, n_blocks=nb, bw=bw, scan_w=min(W, 512))
    return pl.pallas_call(
        kern,
        grid=(B, nt),
        in_specs=[
            pl.BlockSpec((tt, W), lambda b, t: (b * nt + t, xr_cb)),
            pl.BlockSpec((SUBLANES, W), lambda b, t: (jnp.maximum((b * nt + t) * hb - 1, 0), xr_cb)),
            pl.BlockSpec((tt, W), lambda b, t: (b * nt + t, yg_cb)),
            pl.BlockSpec((1, 1, W), lambda b, t: (b, 0, 0)),
            pl.BlockSpec((4, W), lambda b, t: (0, 0)),
            pl.BlockSpec((1, W), lambda b, t: (0, 0)),
            pl.BlockSpec((nb, bw, bw), lambda b, t: (0, 0, 0)),
            pl.BlockSpec((1, W), lambda b, t: (0, 0)),
            pl.BlockSpec((nb, bw, bw), lambda b, t: (0, 0, 0)),
            pl.BlockSpec((1, W), lambda b, t: (0, 0)),
            pl.BlockSpec((1, W), lambda b, t: (0, 0)),
        ],
        out_specs=[pl.BlockSpec((tt, W), lambda b, t: (b * nt + t, 0)),
                   pl.BlockSpec((1, 1, W), lambda b, t: (b, 0, 0))],
        out_shape=[jax.ShapeDtypeStruct((out_rows, out_cols), BF16), jax.ShapeDtypeStruct((B, 1, W), F32)],
        scratch_shapes=[pltpu.VMEM((tt, W), F32), pltpu.VMEM((tt, W), F32), pltpu.VMEM((1, W), F32)],
        compiler_params=_params("parallel", "arbitrary"),
        name="lru_mixer",
    )(xr, xr, yg, h0, conv_w.astype(F32), row(conv_b), ga_w.astype(BF16), row(ga_b),
      gx_w.astype(BF16), row(gx_b), row(lam))


def _swa_kernel(sink_ref, q_ref, kp_ref, kc_ref, vp_ref, vc_ref, *rest, j0, n_kv, group, hd, win):
    o_ref = rest[-1]
    j = pl.program_id(1) + j0
    q = q_ref[...]
    k2 = jnp.concatenate([kp_ref[...], kc_ref[...]], axis=0)
    v2 = jnp.concatenate([vp_ref[...], vc_ref[...]], axis=0)
    tq = q.shape[0]
    qi = lax.broadcasted_iota(jnp.int32, (tq, 2 * win), 0)
    kj = lax.broadcasted_iota(jnp.int32, (tq, 2 * win), 1)
    dist = qi + win - kj
    mask = (dist >= 0) & (dist < win) & ((j > 0) | (kj >= win))
    scale = hd ** -0.5
    for kvh in range(n_kv):
        kh = k2[:, kvh * hd:(kvh + 1) * hd]
        vh = v2[:, kvh * hd:(kvh + 1) * hd]
        hs_ = [kvh * group + g for g in range(group)]
        s = [jnp.where(mask, _nt(q[:, h * hd:(h + 1) * hd], kh) * scale, -jnp.inf) for h in hs_]
        m = [jnp.maximum(jnp.max(s[g], axis=-1, keepdims=True), sink_ref[hs_[g]]) for g in range(group)]
        p = [jnp.exp(s[g] - m[g]) for g in range(group)]
        den = [jnp.sum(p[g], axis=-1, keepdims=True) + jnp.exp(sink_ref[hs_[g]] - m[g]) for g in range(group)]
        outs = [_dot(p[g].astype(BF16), vh) / den[g] for g in range(group)]
        o_ref[:, hs_[0] * hd:(hs_[-1] + 1) * hd] = jnp.concatenate(outs, axis=-1).astype(o_ref.dtype)


def swa_attention(q, k, v, sink, *, B, nq, nkb, j0, win, hd, tq=None, into=None, out_cb=0):
    QW = q.shape[1]
    KW = k.shape[1]
    n_kv = KW // hd
    group = QW // KW
    tq = win if tq is None else tq
    kern = functools.partial(_swa_kernel, j0=j0, n_kv=n_kv, group=group, hd=hd, win=win)
    prev = lambda b, j: (b * nkb + jnp.maximum(j + j0 - 1, 0), 0)
    cur = lambda b, j: (b * nkb + j + j0, 0)
    in_specs = [pl.BlockSpec(memory_space=pltpu.SMEM),
                pl.BlockSpec((tq, QW), lambda b, j: (b * nq + j, 0)),
                pl.BlockSpec((win, KW), prev), pl.BlockSpec((win, KW), cur),
                pl.BlockSpec((win, KW), prev), pl.BlockSpec((win, KW), cur)]
    args = [sink.astype(F32), q, k, k, v, v]
    aliases = {}
    out_shape = jax.ShapeDtypeStruct((B * nq * tq, QW), BF16)
    if into is not None:
        aliases = {len(args): 0}
        in_specs.append(pl.BlockSpec(memory_space=pl.ANY))
        args.append(into)
        out_shape = jax.ShapeDtypeStruct(into.shape, into.dtype)
    return pl.pallas_call(
        kern,
        grid=(B, nq),
        in_specs=in_specs,
        out_specs=pl.BlockSpec((tq, QW), lambda b, j: (b * nq + j, out_cb)),
        out_shape=out_shape,
        input_output_aliases=aliases,
        compiler_params=_params("parallel", "parallel"),
        name="swa_attention",
    )(*args)


def _mem_attn_kernel(q_ref, g_ref, mk_ref, mv_ref, o_ref, *, n_heads, hd):
    q = q_ref[...]
    scale = hd ** -0.5
    for h in range(n_heads):
        sl = slice(h * hd, (h + 1) * hd)
        qh = q[:, sl]
        qn = (qh * lax.rsqrt(jnp.mean(qh * qh, axis=-1, keepdims=True) + NORM_EPS) * g_ref[...]).astype(BF16)
        s = _nt(qn, mk_ref[0, :, sl]) * scale
        m = jnp.max(s, axis=-1, keepdims=True)
        p = jnp.exp(s - m)
        p = p / jnp.sum(p, axis=-1, keepdims=True)
        o_ref[:, sl] = _dot(p.astype(BF16), mv_ref[0, :, sl]).astype(o_ref.dtype)


def mem_attention(q, qn_g, mk, mv, *, B, T, n_heads, tt=256, out_rows=None):
    MW = q.shape[1]
    ML = mk.shape[1]
    hd = MW // n_heads
    tt = min(tt, T)
    nt = T // tt
    out_rows = B * T if out_rows is None else out_rows
    return pl.pallas_call(
        functools.partial(_mem_attn_kernel, n_heads=n_heads, hd=hd),
        grid=(B, nt),
        in_specs=[pl.BlockSpec((tt, MW), lambda b, t: (b * nt + t, 0)),
                  pl.BlockSpec((1, hd), lambda b, t: (0, 0)),
                  pl.BlockSpec((1, ML, MW), lambda b, t: (b, 0, 0)),
                  pl.BlockSpec((1, ML, MW), lambda b, t: (b, 0, 0))],
        out_specs=pl.BlockSpec((tt, MW), lambda b, t: (b * nt + t, 0)),
        out_shape=jax.ShapeDtypeStruct((out_rows, MW), BF16),
        compiler_params=_params("parallel", "parallel"),
        name="mem_attention",
    )(q, qn_g.reshape(1, hd).astype(F32), mk, mv)


def _ffn_act_kernel(g_ref, halo_ref, up_ref, cw_ref, cb_ref, o_ref):
    t = pl.program_id(2)
    g = g_ref[0]
    prev = halo_ref[0] * jnp.where(t > 0, 1.0, 0.0)
    gc = cb_ref[...] + cw_ref[2:3, :] * g
    for s in (1, 2):
        gc = gc + cw_ref[2 - s:3 - s, :] * _shift_rows(g, prev, s)
    o_ref[0] = (jax.nn.gelu(gc, approximate=True) * up_ref[0]).astype(o_ref.dtype)


def ffn_act(gate, gate_cb0, up, up_cb0, conv_w, conv_b, *, tt=256, tc=512):
    B, T, _ = gate.shape
    C = conv_w.shape[1]
    tt = min(tt, T)
    hb = tt // SUBLANES
    return pl.pallas_call(
        _ffn_act_kernel,
        grid=(B, C // tc, T // tt),
        in_specs=[pl.BlockSpec((1, tt, tc), lambda b, c, t: (b, t, gate_cb0 + c)),
                  pl.BlockSpec((1, SUBLANES, tc), lambda b, c, t: (b, jnp.maximum(t * hb - 1, 0), gate_cb0 + c)),
                  pl.BlockSpec((1, tt, tc), lambda b, c, t: (b, t, up_cb0 + c)),
                  pl.BlockSpec((3, tc), lambda b, c, t: (0, c)),
                  pl.BlockSpec((1, tc), lambda b, c, t: (0, c))],
        out_specs=pl.BlockSpec((1, tt, tc), lambda b, c, t: (b, t, c)),
        out_shape=jax.ShapeDtypeStruct((B, T, C), BF16),
        compiler_params=_params("parallel", "parallel", "parallel"),
        name="ffn_act",
    )(gate, gate, up, conv_w.astype(F32), conv_b.reshape(1, C).astype(F32))


def _ffn_in_kernel(*refs, blocks_per_seq, sub, has_into, emit_w):
    x_ref, wg_ref, wu_ref, cw_ref, cb_ref = refs[:5]
    pos = 6 if has_into else 5
    o_ref, tail_ref = refs[pos], refs[pos + 1]
    carry_scr = refs[-1]
    i = pl.program_id(0)
    j = pl.program_id(1)
    tm = x_ref.shape[0]

    @pl.when(i % blocks_per_seq == 0)
    def _():
        carry_scr[j] = jnp.zeros(carry_scr.shape[1:], F32)

    prev = carry_scr[j]
    wg = wg_ref[...].astype(BF16)
    wu = wu_ref[...].astype(BF16)
    if emit_w:
        refs[pos + 2][...] = wg
        refs[pos + 3][...] = wu
    for c in range(tm // sub):
        rows = slice(c * sub, (c + 1) * sub)
        x = x_ref[rows, :]
        gate = _dot(x, wg)
        up = _dot(x, wu)
        gc = cb_ref[...] + cw_ref[2:3, :] * gate
        for s in (1, 2):
            gc = gc + cw_ref[2 - s:3 - s, :] * _shift_rows(gate, prev, s)
        o_ref[rows, :] = (jax.nn.gelu(gc, approximate=True) * up).astype(o_ref.dtype)
        prev = gate[sub - SUBLANES:, :]
    carry_scr[j] = prev
    tail_ref[0] = prev


def _ffn_in_call(x, wg, wu, layer, up_off, conv_w, conv_b, *, seq_len, row0, rows, tm, tn, sub, into, emit_w):
    M, D = x.shape
    FF = conv_w.shape[1]
    tm = min(tm, seq_len)
    bps = seq_len // tm
    nj = FF // tn
    i0 = row0 // tm
    uo = up_off // tn
    wspec = lambda off: (pl.BlockSpec((D, tn), lambda i, j: (0, j + off)) if layer is None else
                         pl.BlockSpec((None, D, tn), lambda i, j: (layer, 0, j + off)))
    in_specs = [pl.BlockSpec((tm, D), lambda i, j: (i + i0, 0), pipeline_mode=pl.Buffered(1)),
                wspec(0), wspec(uo),
                pl.BlockSpec((3, tn), lambda i, j: (0, j)),
                pl.BlockSpec((1, tn), lambda i, j: (0, j))]
    args = [x, wg, wu, conv_w.astype(F32), conv_b.reshape(1, FF).astype(F32)]
    aliases = {}
    if into is not None:
        aliases = {len(args): 0}
        in_specs.append(pl.BlockSpec(memory_space=pl.ANY))
        args.append(into)
    out_specs = [pl.BlockSpec((tm, tn), lambda i, j: (i + i0, j)),
                 pl.BlockSpec((1, SUBLANES, tn), lambda i, j: (i, 0, j))]
    out_shape = [jax.ShapeDtypeStruct((M, FF), BF16),
                 jax.ShapeDtypeStruct((rows // tm, SUBLANES, FF), F32)]
    if emit_w:
        out_specs += [pl.BlockSpec((D, tn), lambda i, j: (0, j))] * 2
        out_shape += [jax.ShapeDtypeStruct((D, FF), BF16)] * 2
    outs = pl.pallas_call(
        functools.partial(_ffn_in_kernel, blocks_per_seq=bps, sub=min(sub, tm), has_into=into is not None,
                          emit_w=emit_w),
        grid=(rows // tm, nj),
        in_specs=in_specs,
        out_specs=out_specs,
        out_shape=out_shape,
        input_output_aliases=aliases,
        scratch_shapes=[pltpu.VMEM((nj, SUBLANES, tn), F32)],
        compiler_params=_params("arbitrary", "arbitrary"),
        name="ffn_in_fused",
    )(*args)
    return (outs[0], outs[1][bps - 1::bps]) + tuple(outs[2:])


def ffn_in_fused(x, w_in, layer, conv_w, conv_b, wcache, key, *, seq_len, rows=None, sub=256):
    M = x.shape[0] if rows is None else rows
    FF = conv_w.shape[1]
    act, tail0, wg, wu = _ffn_in_call(x, w_in, w_in, layer, FF, conv_w, conv_b, seq_len=seq_len, row0=0,
                                      rows=seq_len, tm=2048, tn=256, sub=sub, into=None, emit_w=True)
    wcache[key] = (wg, wu)
    if M == seq_len:
        return act, tail0
    act, tails = _ffn_in_call(x, wg, wu, None, 0, conv_w, conv_b, seq_len=seq_len, row0=seq_len,
                              rows=M - seq_len, tm=1024, tn=512, sub=sub, into=act, emit_w=False)
    return act, jnp.concatenate([tail0, tails], axis=0)


def _rwkv_mix_kernel(x_ref, halo_ref, s0_ref, g_ref, mu_ref, *out_refs, n_valid_last):
    t = pl.program_id(1)
    nt = pl.num_programs(1)
    o_refs, hl_ref = out_refs[:6], out_refs[6]

    def norm(v):
        return v * lax.rsqrt(jnp.mean(v * v, axis=-1, keepdims=True) + NORM_EPS) * g_ref[...]

    h = norm(x_ref[...])
    hp = norm(halo_ref[...])
    first = jnp.where(t > 0, 1.0, 0.0)
    hp = hp * first + jnp.broadcast_to(s0_ref[0], hp.shape) * (1.0 - first)
    xx = _shift_rows(h, hp, 1) - h
    for j in range(6):
        o_refs[j][...] = (h + xx * mu_ref[j:j + 1, :]).astype(o_refs[j].dtype)

    @pl.when(t == nt - 1)
    def _():
        hl_ref[0] = h[n_valid_last - 1:n_valid_last, :]


def rwkv_mix(x, shift0, g, mu, *, B, T, n_valid, tt=256, out_rows=None):
    D = x.shape[1]
    tt = min(tt, T)
    nt = T // tt
    hb = tt // SUBLANES
    out_rows = B * T if out_rows is None else out_rows
    n_valid_last = n_valid - (nt - 1) * tt
    blk = pl.BlockSpec((tt, D), lambda b, t: (b * nt + t, 0))
    one = pl.BlockSpec((1, 1, D), lambda b, t: (b, 0, 0))
    return pl.pallas_call(
        functools.partial(_rwkv_mix_kernel, n_valid_last=n_valid_last),
        grid=(B, nt),
        in_specs=[blk,
                  pl.BlockSpec((SUBLANES, D), lambda b, t: (jnp.maximum((b * nt + t) * hb - 1, 0), 0)),
                  one,
                  pl.BlockSpec((1, D), lambda b, t: (0, 0)),
                  pl.BlockSpec((6, D), lambda b, t: (0, 0))],
        out_specs=[blk] * 6 + [one],
        out_shape=[jax.ShapeDtypeStruct((out_rows, D), BF16)] * 6 + [jax.ShapeDtypeStruct((B, 1, D), F32)],
        compiler_params=_params("parallel", "arbitrary"),
        name="rwkv_mix",
    )(x, x, shift0, g.reshape(1, D).astype(F32), mu.astype(F32))


def _rwkv_chunk_kernel(r_ref, k_ref, v_ref, wp_ref, ap_ref, g_ref, kk_ref, ka_ref, rk_ref, lng_ref, lnb_ref,
                       s0_ref, o_ref, so_ref, s_scr, *, hs):
    c = pl.program_id(2)
    nc = pl.num_programs(2)
    C, L = r_ref.shape
    n_pairs = L // LANES

    @pl.when(c == 0)
    def _():
        s_scr[...] = s0_ref[0]

    r = r_ref[...]
    k = k_ref[...]
    v = v_ref[...]
    logw = -jnp.exp(-jax.nn.softplus(-wp_ref[...]) - 0.5)
    a = jax.nn.sigmoid(ap_ref[...])
    kk = k * kk_ref[...]
    kk = kk / jnp.maximum(jnp.sqrt(_group_sum_bcast(kk * kk, hs)), 1e-12)
    k2 = k * (1.0 + (a - 1.0) * ka_ref[...])
    bm = kk * a

    ti = lax.broadcasted_iota(jnp.int32, (C, C), 0)
    si = lax.broadcasted_iota(jnp.int32, (C, C), 1)
    tri = jnp.where(ti >= si, 1.0, 0.0).astype(BF16)
    cum = _dot_exact_rhs_lhs(tri, logw)
    e_in = jnp.exp(cum)
    e_out = jnp.exp(-cum)
    e_end = jnp.exp(cum[C - 1:C, :] - cum)
    g_end = jnp.exp(cum[C - 1:C, :])
    rt = r * e_in
    kkt = kk * jnp.exp(cum - logw)
    bh = bm * e_out
    kh = k2 * e_out
    bbar = bm * e_end
    kbar = k2 * e_end

    lane = lax.broadcasted_iota(jnp.int32, (1, LANES), 1)
    m0 = lane < hs
    C2 = 2 * C
    ri = lax.broadcasted_iota(jnp.int32, (C2, C2), 0)
    ci = lax.broadcasted_iota(jnp.int32, (C2, C2), 1)
    same = (ri // C) == (ci // C)
    strict = same & ((ri % C) > (ci % C))
    lower = (ri % C) >= (ci % C)
    vi = lax.broadcasted_iota(jnp.int32, (LANES, LANES), 0) // hs
    vj = lax.broadcasted_iota(jnp.int32, (LANES, LANES), 1) // hs
    blockdiag = vi == vj

    def stack_heads(x):
        return jnp.concatenate([jnp.where(m0, x, 0.0), jnp.where(m0, 0.0, x)], axis=0).astype(BF16)

    n_steps = int(math.log2(C))
    P = range(n_pairs)
    sls = [slice(p * LANES, (p + 1) * LANES) for p in P]
    S = [s_scr[p] for p in P]
    S16 = [s.astype(BF16) for s in S]
    V16 = [v[:, sl].astype(BF16) for sl in sls]
    lhk = [stack_heads(kkt[:, sl]) for sl in sls]
    wide = C2 % LANES == 0
    if wide:
        gbk = [_nt(lhk[p], jnp.concatenate([stack_heads(bh[:, sls[p]]), stack_heads(kh[:, sls[p]])], axis=0))
               for p in P]
        nmat = [jnp.where(strict, -g[:, :C2], 0.0) for g in gbk]
        auk = [jnp.where(strict, -g[:, C2:], 0.0).astype(BF16) for g in gbk]
    else:
        nmat = [jnp.where(strict, -_nt(lhk[p], stack_heads(bh[:, sls[p]])), 0.0) for p in P]
        auk = [jnp.where(strict, -_nt(lhk[p], stack_heads(kh[:, sls[p]])), 0.0).astype(BF16) for p in P]
    sprod = [_nt(jnp.concatenate([lhk[p], rt[:, sls[p]].astype(BF16)], axis=0), S16[p]) for p in P]
    u = [_dot(auk[p], jnp.concatenate([V16[p], V16[p]], axis=0)) - sprod[p][:C2] for p in P]
    npow = [n.astype(BF16) for n in nmat]
    for it in range(n_steps):
        last = it + 1 == n_steps
        if wide and not last:
            res = [_dot(npow[p], jnp.concatenate([u[p].astype(BF16), npow[p]], axis=1)) for p in P]
            u = [u[p] + res[p][:, :LANES] for p in P]
            npow = [res[p][:, LANES:].astype(BF16) for p in P]
        else:
            u = [u[p] + _dot(npow[p], u[p].astype(BF16)) for p in P]
            if not last:
                npow = [_dot(npow[p], npow[p]).astype(BF16) for p in P]
    uv = [jnp.concatenate([jnp.where(m0, u[p][:C], u[p][C:]).astype(BF16), V16[p]], axis=0) for p in P]
    ar = [jnp.where(lower, _nt(stack_heads(rt[:, sls[p]]),
                               jnp.concatenate([bh[:, sls[p]], kh[:, sls[p]]], axis=0).astype(BF16)), 0.0).astype(BF16)
          for p in P]
    tmat = [_dot(ar[p], uv[p]) for p in P]
    ys = [sprod[p][C2:] + jnp.where(m0, tmat[p][:C], tmat[p][C:]) for p in P]
    upd = [_tn(uv[p], jnp.concatenate([bbar[:, sls[p]], kbar[:, sls[p]]], axis=0).astype(BF16)) for p in P]
    for p in P:
        s_scr[p] = S[p] * g_end[:, sls[p]] + jnp.where(blockdiag, upd[p], 0.0)

    y = ys[0] if n_pairs == 1 else jnp.concatenate(ys, axis=-1)
    inv = 1.0 / hs
    mean = _group_sum_bcast(y, hs) * inv
    yc = y - mean
    var = _group_sum_bcast(yc * yc, hs) * inv
    yn = yc * lax.rsqrt(var + RW_GN_EPS) * lng_ref[...] + lnb_ref[...]
    bonus = _group_sum_bcast(r * k2 * rk_ref[...], hs) * v
    o_ref[...] = ((yn + bonus) * g_ref[...]).astype(o_ref.dtype)

    @pl.when(c == nc - 1)
    def _():
        so_ref[0] = s_scr[...]


def _dot_exact_rhs_lhs(m_bf16, x):
    hi, mid, lo = _split3(x)
    return _dot(m_bf16, hi) + _dot(m_bf16, mid) + _dot(m_bf16, lo)


def rwkv_chunked(r, k, v, wp, ap, g, k_k, k_a, r_k, ln_g, ln_b, s0, *, B, T, hs, chunk, lanes_per_step=1024,
                 out_rows=None):
    D = r.shape[1]
    L = min(lanes_per_step, D)
    npg = L // LANES
    nc = T // chunk
    out_rows = B * T if out_rows is None else out_rows
    seq = pl.BlockSpec((chunk, L), lambda b, hg, c: (b * nc + c, hg))
    par = pl.BlockSpec((1, L), lambda b, hg, c: (0, hg))
    st = pl.BlockSpec((1, npg, LANES, LANES), lambda b, hg, c: (b, hg, 0, 0))
    row = lambda x: x.reshape(1, D).astype(F32)
    return pl.pallas_call(
        functools.partial(_rwkv_chunk_kernel, hs=hs),
        grid=(B, D // L, nc),
        in_specs=[seq] * 6 + [par] * 5 + [st],
        out_specs=[seq, st],
        out_shape=[jax.ShapeDtypeStruct((out_rows, D), BF16),
                   jax.ShapeDtypeStruct(s0.shape, F32)],
        scratch_shapes=[pltpu.VMEM((npg, LANES, LANES), F32)],
        compiler_params=_params("parallel", "parallel", "arbitrary"),
        name="rwkv_chunked",
    )(r, k, v, wp, ap, g, row(k_k), row(k_a), row(r_k), row(ln_g), row(ln_b), s0)


def _pairs_from_heads(s, hs):
    B, H = s.shape[:2]
    s = s.reshape(B, H // 2, 2, hs, hs)
    z = jnp.zeros_like(s[:, :, 0])
    top = jnp.concatenate([s[:, :, 0], z], axis=-1)
    bot = jnp.concatenate([z, s[:, :, 1]], axis=-1)
    return jnp.concatenate([top, bot], axis=-2)


def _heads_from_pairs(sp, hs):
    B, P = sp.shape[:2]
    return jnp.stack([sp[:, :, :hs, :hs], sp[:, :, hs:, hs:]], axis=2).reshape(B, 2 * P, hs, hs)


def _pad_rows(x, front, back):
    return jnp.pad(x, ((0, 0), (front, back), (0, 0)))


def kernel(x_prompt, x_sample, mem_prompt, state_lru_conv, state_lru_h, cache_swa_k, cache_swa_v,
           state_rwkv_shift, state_rwkv_wkv, cache_mem_k, cache_mem_v, state_ffn_conv,
           a_norm_g, a_w_in, a_conv_w, a_conv_b, a_gate_a_w, a_gate_a_b, a_gate_x_w, a_gate_x_b,
           a_lambda, b_q_norm_g, b_k_norm_g, b_sink, a_w_out,
           c_norm_g, c_mu, c_w_r, c_w_k, c_w_v, c_w_o, c_w0, c_w1, c_w2, c_a0, c_a1, c_a2,
           c_g1, c_g2, c_k_k, c_k_a, c_r_k, c_ln_g, c_ln_b,
           m_norm_g, m_mem_norm_g, m_w_q, m_w_kv, m_q_norm_g, m_k_norm_g, m_w_o,
           f_norm_g, f_w_in, f_conv_w, f_conv_b, f_w_out):
    D = x_prompt.shape[-1]
    depth = m_norm_g.shape[0]
    W = a_conv_w.shape[-1]
    KA = a_conv_w.shape[1]
    hd = b_q_norm_g.shape[-1]
    n_q = b_sink.shape[-1]
    n_kv = cache_swa_k.shape[3]
    win = cache_swa_k.shape[2]
    QW, KW = n_q * hd, n_kv * hd
    hs = c_r_k.shape[-1]
    n_rw = c_r_k.shape[1]
    m_heads, m_hd = cache_mem_k.shape[3], cache_mem_k.shape[4]
    MW = m_heads * m_hd
    FF = f_conv_w.shape[-1]
    KF = f_conv_w.shape[1]
    bf = lambda w: w.astype(BF16)
    wcache = {}

    Bp, S = x_prompt.shape[:2]
    Bs, Ts = x_sample.shape[:2]
    Mp, Ms = Bp * S, Bs * Ts
    Mt = Mp + Ms
    tma = _row_tile(Mt, 1280)
    n_a = a_norm_g.shape[0]
    n_c = c_norm_g.shape[0]
    ML = mem_prompt.shape[1]
    PAD8, PAD16 = SUBLANES, 2 * SUBLANES

    def mm(xin, w, layer, key, **kw):
        return dense(xin, w, layer, wcache, (key, layer), tm=tma, **kw)

    def seq_tail(arr, n, cols=slice(None)):
        return jnp.stack([arr[(b + 1) * S - n:(b + 1) * S, cols] for b in range(Bp)])

    def put_sample(buf, rows_s):
        return lax.dynamic_update_slice(buf, rows_s.astype(buf.dtype), (Mp, 0))

    mem_flat = mem_prompt.reshape(Bp * ML, D)
    mks, mvs = [], []
    for l in range(depth):
        mn = rmsnorm_rows(mem_flat, m_mem_norm_g[l])
        kv = matmul(mn, m_w_kv, layer=l)
        mk = headnorm(kv, 0, MW, m_k_norm_g[l], m_hd, F32)
        mks.append(mk.reshape(Bp, ML, m_heads, m_hd))
        mvs.append(kv[:, MW:].reshape(Bp, ML, m_heads, m_hd))
    p_mem_k = jnp.stack(mks)
    p_mem_v = jnp.stack(mvs)

    xf = jnp.concatenate([x_prompt.reshape(Mp, D), x_sample.reshape(Ms, D)], axis=0)
    po = {k_: [] for k_ in ("lc", "lh", "sk", "sv", "rs", "rw", "fc")}
    so = {k_: [] for k_ in ("lc", "lh", "sk", "sv", "rs", "rw", "fc")}
    ia = ic = 0
    y_p = y_s = None
    for l in range(depth):
        if l % 2 == 0:
            i = ia
            ia += 1
            assert W == QW
            h = rmsnorm_rows(xf, a_norm_g[i])
            zz = mm(h, a_w_in, i, "a_w_in_rg", n=2 * W)
            qkv = mm(h, a_w_in, i, "a_w_in_qkv", n0=2 * W)
            qn = headnorm(qkv, 0, QW, b_q_norm_g[i], hd, BF16)
            kn = headnorm(qkv, QW // KW, KW, b_k_norm_g[i], hd, F32)
            vv = qkv[:, QW + KW:]
            lru_w = (a_conv_w[i], a_conv_b[i], a_gate_a_w[i], a_gate_a_b[i], a_gate_x_w[i], a_gate_x_b[i],
                     a_lambda[i])
            mix, hl_p = lru_mixer(zz, 0, zz, 1, jnp.zeros((Bp, 1, W), F32), *lru_w, B=Bp, T=S, n_pad=0, tt=256,
                                  out_rows=Mt, out_cols=W + QW)
            mix = swa_attention(qn, bf(kn), bf(vv), b_sink[i], B=Bp, nq=S // win, nkb=S // win, j0=0, win=win,
                                hd=hd, into=mix, out_cb=W // QW)
            po["lc"].append(seq_tail(zz, KA - 1, slice(0, W)))
            po["lh"].append(hl_p.reshape(Bp, W))
            po["sk"].append(seq_tail(kn, win).reshape(Bp, win, n_kv, hd))
            po["sv"].append(seq_tail(vv, win).reshape(Bp, win, n_kv, hd))
            zz_s = zz[Mp:].reshape(Bs, Ts, 2 * W)
            n_pad = PAD8 - Ts
            xr_hist = jnp.concatenate([state_lru_conv[i].astype(F32), zz_s[:, :, :W]], axis=1)
            xr_p = _pad_rows(xr_hist, PAD8 - xr_hist.shape[1], 0).reshape(Bs * PAD8, W)
            yg_p = _pad_rows(zz_s[:, :, W:], n_pad, 0).reshape(Bs * PAD8, W)
            oa_s, hl_s = lru_mixer(xr_p, 0, yg_p, 0, state_lru_h[i].reshape(Bs, 1, W).astype(F32), *lru_w,
                                   B=Bs, T=PAD8, n_pad=n_pad, tt=PAD8)
            oa_s = oa_s.reshape(Bs, PAD8, W)[:, n_pad:].reshape(Ms, W)
            kc = cache_swa_k[i].reshape(Bs, win, KW)
            vc = cache_swa_v[i].reshape(Bs, win, KW)
            k_all = jnp.concatenate([kc, kn[Mp:].reshape(Bs, Ts, KW)], axis=1)
            v_all = jnp.concatenate([vc, vv[Mp:].reshape(Bs, Ts, KW)], axis=1)
            q_s = _pad_rows(qn[Mp:].reshape(Bs, Ts, QW), 0, PAD16 - Ts).reshape(Bs * PAD16, QW)
            o_s = swa_attention(q_s, bf(_pad_rows(k_all, 0, win - Ts)).reshape(Bs * 2 * win, KW),
                                bf(_pad_rows(v_all, 0, win - Ts)).reshape(Bs * 2 * win, KW), b_sink[i],
                                B=Bs, nq=1, nkb=2, j0=1, win=win, hd=hd, tq=PAD16)
            o_s = o_s.reshape(Bs, PAD16, QW)[:, :Ts].reshape(Ms, QW)
            mix = put_sample(mix, jnp.concatenate([oa_s, o_s], axis=-1))
            so["lc"].append(xr_hist[:, Ts:])
            so["lh"].append(hl_s.reshape(Bs, W))
            so["sk"].append(k_all[:, Ts:].reshape(Bs, win, n_kv, hd))
            so["sv"].append(v_all[:, Ts:].reshape(Bs, win, n_kv, hd))
            xf = mm(mix, a_w_out, i, "a_w_out", res=xf)
        else:
            i = ic
            ic += 1
            mixes = rwkv_mix(xf, jnp.zeros((Bp, 1, D), F32), c_norm_g[i], c_mu[i], B=Bp, T=S, n_valid=S,
                             out_rows=Mt)
            x_s = _pad_rows(xf[Mp:].reshape(Bs, Ts, D), 0, PAD8 - Ts).reshape(Bs * PAD8, D)
            mixes_s = rwkv_mix(x_s, state_rwkv_shift[i].reshape(Bs, 1, D).astype(F32), c_norm_g[i], c_mu[i],
                               B=Bs, T=PAD8, n_valid=Ts, tt=PAD8)
            take = lambda m: m.reshape(Bs, PAD8, D)[:, :Ts].reshape(Ms, D)
            xr, xw, xk, xv, xa, xg = [put_sample(mixes[j], take(mixes_s[j])) for j in range(6)]
            po["rs"].append(mixes[6].reshape(Bp, D))
            so["rs"].append(mixes_s[6].reshape(Bs, D))
            r = mm(xr, c_w_r, i, "c_w_r")
            k = mm(xk, c_w_k, i, "c_w_k")
            v = mm(xv, c_w_v, i, "c_w_v")
            lora = dict(layer=i, tm=tma)
            wp = matmul(matmul(xw, c_w1, act="tanh", out_dtype=BF16, **lora), c_w2, bias=c_w0[i], **lora)
            ap = matmul(matmul(xa, c_a1, out_dtype=BF16, **lora), c_a2, bias=c_a0[i], **lora)
            gl = c_g1[i].shape[1]
            glp = -(-gl // LANES) * LANES
            g1p = jnp.pad(c_g1[i], ((0, 0), (0, glp - gl)))
            g2p = jnp.pad(c_g2[i], ((0, glp - gl), (0, 0)))
            g = matmul(matmul(xg, g1p, act="sigmoid", out_dtype=BF16, tm=tma), g2p, tm=tma)
            rw_par = (c_k_k[i], c_k_a[i], c_r_k[i], c_ln_g[i], c_ln_b[i])
            yo, s_end = rwkv_chunked(r, k, v, wp, ap, g, *rw_par, jnp.zeros((Bp, n_rw // 2, 2 * hs, 2 * hs), F32),
                                     B=Bp, T=S, hs=hs, chunk=64, out_rows=Mt)
            po["rw"].append(_heads_from_pairs(s_end, hs))
            cs = PAD16
            sq = lambda t, fill=0.0: jnp.pad(t[Mp:].reshape(Bs, Ts, D), ((0, 0), (0, cs - Ts), (0, 0)),
                                             constant_values=fill).reshape(Bs * cs, D)
            yo_s, s_end_s = rwkv_chunked(sq(r), sq(k), sq(v), sq(wp, -1e30), sq(ap), sq(g), *rw_par,
                                         _pairs_from_heads(state_rwkv_wkv[i].astype(F32), hs),
                                         B=Bs, T=cs, hs=hs, chunk=cs)
            yo = put_sample(yo, yo_s.reshape(Bs, cs, D)[:, :Ts].reshape(Ms, D))
            so["rw"].append(_heads_from_pairs(s_end_s, hs))
            xf = mm(yo, c_w_o, i, "c_w_o", res=xf)
        hm = rmsnorm_rows(xf, m_norm_g[l])
        q = mm(hm, m_w_q, l, "m_w_q")
        om = mem_attention(q, m_q_norm_g[l], bf(p_mem_k[l]).reshape(Bp, ML, MW), bf(p_mem_v[l]).reshape(Bp, ML, MW),
                           B=Bp, T=S, n_heads=m_heads, out_rows=Mt)
        q_s = _pad_rows(q[Mp:].reshape(Bs, Ts, MW), 0, PAD8 - Ts).reshape(Bs * PAD8, MW)
        om_s = mem_attention(q_s, m_q_norm_g[l], bf(cache_mem_k[l]).reshape(Bs, ML, MW),
                             bf(cache_mem_v[l]).reshape(Bs, ML, MW), B=Bs, T=PAD8, n_heads=m_heads, tt=PAD8)
        om = put_sample(om, om_s.reshape(Bs, PAD8, MW)[:, :Ts].reshape(Ms, MW))
        xf = mm(om, m_w_o, l, "m_w_o", res=xf)
        hf = rmsnorm_rows(xf, f_norm_g[l])
        act, tail = ffn_in_fused(hf, f_w_in, l, f_conv_w[l], f_conv_b[l], wcache, ("f_w_in", l), seq_len=S, rows=Mp)
        po["fc"].append(tail[:, SUBLANES - (KF - 1):, :])
        wg, wu = wcache[("f_w_in", l)]
        hf_s = hf[Mp:]
        gate_s = matmul(hf_s, wg, tn=1024).reshape(Bs, Ts, FF)
        up_s = matmul(hf_s, wu, tn=1024).reshape(Bs, Ts, FF)
        g_hist = jnp.concatenate([state_ffn_conv[l].astype(F32), gate_s], axis=1)
        g_p = _pad_rows(g_hist, PAD8 - g_hist.shape[1], 0)
        up_p = _pad_rows(up_s, PAD8 - Ts, 0)
        act_s = ffn_act(g_p, 0, up_p, 0, f_conv_w[l], f_conv_b[l], tt=PAD8, tc=FF)[:, PAD8 - Ts:]
        act = put_sample(act, act_s.reshape(Ms, FF))
        so["fc"].append(g_hist[:, Ts:])
        if l + 1 < depth:
            xf = mm(act, f_w_out, l, "f_w_out", res=xf, tn=1024, tk=2048)
        else:
            y_p = dense(act, f_w_out, l, wcache, ("f_w_out", l), rows=Mp, out_rows=Mp, res=xf, tn=1024, tk=2048)
            y_s = matmul(act, wcache[("f_w_out", l)], row0=Mp, rows=Ms, out_rows=Ms, out_row0=0, res=xf,
                         tn=1024, tk=2048)
    st = jnp.stack
    return (y_p.reshape(Bp, S, D), y_s.reshape(Bs, Ts, D),
            st(po["lc"]), st(po["lh"]), st(po["sk"]), st(po["sv"]), st(po["rs"]), st(po["rw"]),
            p_mem_k, p_mem_v, st(po["fc"]),
            st(so["lc"]), st(so["lh"]), st(so["sk"]), st(so["sv"]), st(so["rs"]), st(so["rw"]), st(so["fc"]))
```

```python
import functools
import math

import jax
import jax.numpy as jnp
from jax import lax
from jax.experimental import pallas as pl
from jax.experimental.pallas import tpu as pltpu

F32 = jnp.float32
BF16 = jnp.bfloat16

NORM_EPS = 1e-6
RW_GN_EPS = 64e-5
LRU_C = 8.0
LANES = 128
SUBLANES = 8
VMEM_LIMIT_BYTES = 56 * 1024 * 1024


def _params(*sem):
    return pltpu.CompilerParams(dimension_semantics=sem, vmem_limit_bytes=VMEM_LIMIT_BYTES)


def _nt(a, b):
    return lax.dot_general(a, b, (((1,), (1,)), ((), ())), preferred_element_type=F32)


def _tn(a, b):
    return lax.dot_general(a, b, (((0,), (0,)), ((), ())), preferred_element_type=F32)


def _dot(a, b):
    return jnp.dot(a, b, preferred_element_type=F32)


def _group_sum_bcast(x, width, terms=2):
    m, L = x.shape
    n = L // LANES
    if width == LANES:
        parts = []
        for c in range(n):
            s = jnp.sum(x[:, c * LANES:(c + 1) * LANES], axis=-1, keepdims=True)
            parts.append(jnp.broadcast_to(s, (m, LANES)))
        return parts[0] if n == 1 else jnp.concatenate(parts, axis=-1)
    li = lax.broadcasted_iota(jnp.int32, (LANES, LANES), 0) // width
    lj = lax.broadcasted_iota(jnp.int32, (LANES, LANES), 1) // width
    e = jnp.where(li == lj, 1.0, 0.0).astype(BF16)
    xs = x if n == 1 else jnp.concatenate([x[:, c * LANES:(c + 1) * LANES] for c in range(n)], axis=0)
    hi = xs.astype(BF16)
    out = _dot(hi, e)
    if terms > 1:
        out = out + _dot((xs - hi.astype(F32)).astype(BF16), e)
    return out if n == 1 else jnp.concatenate([out[c * m:(c + 1) * m] for c in range(n)], axis=-1)


def _shift_rows(x, prev8, s):
    rolled = pltpu.roll(x, s, 0)
    top = jnp.where(lax.broadcasted_iota(jnp.int32, (SUBLANES, 1), 0) < s,
                    pltpu.roll(prev8, s, 0), rolled[0:SUBLANES])
    if x.shape[0] == SUBLANES:
        return top
    return jnp.concatenate([top, rolled[SUBLANES:]], axis=0)


def _rmsnorm_kernel(x_ref, g_ref, o_ref):
    x = x_ref[...]
    ms = jnp.mean(x * x, axis=-1, keepdims=True)
    o_ref[...] = (x * lax.rsqrt(ms + NORM_EPS) * g_ref[...]).astype(o_ref.dtype)


def _row_tile(m, target):
    best = None
    for d in range(16, min(m, target) + 1, 16):
        if m % d == 0:
            best = d
    return best or m


def rmsnorm_rows(x, g, out_dtype=BF16, tm=512):
    M, D = x.shape
    tm = _row_tile(M, tm)
    return pl.pallas_call(
        _rmsnorm_kernel,
        grid=(M // tm,),
        in_specs=[pl.BlockSpec((tm, D), lambda i: (i, 0)), pl.BlockSpec((1, D), lambda i: (0, 0))],
        out_specs=pl.BlockSpec((tm, D), lambda i: (i, 0)),
        out_shape=jax.ShapeDtypeStruct((M, D), out_dtype),
        compiler_params=_params("parallel"),
        name="rmsnorm_rows",
    )(x, g.reshape(1, D))


def _mm_kernel(*refs, nk, has_bias, has_res, has_into, emit_w, in_place, act):
    x_ref, w_ref = refs[0], refs[1]
    pos = 2
    b_ref = r_ref = wb_ref = acc_ref = None
    if has_bias:
        b_ref = refs[pos]
        pos += 1
    if has_res:
        r_ref = refs[pos]
        pos += 1
    if has_into:
        pos += 1
    o_ref = refs[pos]
    pos += 1
    if emit_w:
        wb_ref = refs[pos]
        pos += 1
    if nk > 1 and not in_place:
        acc_ref = refs[pos]

    def weights(cs=slice(None)):
        w16 = w_ref[:, cs].astype(BF16)
        if emit_w:
            wb_ref[:, cs] = w16
        return w16

    def epilogue(y):
        if has_bias:
            y = y + b_ref[...]
        if act == "tanh":
            y = jnp.tanh(y)
        elif act == "sigmoid":
            y = jax.nn.sigmoid(y)
        if has_res:
            y = y + r_ref[...]
        o_ref[...] = y.astype(o_ref.dtype)

    if nk > 1 and in_place:
        k = pl.program_id(2)
        tn = o_ref.shape[1]
        cw = min(tn, 2 * LANES)

        def sweep(first):
            x = x_ref[...]
            for c in range(tn // cw):
                cs = slice(c * cw, (c + 1) * cw)
                part = _dot(x, weights(cs))
                if not first:
                    o_ref[:, cs] += part
                elif has_res:
                    o_ref[:, cs] = part + r_ref[:, cs]
                else:
                    o_ref[:, cs] = part

        pl.when(k == 0)(functools.partial(sweep, True))
        pl.when(k > 0)(functools.partial(sweep, False))
        return

    part = _dot(x_ref[...], weights())
    if nk == 1:
        epilogue(part)
        return
    k = pl.program_id(2)

    @pl.when(k == 0)
    def _():
        acc_ref[...] = part

    @pl.when(k > 0)
    def _():
        acc_ref[...] += part

    @pl.when(k == nk - 1)
    def _():
        epilogue(acc_ref[...])


def _pick(n, prefs):
    for p in prefs:
        if n % p == 0:
            return p
    return n


def matmul(x, w, *, layer=None, n0=0, n=None, bias=None, res=None, act=None, out_dtype=F32,
           tm=1024, tn=512, tk=4096, row0=0, rows=None, into=None, emit_w=False, out_rows=None, out_row0=None):
    M, K = x.shape
    N = w.shape[-1] - n0 if n is None else n
    rows = M - row0 if rows is None else rows
    tm = min(tm, rows)
    out_rows = M if out_rows is None else out_rows
    o0 = (row0 if out_row0 is None else out_row0) // tm
    tn = _pick(math.gcd(N, n0) if n0 else N, (tn, 512, 256, 128))
    tk = _pick(K, (tk, 2048, 1024, 512))
    nk = K // tk
    j0 = n0 // tn
    i0 = row0 // tm
    assert row0 % tm == 0 and rows % tm == 0 and (not emit_w or rows == tm)
    in_specs = [pl.BlockSpec((tm, tk), lambda i, j, k: (i + i0, k)),
                pl.BlockSpec((tk, tn), lambda i, j, k: (k, j + j0)) if layer is None else
                pl.BlockSpec((None, tk, tn), lambda i, j, k: (layer, k, j + j0))]
    args = [x, w]
    if bias is not None:
        in_specs.append(pl.BlockSpec((1, tn), lambda i, j, k: (0, j)))
        args.append(bias.reshape(1, N).astype(F32))
    if res is not None:
        in_specs.append(pl.BlockSpec((tm, tn), lambda i, j, k: (i + i0, j)))
        args.append(res)
    aliases = {}
    if into is not None:
        aliases = {len(args): 0}
        in_specs.append(pl.BlockSpec(memory_space=pl.ANY))
        args.append(into)
    in_place = bias is None and act is None and out_dtype == F32
    kern = functools.partial(_mm_kernel, nk=nk, has_bias=bias is not None, has_res=res is not None,
                             has_into=into is not None, emit_w=emit_w, in_place=in_place, act=act)
    out_specs = [pl.BlockSpec((tm, tn), lambda i, j, k: (i + o0, j))]
    out_shape = [jax.ShapeDtypeStruct((out_rows, N), out_dtype)]
    if emit_w:
        out_specs.append(pl.BlockSpec((tk, tn), lambda i, j, k: (k, j)))
        out_shape.append(jax.ShapeDtypeStruct((K, N), BF16))
    outs = pl.pallas_call(
        kern,
        grid=(rows // tm, N // tn, nk),
        in_specs=in_specs,
        out_specs=out_specs,
        out_shape=out_shape,
        input_output_aliases=aliases,
        scratch_shapes=[pltpu.VMEM((tm, tn), F32)] if (nk > 1 and not in_place) else [],
        compiler_params=_params("parallel", "parallel", "arbitrary"),
        name="matmul",
    )(*args)
    return outs if emit_w else outs[0]


def dense(x, w, layer, wcache, key, **kw):
    rows = kw.pop("rows", x.shape[0])
    tm = min(kw.get("tm", 1024), rows)
    first, wb = matmul(x, w, layer=layer, rows=tm, emit_w=True, **kw)
    wcache[key] = wb
    if tm == rows:
        return first
    kw = {a: b for a, b in kw.items() if a not in ("n0", "n")}
    return matmul(x, wb, row0=tm, rows=rows - tm, into=first, **kw)


def _headnorm_kernel(x_ref, g_ref, o_ref, *, hd):
    x = x_ref[...]
    ms = _group_sum_bcast(x * x, hd) * (1.0 / hd)
    o_ref[...] = (x * lax.rsqrt(ms + NORM_EPS) * g_ref[...]).astype(o_ref.dtype)


def headnorm(x, col_block, width, g, hd, out_dtype, tm=512):
    M = x.shape[0]
    tm = _row_tile(M, tm)
    g_row = jnp.tile(g.astype(F32), width // hd).reshape(1, width)
    return pl.pallas_call(
        functools.partial(_headnorm_kernel, hd=hd),
        grid=(M // tm,),
        in_specs=[pl.BlockSpec((tm, width), lambda i: (i, col_block)),
                  pl.BlockSpec((1, width), lambda i: (0, 0))],
        out_specs=pl.BlockSpec((tm, width), lambda i: (i, 0)),
        out_shape=jax.ShapeDtypeStruct((M, width), out_dtype),
        compiler_params=_params("parallel"),
        name="headnorm",
    )(x, g_row)


def _lru_kernel(xr_ref, halo_ref, yg_ref, h0_ref, cw_ref, cb_ref, gaw_ref, gab_ref, gxw_ref, gxb_ref,
                lam_ref, o_ref, hl_ref, a_scr, u_scr, h_scr, *, n_pad, n_blocks, bw, scan_w):
    t = pl.program_id(1)
    tt, W = a_scr.shape

    @pl.when(t == 0)
    def _():
        h_scr[...] = h0_ref[0]

    x = xr_ref[...]
    prev = halo_ref[...] * jnp.where(t > 0, 1.0, 0.0)
    xc = cb_ref[...] + cw_ref[3:4, :] * x
    for s in (1, 2, 3):
        xc = xc + cw_ref[3 - s:4 - s, :] * _shift_rows(x, prev, s)

    nsp = -LRU_C * jax.nn.softplus(-lam_ref[...])
    if n_pad:
        live = (lax.broadcasted_iota(jnp.int32, (tt, 1), 0) >= n_pad) | (t > 0)
    for n in range(n_blocks):
        sl = slice(n * bw, (n + 1) * bw)
        xb = xc[:, sl]
        xb16 = xb.astype(BF16)
        r = jax.nn.sigmoid(_dot(xb16, gaw_ref[n]) + gab_ref[:, sl])
        i = jax.nn.sigmoid(_dot(xb16, gxw_ref[n]) + gxb_ref[:, sl])
        log_a = r * nsp[:, sl]
        a = jnp.exp(log_a)
        u = jnp.sqrt(jnp.maximum(-jnp.tanh(log_a) * (a * a + 1.0), 0.0)) * (i * xb)
        if n_pad:
            a = jnp.where(live, a, 1.0)
            u = jnp.where(live, u, 0.0)
        a_scr[:, sl] = a
        u_scr[:, sl] = u

    row8 = lax.broadcasted_iota(jnp.int32, (SUBLANES, 1), 0)
    for c in range(W // scan_w):
        cs = slice(c * scan_w, (c + 1) * scan_w)

        def body(gi, h, cs=cs):
            r0 = pl.multiple_of(gi * SUBLANES, SUBLANES)
            A = a_scr[pl.ds(r0, SUBLANES), cs]
            U = u_scr[pl.ds(r0, SUBLANES), cs]
            for s in (1, 2, 4):
                As = pltpu.roll(A, s, 0)
                Us = pltpu.roll(U, s, 0)
                m = row8 >= s
                U = jnp.where(m, A * Us + U, U)
                A = jnp.where(m, A * As, A)
            H = A * h + U
            u_scr[pl.ds(r0, SUBLANES), cs] = H
            return H[SUBLANES - 1:SUBLANES, :]

        h_end = lax.fori_loop(0, tt // SUBLANES, body, h_scr[:, cs])
        h_scr[:, cs] = h_end

    o_ref[...] = (u_scr[...] * jax.nn.gelu(yg_ref[...], approximate=True)).astype(o_ref.dtype)
    hl_ref[0] = h_scr[...]


def lru_mixer(xr, xr_cb, yg, yg_cb, h0, conv_w, conv_b, ga_w, ga_b, gx_w, gx_b, lam, *, B, T, n_pad, tt,
              out_rows=None, out_cols=None):
    W = conv_w.shape[1]
    nb, bw = ga_w.shape[0], ga_w.shape[1]
    tt = min(tt, T)
    nt = T // tt
    hb = tt // SUBLANES
    out_rows = B * T if out_rows is None else out_rows
    out_cols = W if out_cols is None else out_cols
    row = lambda v: v.reshape(1, W).astype(F32)
    kern = functools.partial(_lru_kernel, n_pad=n_pad, n_blocks=nb, bw=bw, scan_w=min(W, 512))
    return pl.pallas_call(
        kern,
        grid=(B, nt),
        in_specs=[
            pl.BlockSpec((tt, W), lambda b, t: (b * nt + t, xr_cb)),
            pl.BlockSpec((SUBLANES, W), lambda b, t: (jnp.maximum((b * nt + t) * hb - 1, 0), xr_cb)),
            pl.BlockSpec((tt, W), lambda b, t: (b * nt + t, yg_cb)),
            pl.BlockSpec((1, 1, W), lambda b, t: (b, 0, 0)),
            pl.BlockSpec((4, W), lambda b, t: (0, 0)),
            pl.BlockSpec((1, W), lambda b, t: (0, 0)),
            pl.BlockSpec((nb, bw, bw), lambda b, t: (0, 0, 0)),
            pl.BlockSpec((1, W), lambda b, t: (0, 0)),
            pl.BlockSpec((nb, bw, bw), lambda b, t: (0, 0, 0)),
            pl.BlockSpec((1, W), lambda b, t: (0, 0)),
            pl.BlockSpec((1, W), lambda b, t: (0, 0)),
        ],
        out_specs=[pl.BlockSpec((tt, W), lambda b, t: (b * nt + t, 0)),
                   pl.BlockSpec((1, 1, W), lambda b, t: (b, 0, 0))],
        out_shape=[jax.ShapeDtypeStruct((out_rows, out_cols), BF16), jax.ShapeDtypeStruct((B, 1, W), F32)],
        scratch_shapes=[pltpu.VMEM((tt, W), F32), pltpu.VMEM((tt, W), F32), pltpu.VMEM((1, W), F32)],
        compiler_params=_params("parallel", "arbitrary"),
        name="lru_mixer",
    )(xr, xr, yg, h0, conv_w.astype(F32), row(conv_b), ga_w.astype(BF16), row(ga_b),
      gx_w.astype(BF16), row(gx_b), row(lam))


def _swa_kernel(sink_ref, q_ref, kp_ref, kc_ref, vp_ref, vc_ref, *rest, j0, n_kv, group, hd, win):
    o_ref = rest[-1]
    j = pl.program_id(1) + j0
    q = q_ref[...]
    k2 = jnp.concatenate([kp_ref[...], kc_ref[...]], axis=0)
    v2 = jnp.concatenate([vp_ref[...], vc_ref[...]], axis=0)
    tq = q.shape[0]
    qi = lax.broadcasted_iota(jnp.int32, (tq, 2 * win), 0)
    kj = lax.broadcasted_iota(jnp.int32, (tq, 2 * win), 1)
    dist = qi + win - kj
    mask = (dist >= 0) & (dist < win) & ((j > 0) | (kj >= win))
    scale = hd ** -0.5
    for kvh in range(n_kv):
        kh = k2[:, kvh * hd:(kvh + 1) * hd]
        vh = v2[:, kvh * hd:(kvh + 1) * hd]
        hs_ = [kvh * group + g for g in range(group)]
        s = [jnp.where(mask, _nt(q[:, h * hd:(h + 1) * hd], kh) * scale, -jnp.inf) for h in hs_]
        m = [jnp.maximum(jnp.max(s[g], axis=-1, keepdims=True), sink_ref[hs_[g]]) for g in range(group)]
        p = [jnp.exp(s[g] - m[g]) for g in range(group)]
        den = [jnp.sum(p[g], axis=-1, keepdims=True) + jnp.exp(sink_ref[hs_[g]] - m[g]) for g in range(group)]
        outs = [_dot(p[g].astype(BF16), vh) / den[g] for g in range(group)]
        o_ref[:, hs_[0] * hd:(hs_[-1] + 1) * hd] = jnp.concatenate(outs, axis=-1).astype(o_ref.dtype)


def swa_attention(q, k, v, sink, *, B, nq, nkb, j0, win, hd, tq=None, into=None, out_cb=0):
    QW = q.shape[1]
    KW = k.shape[1]
    n_kv = KW // hd
    group = QW // KW
    tq = win if tq is None else tq
    kern = functools.partial(_swa_kernel, j0=j0, n_kv=n_kv, group=group, hd=hd, win=win)
    prev = lambda b, j: (b * nkb + jnp.maximum(j + j0 - 1, 0), 0)
    cur = lambda b, j: (b * nkb + j + j0, 0)
    in_specs = [pl.BlockSpec(memory_space=pltpu.SMEM),
                pl.BlockSpec((tq, QW), lambda b, j: (b * nq + j, 0)),
                pl.BlockSpec((win, KW), prev), pl.BlockSpec((win, KW), cur),
                pl.BlockSpec((win, KW), prev), pl.BlockSpec((win, KW), cur)]
    args = [sink.astype(F32), q, k, k, v, v]
    aliases = {}
    out_shape = jax.ShapeDtypeStruct((B * nq * tq, QW), BF16)
    if into is not None:
        aliases = {len(args): 0}
        in_specs.append(pl.BlockSpec(memory_space=pl.ANY))
        args.append(into)
        out_shape = jax.ShapeDtypeStruct(into.shape, into.dtype)
    return pl.pallas_call(
        kern,
        grid=(B, nq),
        in_specs=in_specs,
        out_specs=pl.BlockSpec((tq, QW), lambda b, j: (b * nq + j, out_cb)),
        out_shape=out_shape,
        input_output_aliases=aliases,
        compiler_params=_params("parallel", "parallel"),
        name="swa_attention",
    )(*args)


def _mem_attn_kernel(q_ref, g_ref, mk_ref, mv_ref, o_ref, *, n_heads, hd):
    q = q_ref[...]
    scale = hd ** -0.5
    for h in range(n_heads):
        sl = slice(h * hd, (h + 1) * hd)
        qh = q[:, sl]
        qn = (qh * lax.rsqrt(jnp.mean(qh * qh, axis=-1, keepdims=True) + NORM_EPS) * g_ref[...]).astype(BF16)
        s = _nt(qn, mk_ref[0, :, sl]) * scale
        m = jnp.max(s, axis=-1, keepdims=True)
        p = jnp.exp(s - m)
        p = p / jnp.sum(p, axis=-1, keepdims=True)
        o_ref[:, sl] = _dot(p.astype(BF16), mv_ref[0, :, sl]).astype(o_ref.dtype)


def mem_attention(q, qn_g, mk, mv, *, B, T, n_heads, tt=256, out_rows=None):
    MW = q.shape[1]
    ML = mk.shape[1]
    hd = MW // n_heads
    tt = min(tt, T)
    nt = T // tt
    out_rows = B * T if out_rows is None else out_rows
    return pl.pallas_call(
        functools.partial(_mem_attn_kernel, n_heads=n_heads, hd=hd),
        grid=(B, nt),
        in_specs=[pl.BlockSpec((tt, MW), lambda b, t: (b * nt + t, 0)),
                  pl.BlockSpec((1, hd), lambda b, t: (0, 0)),
                  pl.BlockSpec((1, ML, MW), lambda b, t: (b, 0, 0)),
                  pl.BlockSpec((1, ML, MW), lambda b, t: (b, 0, 0))],
        out_specs=pl.BlockSpec((tt, MW), lambda b, t: (b * nt + t, 0)),
        out_shape=jax.ShapeDtypeStruct((out_rows, MW), BF16),
        compiler_params=_params("parallel", "parallel"),
        name="mem_attention",
    )(q, qn_g.reshape(1, hd).astype(F32), mk, mv)


def _ffn_act_kernel(g_ref, halo_ref, up_ref, cw_ref, cb_ref, o_ref):
    t = pl.program_id(2)
    g = g_ref[0]
    prev = halo_ref[0] * jnp.where(t > 0, 1.0, 0.0)
    gc = cb_ref[...] + cw_ref[2:3, :] * g
    for s in (1, 2):
        gc = gc + cw_ref[2 - s:3 - s, :] * _shift_rows(g, prev, s)
    o_ref[0] = (jax.nn.gelu(gc, approximate=True) * up_ref[0]).astype(o_ref.dtype)


def ffn_act(gate, gate_cb0, up, up_cb0, conv_w, conv_b, *, tt=256, tc=512):
    B, T, _ = gate.shape
    C = conv_w.shape[1]
    tt = min(tt, T)
    hb = tt // SUBLANES
    return pl.pallas_call(
        _ffn_act_kernel,
        grid=(B, C // tc, T // tt),
        in_specs=[pl.BlockSpec((1, tt, tc), lambda b, c, t: (b, t, gate_cb0 + c)),
                  pl.BlockSpec((1, SUBLANES, tc), lambda b, c, t: (b, jnp.maximum(t * hb - 1, 0), gate_cb0 + c)),
                  pl.BlockSpec((1, tt, tc), lambda b, c, t: (b, t, up_cb0 + c)),
                  pl.BlockSpec((3, tc), lambda b, c, t: (0, c)),
                  pl.BlockSpec((1, tc), lambda b, c, t: (0, c))],
        out_specs=pl.BlockSpec((1, tt, tc), lambda b, c, t: (b, t, c)),
        out_shape=jax.ShapeDtypeStruct((B, T, C), BF16),
        compiler_params=_params("parallel", "parallel", "parallel"),
        name="ffn_act",
    )(gate, gate, up, conv_w.astype(F32), conv_b.reshape(1, C).astype(F32))


def _ffn_in_kernel(*refs, blocks_per_seq, sub, has_into, emit_w):
    x_ref, wg_ref, wu_ref, cw_ref, cb_ref = refs[:5]
    pos = 6 if has_into else 5
    o_ref, tail_ref = refs[pos], refs[pos + 1]
    carry_scr = refs[-1]
    i = pl.program_id(0)
    j = pl.program_id(1)
    tm = x_ref.shape[0]

    @pl.when(i % blocks_per_seq == 0)
    def _():
        carry_scr[j] = jnp.zeros(carry_scr.shape[1:], F32)

    prev = carry_scr[j]
    wg = wg_ref[...].astype(BF16)
    wu = wu_ref[...].astype(BF16)
    if emit_w:
        refs[pos + 2][...] = wg
        refs[pos + 3][...] = wu
    for c in range(tm // sub):
        rows = slice(c * sub, (c + 1) * sub)
        x = x_ref[rows, :]
        gate = _dot(x, wg)
        up = _dot(x, wu)
        gc = cb_ref[...] + cw_ref[2:3, :] * gate
        for s in (1, 2):
            gc = gc + cw_ref[2 - s:3 - s, :] * _shift_rows(gate, prev, s)
        o_ref[rows, :] = (jax.nn.gelu(gc, approximate=True) * up).astype(o_ref.dtype)
        prev = gate[sub - SUBLANES:, :]
    carry_scr[j] = prev
    tail_ref[0] = prev


def _ffn_in_call(x, wg, wu, layer, up_off, conv_w, conv_b, *, seq_len, row0, rows, tm, tn, sub, into, emit_w):
    M, D = x.shape
    FF = conv_w.shape[1]
    tm = min(tm, seq_len)
    bps = seq_len // tm
    nj = FF // tn
    i0 = row0 // tm
    uo = up_off // tn
    wspec = lambda off: (pl.BlockSpec((D, tn), lambda i, j: (0, j + off)) if layer is None else
                         pl.BlockSpec((None, D, tn), lambda i, j: (layer, 0, j + off)))
    in_specs = [pl.BlockSpec((tm, D), lambda i, j: (i + i0, 0), pipeline_mode=pl.Buffered(1)),
                wspec(0), wspec(uo),
                pl.BlockSpec((3, tn), lambda i, j: (0, j)),
                pl.BlockSpec((1, tn), lambda i, j: (0, j))]
    args = [x, wg, wu, conv_w.astype(F32), conv_b.reshape(1, FF).astype(F32)]
    aliases = {}
    if into is not None:
        aliases = {len(args): 0}
        in_specs.append(pl.BlockSpec(memory_space=pl.ANY))
        args.append(into)
    out_specs = [pl.BlockSpec((tm, tn), lambda i, j: (i + i0, j)),
                 pl.BlockSpec((1, SUBLANES, tn), lambda i, j: (i, 0, j))]
    out_shape = [jax.ShapeDtypeStruct((M, FF), BF16),
                 jax.ShapeDtypeStruct((rows // tm, SUBLANES, FF), F32)]
    if emit_w:
        out_specs += [pl.BlockSpec((D, tn), lambda i, j: (0, j))] * 2
        out_shape += [jax.ShapeDtypeStruct((D, FF), BF16)] * 2
    outs = pl.pallas_call(
        functools.partial(_ffn_in_kernel, blocks_per_seq=bps, sub=min(sub, tm), has_into=into is not None,
                          emit_w=emit_w),
        grid=(rows // tm, nj),
        in_specs=in_specs,
        out_specs=out_specs,
        out_shape=out_shape,
        input_output_aliases=aliases,
        scratch_shapes=[pltpu.VMEM((nj, SUBLANES, tn), F32)],
        compiler_params=_params("arbitrary", "arbitrary"),
        name="ffn_in_fused",
    )(*args)
    return (outs[0], outs[1][bps - 1::bps]) + tuple(outs[2:])


def ffn_in_fused(x, w_in, layer, conv_w, conv_b, wcache, key, *, seq_len, rows=None, sub=256):
    M = x.shape[0] if rows is None else rows
    FF = conv_w.shape[1]
    act, tail0, wg, wu = _ffn_in_call(x, w_in, w_in, layer, FF, conv_w, conv_b, seq_len=seq_len, row0=0,
                                      rows=seq_len, tm=2048, tn=256, sub=sub, into=None, emit_w=True)
    wcache[key] = (wg, wu)
    if M == seq_len:
        return act, tail0
    act, tails = _ffn_in_call(x, wg, wu, None, 0, conv_w, conv_b, seq_len=seq_len, row0=seq_len,
                              rows=M - seq_len, tm=1024, tn=512, sub=sub, into=act, emit_w=False)
    return act, jnp.concatenate([tail0, tails], axis=0)


def _rwkv_mix_kernel(x_ref, halo_ref, s0_ref, g_ref, mu_ref, *out_refs, n_valid_last):
    t = pl.program_id(1)
    nt = pl.num_programs(1)
    o_refs, hl_ref = out_refs[:6], out_refs[6]

    def norm(v):
        return v * lax.rsqrt(jnp.mean(v * v, axis=-1, keepdims=True) + NORM_EPS) * g_ref[...]

    h = norm(x_ref[...])
    hp = norm(halo_ref[...])
    first = jnp.where(t > 0, 1.0, 0.0)
    hp = hp * first + jnp.broadcast_to(s0_ref[0], hp.shape) * (1.0 - first)
    xx = _shift_rows(h, hp, 1) - h
    for j in range(6):
        o_refs[j][...] = (h + xx * mu_ref[j:j + 1, :]).astype(o_refs[j].dtype)

    @pl.when(t == nt - 1)
    def _():
        hl_ref[0] = h[n_valid_last - 1:n_valid_last, :]


def rwkv_mix(x, shift0, g, mu, *, B, T, n_valid, tt=256, out_rows=None):
    D = x.shape[1]
    tt = min(tt, T)
    nt = T // tt
    hb = tt // SUBLANES
    out_rows = B * T if out_rows is None else out_rows
    n_valid_last = n_valid - (nt - 1) * tt
    blk = pl.BlockSpec((tt, D), lambda b, t: (b * nt + t, 0))
    one = pl.BlockSpec((1, 1, D), lambda b, t: (b, 0, 0))
    return pl.pallas_call(
        functools.partial(_rwkv_mix_kernel, n_valid_last=n_valid_last),
        grid=(B, nt),
        in_specs=[blk,
                  pl.BlockSpec((SUBLANES, D), lambda b, t: (jnp.maximum((b * nt + t) * hb - 1, 0), 0)),
                  one,
                  pl.BlockSpec((1, D), lambda b, t: (0, 0)),
                  pl.BlockSpec((6, D), lambda b, t: (0, 0))],
        out_specs=[blk] * 6 + [one],
        out_shape=[jax.ShapeDtypeStruct((out_rows, D), BF16)] * 6 + [jax.ShapeDtypeStruct((B, 1, D), F32)],
        compiler_params=_params("parallel", "arbitrary"),
        name="rwkv_mix",
    )(x, x, shift0, g.reshape(1, D).astype(F32), mu.astype(F32))


def _rwkv_chunk_kernel(r_ref, k_ref, v_ref, wp_ref, ap_ref, g_ref, kk_ref, ka_ref, rk_ref, lng_ref, lnb_ref,
                       s0_ref, o_ref, so_ref, s_scr, *, hs, C):
    c = pl.program_id(2)
    nc = pl.num_programs(2)
    R, L = r_ref.shape
    n_pairs = L // LANES

    @pl.when(c == 0)
    def _():
        z = jnp.zeros((hs, hs), F32)
        for p in range(n_pairs):
            s_scr[p] = jnp.concatenate([jnp.concatenate([s0_ref[0, 2 * p], z], axis=1),
                                        jnp.concatenate([z, s0_ref[0, 2 * p + 1]], axis=1)], axis=0)

    ti = lax.broadcasted_iota(jnp.int32, (C, C), 0)
    si = lax.broadcasted_iota(jnp.int32, (C, C), 1)
    tri = jnp.where(ti >= si, 1.0, 0.0).astype(BF16)
    lane = lax.broadcasted_iota(jnp.int32, (1, LANES), 1)
    m0 = lane < hs
    C2 = 2 * C
    ri = lax.broadcasted_iota(jnp.int32, (C2, C2), 0)
    ci = lax.broadcasted_iota(jnp.int32, (C2, C2), 1)
    same = (ri // C) == (ci // C)
    strict = same & ((ri % C) > (ci % C))
    lower = (ri % C) >= (ci % C)
    vi = lax.broadcasted_iota(jnp.int32, (LANES, LANES), 0) // hs
    vj = lax.broadcasted_iota(jnp.int32, (LANES, LANES), 1) // hs
    blockdiag = vi == vj
    n_steps = int(math.log2(C))
    wide = C2 % LANES == 0
    P = range(n_pairs)
    sls = [slice(p * LANES, (p + 1) * LANES) for p in P]
    inv = 1.0 / hs

    def stack_heads(x):
        return jnp.concatenate([jnp.where(m0, x, 0.0), jnp.where(m0, 0.0, x)], axis=0).astype(BF16)

    S = [s_scr[p] for p in P]
    for sc in range(R // C):
        rows = slice(sc * C, (sc + 1) * C)
        r = r_ref[rows, :]
        k = k_ref[rows, :]
        v = v_ref[rows, :]
        logw = -jnp.exp(-jax.nn.softplus(-wp_ref[rows, :]) - 0.5)
        a = jax.nn.sigmoid(ap_ref[rows, :])
        kk = k * kk_ref[...]
        kk = kk / jnp.maximum(jnp.sqrt(_group_sum_bcast(kk * kk, hs, terms=1)), 1e-12)
        k2 = k * (1.0 + (a - 1.0) * ka_ref[...])
        bm = kk * a
        cum = _dot_exact_rhs_lhs(tri, logw)
        e_in = jnp.exp(cum)
        e_out = jnp.exp(-cum)
        e_end = jnp.exp(cum[C - 1:C, :] - cum)
        g_end = jnp.exp(cum[C - 1:C, :])
        rt = r * e_in
        kkt = kk * jnp.exp(cum - logw)
        bh = bm * e_out
        kh = k2 * e_out
        bbar = bm * e_end
        kbar = k2 * e_end

        S16 = [s_.astype(BF16) for s_ in S]
        V16 = [v[:, sl].astype(BF16) for sl in sls]
        lhk = [stack_heads(kkt[:, sl]) for sl in sls]
        if wide:
            gbk = [_nt(lhk[p], jnp.concatenate([stack_heads(bh[:, sls[p]]), stack_heads(kh[:, sls[p]])], axis=0))
                   for p in P]
            nmat = [jnp.where(strict, -g_[:, :C2], 0.0) for g_ in gbk]
            auk = [jnp.where(strict, -g_[:, C2:], 0.0).astype(BF16) for g_ in gbk]
        else:
            nmat = [jnp.where(strict, -_nt(lhk[p], stack_heads(bh[:, sls[p]])), 0.0) for p in P]
            auk = [jnp.where(strict, -_nt(lhk[p], stack_heads(kh[:, sls[p]])), 0.0).astype(BF16) for p in P]
        sprod = [_nt(jnp.concatenate([lhk[p], rt[:, sls[p]].astype(BF16)], axis=0), S16[p]) for p in P]
        u = [_dot(auk[p], jnp.concatenate([V16[p], V16[p]], axis=0)) - sprod[p][:C2] for p in P]
        npow = [n_.astype(BF16) for n_ in nmat]
        for it in range(n_steps):
            last = it + 1 == n_steps
            if wide and not last:
                res = [_dot(npow[p], jnp.concatenate([u[p].astype(BF16), npow[p]], axis=1)) for p in P]
                u = [u[p] + res[p][:, :LANES] for p in P]
                npow = [res[p][:, LANES:].astype(BF16) for p in P]
            else:
                u = [u[p] + _dot(npow[p], u[p].astype(BF16)) for p in P]
                if not last:
                    npow = [_dot(npow[p], npow[p]).astype(BF16) for p in P]
        uv = [jnp.concatenate([jnp.where(m0, u[p][:C], u[p][C:]).astype(BF16), V16[p]], axis=0) for p in P]
        ar = [jnp.where(lower, _nt(stack_heads(rt[:, sls[p]]),
                                   jnp.concatenate([bh[:, sls[p]], kh[:, sls[p]]], axis=0).astype(BF16)),
                        0.0).astype(BF16) for p in P]
        tmat = [_dot(ar[p], uv[p]) for p in P]
        ys = [sprod[p][C2:] + jnp.where(m0, tmat[p][:C], tmat[p][C:]) for p in P]
        upd = [_tn(uv[p], jnp.concatenate([bbar[:, sls[p]], kbar[:, sls[p]]], axis=0).astype(BF16)) for p in P]
        S = [S[p] * g_end[:, sls[p]] + jnp.where(blockdiag, upd[p], 0.0) for p in P]

        y = ys[0] if n_pairs == 1 else jnp.concatenate(ys, axis=-1)
        mean = _group_sum_bcast(y, hs, terms=1) * inv
        yc = y - mean
        var = _group_sum_bcast(yc * yc, hs, terms=1) * inv
        yn = yc * lax.rsqrt(var + RW_GN_EPS) * lng_ref[...] + lnb_ref[...]
        bonus = _group_sum_bcast(r * k2 * rk_ref[...], hs, terms=1) * v
        o_ref[rows, :] = ((yn + bonus) * g_ref[rows, :]).astype(o_ref.dtype)

    for p in P:
        s_scr[p] = S[p]

    @pl.when(c == nc - 1)
    def _():
        for p in P:
            so_ref[0, 2 * p] = S[p][:hs, :hs]
            so_ref[0, 2 * p + 1] = S[p][hs:, hs:]


def _dot_exact_rhs_lhs(m_bf16, x):
    hi = x.astype(BF16)
    lo = (x - hi.astype(F32)).astype(BF16)
    return _dot(m_bf16, hi) + _dot(m_bf16, lo)


def rwkv_chunked(r, k, v, wp, ap, g, k_k, k_a, r_k, ln_g, ln_b, s0, *, B, T, hs, chunk, chunks_per_step=1,
                 lanes_per_step=1024, out_rows=None):
    D = r.shape[1]
    L = min(lanes_per_step, D)
    npg = L // LANES
    rows = chunk * chunks_per_step
    nc = T // rows
    out_rows = B * T if out_rows is None else out_rows
    seq = pl.BlockSpec((rows, L), lambda b, hg, c: (b * nc + c, hg))
    par = pl.BlockSpec((1, L), lambda b, hg, c: (0, hg))
    st = pl.BlockSpec((1, 2 * npg, hs, hs), lambda b, hg, c: (b, hg, 0, 0))
    row = lambda x: x.reshape(1, D).astype(F32)
    return pl.pallas_call(
        functools.partial(_rwkv_chunk_kernel, hs=hs, C=chunk),
        grid=(B, D // L, nc),
        in_specs=[seq] * 6 + [par] * 5 + [st],
        out_specs=[seq, st],
        out_shape=[jax.ShapeDtypeStruct((out_rows, D), BF16),
                   jax.ShapeDtypeStruct(s0.shape, F32)],
        scratch_shapes=[pltpu.VMEM((npg, LANES, LANES), F32)],
        compiler_params=_params("parallel", "parallel", "arbitrary"),
        name="rwkv_chunked",
    )(r, k, v, wp, ap, g, row(k_k), row(k_a), row(r_k), row(ln_g), row(ln_b), s0)


def _pad_rows(x, front, back):
    return jnp.pad(x, ((0, 0), (front, back), (0, 0)))


def kernel(x_prompt, x_sample, mem_prompt, state_lru_conv, state_lru_h, cache_swa_k, cache_swa_v,
           state_rwkv_shift, state_rwkv_wkv, cache_mem_k, cache_mem_v, state_ffn_conv,
           a_norm_g, a_w_in, a_conv_w, a_conv_b, a_gate_a_w, a_gate_a_b, a_gate_x_w, a_gate_x_b,
           a_lambda, b_q_norm_g, b_k_norm_g, b_sink, a_w_out,
           c_norm_g, c_mu, c_w_r, c_w_k, c_w_v, c_w_o, c_w0, c_w1, c_w2, c_a0, c_a1, c_a2,
           c_g1, c_g2, c_k_k, c_k_a, c_r_k, c_ln_g, c_ln_b,
           m_norm_g, m_mem_norm_g, m_w_q, m_w_kv, m_q_norm_g, m_k_norm_g, m_w_o,
           f_norm_g, f_w_in, f_conv_w, f_conv_b, f_w_out):
    D = x_prompt.shape[-1]
    depth = m_norm_g.shape[0]
    W = a_conv_w.shape[-1]
    KA = a_conv_w.shape[1]
    hd = b_q_norm_g.shape[-1]
    n_q = b_sink.shape[-1]
    n_kv = cache_swa_k.shape[3]
    win = cache_swa_k.shape[2]
    QW, KW = n_q * hd, n_kv * hd
    hs = c_r_k.shape[-1]
    n_rw = c_r_k.shape[1]
    m_heads, m_hd = cache_mem_k.shape[3], cache_mem_k.shape[4]
    MW = m_heads * m_hd
    FF = f_conv_w.shape[-1]
    KF = f_conv_w.shape[1]
    bf = lambda w: w.astype(BF16)
    wcache = {}

    Bp, S = x_prompt.shape[:2]
    Bs, Ts = x_sample.shape[:2]
    Mp, Ms = Bp * S, Bs * Ts
    Mt = Mp + Ms
    tma = _row_tile(Mt, 1280)
    n_a = a_norm_g.shape[0]
    n_c = c_norm_g.shape[0]
    ML = mem_prompt.shape[1]
    PAD8, PAD16 = SUBLANES, 2 * SUBLANES

    def mm(xin, w, layer, key, **kw):
        return dense(xin, w, layer, wcache, (key, layer), tm=tma, **kw)

    def seq_tail(arr, n, cols=slice(None)):
        return jnp.stack([arr[(b + 1) * S - n:(b + 1) * S, cols] for b in range(Bp)])

    def put_sample(buf, rows_s):
        return lax.dynamic_update_slice(buf, rows_s.astype(buf.dtype), (Mp, 0))

    mem_flat = mem_prompt.reshape(Bp * ML, D)
    mks, mvs = [], []
    for l in range(depth):
        mn = rmsnorm_rows(mem_flat, m_mem_norm_g[l])
        kv = matmul(mn, m_w_kv, layer=l)
        mk = headnorm(kv, 0, MW, m_k_norm_g[l], m_hd, F32)
        mks.append(mk.reshape(Bp, ML, m_heads, m_hd))
        mvs.append(kv[:, MW:].reshape(Bp, ML, m_heads, m_hd))
    p_mem_k = jnp.stack(mks)
    p_mem_v = jnp.stack(mvs)

    xf = jnp.concatenate([x_prompt.reshape(Mp, D), x_sample.reshape(Ms, D)], axis=0)
    po = {k_: [] for k_ in ("lc", "lh", "sk", "sv", "rs", "rw", "fc")}
    so = {k_: [] for k_ in ("lc", "lh", "sk", "sv", "rs", "rw", "fc")}
    ia = ic = 0
    y_p = y_s = None
    for l in range(depth):
        if l % 2 == 0:
            i = ia
            ia += 1
            assert W == QW
            h = rmsnorm_rows(xf, a_norm_g[i])
            zz = mm(h, a_w_in, i, "a_w_in_rg", n=2 * W)
            qkv = mm(h, a_w_in, i, "a_w_in_qkv", n0=2 * W)
            qn = headnorm(qkv, 0, QW, b_q_norm_g[i], hd, BF16)
            kn = headnorm(qkv, QW // KW, KW, b_k_norm_g[i], hd, F32)
            vv = qkv[:, QW + KW:]
            lru_w = (a_conv_w[i], a_conv_b[i], a_gate_a_w[i], a_gate_a_b[i], a_gate_x_w[i], a_gate_x_b[i],
                     a_lambda[i])
            mix, hl_p = lru_mixer(zz, 0, zz, 1, jnp.zeros((Bp, 1, W), F32), *lru_w, B=Bp, T=S, n_pad=0, tt=256,
                                  out_rows=Mt, out_cols=W + QW)
            mix = swa_attention(qn, bf(kn), bf(vv), b_sink[i], B=Bp, nq=S // win, nkb=S // win, j0=0, win=win,
                                hd=hd, into=mix, out_cb=W // QW)
            po["lc"].append(seq_tail(zz, KA - 1, slice(0, W)))
            po["lh"].append(hl_p.reshape(Bp, W))
            po["sk"].append(seq_tail(kn, win).reshape(Bp, win, n_kv, hd))
            po["sv"].append(seq_tail(vv, win).reshape(Bp, win, n_kv, hd))
            zz_s = zz[Mp:].reshape(Bs, Ts, 2 * W)
            n_pad = PAD8 - Ts
            xr_hist = jnp.concatenate([state_lru_conv[i].astype(F32), zz_s[:, :, :W]], axis=1)
            xr_p = _pad_rows(xr_hist, PAD8 - xr_hist.shape[1], 0).reshape(Bs * PAD8, W)
            yg_p = _pad_rows(zz_s[:, :, W:], n_pad, 0).reshape(Bs * PAD8, W)
            oa_s, hl_s = lru_mixer(xr_p, 0, yg_p, 0, state_lru_h[i].reshape(Bs, 1, W).astype(F32), *lru_w,
                                   B=Bs, T=PAD8, n_pad=n_pad, tt=PAD8)
            oa_s = oa_s.reshape(Bs, PAD8, W)[:, n_pad:].reshape(Ms, W)
            kc = cache_swa_k[i].reshape(Bs, win, KW)
            vc = cache_swa_v[i].reshape(Bs, win, KW)
            k_all = jnp.concatenate([kc, kn[Mp:].reshape(Bs, Ts, KW)], axis=1)
            v_all = jnp.concatenate([vc, vv[Mp:].reshape(Bs, Ts, KW)], axis=1)
            q_s = _pad_rows(qn[Mp:].reshape(Bs, Ts, QW), 0, PAD16 - Ts).reshape(Bs * PAD16, QW)
            o_s = swa_attention(q_s, bf(_pad_rows(k_all, 0, win - Ts)).reshape(Bs * 2 * win, KW),
                                bf(_pad_rows(v_all, 0, win - Ts)).reshape(Bs * 2 * win, KW), b_sink[i],
                                B=Bs, nq=1, nkb=2, j0=1, win=win, hd=hd, tq=PAD16)
            o_s = o_s.reshape(Bs, PAD16, QW)[:, :Ts].reshape(Ms, QW)
            mix = put_sample(mix, jnp.concatenate([oa_s, o_s], axis=-1))
            so["lc"].append(xr_hist[:, Ts:])
            so["lh"].append(hl_s.reshape(Bs, W))
            so["sk"].append(k_all[:, Ts:].reshape(Bs, win, n_kv, hd))
            so["sv"].append(v_all[:, Ts:].reshape(Bs, win, n_kv, hd))
            xf = mm(mix, a_w_out, i, "a_w_out", res=xf)
        else:
            i = ic
            ic += 1
            mixes = rwkv_mix(xf, jnp.zeros((Bp, 1, D), F32), c_norm_g[i], c_mu[i], B=Bp, T=S, n_valid=S,
                             out_rows=Mt)
            x_s = _pad_rows(xf[Mp:].reshape(Bs, Ts, D), 0, PAD8 - Ts).reshape(Bs * PAD8, D)
            mixes_s = rwkv_mix(x_s, state_rwkv_shift[i].reshape(Bs, 1, D).astype(F32), c_norm_g[i], c_mu[i],
                               B=Bs, T=PAD8, n_valid=Ts, tt=PAD8)
            take = lambda m: m.reshape(Bs, PAD8, D)[:, :Ts].reshape(Ms, D)
            xr, xw, xk, xv, xa, xg = [put_sample(mixes[j], take(mixes_s[j])) for j in range(6)]
            po["rs"].append(mixes[6].reshape(Bp, D))
            so["rs"].append(mixes_s[6].reshape(Bs, D))
            r = mm(xr, c_w_r, i, "c_w_r")
            k = mm(xk, c_w_k, i, "c_w_k")
            v = mm(xv, c_w_v, i, "c_w_v")
            lora = dict(layer=i, tm=tma)
            wp = matmul(matmul(xw, c_w1, act="tanh", out_dtype=BF16, **lora), c_w2, bias=c_w0[i], **lora)
            ap = matmul(matmul(xa, c_a1, out_dtype=BF16, **lora), c_a2, bias=c_a0[i], **lora)
            gl = c_g1[i].shape[1]
            glp = -(-gl // LANES) * LANES
            g1p = jnp.pad(c_g1[i], ((0, 0), (0, glp - gl)))
            g2p = jnp.pad(c_g2[i], ((0, glp - gl), (0, 0)))
            g = matmul(matmul(xg, g1p, act="sigmoid", out_dtype=BF16, tm=tma), g2p, tm=tma)
            rw_par = (c_k_k[i], c_k_a[i], c_r_k[i], c_ln_g[i], c_ln_b[i])
            yo, s_end = rwkv_chunked(r, k, v, wp, ap, g, *rw_par, jnp.zeros((Bp, n_rw, hs, hs), F32),
                                     B=Bp, T=S, hs=hs, chunk=64, chunks_per_step=4, out_rows=Mt)
            po["rw"].append(s_end)
            cs = PAD16
            sq = lambda t, fill=0.0: jnp.pad(t[Mp:].reshape(Bs, Ts, D), ((0, 0), (0, cs - Ts), (0, 0)),
                                             constant_values=fill).reshape(Bs * cs, D)
            yo_s, s_end_s = rwkv_chunked(sq(r), sq(k), sq(v), sq(wp, -1e30), sq(ap), sq(g), *rw_par,
                                         state_rwkv_wkv[i].astype(F32), B=Bs, T=cs, hs=hs, chunk=cs,
                                         lanes_per_step=2048)
            yo = put_sample(yo, yo_s.reshape(Bs, cs, D)[:, :Ts].reshape(Ms, D))
            so["rw"].append(s_end_s)
            xf = mm(yo, c_w_o, i, "c_w_o", res=xf)
        hm = rmsnorm_rows(xf, m_norm_g[l])
        q = mm(hm, m_w_q, l, "m_w_q")
        om = mem_attention(q, m_q_norm_g[l], bf(p_mem_k[l]).reshape(Bp, ML, MW), bf(p_mem_v[l]).reshape(Bp, ML, MW),
                           B=Bp, T=S, n_heads=m_heads, out_rows=Mt)
        q_s = _pad_rows(q[Mp:].reshape(Bs, Ts, MW), 0, PAD8 - Ts).reshape(Bs * PAD8, MW)
        om_s = mem_attention(q_s, m_q_norm_g[l], bf(cache_mem_k[l]).reshape(Bs, ML, MW),
                             bf(cache_mem_v[l]).reshape(Bs, ML, MW), B=Bs, T=PAD8, n_heads=m_heads, tt=PAD8)
        om = put_sample(om, om_s.reshape(Bs, PAD8, MW)[:, :Ts].reshape(Ms, MW))
        xf = mm(om, m_w_o, l, "m_w_o", res=xf)
        hf = rmsnorm_rows(xf, f_norm_g[l])
        act, tail = ffn_in_fused(hf, f_w_in, l, f_conv_w[l], f_conv_b[l], wcache, ("f_w_in", l), seq_len=S, rows=Mp)
        po["fc"].append(tail[:, SUBLANES - (KF - 1):, :])
        wg, wu = wcache[("f_w_in", l)]
        hf_s = hf[Mp:]
        gate_s = matmul(hf_s, wg, tn=1024).reshape(Bs, Ts, FF)
        up_s = matmul(hf_s, wu, tn=1024).reshape(Bs, Ts, FF)
        g_hist = jnp.concatenate([state_ffn_conv[l].astype(F32), gate_s], axis=1)
        g_p = _pad_rows(g_hist, PAD8 - g_hist.shape[1], 0)
        up_p = _pad_rows(up_s, PAD8 - Ts, 0)
        act_s = ffn_act(g_p, 0, up_p, 0, f_conv_w[l], f_conv_b[l], tt=PAD8, tc=FF)[:, PAD8 - Ts:]
        act = put_sample(act, act_s.reshape(Ms, FF))
        so["fc"].append(g_hist[:, Ts:])
        if l + 1 < depth:
            xf = mm(act, f_w_out, l, "f_w_out", res=xf, tn=1024, tk=2048)
        else:
            y_p = dense(act, f_w_out, l, wcache, ("f_w_out", l), rows=Mp, out_rows=Mp, res=xf, tn=1024, tk=2048)
            y_s = matmul(act, wcache[("f_w_out", l)], row0=Mp, rows=Ms, out_rows=Ms, out_row0=0, res=xf,
                         tn=1024, tk=2048)
    st = jnp.stack
    return (y_p.reshape(Bp, S, D), y_s.reshape(Bs, Ts, D),
            st(po["lc"]), st(po["lh"]), st(po["sk"]), st(po["sv"]), st(po["rs"]), st(po["rw"]),
            p_mem_k, p_mem_v, st(po["fc"]),
            st(so["lc"]), st(so["lh"]), st(so["sk"]), st(so["sv"]), st(so["rs"]), st(so["rw"]), st(so["fc"]))
```

```python
import functools
import math

import jax
import jax.numpy as jnp
from jax import lax
from jax.experimental import pallas as pl
from jax.experimental.pallas import tpu as pltpu

F32 = jnp.float32
BF16 = jnp.bfloat16

NORM_EPS = 1e-6
RW_GN_EPS = 64e-5
LRU_C = 8.0
LANES = 128
SUBLANES = 8
VMEM_LIMIT_BYTES = 56 * 1024 * 1024


def _params(*sem):
    return pltpu.CompilerParams(dimension_semantics=sem, vmem_limit_bytes=VMEM_LIMIT_BYTES)


def _nt(a, b):
    return lax.dot_general(a, b, (((1,), (1,)), ((), ())), preferred_element_type=F32)


def _tn(a, b):
    return lax.dot_general(a, b, (((0,), (0,)), ((), ())), preferred_element_type=F32)


def _dot(a, b):
    return jnp.dot(a, b, preferred_element_type=F32)


def _group_sum_bcast(x, width, terms=2):
    m, L = x.shape
    n = L // LANES
    if width == LANES:
        parts = []
        for c in range(n):
            s = jnp.sum(x[:, c * LANES:(c + 1) * LANES], axis=-1, keepdims=True)
            parts.append(jnp.broadcast_to(s, (m, LANES)))
        return parts[0] if n == 1 else jnp.concatenate(parts, axis=-1)
    li = lax.broadcasted_iota(jnp.int32, (LANES, LANES), 0) // width
    lj = lax.broadcasted_iota(jnp.int32, (LANES, LANES), 1) // width
    e = jnp.where(li == lj, 1.0, 0.0).astype(BF16)
    xs = x if n == 1 else jnp.concatenate([x[:, c * LANES:(c + 1) * LANES] for c in range(n)], axis=0)
    hi = xs.astype(BF16)
    out = _dot(hi, e)
    if terms > 1:
        out = out + _dot((xs - hi.astype(F32)).astype(BF16), e)
    return out if n == 1 else jnp.concatenate([out[c * m:(c + 1) * m] for c in range(n)], axis=-1)


def _shift_rows(x, prev8, s):
    rolled = pltpu.roll(x, s, 0)
    top = jnp.where(lax.broadcasted_iota(jnp.int32, (SUBLANES, 1), 0) < s,
                    pltpu.roll(prev8, s, 0), rolled[0:SUBLANES])
    if x.shape[0] == SUBLANES:
        return top
    return jnp.concatenate([top, rolled[SUBLANES:]], axis=0)


def _rmsnorm_kernel(x_ref, g_ref, o_ref):
    x = x_ref[...]
    ms = jnp.mean(x * x, axis=-1, keepdims=True)
    o_ref[...] = (x * lax.rsqrt(ms + NORM_EPS) * g_ref[...]).astype(o_ref.dtype)


def _row_tile(m, target):
    best = None
    for d in range(16, min(m, target) + 1, 16):
        if m % d == 0:
            best = d
    return best or m


def rmsnorm_rows(x, g, out_dtype=BF16, tm=512):
    M, D = x.shape
    tm = _row_tile(M, tm)
    return pl.pallas_call(
        _rmsnorm_kernel,
        grid=(M // tm,),
        in_specs=[pl.BlockSpec((tm, D), lambda i: (i, 0)), pl.BlockSpec((1, D), lambda i: (0, 0))],
        out_specs=pl.BlockSpec((tm, D), lambda i: (i, 0)),
        out_shape=jax.ShapeDtypeStruct((M, D), out_dtype),
        compiler_params=_params("parallel"),
        name="rmsnorm_rows",
    )(x, g.reshape(1, D))


def _mm_kernel(*refs, nk, has_bias, has_res, has_into, emit_w, in_place, act):
    x_ref, w_ref = refs[0], refs[1]
    pos = 2
    b_ref = r_ref = wb_ref = acc_ref = None
    if has_bias:
        b_ref = refs[pos]
        pos += 1
    if has_res:
        r_ref = refs[pos]
        pos += 1
    if has_into:
        pos += 1
    o_ref = refs[pos]
    pos += 1
    if emit_w:
        wb_ref = refs[pos]
        pos += 1
    if nk > 1 and not in_place:
        acc_ref = refs[pos]

    def weights(cs=slice(None)):
        w16 = w_ref[:, cs].astype(BF16)
        if emit_w:
            wb_ref[:, cs] = w16
        return w16

    def epilogue(y):
        if has_bias:
            y = y + b_ref[...]
        if act == "tanh":
            y = jnp.tanh(y)
        elif act == "sigmoid":
            y = jax.nn.sigmoid(y)
        if has_res:
            y = y + r_ref[...]
        o_ref[...] = y.astype(o_ref.dtype)

    if nk > 1 and in_place:
        k = pl.program_id(2)
        tn = o_ref.shape[1]
        cw = min(tn, 2 * LANES)

        def sweep(first):
            x = x_ref[...]
            for c in range(tn // cw):
                cs = slice(c * cw, (c + 1) * cw)
                part = _dot(x, weights(cs))
                if not first:
                    o_ref[:, cs] += part
                elif has_res:
                    o_ref[:, cs] = part + r_ref[:, cs]
                else:
                    o_ref[:, cs] = part

        pl.when(k == 0)(functools.partial(sweep, True))
        pl.when(k > 0)(functools.partial(sweep, False))
        return

    part = _dot(x_ref[...], weights())
    if nk == 1:
        epilogue(part)
        return
    k = pl.program_id(2)

    @pl.when(k == 0)
    def _():
        acc_ref[...] = part

    @pl.when(k > 0)
    def _():
        acc_ref[...] += part

    @pl.when(k == nk - 1)
    def _():
        epilogue(acc_ref[...])


def _pick(n, prefs):
    for p in prefs:
        if n % p == 0:
            return p
    return n


def matmul(x, w, *, layer=None, n0=0, n=None, bias=None, res=None, act=None, out_dtype=F32,
           tm=1024, tn=512, tk=4096, row0=0, rows=None, into=None, emit_w=False, out_rows=None, out_row0=None):
    M, K = x.shape
    N = w.shape[-1] - n0 if n is None else n
    rows = M - row0 if rows is None else rows
    tm = min(tm, rows)
    out_rows = M if out_rows is None else out_rows
    o0 = (row0 if out_row0 is None else out_row0) // tm
    tn = _pick(math.gcd(N, n0) if n0 else N, (tn, 512, 256, 128))
    tk = _pick(K, (tk, 2048, 1024, 512))
    nk = K // tk
    j0 = n0 // tn
    i0 = row0 // tm
    assert row0 % tm == 0 and rows % tm == 0 and (not emit_w or rows == tm)
    in_specs = [pl.BlockSpec((tm, tk), lambda i, j, k: (i + i0, k)),
                pl.BlockSpec((tk, tn), lambda i, j, k: (k, j + j0)) if layer is None else
                pl.BlockSpec((None, tk, tn), lambda i, j, k: (layer, k, j + j0))]
    args = [x, w]
    if bias is not None:
        in_specs.append(pl.BlockSpec((1, tn), lambda i, j, k: (0, j)))
        args.append(bias.reshape(1, N).astype(F32))
    if res is not None:
        in_specs.append(pl.BlockSpec((tm, tn), lambda i, j, k: (i + i0, j)))
        args.append(res)
    aliases = {}
    if into is not None:
        aliases = {len(args): 0}
        in_specs.append(pl.BlockSpec(memory_space=pl.ANY))
        args.append(into)
    in_place = bias is None and act is None and out_dtype == F32
    kern = functools.partial(_mm_kernel, nk=nk, has_bias=bias is not None, has_res=res is not None,
                             has_into=into is not None, emit_w=emit_w, in_place=in_place, act=act)
    out_specs = [pl.BlockSpec((tm, tn), lambda i, j, k: (i + o0, j))]
    out_shape = [jax.ShapeDtypeStruct((out_rows, N), out_dtype)]
    if emit_w:
        out_specs.append(pl.BlockSpec((tk, tn), lambda i, j, k: (k, j)))
        out_shape.append(jax.ShapeDtypeStruct((K, N), BF16))
    outs = pl.pallas_call(
        kern,
        grid=(rows // tm, N // tn, nk),
        in_specs=in_specs,
        out_specs=out_specs,
        out_shape=out_shape,
        input_output_aliases=aliases,
        scratch_shapes=[pltpu.VMEM((tm, tn), F32)] if (nk > 1 and not in_place) else [],
        compiler_params=_params("parallel", "parallel", "arbitrary"),
        name="matmul",
    )(*args)
    return outs if emit_w else outs[0]


def dense(x, w, layer, wcache, key, **kw):
    rows = kw.pop("rows", x.shape[0])
    tm = min(kw.get("tm", 1024), rows)
    first, wb = matmul(x, w, layer=layer, rows=tm, emit_w=True, **kw)
    wcache[key] = wb
    if tm == rows:
        return first
    kw = {a: b for a, b in kw.items() if a not in ("n0", "n")}
    return matmul(x, wb, row0=tm, rows=rows - tm, into=first, **kw)


def _headnorm_kernel(x_ref, g_ref, o_ref, *, hd):
    x = x_ref[...]
    ms = _group_sum_bcast(x * x, hd) * (1.0 / hd)
    o_ref[...] = (x * lax.rsqrt(ms + NORM_EPS) * g_ref[...]).astype(o_ref.dtype)


def headnorm(x, col_block, width, g, hd, out_dtype, tm=512):
    M = x.shape[0]
    tm = _row_tile(M, tm)
    g_row = jnp.tile(g.astype(F32), width // hd).reshape(1, width)
    return pl.pallas_call(
        functools.partial(_headnorm_kernel, hd=hd),
        grid=(M // tm,),
        in_specs=[pl.BlockSpec((tm, width), lambda i: (i, col_block)),
                  pl.BlockSpec((1, width), lambda i: (0, 0))],
        out_specs=pl.BlockSpec((tm, width), lambda i: (i, 0)),
        out_shape=jax.ShapeDtypeStruct((M, width), out_dtype),
        compiler_params=_params("parallel"),
        name="headnorm",
    )(x, g_row)


def _lru_kernel(xr_ref, halo_ref, yg_ref, h0_ref, cw_ref, cb_ref, gaw_ref, gab_ref, gxw_ref, gxb_ref,
                lam_ref, o_ref, hl_ref, a_scr, u_scr, h_scr, *, n_pad, n_blocks, bw, scan_w):
    t = pl.program_id(1)
    tt, W = a_scr.shape

    @pl.when(t == 0)
    def _():
        h_scr[...] = h0_ref[0]

    x = xr_ref[...]
    prev = halo_ref[...] * jnp.where(t > 0, 1.0, 0.0)
    xc = cb_ref[...] + cw_ref[3:4, :] * x
    for s in (1, 2, 3):
        xc = xc + cw_ref[3 - s:4 - s, :] * _shift_rows(x, prev, s)

    nsp = -LRU_C * jax.nn.softplus(-lam_ref[...])
    if n_pad:
        live = (lax.broadcasted_iota(jnp.int32, (tt, 1), 0) >= n_pad) | (t > 0)
    for n in range(n_blocks):
        sl = slice(n * bw, (n + 1) * bw)
        xb = xc[:, sl]
        xb16 = xb.astype(BF16)
        r = jax.nn.sigmoid(_dot(xb16, gaw_ref[n]) + gab_ref[:, sl])
        i = jax.nn.sigmoid(_dot(xb16, gxw_ref[n]) + gxb_ref[:, sl])
        log_a = r * nsp[:, sl]
        a = jnp.exp(log_a)
        u = jnp.sqrt(jnp.maximum(-jnp.tanh(log_a) * (a * a + 1.0), 0.0)) * (i * xb)
        if n_pad:
            a = jnp.where(live, a, 1.0)
            u = jnp.where(live, u, 0.0)
        a_scr[:, sl] = a
        u_scr[:, sl] = u

    row8 = lax.broadcasted_iota(jnp.int32, (SUBLANES, 1), 0)
    for c in range(W // scan_w):
        cs = slice(c * scan_w, (c + 1) * scan_w)

        def body(gi, h, cs=cs):
            r0 = pl.multiple_of(gi * SUBLANES, SUBLANES)
            A = a_scr[pl.ds(r0, SUBLANES), cs]
            U = u_scr[pl.ds(r0, SUBLANES), cs]
            for s in (1, 2, 4):
                As = pltpu.roll(A, s, 0)
                Us = pltpu.roll(U, s, 0)
                m = row8 >= s
                U = jnp.where(m, A * Us + U, U)
                A = jnp.where(m, A * As, A)
            H = A * h + U
            u_scr[pl.ds(r0, SUBLANES), cs] = H
            return H[SUBLANES - 1:SUBLANES, :]

        h_end = lax.fori_loop(0, tt // SUBLANES, body, h_scr[:, cs])
        h_scr[:, cs] = h_end

    o_ref[...] = (u_scr[...] * jax.nn.gelu(yg_ref[...], approximate=True)).astype(o_ref.dtype)
    hl_ref[0] = h_scr[...]


def lru_mixer(xr, xr_cb, yg, yg_cb, h0, conv_w, conv_b, ga_w, ga_b, gx_w, gx_b, lam, *, B, T, n_pad, tt,
              out_rows=None, out_cols=None):
    W = conv_w.shape[1]
    nb, bw = ga_w.shape[0], ga_w.shape[1]
    tt = min(tt, T)
    nt = T // tt
    hb = tt // SUBLANES
    out_rows = B * T if out_rows is None else out_rows
    out_cols = W if out_cols is None else out_cols
    row = lambda v: v.reshape(1, W).astype(F32)
    kern = functools.partial(_lru_kernel, n_pad=n_pad, n_blocks=nb, bw=bw, scan_w=min(W, 512))
    return pl.pallas_call(
        kern,
        grid=(B, nt),
        in_specs=[
            pl.BlockSpec((tt, W), lambda b, t: (b * nt + t, xr_cb)),
            pl.BlockSpec((SUBLANES, W), lambda b, t: (jnp.maximum((b * nt + t) * hb - 1, 0), xr_cb)),
            pl.BlockSpec((tt, W), lambda b, t: (b * nt + t, yg_cb)),
            pl.BlockSpec((1, 1, W), lambda b, t: (b, 0, 0)),
            pl.BlockSpec((4, W), lambda b, t: (0, 0)),
            pl.BlockSpec((1, W), lambda b, t: (0, 0)),
            pl.BlockSpec((nb, bw, bw), lambda b, t: (0, 0, 0)),
            pl.BlockSpec((1, W), lambda b, t: (0, 0)),
            pl.BlockSpec((nb, bw, bw), lambda b, t: (0, 0, 0)),
            pl.BlockSpec((1, W), lambda b, t: (0, 0)),
            pl.BlockSpec((1, W), lambda b, t: (0, 0)),
        ],
        out_specs=[pl.BlockSpec((tt, W), lambda b, t: (b * nt + t, 0)),
                   pl.BlockSpec((1, 1, W), lambda b, t: (b, 0, 0))],
        out_shape=[jax.ShapeDtypeStruct((out_rows, out_cols), BF16), jax.ShapeDtypeStruct((B, 1, W), F32)],
        scratch_shapes=[pltpu.VMEM((tt, W), F32), pltpu.VMEM((tt, W), F32), pltpu.VMEM((1, W), F32)],
        compiler_params=_params("parallel", "arbitrary"),
        name="lru_mixer",
    )(xr, xr, yg, h0, conv_w.astype(F32), row(conv_b), ga_w.astype(BF16), row(ga_b),
      gx_w.astype(BF16), row(gx_b), row(lam))


def _swa_kernel(sink_ref, q_ref, kp_ref, kc_ref, vp_ref, vc_ref, *rest, j0, n_kv, group, hd, win):
    o_ref = rest[-1]
    j = pl.program_id(1) + j0
    q = q_ref[...]
    k2 = jnp.concatenate([kp_ref[...], kc_ref[...]], axis=0)
    v2 = jnp.concatenate([vp_ref[...], vc_ref[...]], axis=0)
    tq = q.shape[0]
    qi = lax.broadcasted_iota(jnp.int32, (tq, 2 * win), 0)
    kj = lax.broadcasted_iota(jnp.int32, (tq, 2 * win), 1)
    dist = qi + win - kj
    mask = (dist >= 0) & (dist < win) & ((j > 0) | (kj >= win))
    scale = hd ** -0.5
    for kvh in range(n_kv):
        kh = k2[:, kvh * hd:(kvh + 1) * hd]
        vh = v2[:, kvh * hd:(kvh + 1) * hd]
        hs_ = [kvh * group + g for g in range(group)]
        s = [jnp.where(mask, _nt(q[:, h * hd:(h + 1) * hd], kh) * scale, -jnp.inf) for h in hs_]
        m = [jnp.maximum(jnp.max(s[g], axis=-1, keepdims=True), sink_ref[hs_[g]]) for g in range(group)]
        p = [jnp.exp(s[g] - m[g]) for g in range(group)]
        den = [jnp.sum(p[g], axis=-1, keepdims=True) + jnp.exp(sink_ref[hs_[g]] - m[g]) for g in range(group)]
        outs = [_dot(p[g].astype(BF16), vh) / den[g] for g in range(group)]
        o_ref[:, hs_[0] * hd:(hs_[-1] + 1) * hd] = jnp.concatenate(outs, axis=-1).astype(o_ref.dtype)


def swa_attention(q, k, v, sink, *, B, nq, nkb, j0, win, hd, tq=None, into=None, out_cb=0):
    QW = q.shape[1]
    KW = k.shape[1]
    n_kv = KW // hd
    group = QW // KW
    tq = win if tq is None else tq
    kern = functools.partial(_swa_kernel, j0=j0, n_kv=n_kv, group=group, hd=hd, win=win)
    prev = lambda b, j: (b * nkb + jnp.maximum(j + j0 - 1, 0), 0)
    cur = lambda b, j: (b * nkb + j + j0, 0)
    in_specs = [pl.BlockSpec(memory_space=pltpu.SMEM),
                pl.BlockSpec((tq, QW), lambda b, j: (b * nq + j, 0)),
                pl.BlockSpec((win, KW), prev), pl.BlockSpec((win, KW), cur),
                pl.BlockSpec((win, KW), prev), pl.BlockSpec((win, KW), cur)]
    args = [sink.astype(F32), q, k, k, v, v]
    aliases = {}
    out_shape = jax.ShapeDtypeStruct((B * nq * tq, QW), BF16)
    if into is not None:
        aliases = {len(args): 0}
        in_specs.append(pl.BlockSpec(memory_space=pl.ANY))
        args.append(into)
        out_shape = jax.ShapeDtypeStruct(into.shape, into.dtype)
    return pl.pallas_call(
        kern,
        grid=(B, nq),
        in_specs=in_specs,
        out_specs=pl.BlockSpec((tq, QW), lambda b, j: (b * nq + j, out_cb)),
        out_shape=out_shape,
        input_output_aliases=aliases,
        compiler_params=_params("parallel", "parallel"),
        name="swa_attention",
    )(*args)


def _mem_attn_kernel(q_ref, g_ref, mk_ref, mv_ref, o_ref, *, n_heads, hd):
    q = q_ref[...]
    scale = hd ** -0.5
    for h in range(n_heads):
        sl = slice(h * hd, (h + 1) * hd)
        qh = q[:, sl]
        qn = (qh * lax.rsqrt(jnp.mean(qh * qh, axis=-1, keepdims=True) + NORM_EPS) * g_ref[...]).astype(BF16)
        s = _nt(qn, mk_ref[0, :, sl]) * scale
        m = jnp.max(s, axis=-1, keepdims=True)
        p = jnp.exp(s - m)
        p = p / jnp.sum(p, axis=-1, keepdims=True)
        o_ref[:, sl] = _dot(p.astype(BF16), mv_ref[0, :, sl]).astype(o_ref.dtype)


def mem_attention(q, qn_g, mk, mv, *, B, T, n_heads, tt=256, out_rows=None):
    MW = q.shape[1]
    ML = mk.shape[1]
    hd = MW // n_heads
    tt = min(tt, T)
    nt = T // tt
    out_rows = B * T if out_rows is None else out_rows
    return pl.pallas_call(
        functools.partial(_mem_attn_kernel, n_heads=n_heads, hd=hd),
        grid=(B, nt),
        in_specs=[pl.BlockSpec((tt, MW), lambda b, t: (b * nt + t, 0)),
                  pl.BlockSpec((1, hd), lambda b, t: (0, 0)),
                  pl.BlockSpec((1, ML, MW), lambda b, t: (b, 0, 0)),
                  pl.BlockSpec((1, ML, MW), lambda b, t: (b, 0, 0))],
        out_specs=pl.BlockSpec((tt, MW), lambda b, t: (b * nt + t, 0)),
        out_shape=jax.ShapeDtypeStruct((out_rows, MW), BF16),
        compiler_params=_params("parallel", "parallel"),
        name="mem_attention",
    )(q, qn_g.reshape(1, hd).astype(F32), mk, mv)


def _ffn_act_kernel(g_ref, halo_ref, up_ref, cw_ref, cb_ref, o_ref):
    t = pl.program_id(2)
    g = g_ref[0]
    prev = halo_ref[0] * jnp.where(t > 0, 1.0, 0.0)
    gc = cb_ref[...] + cw_ref[2:3, :] * g
    for s in (1, 2):
        gc = gc + cw_ref[2 - s:3 - s, :] * _shift_rows(g, prev, s)
    o_ref[0] = (jax.nn.gelu(gc, approximate=True) * up_ref[0]).astype(o_ref.dtype)


def ffn_act(gate, gate_cb0, up, up_cb0, conv_w, conv_b, *, tt=256, tc=512):
    B, T, _ = gate.shape
    C = conv_w.shape[1]
    tt = min(tt, T)
    hb = tt // SUBLANES
    return pl.pallas_call(
        _ffn_act_kernel,
        grid=(B, C // tc, T // tt),
        in_specs=[pl.BlockSpec((1, tt, tc), lambda b, c, t: (b, t, gate_cb0 + c)),
                  pl.BlockSpec((1, SUBLANES, tc), lambda b, c, t: (b, jnp.maximum(t * hb - 1, 0), gate_cb0 + c)),
                  pl.BlockSpec((1, tt, tc), lambda b, c, t: (b, t, up_cb0 + c)),
                  pl.BlockSpec((3, tc), lambda b, c, t: (0, c)),
                  pl.BlockSpec((1, tc), lambda b, c, t: (0, c))],
        out_specs=pl.BlockSpec((1, tt, tc), lambda b, c, t: (b, t, c)),
        out_shape=jax.ShapeDtypeStruct((B, T, C), BF16),
        compiler_params=_params("parallel", "parallel", "parallel"),
        name="ffn_act",
    )(gate, gate, up, conv_w.astype(F32), conv_b.reshape(1, C).astype(F32))


def _ffn_in_kernel(*refs, blocks_per_seq, sub, has_into, emit_w):
    x_ref, wg_ref, wu_ref, cw_ref, cb_ref = refs[:5]
    pos = 6 if has_into else 5
    o_ref, tail_ref = refs[pos], refs[pos + 1]
    carry_scr = refs[-1]
    i = pl.program_id(0)
    j = pl.program_id(1)
    tm = x_ref.shape[0]

    @pl.when(i % blocks_per_seq == 0)
    def _():
        carry_scr[j] = jnp.zeros(carry_scr.shape[1:], F32)

    prev = carry_scr[j]
    wg = wg_ref[...].astype(BF16)
    wu = wu_ref[...].astype(BF16)
    if emit_w:
        refs[pos + 2][...] = wg
        refs[pos + 3][...] = wu
    for c in range(tm // sub):
        rows = slice(c * sub, (c + 1) * sub)
        x = x_ref[rows, :]
        gate = _dot(x, wg)
        up = _dot(x, wu)
        gc = cb_ref[...] + cw_ref[2:3, :] * gate
        for s in (1, 2):
            gc = gc + cw_ref[2 - s:3 - s, :] * _shift_rows(gate, prev, s)
        o_ref[rows, :] = (jax.nn.gelu(gc, approximate=True) * up).astype(o_ref.dtype)
        prev = gate[sub - SUBLANES:, :]
    carry_scr[j] = prev
    tail_ref[0] = prev


def _ffn_in_call(x, wg, wu, layer, up_off, conv_w, conv_b, *, seq_len, row0, rows, tm, tn, sub, into, emit_w):
    M, D = x.shape
    FF = conv_w.shape[1]
    tm = min(tm, seq_len)
    bps = seq_len // tm
    nj = FF // tn
    i0 = row0 // tm
    uo = up_off // tn
    wspec = lambda off: (pl.BlockSpec((D, tn), lambda i, j: (0, j + off)) if layer is None else
                         pl.BlockSpec((None, D, tn), lambda i, j: (layer, 0, j + off)))
    in_specs = [pl.BlockSpec((tm, D), lambda i, j: (i + i0, 0), pipeline_mode=pl.Buffered(1)),
                wspec(0), wspec(uo),
                pl.BlockSpec((3, tn), lambda i, j: (0, j)),
                pl.BlockSpec((1, tn), lambda i, j: (0, j))]
    args = [x, wg, wu, conv_w.astype(F32), conv_b.reshape(1, FF).astype(F32)]
    aliases = {}
    if into is not None:
        aliases = {len(args): 0}
        in_specs.append(pl.BlockSpec(memory_space=pl.ANY))
        args.append(into)
    out_specs = [pl.BlockSpec((tm, tn), lambda i, j: (i + i0, j)),
                 pl.BlockSpec((1, SUBLANES, tn), lambda i, j: (i, 0, j))]
    out_shape = [jax.ShapeDtypeStruct((M, FF), BF16),
                 jax.ShapeDtypeStruct((rows // tm, SUBLANES, FF), F32)]
    if emit_w:
        out_specs += [pl.BlockSpec((D, tn), lambda i, j: (0, j))] * 2
        out_shape += [jax.ShapeDtypeStruct((D, FF), BF16)] * 2
    outs = pl.pallas_call(
        functools.partial(_ffn_in_kernel, blocks_per_seq=bps, sub=min(sub, tm), has_into=into is not None,
                          emit_w=emit_w),
        grid=(rows // tm, nj),
        in_specs=in_specs,
        out_specs=out_specs,
        out_shape=out_shape,
        input_output_aliases=aliases,
        scratch_shapes=[pltpu.VMEM((nj, SUBLANES, tn), F32)],
        compiler_params=_params("arbitrary", "arbitrary"),
        name="ffn_in_fused",
    )(*args)
    return (outs[0], outs[1][bps - 1::bps]) + tuple(outs[2:])


def ffn_in_fused(x, w_in, layer, conv_w, conv_b, wcache, key, *, seq_len, rows=None, sub=256):
    M = x.shape[0] if rows is None else rows
    FF = conv_w.shape[1]
    act, tail0, wg, wu = _ffn_in_call(x, w_in, w_in, layer, FF, conv_w, conv_b, seq_len=seq_len, row0=0,
                                      rows=seq_len, tm=2048, tn=256, sub=sub, into=None, emit_w=True)
    wcache[key] = (wg, wu)
    if M == seq_len:
        return act, tail0
    act, tails = _ffn_in_call(x, wg, wu, None, 0, conv_w, conv_b, seq_len=seq_len, row0=seq_len,
                              rows=M - seq_len, tm=1024, tn=512, sub=sub, into=act, emit_w=False)
    return act, jnp.concatenate([tail0, tails], axis=0)


def _rwkv_mix_kernel(x_ref, halo_ref, s0_ref, g_ref, mu_ref, w1_ref, a1_ref, g1_ref, *out_refs, n_valid_last):
    t = pl.program_id(1)
    nt = pl.num_programs(1)
    xr_ref, xk_ref, xv_ref, lw_ref, la_ref, lg_ref, hl_ref = out_refs

    def norm(v):
        return v * lax.rsqrt(jnp.mean(v * v, axis=-1, keepdims=True) + NORM_EPS) * g_ref[...]

    h = norm(x_ref[...])
    hp = norm(halo_ref[...])
    first = jnp.where(t > 0, 1.0, 0.0)
    hp = hp * first + jnp.broadcast_to(s0_ref[0], hp.shape) * (1.0 - first)
    xx = _shift_rows(h, hp, 1) - h
    mix = lambda j: (h + xx * mu_ref[j:j + 1, :]).astype(BF16)
    xr_ref[...] = mix(0)
    xk_ref[...] = mix(2)
    xv_ref[...] = mix(3)
    lw_ref[...] = jnp.tanh(_dot(mix(1), w1_ref[...])).astype(BF16)
    la_ref[...] = _dot(mix(4), a1_ref[...]).astype(BF16)
    lg_ref[...] = jax.nn.sigmoid(_dot(mix(5), g1_ref[...])).astype(BF16)

    @pl.when(t == nt - 1)
    def _():
        hl_ref[0] = h[n_valid_last - 1:n_valid_last, :]


def rwkv_mix(x, shift0, g, mu, w1, a1, g1, *, B, T, n_valid, tt=256, out_rows=None):
    D = x.shape[1]
    tt = min(tt, T)
    nt = T // tt
    hb = tt // SUBLANES
    out_rows = B * T if out_rows is None else out_rows
    n_valid_last = n_valid - (nt - 1) * tt
    blk = pl.BlockSpec((tt, D), lambda b, t: (b * nt + t, 0))
    one = pl.BlockSpec((1, 1, D), lambda b, t: (b, 0, 0))
    return pl.pallas_call(
        functools.partial(_rwkv_mix_kernel, n_valid_last=n_valid_last),
        grid=(B, nt),
        in_specs=[blk,
                  pl.BlockSpec((SUBLANES, D), lambda b, t: (jnp.maximum((b * nt + t) * hb - 1, 0), 0)),
                  one,
                  pl.BlockSpec((1, D), lambda b, t: (0, 0)),
                  pl.BlockSpec((6, D), lambda b, t: (0, 0))] +
                 [pl.BlockSpec(w.shape, lambda b, t: (0, 0)) for w in (w1, a1, g1)],
        out_specs=[blk] * 3 + [pl.BlockSpec((tt, w.shape[1]), lambda b, t: (b * nt + t, 0)) for w in (w1, a1, g1)]
                  + [one],
        out_shape=[jax.ShapeDtypeStruct((out_rows, D), BF16)] * 3 +
                  [jax.ShapeDtypeStruct((out_rows, w.shape[1]), BF16) for w in (w1, a1, g1)] +
                  [jax.ShapeDtypeStruct((B, 1, D), F32)],
        compiler_params=_params("parallel", "arbitrary"),
        name="rwkv_mix",
    )(x, x, shift0, g.reshape(1, D).astype(F32), mu.astype(F32), w1, a1, g1)


def _rwkv_chunk_kernel(r_ref, k_ref, v_ref, lw_ref, la_ref, lg_ref, w2_ref, a2_ref, g2_ref, w0_ref, a0_ref,
                       kk_ref, ka_ref, rk_ref, lng_ref, lnb_ref, s0_ref, o_ref, so_ref, s_scr, *, hs, C, n_valid):
    c = pl.program_id(2)
    nc = pl.num_programs(2)
    R, L = r_ref.shape
    n_pairs = L // LANES

    @pl.when(c == 0)
    def _():
        z = jnp.zeros((hs, hs), F32)
        for p in range(n_pairs):
            s_scr[p] = jnp.concatenate([jnp.concatenate([s0_ref[0, 2 * p], z], axis=1),
                                        jnp.concatenate([z, s0_ref[0, 2 * p + 1]], axis=1)], axis=0)

    ti = lax.broadcasted_iota(jnp.int32, (C, C), 0)
    si = lax.broadcasted_iota(jnp.int32, (C, C), 1)
    tri = jnp.where(ti >= si, 1.0, 0.0).astype(BF16)
    lane = lax.broadcasted_iota(jnp.int32, (1, LANES), 1)
    m0 = lane < hs
    C2 = 2 * C
    ri = lax.broadcasted_iota(jnp.int32, (C2, C2), 0)
    ci = lax.broadcasted_iota(jnp.int32, (C2, C2), 1)
    same = (ri // C) == (ci // C)
    strict = same & ((ri % C) > (ci % C))
    lower = (ri % C) >= (ci % C)
    vi = lax.broadcasted_iota(jnp.int32, (LANES, LANES), 0) // hs
    vj = lax.broadcasted_iota(jnp.int32, (LANES, LANES), 1) // hs
    blockdiag = vi == vj
    n_steps = int(math.log2(C))
    wide = C2 % LANES == 0
    P = range(n_pairs)
    sls = [slice(p * LANES, (p + 1) * LANES) for p in P]
    inv = 1.0 / hs

    def stack_heads(x):
        return jnp.concatenate([jnp.where(m0, x, 0.0), jnp.where(m0, 0.0, x)], axis=0).astype(BF16)

    S = [s_scr[p] for p in P]
    for sc in range(R // C):
        rows = slice(sc * C, (sc + 1) * C)
        r = r_ref[rows, :]
        k = k_ref[rows, :]
        v = v_ref[rows, :]
        wp = w0_ref[...] + _dot(lw_ref[rows, :], w2_ref[...])
        logw = -jnp.exp(-jax.nn.softplus(-wp) - 0.5)
        if n_valid is not None:
            logw = jnp.where(lax.broadcasted_iota(jnp.int32, (C, 1), 0) + sc * C < n_valid, logw, 0.0)
        a = jax.nn.sigmoid(a0_ref[...] + _dot(la_ref[rows, :], a2_ref[...]))
        kk = k * kk_ref[...]
        kk = kk / jnp.maximum(jnp.sqrt(_group_sum_bcast(kk * kk, hs, terms=1)), 1e-12)
        k2 = k * (1.0 + (a - 1.0) * ka_ref[...])
        bm = kk * a
        cum = _dot_exact_rhs_lhs(tri, logw)
        e_in = jnp.exp(cum)
        e_out = jnp.exp(-cum)
        e_end = jnp.exp(cum[C - 1:C, :] - cum)
        g_end = jnp.exp(cum[C - 1:C, :])
        rt = r * e_in
        kkt = kk * jnp.exp(cum - logw)
        bh = bm * e_out
        kh = k2 * e_out
        bbar = bm * e_end
        kbar = k2 * e_end

        S16 = [s_.astype(BF16) for s_ in S]
        V16 = [v[:, sl].astype(BF16) for sl in sls]
        lhk = [stack_heads(kkt[:, sl]) for sl in sls]
        if wide:
            gbk = [_nt(lhk[p], jnp.concatenate([stack_heads(bh[:, sls[p]]), stack_heads(kh[:, sls[p]])], axis=0))
                   for p in P]
            nmat = [jnp.where(strict, -g_[:, :C2], 0.0) for g_ in gbk]
            auk = [jnp.where(strict, -g_[:, C2:], 0.0).astype(BF16) for g_ in gbk]
        else:
            nmat = [jnp.where(strict, -_nt(lhk[p], stack_heads(bh[:, sls[p]])), 0.0) for p in P]
            auk = [jnp.where(strict, -_nt(lhk[p], stack_heads(kh[:, sls[p]])), 0.0).astype(BF16) for p in P]
        sprod = [_nt(jnp.concatenate([lhk[p], rt[:, sls[p]].astype(BF16)], axis=0), S16[p]) for p in P]
        u = [_dot(auk[p], jnp.concatenate([V16[p], V16[p]], axis=0)) - sprod[p][:C2] for p in P]
        npow = [n_.astype(BF16) for n_ in nmat]
        for it in range(n_steps):
            last = it + 1 == n_steps
            if wide and not last:
                res = [_dot(npow[p], jnp.concatenate([u[p].astype(BF16), npow[p]], axis=1)) for p in P]
                u = [u[p] + res[p][:, :LANES] for p in P]
                npow = [res[p][:, LANES:].astype(BF16) for p in P]
            else:
                u = [u[p] + _dot(npow[p], u[p].astype(BF16)) for p in P]
                if not last:
                    npow = [_dot(npow[p], npow[p]).astype(BF16) for p in P]
        uv = [jnp.concatenate([jnp.where(m0, u[p][:C], u[p][C:]).astype(BF16), V16[p]], axis=0) for p in P]
        ar = [jnp.where(lower, _nt(stack_heads(rt[:, sls[p]]),
                                   jnp.concatenate([bh[:, sls[p]], kh[:, sls[p]]], axis=0).astype(BF16)),
                        0.0).astype(BF16) for p in P]
        tmat = [_dot(ar[p], uv[p]) for p in P]
        ys = [sprod[p][C2:] + jnp.where(m0, tmat[p][:C], tmat[p][C:]) for p in P]
        upd = [_tn(uv[p], jnp.concatenate([bbar[:, sls[p]], kbar[:, sls[p]]], axis=0).astype(BF16)) for p in P]
        S = [S[p] * g_end[:, sls[p]] + jnp.where(blockdiag, upd[p], 0.0) for p in P]

        y = ys[0] if n_pairs == 1 else jnp.concatenate(ys, axis=-1)
        mean = _group_sum_bcast(y, hs, terms=1) * inv
        yc = y - mean
        var = _group_sum_bcast(yc * yc, hs, terms=1) * inv
        yn = yc * lax.rsqrt(var + RW_GN_EPS) * lng_ref[...] + lnb_ref[...]
        bonus = _group_sum_bcast(r * k2 * rk_ref[...], hs, terms=1) * v
        o_ref[rows, :] = ((yn + bonus) * _dot(lg_ref[rows, :], g2_ref[...])).astype(o_ref.dtype)

    for p in P:
        s_scr[p] = S[p]

    @pl.when(c == nc - 1)
    def _():
        for p in P:
            so_ref[0, 2 * p] = S[p][:hs, :hs]
            so_ref[0, 2 * p + 1] = S[p][hs:, hs:]


def _dot_exact_rhs_lhs(m_bf16, x):
    hi = x.astype(BF16)
    lo = (x - hi.astype(F32)).astype(BF16)
    return _dot(m_bf16, hi) + _dot(m_bf16, lo)


def rwkv_chunked(r, k, v, lw, la, lg, w2, a2, g2, w0, a0, k_k, k_a, r_k, ln_g, ln_b, s0, *, B, T, hs, chunk,
                 chunks_per_step=1, lanes_per_step=1024, out_rows=None, n_valid=None):
    D = r.shape[1]
    L = min(lanes_per_step, D)
    npg = L // LANES
    rows = chunk * chunks_per_step
    nc = T // rows
    out_rows = B * T if out_rows is None else out_rows
    seq = pl.BlockSpec((rows, L), lambda b, hg, c: (b * nc + c, hg))
    low = lambda a_: pl.BlockSpec((rows, a_.shape[1]), lambda b, hg, c: (b * nc + c, 0))
    wgt = lambda w_: pl.BlockSpec((w_.shape[0], L), lambda b, hg, c: (0, hg))
    par = pl.BlockSpec((1, L), lambda b, hg, c: (0, hg))
    st = pl.BlockSpec((1, 2 * npg, hs, hs), lambda b, hg, c: (b, hg, 0, 0))
    row = lambda x: x.reshape(1, D).astype(F32)
    return pl.pallas_call(
        functools.partial(_rwkv_chunk_kernel, hs=hs, C=chunk, n_valid=n_valid),
        grid=(B, D // L, nc),
        in_specs=[seq] * 3 + [low(lw), low(la), low(lg), wgt(w2), wgt(a2), wgt(g2)] + [par] * 7 + [st],
        out_specs=[seq, st],
        out_shape=[jax.ShapeDtypeStruct((out_rows, D), BF16),
                   jax.ShapeDtypeStruct(s0.shape, F32)],
        scratch_shapes=[pltpu.VMEM((npg, LANES, LANES), F32)],
        compiler_params=_params("parallel", "parallel", "arbitrary"),
        name="rwkv_chunked",
    )(r, k, v, lw, la, lg, w2, a2, g2, row(w0), row(a0), row(k_k), row(k_a), row(r_k), row(ln_g), row(ln_b), s0)


def _pad_rows(x, front, back):
    return jnp.pad(x, ((0, 0), (front, back), (0, 0)))


def kernel(x_prompt, x_sample, mem_prompt, state_lru_conv, state_lru_h, cache_swa_k, cache_swa_v,
           state_rwkv_shift, state_rwkv_wkv, cache_mem_k, cache_mem_v, state_ffn_conv,
           a_norm_g, a_w_in, a_conv_w, a_conv_b, a_gate_a_w, a_gate_a_b, a_gate_x_w, a_gate_x_b,
           a_lambda, b_q_norm_g, b_k_norm_g, b_sink, a_w_out,
           c_norm_g, c_mu, c_w_r, c_w_k, c_w_v, c_w_o, c_w0, c_w1, c_w2, c_a0, c_a1, c_a2,
           c_g1, c_g2, c_k_k, c_k_a, c_r_k, c_ln_g, c_ln_b,
           m_norm_g, m_mem_norm_g, m_w_q, m_w_kv, m_q_norm_g, m_k_norm_g, m_w_o,
           f_norm_g, f_w_in, f_conv_w, f_conv_b, f_w_out):
    D = x_prompt.shape[-1]
    depth = m_norm_g.shape[0]
    W = a_conv_w.shape[-1]
    KA = a_conv_w.shape[1]
    hd = b_q_norm_g.shape[-1]
    n_q = b_sink.shape[-1]
    n_kv = cache_swa_k.shape[3]
    win = cache_swa_k.shape[2]
    QW, KW = n_q * hd, n_kv * hd
    hs = c_r_k.shape[-1]
    n_rw = c_r_k.shape[1]
    m_heads, m_hd = cache_mem_k.shape[3], cache_mem_k.shape[4]
    MW = m_heads * m_hd
    FF = f_conv_w.shape[-1]
    KF = f_conv_w.shape[1]
    bf = lambda w: w.astype(BF16)
    wcache = {}

    Bp, S = x_prompt.shape[:2]
    Bs, Ts = x_sample.shape[:2]
    Mp, Ms = Bp * S, Bs * Ts
    Mt = Mp + Ms
    tma = _row_tile(Mt, 1280)
    n_a = a_norm_g.shape[0]
    n_c = c_norm_g.shape[0]
    ML = mem_prompt.shape[1]
    PAD8, PAD16 = SUBLANES, 2 * SUBLANES

    def mm(xin, w, layer, key, **kw):
        return dense(xin, w, layer, wcache, (key, layer), tm=tma, **kw)

    def seq_tail(arr, n, cols=slice(None)):
        return jnp.stack([arr[(b + 1) * S - n:(b + 1) * S, cols] for b in range(Bp)])

    def put_sample(buf, rows_s):
        return lax.dynamic_update_slice(buf, rows_s.astype(buf.dtype), (Mp, 0))

    mem_flat = mem_prompt.reshape(Bp * ML, D)
    mks, mvs = [], []
    for l in range(depth):
        mn = rmsnorm_rows(mem_flat, m_mem_norm_g[l])
        kv = matmul(mn, m_w_kv, layer=l)
        mk = headnorm(kv, 0, MW, m_k_norm_g[l], m_hd, F32)
        mks.append(mk.reshape(Bp, ML, m_heads, m_hd))
        mvs.append(kv[:, MW:].reshape(Bp, ML, m_heads, m_hd))
    p_mem_k = jnp.stack(mks)
    p_mem_v = jnp.stack(mvs)

    xf = jnp.concatenate([x_prompt.reshape(Mp, D), x_sample.reshape(Ms, D)], axis=0)
    po = {k_: [] for k_ in ("lc", "lh", "sk", "sv", "rs", "rw", "fc")}
    so = {k_: [] for k_ in ("lc", "lh", "sk", "sv", "rs", "rw", "fc")}
    ia = ic = 0
    y_p = y_s = None
    for l in range(depth):
        if l % 2 == 0:
            i = ia
            ia += 1
            assert W == QW
            h = rmsnorm_rows(xf, a_norm_g[i])
            zz = mm(h, a_w_in, i, "a_w_in_rg", n=2 * W)
            qkv = mm(h, a_w_in, i, "a_w_in_qkv", n0=2 * W)
            qn = headnorm(qkv, 0, QW, b_q_norm_g[i], hd, BF16)
            kn = headnorm(qkv, QW // KW, KW, b_k_norm_g[i], hd, F32)
            vv = qkv[:, QW + KW:]
            lru_w = (a_conv_w[i], a_conv_b[i], a_gate_a_w[i], a_gate_a_b[i], a_gate_x_w[i], a_gate_x_b[i],
                     a_lambda[i])
            mix, hl_p = lru_mixer(zz, 0, zz, 1, jnp.zeros((Bp, 1, W), F32), *lru_w, B=Bp, T=S, n_pad=0, tt=256,
                                  out_rows=Mt, out_cols=W + QW)
            mix = swa_attention(qn, bf(kn), bf(vv), b_sink[i], B=Bp, nq=S // win, nkb=S // win, j0=0, win=win,
                                hd=hd, into=mix, out_cb=W // QW)
            po["lc"].append(seq_tail(zz, KA - 1, slice(0, W)))
            po["lh"].append(hl_p.reshape(Bp, W))
            po["sk"].append(seq_tail(kn, win).reshape(Bp, win, n_kv, hd))
            po["sv"].append(seq_tail(vv, win).reshape(Bp, win, n_kv, hd))
            zz_s = zz[Mp:].reshape(Bs, Ts, 2 * W)
            n_pad = PAD8 - Ts
            xr_hist = jnp.concatenate([state_lru_conv[i].astype(F32), zz_s[:, :, :W]], axis=1)
            xr_p = _pad_rows(xr_hist, PAD8 - xr_hist.shape[1], 0).reshape(Bs * PAD8, W)
            yg_p = _pad_rows(zz_s[:, :, W:], n_pad, 0).reshape(Bs * PAD8, W)
            oa_s, hl_s = lru_mixer(xr_p, 0, yg_p, 0, state_lru_h[i].reshape(Bs, 1, W).astype(F32), *lru_w,
                                   B=Bs, T=PAD8, n_pad=n_pad, tt=PAD8)
            oa_s = oa_s.reshape(Bs, PAD8, W)[:, n_pad:].reshape(Ms, W)
            kc = cache_swa_k[i].reshape(Bs, win, KW)
            vc = cache_swa_v[i].reshape(Bs, win, KW)
            k_all = jnp.concatenate([kc, kn[Mp:].reshape(Bs, Ts, KW)], axis=1)
            v_all = jnp.concatenate([vc, vv[Mp:].reshape(Bs, Ts, KW)], axis=1)
            q_s = _pad_rows(qn[Mp:].reshape(Bs, Ts, QW), 0, PAD16 - Ts).reshape(Bs * PAD16, QW)
            o_s = swa_attention(q_s, bf(_pad_rows(k_all, 0, win - Ts)).reshape(Bs * 2 * win, KW),
                                bf(_pad_rows(v_all, 0, win - Ts)).reshape(Bs * 2 * win, KW), b_sink[i],
                                B=Bs, nq=1, nkb=2, j0=1, win=win, hd=hd, tq=PAD16)
            o_s = o_s.reshape(Bs, PAD16, QW)[:, :Ts].reshape(Ms, QW)
            mix = put_sample(mix, jnp.concatenate([oa_s, o_s], axis=-1))
            so["lc"].append(xr_hist[:, Ts:])
            so["lh"].append(hl_s.reshape(Bs, W))
            so["sk"].append(k_all[:, Ts:].reshape(Bs, win, n_kv, hd))
            so["sv"].append(v_all[:, Ts:].reshape(Bs, win, n_kv, hd))
            xf = mm(mix, a_w_out, i, "a_w_out", res=xf)
        else:
            i = ic
            ic += 1
            gl = c_g1[i].shape[1]
            glp = -(-gl // LANES) * LANES
            lora1 = (bf(c_w1[i]), bf(c_a1[i]), bf(jnp.pad(c_g1[i], ((0, 0), (0, glp - gl)))))
            lora2 = (bf(c_w2[i]), bf(c_a2[i]), bf(jnp.pad(c_g2[i], ((0, glp - gl), (0, 0)))), c_w0[i], c_a0[i])
            mixes = rwkv_mix(xf, jnp.zeros((Bp, 1, D), F32), c_norm_g[i], c_mu[i], *lora1, B=Bp, T=S, n_valid=S,
                             out_rows=Mt)
            x_s = _pad_rows(xf[Mp:].reshape(Bs, Ts, D), 0, PAD8 - Ts).reshape(Bs * PAD8, D)
            mixes_s = rwkv_mix(x_s, state_rwkv_shift[i].reshape(Bs, 1, D).astype(F32), c_norm_g[i], c_mu[i], *lora1,
                               B=Bs, T=PAD8, n_valid=Ts, tt=PAD8)
            take = lambda m: m.reshape(Bs, PAD8, m.shape[-1])[:, :Ts].reshape(Ms, m.shape[-1])
            xr, xk, xv, lw, la, lg = [put_sample(mixes[j], take(mixes_s[j])) for j in range(6)]
            po["rs"].append(mixes[6].reshape(Bp, D))
            so["rs"].append(mixes_s[6].reshape(Bs, D))
            r = mm(xr, c_w_r, i, "c_w_r")
            k = mm(xk, c_w_k, i, "c_w_k")
            v = mm(xv, c_w_v, i, "c_w_v")
            rw_par = (c_k_k[i], c_k_a[i], c_r_k[i], c_ln_g[i], c_ln_b[i])
            yo, s_end = rwkv_chunked(r, k, v, lw, la, lg, *lora2, *rw_par, jnp.zeros((Bp, n_rw, hs, hs), F32),
                                     B=Bp, T=S, hs=hs, chunk=64, chunks_per_step=4, out_rows=Mt)
            po["rw"].append(s_end)
            cs = PAD16
            sq = lambda t: jnp.pad(t[Mp:].reshape(Bs, Ts, t.shape[-1]),
                                   ((0, 0), (0, cs - Ts), (0, 0))).reshape(Bs * cs, t.shape[-1])
            yo_s, s_end_s = rwkv_chunked(sq(r), sq(k), sq(v), sq(lw), sq(la), sq(lg), *lora2, *rw_par,
                                         state_rwkv_wkv[i].astype(F32), B=Bs, T=cs, hs=hs, chunk=cs,
                                         lanes_per_step=2048, n_valid=Ts)
            yo = put_sample(yo, yo_s.reshape(Bs, cs, D)[:, :Ts].reshape(Ms, D))
            so["rw"].append(s_end_s)
            xf = mm(yo, c_w_o, i, "c_w_o", res=xf)
        hm = rmsnorm_rows(xf, m_norm_g[l])
        q = mm(hm, m_w_q, l, "m_w_q")
        om = mem_attention(q, m_q_norm_g[l], bf(p_mem_k[l]).reshape(Bp, ML, MW), bf(p_mem_v[l]).reshape(Bp, ML, MW),
                           B=Bp, T=S, n_heads=m_heads, out_rows=Mt)
        q_s = _pad_rows(q[Mp:].reshape(Bs, Ts, MW), 0, PAD8 - Ts).reshape(Bs * PAD8, MW)
        om_s = mem_attention(q_s, m_q_norm_g[l], bf(cache_mem_k[l]).reshape(Bs, ML, MW),
                             bf(cache_mem_v[l]).reshape(Bs, ML, MW), B=Bs, T=PAD8, n_heads=m_heads, tt=PAD8)
        om = put_sample(om, om_s.reshape(Bs, PAD8, MW)[:, :Ts].reshape(Ms, MW))
        xf = mm(om, m_w_o, l, "m_w_o", res=xf)
        hf = rmsnorm_rows(xf, f_norm_g[l])
        act, tail = ffn_in_fused(hf, f_w_in, l, f_conv_w[l], f_conv_b[l], wcache, ("f_w_in", l), seq_len=S, rows=Mp)
        po["fc"].append(tail[:, SUBLANES - (KF - 1):, :])
        wg, wu = wcache[("f_w_in", l)]
        hf_s = hf[Mp:]
        gate_s = matmul(hf_s, wg, tn=1024).reshape(Bs, Ts, FF)
        up_s = matmul(hf_s, wu, tn=1024).reshape(Bs, Ts, FF)
        g_hist = jnp.concatenate([state_ffn_conv[l].astype(F32), gate_s], axis=1)
        g_p = _pad_rows(g_hist, PAD8 - g_hist.shape[1], 0)
        up_p = _pad_rows(up_s, PAD8 - Ts, 0)
        act_s = ffn_act(g_p, 0, up_p, 0, f_conv_w[l], f_conv_b[l], tt=PAD8, tc=FF)[:, PAD8 - Ts:]
        act = put_sample(act, act_s.reshape(Ms, FF))
        so["fc"].append(g_hist[:, Ts:])
        if l + 1 < depth:
            xf = mm(act, f_w_out, l, "f_w_out", res=xf, tn=1024, tk=2048)
        else:
            y_p = dense(act, f_w_out, l, wcache, ("f_w_out", l), rows=Mp, out_rows=Mp, res=xf, tn=1024, tk=2048)
            y_s = matmul(act, wcache[("f_w_out", l)], row0=Mp, rows=Ms, out_rows=Ms, out_row0=0, res=xf,
                         tn=1024, tk=2048)
    st = jnp.stack
    return (y_p.reshape(Bp, S, D), y_s.reshape(Bs, Ts, D),
            st(po["lc"]), st(po["lh"]), st(po["sk"]), st(po["sv"]), st(po["rs"]), st(po["rw"]),
            p_mem_k, p_mem_v, st(po["fc"]),
            st(so["lc"]), st(so["lh"]), st(so["sk"]), st(so["sv"]), st(so["rs"]), st(so["rw"]), st(so["fc"]))
```

```python
import functools
import math

import jax
import jax.numpy as jnp
from jax import lax
from jax.experimental import pallas as pl
from jax.experimental.pallas import tpu as pltpu

F32 = jnp.float32
BF16 = jnp.bfloat16

NORM_EPS = 1e-6
RW_GN_EPS = 64e-5
LRU_C = 8.0
LANES = 128
SUBLANES = 8
VMEM_LIMIT_BYTES = 56 * 1024 * 1024


def _params(*sem):
    return pltpu.CompilerParams(dimension_semantics=sem, vmem_limit_bytes=VMEM_LIMIT_BYTES)


def _nt(a, b):
    return lax.dot_general(a, b, (((1,), (1,)), ((), ())), preferred_element_type=F32)


def _tn(a, b):
    return lax.dot_general(a, b, (((0,), (0,)), ((), ())), preferred_element_type=F32)


def _dot(a, b):
    return jnp.dot(a, b, preferred_element_type=F32)


def _group_sum_bcast(x, width, terms=2):
    m, L = x.shape
    n = L // LANES
    if width == LANES:
        parts = []
        for c in range(n):
            s = jnp.sum(x[:, c * LANES:(c + 1) * LANES], axis=-1, keepdims=True)
            parts.append(jnp.broadcast_to(s, (m, LANES)))
        return parts[0] if n == 1 else jnp.concatenate(parts, axis=-1)
    li = lax.broadcasted_iota(jnp.int32, (LANES, LANES), 0) // width
    lj = lax.broadcasted_iota(jnp.int32, (LANES, LANES), 1) // width
    e = jnp.where(li == lj, 1.0, 0.0).astype(BF16)
    xs = x if n == 1 else jnp.concatenate([x[:, c * LANES:(c + 1) * LANES] for c in range(n)], axis=0)
    hi = xs.astype(BF16)
    out = _dot(hi, e)
    if terms > 1:
        out = out + _dot((xs - hi.astype(F32)).astype(BF16), e)
    return out if n == 1 else jnp.concatenate([out[c * m:(c + 1) * m] for c in range(n)], axis=-1)


def _shift_rows(x, prev8, s):
    rolled = pltpu.roll(x, s, 0)
    top = jnp.where(lax.broadcasted_iota(jnp.int32, (SUBLANES, 1), 0) < s,
                    pltpu.roll(prev8, s, 0), rolled[0:SUBLANES])
    if x.shape[0] == SUBLANES:
        return top
    return jnp.concatenate([top, rolled[SUBLANES:]], axis=0)


def _rmsnorm_kernel(x_ref, g_ref, o_ref):
    x = x_ref[...]
    ms = jnp.mean(x * x, axis=-1, keepdims=True)
    o_ref[...] = (x * lax.rsqrt(ms + NORM_EPS) * g_ref[...]).astype(o_ref.dtype)


def _row_tile(m, target):
    best = None
    for d in range(16, min(m, target) + 1, 16):
        if m % d == 0:
            best = d
    return best or m


def rmsnorm_rows(x, g, out_dtype=BF16, tm=512, out_rows=None):
    M, D = x.shape
    tm = _row_tile(M, tm)
    return pl.pallas_call(
        _rmsnorm_kernel,
        grid=(M // tm,),
        in_specs=[pl.BlockSpec((tm, D), lambda i: (i, 0)), pl.BlockSpec((1, D), lambda i: (0, 0))],
        out_specs=pl.BlockSpec((tm, D), lambda i: (i, 0)),
        out_shape=jax.ShapeDtypeStruct((M if out_rows is None else out_rows, D), out_dtype),
        compiler_params=_params("parallel"),
        name="rmsnorm_rows",
    )(x, g.reshape(1, D))


def _mm_kernel(*refs, nk, has_bias, has_res, has_into, emit_w, in_place, act):
    x_ref, w_ref = refs[0], refs[1]
    pos = 2
    b_ref = r_ref = wb_ref = acc_ref = None
    if has_bias:
        b_ref = refs[pos]
        pos += 1
    if has_res:
        r_ref = refs[pos]
        pos += 1
    if has_into:
        pos += 1
    o_ref = refs[pos]
    pos += 1
    if emit_w:
        wb_ref = refs[pos]
        pos += 1
    if nk > 1 and not in_place:
        acc_ref = refs[pos]

    def weights(cs=slice(None)):
        w16 = w_ref[:, cs].astype(BF16)
        if emit_w:
            wb_ref[:, cs] = w16
        return w16

    def epilogue(y):
        if has_bias:
            y = y + b_ref[...]
        if act == "tanh":
            y = jnp.tanh(y)
        elif act == "sigmoid":
            y = jax.nn.sigmoid(y)
        if has_res:
            y = y + r_ref[...]
        o_ref[...] = y.astype(o_ref.dtype)

    if nk > 1 and in_place:
        k = pl.program_id(2)
        tn = o_ref.shape[1]
        cw = min(tn, 2 * LANES)

        def sweep(first):
            x = x_ref[...]
            for c in range(tn // cw):
                cs = slice(c * cw, (c + 1) * cw)
                part = _dot(x, weights(cs))
                if not first:
                    o_ref[:, cs] += part
                elif has_res:
                    o_ref[:, cs] = part + r_ref[:, cs]
                else:
                    o_ref[:, cs] = part

        pl.when(k == 0)(functools.partial(sweep, True))
        pl.when(k > 0)(functools.partial(sweep, False))
        return

    part = _dot(x_ref[...], weights())
    if nk == 1:
        epilogue(part)
        return
    k = pl.program_id(2)

    @pl.when(k == 0)
    def _():
        acc_ref[...] = part

    @pl.when(k > 0)
    def _():
        acc_ref[...] += part

    @pl.when(k == nk - 1)
    def _():
        epilogue(acc_ref[...])


def _pick(n, prefs):
    for p in prefs:
        if n % p == 0:
            return p
    return n


def matmul(x, w, *, layer=None, n0=0, n=None, bias=None, res=None, act=None, out_dtype=F32,
           tm=1024, tn=512, tk=4096, row0=0, rows=None, into=None, emit_w=False, out_rows=None, out_row0=None,
           res_row0=None):
    M, K = x.shape
    N = w.shape[-1] - n0 if n is None else n
    rows = M - row0 if rows is None else rows
    tm = min(tm, rows)
    out_rows = M if out_rows is None else out_rows
    o0 = (row0 if out_row0 is None else out_row0) // tm
    tn = _pick(math.gcd(N, n0) if n0 else N, (tn, 512, 256, 128))
    tk = _pick(K, (tk, 2048, 1024, 512))
    nk = K // tk
    j0 = n0 // tn
    i0 = row0 // tm
    assert row0 % tm == 0 and rows % tm == 0 and (not emit_w or rows == tm)
    in_specs = [pl.BlockSpec((tm, tk), lambda i, j, k: (i + i0, k)),
                pl.BlockSpec((tk, tn), lambda i, j, k: (k, j + j0)) if layer is None else
                pl.BlockSpec((None, tk, tn), lambda i, j, k: (layer, k, j + j0))]
    args = [x, w]
    if bias is not None:
        in_specs.append(pl.BlockSpec((1, tn), lambda i, j, k: (0, j)))
        args.append(bias.reshape(1, N).astype(F32))
    if res is not None:
        r0 = i0 if res_row0 is None else res_row0 // tm
        in_specs.append(pl.BlockSpec((tm, tn), lambda i, j, k: (i + r0, j)))
        args.append(res)
    aliases = {}
    if into is not None:
        aliases = {len(args): 0}
        in_specs.append(pl.BlockSpec(memory_space=pl.ANY))
        args.append(into)
    in_place = bias is None and act is None and out_dtype == F32
    kern = functools.partial(_mm_kernel, nk=nk, has_bias=bias is not None, has_res=res is not None,
                             has_into=into is not None, emit_w=emit_w, in_place=in_place, act=act)
    out_specs = [pl.BlockSpec((tm, tn), lambda i, j, k: (i + o0, j))]
    out_shape = [jax.ShapeDtypeStruct((out_rows, N), out_dtype)]
    if emit_w:
        out_specs.append(pl.BlockSpec((tk, tn), lambda i, j, k: (k, j)))
        out_shape.append(jax.ShapeDtypeStruct((K, N), BF16))
    outs = pl.pallas_call(
        kern,
        grid=(rows // tm, N // tn, nk),
        in_specs=in_specs,
        out_specs=out_specs,
        out_shape=out_shape,
        input_output_aliases=aliases,
        scratch_shapes=[pltpu.VMEM((tm, tn), F32)] if (nk > 1 and not in_place) else [],
        compiler_params=_params("parallel", "parallel", "arbitrary"),
        name="matmul",
    )(*args)
    return outs if emit_w else outs[0]


def dense(x, w, layer, wcache, key, **kw):
    rows = kw.pop("rows", x.shape[0])
    tm = min(kw.get("tm", 1024), rows)
    first, wb = matmul(x, w, layer=layer, rows=tm, emit_w=True, **kw)
    wcache[key] = wb
    if tm == rows:
        return first
    kw = {a: b for a, b in kw.items() if a not in ("n0", "n")}
    return matmul(x, wb, row0=tm, rows=rows - tm, into=first, **kw)


def _proj_res_norm_kernel(x_ref, w_ref, r_ref, g_ref, y_ref, h_ref, w_scr):
    @pl.when(pl.program_id(0) == 0)
    def _():
        w_scr[...] = w_ref[...].astype(BF16)

    y = _dot(x_ref[...], w_scr[...]) + r_ref[...]
    y_ref[...] = y
    ms = jnp.mean(y * y, axis=-1, keepdims=True)
    h_ref[...] = (y * lax.rsqrt(ms + NORM_EPS) * g_ref[...]).astype(h_ref.dtype)


def proj_res_norm(x, w, layer, res, g, tm=320):
    M, K = x.shape
    N = w.shape[-1]
    tm = _row_tile(M, tm)
    row_blk = lambda c: pl.BlockSpec((tm, c), lambda i: (i, 0))
    return pl.pallas_call(
        _proj_res_norm_kernel,
        grid=(M // tm,),
        in_specs=[row_blk(K), pl.BlockSpec((None, K, N), lambda i: (layer, 0, 0)), row_blk(N),
                  pl.BlockSpec((1, N), lambda i: (0, 0))],
        out_specs=[row_blk(N), row_blk(N)],
        out_shape=[jax.ShapeDtypeStruct((M, N), F32), jax.ShapeDtypeStruct((M, N), BF16)],
        scratch_shapes=[pltpu.VMEM((K, N), BF16)],
        compiler_params=_params("arbitrary"),
        name="proj_res_norm",
    )(x, w, res, g.reshape(1, N).astype(F32))


def _headnorm_kernel(x_ref, g_ref, o_ref, *, hd):
    x = x_ref[...]
    ms = _group_sum_bcast(x * x, hd) * (1.0 / hd)
    o_ref[...] = (x * lax.rsqrt(ms + NORM_EPS) * g_ref[...]).astype(o_ref.dtype)


def headnorm(x, col_block, width, g, hd, out_dtype, tm=512):
    M = x.shape[0]
    tm = _row_tile(M, tm)
    g_row = jnp.tile(g.astype(F32), width // hd).reshape(1, width)
    return pl.pallas_call(
        functools.partial(_headnorm_kernel, hd=hd),
        grid=(M // tm,),
        in_specs=[pl.BlockSpec((tm, width), lambda i: (i, col_block)),
                  pl.BlockSpec((1, width), lambda i: (0, 0))],
        out_specs=pl.BlockSpec((tm, width), lambda i: (i, 0)),
        out_shape=jax.ShapeDtypeStruct((M, width), out_dtype),
        compiler_params=_params("parallel"),
        name="headnorm",
    )(x, g_row)


def _lru_kernel(xr_ref, halo_ref, yg_ref, h0_ref, cw_ref, cb_ref, gaw_ref, gab_ref, gxw_ref, gxb_ref,
                lam_ref, o_ref, hl_ref, a_scr, u_scr, h_scr, *, n_pad, n_blocks, bw, scan_w):
    t = pl.program_id(1)
    tt, W = a_scr.shape

    @pl.when(t == 0)
    def _():
        h_scr[...] = h0_ref[0]

    x = xr_ref[...]
    prev = halo_ref[...] * jnp.where(t > 0, 1.0, 0.0)
    xc = cb_ref[...] + cw_ref[3:4, :] * x
    for s in (1, 2, 3):
        xc = xc + cw_ref[3 - s:4 - s, :] * _shift_rows(x, prev, s)

    nsp = -LRU_C * jax.nn.softplus(-lam_ref[...])
    if n_pad:
        live = (lax.broadcasted_iota(jnp.int32, (tt, 1), 0) >= n_pad) | (t > 0)
    for n in range(n_blocks):
        sl = slice(n * bw, (n + 1) * bw)
        xb = xc[:, sl]
        xb16 = xb.astype(BF16)
        r = jax.nn.sigmoid(_dot(xb16, gaw_ref[n]) + gab_ref[:, sl])
        i = jax.nn.sigmoid(_dot(xb16, gxw_ref[n]) + gxb_ref[:, sl])
        log_a = r * nsp[:, sl]
        a = jnp.exp(log_a)
        u = jnp.sqrt(jnp.maximum(-jnp.tanh(log_a) * (a * a + 1.0), 0.0)) * (i * xb)
        if n_pad:
            a = jnp.where(live, a, 1.0)
            u = jnp.where(live, u, 0.0)
        a_scr[:, sl] = a
        u_scr[:, sl] = u

    row8 = lax.broadcasted_iota(jnp.int32, (SUBLANES, 1), 0)
    for c in range(W // scan_w):
        cs = slice(c * scan_w, (c + 1) * scan_w)

        def body(gi, h, cs=cs):
            r0 = pl.multiple_of(gi * SUBLANES, SUBLANES)
            A = a_scr[pl.ds(r0, SUBLANES), cs]
            U = u_scr[pl.ds(r0, SUBLANES), cs]
            for s in (1, 2, 4):
                As = pltpu.roll(A, s, 0)
                Us = pltpu.roll(U, s, 0)
                m = row8 >= s
                U = jnp.where(m, A * Us + U, U)
                A = jnp.where(m, A * As, A)
            H = A * h + U
            u_scr[pl.ds(r0, SUBLANES), cs] = H
            return H[SUBLANES - 1:SUBLANES, :]

        h_end = lax.fori_loop(0, tt // SUBLANES, body, h_scr[:, cs])
        h_scr[:, cs] = h_end

    o_ref[...] = (u_scr[...] * jax.nn.gelu(yg_ref[...], approximate=True)).astype(o_ref.dtype)
    hl_ref[0] = h_scr[...]


def lru_mixer(xr, xr_cb, yg, yg_cb, h0, conv_w, conv_b, ga_w, ga_b, gx_w, gx_b, lam, *, B, T, n_pad, tt,
              out_rows=None, out_cols=None):
    W = conv_w.shape[1]
    nb, bw = ga_w.shape[0], ga_w.shape[1]
    tt = min(tt, T)
    nt = T // tt
    hb = tt // SUBLANES
    out_rows = B * T if out_rows is None else out_rows
    out_cols = W if out_cols is None else out_cols
    row = lambda v: v.reshape(1, W).astype(F32)
    kern = functools.partial(_lru_kernel, n_pad=n_pad, n_blocks=nb, bw=bw, scan_w=min(W, 512))
    return pl.pallas_call(
        kern,
        grid=(B, nt),
        in_specs=[
            pl.BlockSpec((tt, W), lambda b, t: (b * nt + t, xr_cb)),
            pl.BlockSpec((SUBLANES, W), lambda b, t: (jnp.maximum((b * nt + t) * hb - 1, 0), xr_cb)),
            pl.BlockSpec((tt, W), lambda b, t: (b * nt + t, yg_cb)),
            pl.BlockSpec((1, 1, W), lambda b, t: (b, 0, 0)),
            pl.BlockSpec((4, W), lambda b, t: (0, 0)),
            pl.BlockSpec((1, W), lambda b, t: (0, 0)),
            pl.BlockSpec((nb, bw, bw), lambda b, t: (0, 0, 0)),
            pl.BlockSpec((1, W), lambda b, t: (0, 0)),
            pl.BlockSpec((nb, bw, bw), lambda b, t: (0, 0, 0)),
            pl.BlockSpec((1, W), lambda b, t: (0, 0)),
            pl.BlockSpec((1, W), lambda b, t: (0, 0)),
        ],
        out_specs=[pl.BlockSpec((tt, W), lambda b, t: (b * nt + t, 0)),
                   pl.BlockSpec((1, 1, W), lambda b, t: (b, 0, 0))],
        out_shape=[jax.ShapeDtypeStruct((out_rows, out_cols), BF16), jax.ShapeDtypeStruct((B, 1, W), F32)],
        scratch_shapes=[pltpu.VMEM((tt, W), F32), pltpu.VMEM((tt, W), F32), pltpu.VMEM((1, W), F32)],
        compiler_params=_params("parallel", "arbitrary"),
        name="lru_mixer",
    )(xr, xr, yg, h0, conv_w.astype(F32), row(conv_b), ga_w.astype(BF16), row(ga_b),
      gx_w.astype(BF16), row(gx_b), row(lam))


def _swa_kernel(sink_ref, q_ref, kp_ref, kc_ref, vp_ref, vc_ref, *rest, j0, n_kv, group, hd, win):
    o_ref = rest[-1]
    j = pl.program_id(1) + j0
    q = q_ref[...]
    k2 = jnp.concatenate([kp_ref[...], kc_ref[...]], axis=0)
    v2 = jnp.concatenate([vp_ref[...], vc_ref[...]], axis=0)
    tq = q.shape[0]
    qi = lax.broadcasted_iota(jnp.int32, (tq, 2 * win), 0)
    kj = lax.broadcasted_iota(jnp.int32, (tq, 2 * win), 1)
    dist = qi + win - kj
    mask = (dist >= 0) & (dist < win) & ((j > 0) | (kj >= win))
    scale = hd ** -0.5
    for kvh in range(n_kv):
        kh = k2[:, kvh * hd:(kvh + 1) * hd]
        vh = v2[:, kvh * hd:(kvh + 1) * hd]
        hs_ = [kvh * group + g for g in range(group)]
        s = [jnp.where(mask, _nt(q[:, h * hd:(h + 1) * hd], kh) * scale, -jnp.inf) for h in hs_]
        m = [jnp.maximum(jnp.max(s[g], axis=-1, keepdims=True), sink_ref[hs_[g]]) for g in range(group)]
        p = [jnp.exp(s[g] - m[g]) for g in range(group)]
        den = [jnp.sum(p[g], axis=-1, keepdims=True) + jnp.exp(sink_ref[hs_[g]] - m[g]) for g in range(group)]
        outs = [_dot(p[g].astype(BF16), vh) / den[g] for g in range(group)]
        o_ref[:, hs_[0] * hd:(hs_[-1] + 1) * hd] = jnp.concatenate(outs, axis=-1).astype(o_ref.dtype)


def swa_attention(q, k, v, sink, *, B, nq, nkb, j0, win, hd, tq=None, into=None, out_cb=0):
    QW = q.shape[1]
    KW = k.shape[1]
    n_kv = KW // hd
    group = QW // KW
    tq = win if tq is None else tq
    kern = functools.partial(_swa_kernel, j0=j0, n_kv=n_kv, group=group, hd=hd, win=win)
    prev = lambda b, j: (b * nkb + jnp.maximum(j + j0 - 1, 0), 0)
    cur = lambda b, j: (b * nkb + j + j0, 0)
    in_specs = [pl.BlockSpec(memory_space=pltpu.SMEM),
                pl.BlockSpec((tq, QW), lambda b, j: (b * nq + j, 0)),
                pl.BlockSpec((win, KW), prev), pl.BlockSpec((win, KW), cur),
                pl.BlockSpec((win, KW), prev), pl.BlockSpec((win, KW), cur)]
    args = [sink.astype(F32), q, k, k, v, v]
    aliases = {}
    out_shape = jax.ShapeDtypeStruct((B * nq * tq, QW), BF16)
    if into is not None:
        aliases = {len(args): 0}
        in_specs.append(pl.BlockSpec(memory_space=pl.ANY))
        args.append(into)
        out_shape = jax.ShapeDtypeStruct(into.shape, into.dtype)
    return pl.pallas_call(
        kern,
        grid=(B, nq),
        in_specs=in_specs,
        out_specs=pl.BlockSpec((tq, QW), lambda b, j: (b * nq + j, out_cb)),
        out_shape=out_shape,
        input_output_aliases=aliases,
        compiler_params=_params("parallel", "parallel"),
        name="swa_attention",
    )(*args)


def _mem_attn_kernel(q_ref, g_ref, mk_ref, mv_ref, o_ref, *, n_heads, hd):
    q = q_ref[...]
    scale = hd ** -0.5
    for h in range(n_heads):
        sl = slice(h * hd, (h + 1) * hd)
        qh = q[:, sl]
        qn = (qh * lax.rsqrt(jnp.mean(qh * qh, axis=-1, keepdims=True) + NORM_EPS) * g_ref[...]).astype(BF16)
        s = _nt(qn, mk_ref[0, :, sl]) * scale
        m = jnp.max(s, axis=-1, keepdims=True)
        p = jnp.exp(s - m)
        p = p / jnp.sum(p, axis=-1, keepdims=True)
        o_ref[:, sl] = _dot(p.astype(BF16), mv_ref[0, :, sl]).astype(o_ref.dtype)


def mem_attention(q, qn_g, mk, mv, *, B, T, n_heads, tt=256, out_rows=None):
    MW = q.shape[1]
    ML = mk.shape[1]
    hd = MW // n_heads
    tt = min(tt, T)
    nt = T // tt
    out_rows = B * T if out_rows is None else out_rows
    return pl.pallas_call(
        functools.partial(_mem_attn_kernel, n_heads=n_heads, hd=hd),
        grid=(B, nt),
        in_specs=[pl.BlockSpec((tt, MW), lambda b, t: (b * nt + t, 0)),
                  pl.BlockSpec((1, hd), lambda b, t: (0, 0)),
                  pl.BlockSpec((1, ML, MW), lambda b, t: (b, 0, 0)),
                  pl.BlockSpec((1, ML, MW), lambda b, t: (b, 0, 0))],
        out_specs=pl.BlockSpec((tt, MW), lambda b, t: (b * nt + t, 0)),
        out_shape=jax.ShapeDtypeStruct((out_rows, MW), BF16),
        compiler_params=_params("parallel", "parallel"),
        name="mem_attention",
    )(q, qn_g.reshape(1, hd).astype(F32), mk, mv)


def _ffn_act_kernel(g_ref, halo_ref, up_ref, cw_ref, cb_ref, o_ref):
    t = pl.program_id(2)
    g = g_ref[0]
    prev = halo_ref[0] * jnp.where(t > 0, 1.0, 0.0)
    gc = cb_ref[...] + cw_ref[2:3, :] * g
    for s in (1, 2):
        gc = gc + cw_ref[2 - s:3 - s, :] * _shift_rows(g, prev, s)
    o_ref[0] = (jax.nn.gelu(gc, approximate=True) * up_ref[0]).astype(o_ref.dtype)


def ffn_act(gate, gate_cb0, up, up_cb0, conv_w, conv_b, *, tt=256, tc=512):
    B, T, _ = gate.shape
    C = conv_w.shape[1]
    tt = min(tt, T)
    hb = tt // SUBLANES
    return pl.pallas_call(
        _ffn_act_kernel,
        grid=(B, C // tc, T // tt),
        in_specs=[pl.BlockSpec((1, tt, tc), lambda b, c, t: (b, t, gate_cb0 + c)),
                  pl.BlockSpec((1, SUBLANES, tc), lambda b, c, t: (b, jnp.maximum(t * hb - 1, 0), gate_cb0 + c)),
                  pl.BlockSpec((1, tt, tc), lambda b, c, t: (b, t, up_cb0 + c)),
                  pl.BlockSpec((3, tc), lambda b, c, t: (0, c)),
                  pl.BlockSpec((1, tc), lambda b, c, t: (0, c))],
        out_specs=pl.BlockSpec((1, tt, tc), lambda b, c, t: (b, t, c)),
        out_shape=jax.ShapeDtypeStruct((B, T, C), BF16),
        compiler_params=_params("parallel", "parallel", "parallel"),
        name="ffn_act",
    )(gate, gate, up, conv_w.astype(F32), conv_b.reshape(1, C).astype(F32))


def _ffn_in_kernel(*refs, blocks_per_seq, sub, has_into, emit_w):
    x_ref, wg_ref, wu_ref, cw_ref, cb_ref = refs[:5]
    pos = 6 if has_into else 5
    o_ref, tail_ref = refs[pos], refs[pos + 1]
    carry_scr = refs[-1]
    i = pl.program_id(0)
    j = pl.program_id(1)
    tm = x_ref.shape[0]

    @pl.when(i % blocks_per_seq == 0)
    def _():
        carry_scr[j] = jnp.zeros(carry_scr.shape[1:], F32)

    prev = carry_scr[j]
    wg = wg_ref[...].astype(BF16)
    wu = wu_ref[...].astype(BF16)
    if emit_w:
        refs[pos + 2][...] = wg
        refs[pos + 3][...] = wu
    for c in range(tm // sub):
        rows = slice(c * sub, (c + 1) * sub)
        x = x_ref[rows, :]
        gate = _dot(x, wg)
        up = _dot(x, wu)
        gc = cb_ref[...] + cw_ref[2:3, :] * gate
        for s in (1, 2):
            gc = gc + cw_ref[2 - s:3 - s, :] * _shift_rows(gate, prev, s)
        o_ref[rows, :] = (jax.nn.gelu(gc, approximate=True) * up).astype(o_ref.dtype)
        prev = gate[sub - SUBLANES:, :]
    carry_scr[j] = prev
    tail_ref[0] = prev


def _ffn_in_call(x, wg, wu, layer, up_off, conv_w, conv_b, *, seq_len, row0, rows, tm, tn, sub, into, emit_w):
    M, D = x.shape
    FF = conv_w.shape[1]
    tm = min(tm, seq_len)
    bps = seq_len // tm
    nj = FF // tn
    i0 = row0 // tm
    uo = up_off // tn
    wspec = lambda off: (pl.BlockSpec((D, tn), lambda i, j: (0, j + off)) if layer is None else
                         pl.BlockSpec((None, D, tn), lambda i, j: (layer, 0, j + off)))
    in_specs = [pl.BlockSpec((tm, D), lambda i, j: (i + i0, 0), pipeline_mode=pl.Buffered(1)),
                wspec(0), wspec(uo),
                pl.BlockSpec((3, tn), lambda i, j: (0, j)),
                pl.BlockSpec((1, tn), lambda i, j: (0, j))]
    args = [x, wg, wu, conv_w.astype(F32), conv_b.reshape(1, FF).astype(F32)]
    aliases = {}
    if into is not None:
        aliases = {len(args): 0}
        in_specs.append(pl.BlockSpec(memory_space=pl.ANY))
        args.append(into)
    out_specs = [pl.BlockSpec((tm, tn), lambda i, j: (i + i0, j)),
                 pl.BlockSpec((1, SUBLANES, tn), lambda i, j: (i, 0, j))]
    out_shape = [jax.ShapeDtypeStruct((M, FF), BF16),
                 jax.ShapeDtypeStruct((rows // tm, SUBLANES, FF), F32)]
    if emit_w:
        out_specs += [pl.BlockSpec((D, tn), lambda i, j: (0, j))] * 2
        out_shape += [jax.ShapeDtypeStruct((D, FF), BF16)] * 2
    outs = pl.pallas_call(
        functools.partial(_ffn_in_kernel, blocks_per_seq=bps, sub=min(sub, tm), has_into=into is not None,
                          emit_w=emit_w),
        grid=(rows // tm, nj),
        in_specs=in_specs,
        out_specs=out_specs,
        out_shape=out_shape,
        input_output_aliases=aliases,
        scratch_shapes=[pltpu.VMEM((nj, SUBLANES, tn), F32)],
        compiler_params=_params("arbitrary", "arbitrary"),
        name="ffn_in_fused",
    )(*args)
    return (outs[0], outs[1][bps - 1::bps]) + tuple(outs[2:])


def ffn_in_fused(x, w_in, layer, conv_w, conv_b, wcache, key, *, seq_len, rows=None, sub=256):
    M = x.shape[0] if rows is None else rows
    FF = conv_w.shape[1]
    act, tail0, wg, wu = _ffn_in_call(x, w_in, w_in, layer, FF, conv_w, conv_b, seq_len=seq_len, row0=0,
                                      rows=seq_len, tm=2048, tn=256, sub=sub, into=None, emit_w=True)
    wcache[key] = (wg, wu)
    if M == seq_len:
        return act, tail0
    act, tails = _ffn_in_call(x, wg, wu, None, 0, conv_w, conv_b, seq_len=seq_len, row0=seq_len,
                              rows=M - seq_len, tm=1024, tn=512, sub=sub, into=act, emit_w=False)
    return act, jnp.concatenate([tail0, tails], axis=0)


def _rwkv_mix_kernel(x_ref, halo_ref, s0_ref, g_ref, mu_ref, w1_ref, a1_ref, g1_ref, *out_refs, n_valid_last):
    t = pl.program_id(1)
    nt = pl.num_programs(1)
    xr_ref, xk_ref, xv_ref, lw_ref, la_ref, lg_ref, hl_ref = out_refs

    def norm(v):
        return v * lax.rsqrt(jnp.mean(v * v, axis=-1, keepdims=True) + NORM_EPS) * g_ref[...]

    h = norm(x_ref[...])
    hp = norm(halo_ref[...])
    first = jnp.where(t > 0, 1.0, 0.0)
    hp = hp * first + jnp.broadcast_to(s0_ref[0], hp.shape) * (1.0 - first)
    xx = _shift_rows(h, hp, 1) - h
    mix = lambda j: (h + xx * mu_ref[j:j + 1, :]).astype(BF16)
    xr_ref[...] = mix(0)
    xk_ref[...] = mix(2)
    xv_ref[...] = mix(3)
    lw_ref[...] = jnp.tanh(_dot(mix(1), w1_ref[...])).astype(BF16)
    la_ref[...] = _dot(mix(4), a1_ref[...]).astype(BF16)
    lg_ref[...] = jax.nn.sigmoid(_dot(mix(5), g1_ref[...])).astype(BF16)

    @pl.when(t == nt - 1)
    def _():
        hl_ref[0] = h[n_valid_last - 1:n_valid_last, :]


def rwkv_mix(x, shift0, g, mu, w1, a1, g1, *, B, T, n_valid, tt=256, out_rows=None):
    D = x.shape[1]
    tt = min(tt, T)
    nt = T // tt
    hb = tt // SUBLANES
    out_rows = B * T if out_rows is None else out_rows
    n_valid_last = n_valid - (nt - 1) * tt
    blk = pl.BlockSpec((tt, D), lambda b, t: (b * nt + t, 0))
    one = pl.BlockSpec((1, 1, D), lambda b, t: (b, 0, 0))
    return pl.pallas_call(
        functools.partial(_rwkv_mix_kernel, n_valid_last=n_valid_last),
        grid=(B, nt),
        in_specs=[blk,
                  pl.BlockSpec((SUBLANES, D), lambda b, t: (jnp.maximum((b * nt + t) * hb - 1, 0), 0)),
                  one,
                  pl.BlockSpec((1, D), lambda b, t: (0, 0)),
                  pl.BlockSpec((6, D), lambda b, t: (0, 0))] +
                 [pl.BlockSpec(w.shape, lambda b, t: (0, 0)) for w in (w1, a1, g1)],
        out_specs=[blk] * 3 + [pl.BlockSpec((tt, w.shape[1]), lambda b, t: (b * nt + t, 0)) for w in (w1, a1, g1)]
                  + [one],
        out_shape=[jax.ShapeDtypeStruct((out_rows, D), BF16)] * 3 +
                  [jax.ShapeDtypeStruct((out_rows, w.shape[1]), BF16) for w in (w1, a1, g1)] +
                  [jax.ShapeDtypeStruct((B, 1, D), F32)],
        compiler_params=_params("parallel", "arbitrary"),
        name="rwkv_mix",
    )(x, x, shift0, g.reshape(1, D).astype(F32), mu.astype(F32), w1, a1, g1)


def _rwkv_chunk_kernel(r_ref, k_ref, v_ref, lw_ref, la_ref, lg_ref, w2_ref, a2_ref, g2_ref, w0_ref, a0_ref,
                       kk_ref, ka_ref, rk_ref, lng_ref, lnb_ref, s0_ref, o_ref, so_ref, s_scr, *, hs, C, n_valid):
    c = pl.program_id(2)
    nc = pl.num_programs(2)
    R, L = r_ref.shape
    n_pairs = L // LANES

    @pl.when(c == 0)
    def _():
        z = jnp.zeros((hs, hs), F32)
        for p in range(n_pairs):
            s_scr[p] = jnp.concatenate([jnp.concatenate([s0_ref[0, 2 * p], z], axis=1),
                                        jnp.concatenate([z, s0_ref[0, 2 * p + 1]], axis=1)], axis=0)

    ti = lax.broadcasted_iota(jnp.int32, (C, C), 0)
    si = lax.broadcasted_iota(jnp.int32, (C, C), 1)
    tri = jnp.where(ti >= si, 1.0, 0.0).astype(BF16)
    lane = lax.broadcasted_iota(jnp.int32, (1, LANES), 1)
    m0 = lane < hs
    C2 = 2 * C
    ri = lax.broadcasted_iota(jnp.int32, (C2, C2), 0)
    ci = lax.broadcasted_iota(jnp.int32, (C2, C2), 1)
    same = (ri // C) == (ci // C)
    strict = same & ((ri % C) > (ci % C))
    lower = (ri % C) >= (ci % C)
    vi = lax.broadcasted_iota(jnp.int32, (LANES, LANES), 0) // hs
    vj = lax.broadcasted_iota(jnp.int32, (LANES, LANES), 1) // hs
    blockdiag = vi == vj
    n_steps = int(math.log2(C))
    wide = C2 % LANES == 0
    P = range(n_pairs)
    sls = [slice(p * LANES, (p + 1) * LANES) for p in P]
    inv = 1.0 / hs

    def stack_heads(x):
        return jnp.concatenate([jnp.where(m0, x, 0.0), jnp.where(m0, 0.0, x)], axis=0).astype(BF16)

    S = [s_scr[p] for p in P]
    for sc in range(R // C):
        rows = slice(sc * C, (sc + 1) * C)
        r = r_ref[rows, :]
        k = k_ref[rows, :]
        v = v_ref[rows, :]
        wp = w0_ref[...] + _dot(lw_ref[rows, :], w2_ref[...])
        logw = -jnp.exp(-jax.nn.softplus(-wp) - 0.5)
        if n_valid is not None:
            logw = jnp.where(lax.broadcasted_iota(jnp.int32, (C, 1), 0) + sc * C < n_valid, logw, 0.0)
        a = jax.nn.sigmoid(a0_ref[...] + _dot(la_ref[rows, :], a2_ref[...]))
        kk = k * kk_ref[...]
        kk = kk / jnp.maximum(jnp.sqrt(_group_sum_bcast(kk * kk, hs, terms=1)), 1e-12)
        k2 = k * (1.0 + (a - 1.0) * ka_ref[...])
        bm = kk * a
        cum = _dot_exact_rhs_lhs(tri, logw)
        e_in = jnp.exp(cum)
        e_out = jnp.exp(-cum)
        e_end = jnp.exp(cum[C - 1:C, :] - cum)
        g_end = jnp.exp(cum[C - 1:C, :])
        rt = r * e_in
        kkt = kk * jnp.exp(cum - logw)
        bh = bm * e_out
        kh = k2 * e_out
        bbar = bm * e_end
        kbar = k2 * e_end

        S16 = [s_.astype(BF16) for s_ in S]
        V16 = [v[:, sl].astype(BF16) for sl in sls]
        lhk = [stack_heads(kkt[:, sl]) for sl in sls]
        if wide:
            gbk = [_nt(lhk[p], jnp.concatenate([stack_heads(bh[:, sls[p]]), stack_heads(kh[:, sls[p]])], axis=0))
                   for p in P]
            nmat = [jnp.where(strict, -g_[:, :C2], 0.0) for g_ in gbk]
            auk = [jnp.where(strict, -g_[:, C2:], 0.0).astype(BF16) for g_ in gbk]
        else:
            nmat = [jnp.where(strict, -_nt(lhk[p], stack_heads(bh[:, sls[p]])), 0.0) for p in P]
            auk = [jnp.where(strict, -_nt(lhk[p], stack_heads(kh[:, sls[p]])), 0.0).astype(BF16) for p in P]
        sprod = [_nt(jnp.concatenate([lhk[p], rt[:, sls[p]].astype(BF16)], axis=0), S16[p]) for p in P]
        u = [_dot(auk[p], jnp.concatenate([V16[p], V16[p]], axis=0)) - sprod[p][:C2] for p in P]
        npow = [n_.astype(BF16) for n_ in nmat]
        for it in range(n_steps):
            last = it + 1 == n_steps
            if wide and not last:
                res = [_dot(npow[p], jnp.concatenate([u[p].astype(BF16), npow[p]], axis=1)) for p in P]
                u = [u[p] + res[p][:, :LANES] for p in P]
                npow = [res[p][:, LANES:].astype(BF16) for p in P]
            else:
                u = [u[p] + _dot(npow[p], u[p].astype(BF16)) for p in P]
                if not last:
                    npow = [_dot(npow[p], npow[p]).astype(BF16) for p in P]
        uv = [jnp.concatenate([jnp.where(m0, u[p][:C], u[p][C:]).astype(BF16), V16[p]], axis=0) for p in P]
        ar = [jnp.where(lower, _nt(stack_heads(rt[:, sls[p]]),
                                   jnp.concatenate([bh[:, sls[p]], kh[:, sls[p]]], axis=0).astype(BF16)),
                        0.0).astype(BF16) for p in P]
        tmat = [_dot(ar[p], uv[p]) for p in P]
        ys = [sprod[p][C2:] + jnp.where(m0, tmat[p][:C], tmat[p][C:]) for p in P]
        upd = [_tn(uv[p], jnp.concatenate([bbar[:, sls[p]], kbar[:, sls[p]]], axis=0).astype(BF16)) for p in P]
        S = [S[p] * g_end[:, sls[p]] + jnp.where(blockdiag, upd[p], 0.0) for p in P]

        y = ys[0] if n_pairs == 1 else jnp.concatenate(ys, axis=-1)
        mean = _group_sum_bcast(y, hs, terms=1) * inv
        yc = y - mean
        var = _group_sum_bcast(yc * yc, hs, terms=1) * inv
        yn = yc * lax.rsqrt(var + RW_GN_EPS) * lng_ref[...] + lnb_ref[...]
        bonus = _group_sum_bcast(r * k2 * rk_ref[...], hs, terms=1) * v
        o_ref[rows, :] = ((yn + bonus) * _dot(lg_ref[rows, :], g2_ref[...])).astype(o_ref.dtype)

    for p in P:
        s_scr[p] = S[p]

    @pl.when(c == nc - 1)
    def _():
        for p in P:
            so_ref[0, 2 * p] = S[p][:hs, :hs]
            so_ref[0, 2 * p + 1] = S[p][hs:, hs:]


def _dot_exact_rhs_lhs(m_bf16, x):
    hi = x.astype(BF16)
    lo = (x - hi.astype(F32)).astype(BF16)
    return _dot(m_bf16, hi) + _dot(m_bf16, lo)


def rwkv_chunked(r, k, v, lw, la, lg, w2, a2, g2, w0, a0, k_k, k_a, r_k, ln_g, ln_b, s0, *, B, T, hs, chunk,
                 chunks_per_step=1, lanes_per_step=1024, out_rows=None, n_valid=None):
    D = r.shape[1]
    L = min(lanes_per_step, D)
    npg = L // LANES
    rows = chunk * chunks_per_step
    nc = T // rows
    out_rows = B * T if out_rows is None else out_rows
    seq = pl.BlockSpec((rows, L), lambda b, hg, c: (b * nc + c, hg))
    low = lambda a_: pl.BlockSpec((rows, a_.shape[1]), lambda b, hg, c: (b * nc + c, 0))
    wgt = lambda w_: pl.BlockSpec((w_.shape[0], L), lambda b, hg, c: (0, hg))
    par = pl.BlockSpec((1, L), lambda b, hg, c: (0, hg))
    st = pl.BlockSpec((1, 2 * npg, hs, hs), lambda b, hg, c: (b, hg, 0, 0))
    row = lambda x: x.reshape(1, D).astype(F32)
    return pl.pallas_call(
        functools.partial(_rwkv_chunk_kernel, hs=hs, C=chunk, n_valid=n_valid),
        grid=(B, D // L, nc),
        in_specs=[seq] * 3 + [low(lw), low(la), low(lg), wgt(w2), wgt(a2), wgt(g2)] + [par] * 7 + [st],
        out_specs=[seq, st],
        out_shape=[jax.ShapeDtypeStruct((out_rows, D), BF16),
                   jax.ShapeDtypeStruct(s0.shape, F32)],
        scratch_shapes=[pltpu.VMEM((npg, LANES, LANES), F32)],
        compiler_params=_params("parallel", "parallel", "arbitrary"),
        name="rwkv_chunked",
    )(r, k, v, lw, la, lg, w2, a2, g2, row(w0), row(a0), row(k_k), row(k_a), row(r_k), row(ln_g), row(ln_b), s0)


def _pad_rows(x, front, back):
    return jnp.pad(x, ((0, 0), (front, back), (0, 0)))


def kernel(x_prompt, x_sample, mem_prompt, state_lru_conv, state_lru_h, cache_swa_k, cache_swa_v,
           state_rwkv_shift, state_rwkv_wkv, cache_mem_k, cache_mem_v, state_ffn_conv,
           a_norm_g, a_w_in, a_conv_w, a_conv_b, a_gate_a_w, a_gate_a_b, a_gate_x_w, a_gate_x_b,
           a_lambda, b_q_norm_g, b_k_norm_g, b_sink, a_w_out,
           c_norm_g, c_mu, c_w_r, c_w_k, c_w_v, c_w_o, c_w0, c_w1, c_w2, c_a0, c_a1, c_a2,
           c_g1, c_g2, c_k_k, c_k_a, c_r_k, c_ln_g, c_ln_b,
           m_norm_g, m_mem_norm_g, m_w_q, m_w_kv, m_q_norm_g, m_k_norm_g, m_w_o,
           f_norm_g, f_w_in, f_conv_w, f_conv_b, f_w_out):
    D = x_prompt.shape[-1]
    depth = m_norm_g.shape[0]
    W = a_conv_w.shape[-1]
    KA = a_conv_w.shape[1]
    hd = b_q_norm_g.shape[-1]
    n_q = b_sink.shape[-1]
    n_kv = cache_swa_k.shape[3]
    win = cache_swa_k.shape[2]
    QW, KW = n_q * hd, n_kv * hd
    hs = c_r_k.shape[-1]
    n_rw = c_r_k.shape[1]
    m_heads, m_hd = cache_mem_k.shape[3], cache_mem_k.shape[4]
    MW = m_heads * m_hd
    FF = f_conv_w.shape[-1]
    KF = f_conv_w.shape[1]
    bf = lambda w: w.astype(BF16)
    wcache = {}

    Bp, S = x_prompt.shape[:2]
    Bs, Ts = x_sample.shape[:2]
    Mp, Ms = Bp * S, Bs * Ts
    Mt = Mp + Ms
    tma = _row_tile(Mt, 1280)
    n_a = a_norm_g.shape[0]
    n_c = c_norm_g.shape[0]
    ML = mem_prompt.shape[1]
    PAD8, PAD16 = SUBLANES, 2 * SUBLANES

    def mm(xin, w, layer, key, **kw):
        return dense(xin, w, layer, wcache, (key, layer), tm=tma, **kw)

    def seq_tail(arr, n, cols=slice(None)):
        return jnp.stack([arr[(b + 1) * S - n:(b + 1) * S, cols] for b in range(Bp)])

    def put_sample(buf, rows_s):
        return lax.dynamic_update_slice(buf, rows_s.astype(buf.dtype), (Mp, 0))

    mem_flat = mem_prompt.reshape(Bp * ML, D)
    mks, mvs = [], []
    for l in range(depth):
        mn = rmsnorm_rows(mem_flat, m_mem_norm_g[l])
        kv = matmul(mn, m_w_kv, layer=l)
        mk = headnorm(kv, 0, MW, m_k_norm_g[l], m_hd, F32)
        mks.append(mk.reshape(Bp, ML, m_heads, m_hd))
        mvs.append(kv[:, MW:].reshape(Bp, ML, m_heads, m_hd))
    p_mem_k = jnp.stack(mks)
    p_mem_v = jnp.stack(mvs)

    xp2, xs2 = x_prompt.reshape(Mp, D), x_sample.reshape(Ms, D)
    xf = None

    def first_norm(g):
        return put_sample(rmsnorm_rows(xp2, g, out_rows=Mt), rmsnorm_rows(xs2, g))

    def first_residual(xin, w, layer, key):
        y = dense(xin, w, layer, wcache, (key, layer), rows=Mp, out_rows=Mt, res=xp2)
        return matmul(xin, wcache[(key, layer)], row0=Mp, rows=Ms, res=xs2, res_row0=0, into=y)

    po ={k_: [] for k_ in ("lc", "lh", "sk", "sv", "rs", "rw", "fc")}
    so = {k_: [] for k_ in ("lc", "lh", "sk", "sv", "rs", "rw", "fc")}
    ia = ic = 0
    y_p = y_s = None
    for l in range(depth):
        if l % 2 == 0:
            i = ia
            ia += 1
            assert W == QW
            h = first_norm(a_norm_g[i]) if xf is None else rmsnorm_rows(xf, a_norm_g[i])
            zz = mm(h, a_w_in, i, "a_w_in_rg", n=2 * W)
            qkv = mm(h, a_w_in, i, "a_w_in_qkv", n0=2 * W)
            qn = headnorm(qkv, 0, QW, b_q_norm_g[i], hd, BF16)
            kn = headnorm(qkv, QW // KW, KW, b_k_norm_g[i], hd, F32)
            vv = qkv[:, QW + KW:]
            lru_w = (a_conv_w[i], a_conv_b[i], a_gate_a_w[i], a_gate_a_b[i], a_gate_x_w[i], a_gate_x_b[i],
                     a_lambda[i])
            mix, hl_p = lru_mixer(zz, 0, zz, 1, jnp.zeros((Bp, 1, W), F32), *lru_w, B=Bp, T=S, n_pad=0, tt=256,
                                  out_rows=Mt, out_cols=W + QW)
            mix = swa_attention(qn, bf(kn), bf(vv), b_sink[i], B=Bp, nq=S // win, nkb=S // win, j0=0, win=win,
                                hd=hd, into=mix, out_cb=W // QW)
            po["lc"].append(seq_tail(zz, KA - 1, slice(0, W)))
            po["lh"].append(hl_p.reshape(Bp, W))
            po["sk"].append(seq_tail(kn, win).reshape(Bp, win, n_kv, hd))
            po["sv"].append(seq_tail(vv, win).reshape(Bp, win, n_kv, hd))
            zz_s = zz[Mp:].reshape(Bs, Ts, 2 * W)
            n_pad = PAD8 - Ts
            xr_hist = jnp.concatenate([state_lru_conv[i].astype(F32), zz_s[:, :, :W]], axis=1)
            xr_p = _pad_rows(xr_hist, PAD8 - xr_hist.shape[1], 0).reshape(Bs * PAD8, W)
            yg_p = _pad_rows(zz_s[:, :, W:], n_pad, 0).reshape(Bs * PAD8, W)
            oa_s, hl_s = lru_mixer(xr_p, 0, yg_p, 0, state_lru_h[i].reshape(Bs, 1, W).astype(F32), *lru_w,
                                   B=Bs, T=PAD8, n_pad=n_pad, tt=PAD8)
            oa_s = oa_s.reshape(Bs, PAD8, W)[:, n_pad:].reshape(Ms, W)
            kc = cache_swa_k[i].reshape(Bs, win, KW)
            vc = cache_swa_v[i].reshape(Bs, win, KW)
            k_all = jnp.concatenate([kc, kn[Mp:].reshape(Bs, Ts, KW)], axis=1)
            v_all = jnp.concatenate([vc, vv[Mp:].reshape(Bs, Ts, KW)], axis=1)
            q_s = _pad_rows(qn[Mp:].reshape(Bs, Ts, QW), 0, PAD16 - Ts).reshape(Bs * PAD16, QW)
            o_s = swa_attention(q_s, bf(_pad_rows(k_all, 0, win - Ts)).reshape(Bs * 2 * win, KW),
                                bf(_pad_rows(v_all, 0, win - Ts)).reshape(Bs * 2 * win, KW), b_sink[i],
                                B=Bs, nq=1, nkb=2, j0=1, win=win, hd=hd, tq=PAD16)
            o_s = o_s.reshape(Bs, PAD16, QW)[:, :Ts].reshape(Ms, QW)
            mix = put_sample(mix, jnp.concatenate([oa_s, o_s], axis=-1))
            so["lc"].append(xr_hist[:, Ts:])
            so["lh"].append(hl_s.reshape(Bs, W))
            so["sk"].append(k_all[:, Ts:].reshape(Bs, win, n_kv, hd))
            so["sv"].append(v_all[:, Ts:].reshape(Bs, win, n_kv, hd))
            xf = (first_residual(mix, a_w_out, i, "a_w_out") if xf is None
                  else mm(mix, a_w_out, i, "a_w_out", res=xf))
        else:
            i = ic
            ic += 1
            gl = c_g1[i].shape[1]
            glp = -(-gl // LANES) * LANES
            lora1 = (bf(c_w1[i]), bf(c_a1[i]), bf(jnp.pad(c_g1[i], ((0, 0), (0, glp - gl)))))
            lora2 = (bf(c_w2[i]), bf(c_a2[i]), bf(jnp.pad(c_g2[i], ((0, glp - gl), (0, 0)))), c_w0[i], c_a0[i])
            mixes = rwkv_mix(xf, jnp.zeros((Bp, 1, D), F32), c_norm_g[i], c_mu[i], *lora1, B=Bp, T=S, n_valid=S,
                             out_rows=Mt)
            x_s = _pad_rows(xf[Mp:].reshape(Bs, Ts, D), 0, PAD8 - Ts).reshape(Bs * PAD8, D)
            mixes_s = rwkv_mix(x_s, state_rwkv_shift[i].reshape(Bs, 1, D).astype(F32), c_norm_g[i], c_mu[i], *lora1,
                               B=Bs, T=PAD8, n_valid=Ts, tt=PAD8)
            take = lambda m: m.reshape(Bs, PAD8, m.shape[-1])[:, :Ts].reshape(Ms, m.shape[-1])
            xr, xk, xv, lw, la, lg = [put_sample(mixes[j], take(mixes_s[j])) for j in range(6)]
            po["rs"].append(mixes[6].reshape(Bp, D))
            so["rs"].append(mixes_s[6].reshape(Bs, D))
            r = mm(xr, c_w_r, i, "c_w_r")
            k = mm(xk, c_w_k, i, "c_w_k")
            v = mm(xv, c_w_v, i, "c_w_v")
            rw_par = (c_k_k[i], c_k_a[i], c_r_k[i], c_ln_g[i], c_ln_b[i])
            yo, s_end = rwkv_chunked(r, k, v, lw, la, lg, *lora2, *rw_par, jnp.zeros((Bp, n_rw, hs, hs), F32),
                                     B=Bp, T=S, hs=hs, chunk=64, chunks_per_step=4, out_rows=Mt)
            po["rw"].append(s_end)
            cs = PAD16
            sq = lambda t: jnp.pad(t[Mp:].reshape(Bs, Ts, t.shape[-1]),
                                   ((0, 0), (0, cs - Ts), (0, 0))).reshape(Bs * cs, t.shape[-1])
            yo_s, s_end_s = rwkv_chunked(sq(r), sq(k), sq(v), sq(lw), sq(la), sq(lg), *lora2, *rw_par,
                                         state_rwkv_wkv[i].astype(F32), B=Bs, T=cs, hs=hs, chunk=cs,
                                         lanes_per_step=2048, n_valid=Ts)
            yo = put_sample(yo, yo_s.reshape(Bs, cs, D)[:, :Ts].reshape(Ms, D))
            so["rw"].append(s_end_s)
            xf = mm(yo, c_w_o, i, "c_w_o", res=xf)
        hm = rmsnorm_rows(xf, m_norm_g[l])
        q = mm(hm, m_w_q, l, "m_w_q")
        om = mem_attention(q, m_q_norm_g[l], bf(p_mem_k[l]).reshape(Bp, ML, MW), bf(p_mem_v[l]).reshape(Bp, ML, MW),
                           B=Bp, T=S, n_heads=m_heads, out_rows=Mt)
        q_s = _pad_rows(q[Mp:].reshape(Bs, Ts, MW), 0, PAD8 - Ts).reshape(Bs * PAD8, MW)
        om_s = mem_attention(q_s, m_q_norm_g[l], bf(cache_mem_k[l]).reshape(Bs, ML, MW),
                             bf(cache_mem_v[l]).reshape(Bs, ML, MW), B=Bs, T=PAD8, n_heads=m_heads, tt=PAD8)
        om = put_sample(om, om_s.reshape(Bs, PAD8, MW)[:, :Ts].reshape(Ms, MW))
        xf, hf = proj_res_norm(om, m_w_o, l, xf, f_norm_g[l])
        act, tail = ffn_in_fused(hf, f_w_in, l, f_conv_w[l], f_conv_b[l], wcache, ("f_w_in", l), seq_len=S, rows=Mp)
        po["fc"].append(tail[:, SUBLANES - (KF - 1):, :])
        wg, wu = wcache[("f_w_in", l)]
        hf_s = hf[Mp:]
        gate_s = matmul(hf_s, wg, tn=1024).reshape(Bs, Ts, FF)
        up_s = matmul(hf_s, wu, tn=1024).reshape(Bs, Ts, FF)
        g_hist = jnp.concatenate([state_ffn_conv[l].astype(F32), gate_s], axis=1)
        g_p = _pad_rows(g_hist, PAD8 - g_hist.shape[1], 0)
        up_p = _pad_rows(up_s, PAD8 - Ts, 0)
        act_s = ffn_act(g_p, 0, up_p, 0, f_conv_w[l], f_conv_b[l], tt=PAD8, tc=FF)[:, PAD8 - Ts:]
        act = put_sample(act, act_s.reshape(Ms, FF))
        so["fc"].append(g_hist[:, Ts:])
        if l + 1 < depth:
            xf = mm(act, f_w_out, l, "f_w_out", res=xf, tn=1024, tk=2048)
        else:
            y_p = dense(act, f_w_out, l, wcache, ("f_w_out", l), rows=Mp, out_rows=Mp, res=xf, tn=1024, tk=2048)
            y_s = matmul(act, wcache[("f_w_out", l)], row0=Mp, rows=Ms, out_rows=Ms, out_row0=0, res=xf,
                         tn=1024, tk=2048)
    st = jnp.stack
    return (y_p.reshape(Bp, S, D), y_s.reshape(Bs, Ts, D),
            st(po["lc"]), st(po["lh"]), st(po["sk"]), st(po["sv"]), st(po["rs"]), st(po["rw"]),
            p_mem_k, p_mem_v, st(po["fc"]),
            st(so["lc"]), st(so["lh"]), st(so["sk"]), st(so["sv"]), st(so["rs"]), st(so["rw"]), st(so["fc"]))
```

```python
import functools
import math

import jax
import jax.numpy as jnp
from jax import lax
from jax.experimental import pallas as pl
from jax.experimental.pallas import tpu as pltpu

F32 = jnp.float32
BF16 = jnp.bfloat16

NORM_EPS = 1e-6
RW_GN_EPS = 64e-5
LRU_C = 8.0
LANES = 128
SUBLANES = 8
VMEM_LIMIT_BYTES = 56 * 1024 * 1024


def _params(*sem):
    return pltpu.CompilerParams(dimension_semantics=sem, vmem_limit_bytes=VMEM_LIMIT_BYTES)


def _nt(a, b):
    return lax.dot_general(a, b, (((1,), (1,)), ((), ())), preferred_element_type=F32)


def _tn(a, b):
    return lax.dot_general(a, b, (((0,), (0,)), ((), ())), preferred_element_type=F32)


def _dot(a, b):
    return jnp.dot(a, b, preferred_element_type=F32)


def _group_sum_bcast(x, width, terms=2):
    m, L = x.shape
    n = L // LANES
    if width == LANES:
        parts = []
        for c in range(n):
            s = jnp.sum(x[:, c * LANES:(c + 1) * LANES], axis=-1, keepdims=True)
            parts.append(jnp.broadcast_to(s, (m, LANES)))
        return parts[0] if n == 1 else jnp.concatenate(parts, axis=-1)
    li = lax.broadcasted_iota(jnp.int32, (LANES, LANES), 0) // width
    lj = lax.broadcasted_iota(jnp.int32, (LANES, LANES), 1) // width
    e = jnp.where(li == lj, 1.0, 0.0).astype(BF16)
    xs = x if n == 1 else jnp.concatenate([x[:, c * LANES:(c + 1) * LANES] for c in range(n)], axis=0)
    hi = xs.astype(BF16)
    out = _dot(hi, e)
    if terms > 1:
        out = out + _dot((xs - hi.astype(F32)).astype(BF16), e)
    return out if n == 1 else jnp.concatenate([out[c * m:(c + 1) * m] for c in range(n)], axis=-1)


def _shift_rows(x, prev8, s):
    rolled = pltpu.roll(x, s, 0)
    top = jnp.where(lax.broadcasted_iota(jnp.int32, (SUBLANES, 1), 0) < s,
                    pltpu.roll(prev8, s, 0), rolled[0:SUBLANES])
    if x.shape[0] == SUBLANES:
        return top
    return jnp.concatenate([top, rolled[SUBLANES:]], axis=0)


def _rmsnorm_kernel(x_ref, g_ref, o_ref):
    x = x_ref[...]
    ms = jnp.mean(x * x, axis=-1, keepdims=True)
    o_ref[...] = (x * lax.rsqrt(ms + NORM_EPS) * g_ref[...]).astype(o_ref.dtype)


def _row_tile(m, target):
    best = None
    for d in range(16, min(m, target) + 1, 16):
        if m % d == 0:
            best = d
    return best or m


def rmsnorm_rows(x, g, out_dtype=BF16, tm=512, out_rows=None):
    M, D = x.shape
    tm = _row_tile(M, tm)
    return pl.pallas_call(
        _rmsnorm_kernel,
        grid=(M // tm,),
        in_specs=[pl.BlockSpec((tm, D), lambda i: (i, 0)), pl.BlockSpec((1, D), lambda i: (0, 0))],
        out_specs=pl.BlockSpec((tm, D), lambda i: (i, 0)),
        out_shape=jax.ShapeDtypeStruct((M if out_rows is None else out_rows, D), out_dtype),
        compiler_params=_params("parallel"),
        name="rmsnorm_rows",
    )(x, g.reshape(1, D))


def _mm_kernel(*refs, nk, has_bias, has_res, has_into, emit_w, in_place, act):
    x_ref, w_ref = refs[0], refs[1]
    pos = 2
    b_ref = r_ref = wb_ref = acc_ref = None
    if has_bias:
        b_ref = refs[pos]
        pos += 1
    if has_res:
        r_ref = refs[pos]
        pos += 1
    if has_into:
        pos += 1
    o_ref = refs[pos]
    pos += 1
    if emit_w:
        wb_ref = refs[pos]
        pos += 1
    if nk > 1 and not in_place:
        acc_ref = refs[pos]

    def weights(cs=slice(None)):
        w16 = w_ref[:, cs].astype(BF16)
        if emit_w:
            wb_ref[:, cs] = w16
        return w16

    def epilogue(y):
        if has_bias:
            y = y + b_ref[...]
        if act == "tanh":
            y = jnp.tanh(y)
        elif act == "sigmoid":
            y = jax.nn.sigmoid(y)
        if has_res:
            y = y + r_ref[...]
        o_ref[...] = y.astype(o_ref.dtype)

    if nk > 1 and in_place:
        k = pl.program_id(2)
        tn = o_ref.shape[1]
        cw = min(tn, 2 * LANES)

        def sweep(first):
            x = x_ref[...]
            for c in range(tn // cw):
                cs = slice(c * cw, (c + 1) * cw)
                part = _dot(x, weights(cs))
                if not first:
                    o_ref[:, cs] += part
                elif has_res:
                    o_ref[:, cs] = part + r_ref[:, cs]
                else:
                    o_ref[:, cs] = part

        pl.when(k == 0)(functools.partial(sweep, True))
        pl.when(k > 0)(functools.partial(sweep, False))
        return

    part = _dot(x_ref[...], weights())
    if nk == 1:
        epilogue(part)
        return
    k = pl.program_id(2)

    @pl.when(k == 0)
    def _():
        acc_ref[...] = part

    @pl.when(k > 0)
    def _():
        acc_ref[...] += part

    @pl.when(k == nk - 1)
    def _():
        epilogue(acc_ref[...])


def _pick(n, prefs):
    for p in prefs:
        if n % p == 0:
            return p
    return n


def matmul(x, w, *, layer=None, n0=0, n=None, bias=None, res=None, act=None, out_dtype=F32,
           tm=1024, tn=512, tk=4096, row0=0, rows=None, into=None, emit_w=False, out_rows=None, out_row0=None,
           res_row0=None):
    M, K = x.shape
    N = w.shape[-1] - n0 if n is None else n
    rows = M - row0 if rows is None else rows
    tm = min(tm, rows)
    out_rows = M if out_rows is None else out_rows
    o0 = (row0 if out_row0 is None else out_row0) // tm
    tn = _pick(math.gcd(N, n0) if n0 else N, (tn, 512, 256, 128))
    tk = _pick(K, (tk, 2048, 1024, 512))
    nk = K // tk
    j0 = n0 // tn
    i0 = row0 // tm
    assert row0 % tm == 0 and rows % tm == 0 and (not emit_w or rows == tm)
    in_specs = [pl.BlockSpec((tm, tk), lambda i, j, k: (i + i0, k)),
                pl.BlockSpec((tk, tn), lambda i, j, k: (k, j + j0)) if layer is None else
                pl.BlockSpec((None, tk, tn), lambda i, j, k: (layer, k, j + j0))]
    args = [x, w]
    if bias is not None:
        in_specs.append(pl.BlockSpec((1, tn), lambda i, j, k: (0, j)))
        args.append(bias.reshape(1, N).astype(F32))
    if res is not None:
        r0 = i0 if res_row0 is None else res_row0 // tm
        in_specs.append(pl.BlockSpec((tm, tn), lambda i, j, k: (i + r0, j)))
        args.append(res)
    aliases = {}
    if into is not None:
        aliases = {len(args): 0}
        in_specs.append(pl.BlockSpec(memory_space=pl.ANY))
        args.append(into)
    in_place = bias is None and act is None and out_dtype == F32
    kern = functools.partial(_mm_kernel, nk=nk, has_bias=bias is not None, has_res=res is not None,
                             has_into=into is not None, emit_w=emit_w, in_place=in_place, act=act)
    out_specs = [pl.BlockSpec((tm, tn), lambda i, j, k: (i + o0, j))]
    out_shape = [jax.ShapeDtypeStruct((out_rows, N), out_dtype)]
    if emit_w:
        out_specs.append(pl.BlockSpec((tk, tn), lambda i, j, k: (k, j)))
        out_shape.append(jax.ShapeDtypeStruct((K, N), BF16))
    outs = pl.pallas_call(
        kern,
        grid=(rows // tm, N // tn, nk),
        in_specs=in_specs,
        out_specs=out_specs,
        out_shape=out_shape,
        input_output_aliases=aliases,
        scratch_shapes=[pltpu.VMEM((tm, tn), F32)] if (nk > 1 and not in_place) else [],
        compiler_params=_params("parallel", "parallel", "arbitrary"),
        name="matmul",
    )(*args)
    return outs if emit_w else outs[0]


def dense(x, w, layer, wcache, key, **kw):
    rows = kw.pop("rows", x.shape[0])
    tm = min(kw.get("tm", 1024), rows)
    first, wb = matmul(x, w, layer=layer, rows=tm, emit_w=True, **kw)
    wcache[key] = wb
    if tm == rows:
        return first
    kw = {a: b for a, b in kw.items() if a not in ("n0", "n")}
    return matmul(x, wb, row0=tm, rows=rows - tm, into=first, **kw)


def _proj_res_norm_kernel(x_ref, w_ref, r_ref, g_ref, y_ref, h_ref, w_scr):
    @pl.when(pl.program_id(0) == 0)
    def _():
        w_scr[...] = w_ref[...].astype(BF16)

    y = _dot(x_ref[...], w_scr[...]) + r_ref[...]
    y_ref[...] = y
    ms = jnp.mean(y * y, axis=-1, keepdims=True)
    h_ref[...] = (y * lax.rsqrt(ms + NORM_EPS) * g_ref[...]).astype(h_ref.dtype)


def proj_res_norm(x, w, layer, res, g, tm=320):
    M, K = x.shape
    N = w.shape[-1]
    tm = _row_tile(M, tm)
    row_blk = lambda c: pl.BlockSpec((tm, c), lambda i: (i, 0))
    return pl.pallas_call(
        _proj_res_norm_kernel,
        grid=(M // tm,),
        in_specs=[row_blk(K), pl.BlockSpec((None, K, N), lambda i: (layer, 0, 0)), row_blk(N),
                  pl.BlockSpec((1, N), lambda i: (0, 0))],
        out_specs=[row_blk(N), row_blk(N)],
        out_shape=[jax.ShapeDtypeStruct((M, N), F32), jax.ShapeDtypeStruct((M, N), BF16)],
        scratch_shapes=[pltpu.VMEM((K, N), BF16)],
        compiler_params=_params("arbitrary"),
        name="proj_res_norm",
    )(x, w, res, g.reshape(1, N).astype(F32))


def _norm_proj_kernel(x_ref, g_ref, w_ref, y_ref, w_scr):
    @pl.when(pl.program_id(0) == 0)
    def _():
        w_scr[...] = w_ref[...].astype(BF16)

    x = x_ref[...]
    ms = jnp.mean(x * x, axis=-1, keepdims=True)
    h = (x * lax.rsqrt(ms + NORM_EPS) * g_ref[...]).astype(BF16)
    y_ref[...] = _dot(h, w_scr[...])


def norm_proj(x, g, w, layer, tm=320):
    M, D = x.shape
    N = w.shape[-1]
    tm = _row_tile(M, tm)
    return pl.pallas_call(
        _norm_proj_kernel,
        grid=(M // tm,),
        in_specs=[pl.BlockSpec((tm, D), lambda i: (i, 0)), pl.BlockSpec((1, D), lambda i: (0, 0)),
                  pl.BlockSpec((None, D, N), lambda i: (layer, 0, 0))],
        out_specs=pl.BlockSpec((tm, N), lambda i: (i, 0)),
        out_shape=jax.ShapeDtypeStruct((M, N), F32),
        scratch_shapes=[pltpu.VMEM((D, N), BF16)],
        compiler_params=_params("arbitrary"),
        name="norm_proj",
    )(x, g.reshape(1, D).astype(F32), w)


def _headnorm_kernel(x_ref, g_ref, o_ref, *, hd):
    x = x_ref[...]
    ms = _group_sum_bcast(x * x, hd) * (1.0 / hd)
    o_ref[...] = (x * lax.rsqrt(ms + NORM_EPS) * g_ref[...]).astype(o_ref.dtype)


def headnorm(x, col_block, width, g, hd, out_dtype, tm=512):
    M = x.shape[0]
    tm = _row_tile(M, tm)
    g_row = jnp.tile(g.astype(F32), width // hd).reshape(1, width)
    return pl.pallas_call(
        functools.partial(_headnorm_kernel, hd=hd),
        grid=(M // tm,),
        in_specs=[pl.BlockSpec((tm, width), lambda i: (i, col_block)),
                  pl.BlockSpec((1, width), lambda i: (0, 0))],
        out_specs=pl.BlockSpec((tm, width), lambda i: (i, 0)),
        out_shape=jax.ShapeDtypeStruct((M, width), out_dtype),
        compiler_params=_params("parallel"),
        name="headnorm",
    )(x, g_row)


def _lru_kernel(xr_ref, halo_ref, yg_ref, h0_ref, cw_ref, cb_ref, gaw_ref, gab_ref, gxw_ref, gxb_ref,
                lam_ref, o_ref, hl_ref, a_scr, u_scr, h_scr, *, n_pad, n_blocks, bw, scan_w):
    t = pl.program_id(1)
    tt, W = a_scr.shape

    @pl.when(t == 0)
    def _():
        h_scr[...] = h0_ref[0]

    x = xr_ref[...]
    prev = halo_ref[...] * jnp.where(t > 0, 1.0, 0.0)
    xc = cb_ref[...] + cw_ref[3:4, :] * x
    for s in (1, 2, 3):
        xc = xc + cw_ref[3 - s:4 - s, :] * _shift_rows(x, prev, s)

    nsp = -LRU_C * jax.nn.softplus(-lam_ref[...])
    if n_pad:
        live = (lax.broadcasted_iota(jnp.int32, (tt, 1), 0) >= n_pad) | (t > 0)
    for n in range(n_blocks):
        sl = slice(n * bw, (n + 1) * bw)
        xb = xc[:, sl]
        xb16 = xb.astype(BF16)
        r = jax.nn.sigmoid(_dot(xb16, gaw_ref[n]) + gab_ref[:, sl])
        i = jax.nn.sigmoid(_dot(xb16, gxw_ref[n]) + gxb_ref[:, sl])
        log_a = r * nsp[:, sl]
        a = jnp.exp(log_a)
        u = jnp.sqrt(jnp.maximum(-jnp.tanh(log_a) * (a * a + 1.0), 0.0)) * (i * xb)
        if n_pad:
            a = jnp.where(live, a, 1.0)
            u = jnp.where(live, u, 0.0)
        a_scr[:, sl] = a
        u_scr[:, sl] = u

    row8 = lax.broadcasted_iota(jnp.int32, (SUBLANES, 1), 0)
    for c in range(W // scan_w):
        cs = slice(c * scan_w, (c + 1) * scan_w)

        def body(gi, h, cs=cs):
            r0 = pl.multiple_of(gi * SUBLANES, SUBLANES)
            A = a_scr[pl.ds(r0, SUBLANES), cs]
            U = u_scr[pl.ds(r0, SUBLANES), cs]
            for s in (1, 2, 4):
                As = pltpu.roll(A, s, 0)
                Us = pltpu.roll(U, s, 0)
                m = row8 >= s
                U = jnp.where(m, A * Us + U, U)
                A = jnp.where(m, A * As, A)
            H = A * h + U
            u_scr[pl.ds(r0, SUBLANES), cs] = H
            return H[SUBLANES - 1:SUBLANES, :]

        h_end = lax.fori_loop(0, tt // SUBLANES, body, h_scr[:, cs])
        h_scr[:, cs] = h_end

    o_ref[...] = (u_scr[...] * jax.nn.gelu(yg_ref[...], approximate=True)).astype(o_ref.dtype)
    hl_ref[0] = h_scr[...]


def lru_mixer(xr, xr_cb, yg, yg_cb, h0, conv_w, conv_b, ga_w, ga_b, gx_w, gx_b, lam, *, B, T, n_pad, tt,
              out_rows=None, out_cols=None):
    W = conv_w.shape[1]
    nb, bw = ga_w.shape[0], ga_w.shape[1]
    tt = min(tt, T)
    nt = T // tt
    hb = tt // SUBLANES
    out_rows = B * T if out_rows is None else out_rows
    out_cols = W if out_cols is None else out_cols
    row = lambda v: v.reshape(1, W).astype(F32)
    kern = functools.partial(_lru_kernel, n_pad=n_pad, n_blocks=nb, bw=bw, scan_w=min(W, 512))
    return pl.pallas_call(
        kern,
        grid=(B, nt),
        in_specs=[
            pl.BlockSpec((tt, W), lambda b, t: (b * nt + t, xr_cb)),
            pl.BlockSpec((SUBLANES, W), lambda b, t: (jnp.maximum((b * nt + t) * hb - 1, 0), xr_cb)),
            pl.BlockSpec((tt, W), lambda b, t: (b * nt + t, yg_cb)),
            pl.BlockSpec((1, 1, W), lambda b, t: (b, 0, 0)),
            pl.BlockSpec((4, W), lambda b, t: (0, 0)),
            pl.BlockSpec((1, W), lambda b, t: (0, 0)),
            pl.BlockSpec((nb, bw, bw), lambda b, t: (0, 0, 0)),
            pl.BlockSpec((1, W), lambda b, t: (0, 0)),
            pl.BlockSpec((nb, bw, bw), lambda b, t: (0, 0, 0)),
            pl.BlockSpec((1, W), lambda b, t: (0, 0)),
            pl.BlockSpec((1, W), lambda b, t: (0, 0)),
        ],
        out_specs=[pl.BlockSpec((tt, W), lambda b, t: (b * nt + t, 0)),
                   pl.BlockSpec((1, 1, W), lambda b, t: (b, 0, 0))],
        out_shape=[jax.ShapeDtypeStruct((out_rows, out_cols), BF16), jax.ShapeDtypeStruct((B, 1, W), F32)],
        scratch_shapes=[pltpu.VMEM((tt, W), F32), pltpu.VMEM((tt, W), F32), pltpu.VMEM((1, W), F32)],
        compiler_params=_params("parallel", "arbitrary"),
        name="lru_mixer",
    )(xr, xr, yg, h0, conv_w.astype(F32), row(conv_b), ga_w.astype(BF16), row(ga_b),
      gx_w.astype(BF16), row(gx_b), row(lam))


def _swa_kernel(sink_ref, q_ref, kp_ref, kc_ref, vp_ref, vc_ref, *rest, j0, n_kv, group, hd, win):
    o_ref = rest[-1]
    j = pl.program_id(1) + j0
    q = q_ref[...]
    k2 = jnp.concatenate([kp_ref[...], kc_ref[...]], axis=0)
    v2 = jnp.concatenate([vp_ref[...], vc_ref[...]], axis=0)
    tq = q.shape[0]
    qi = lax.broadcasted_iota(jnp.int32, (tq, 2 * win), 0)
    kj = lax.broadcasted_iota(jnp.int32, (tq, 2 * win), 1)
    dist = qi + win - kj
    mask = (dist >= 0) & (dist < win) & ((j > 0) | (kj >= win))
    scale = hd ** -0.5
    for kvh in range(n_kv):
        kh = k2[:, kvh * hd:(kvh + 1) * hd]
        vh = v2[:, kvh * hd:(kvh + 1) * hd]
        hs_ = [kvh * group + g for g in range(group)]
        s = [jnp.where(mask, _nt(q[:, h * hd:(h + 1) * hd], kh) * scale, -jnp.inf) for h in hs_]
        m = [jnp.maximum(jnp.max(s[g], axis=-1, keepdims=True), sink_ref[hs_[g]]) for g in range(group)]
        p = [jnp.exp(s[g] - m[g]) for g in range(group)]
        den = [jnp.sum(p[g], axis=-1, keepdims=True) + jnp.exp(sink_ref[hs_[g]] - m[g]) for g in range(group)]
        outs = [_dot(p[g].astype(BF16), vh) / den[g] for g in range(group)]
        o_ref[:, hs_[0] * hd:(hs_[-1] + 1) * hd] = jnp.concatenate(outs, axis=-1).astype(o_ref.dtype)


def swa_attention(q, k, v, sink, *, B, nq, nkb, j0, win, hd, tq=None, into=None, out_cb=0):
    QW = q.shape[1]
    KW = k.shape[1]
    n_kv = KW // hd
    group = QW // KW
    tq = win if tq is None else tq
    kern = functools.partial(_swa_kernel, j0=j0, n_kv=n_kv, group=group, hd=hd, win=win)
    prev = lambda b, j: (b * nkb + jnp.maximum(j + j0 - 1, 0), 0)
    cur = lambda b, j: (b * nkb + j + j0, 0)
    in_specs = [pl.BlockSpec(memory_space=pltpu.SMEM),
                pl.BlockSpec((tq, QW), lambda b, j: (b * nq + j, 0)),
                pl.BlockSpec((win, KW), prev), pl.BlockSpec((win, KW), cur),
                pl.BlockSpec((win, KW), prev), pl.BlockSpec((win, KW), cur)]
    args = [sink.astype(F32), q, k, k, v, v]
    aliases = {}
    out_shape = jax.ShapeDtypeStruct((B * nq * tq, QW), BF16)
    if into is not None:
        aliases = {len(args): 0}
        in_specs.append(pl.BlockSpec(memory_space=pl.ANY))
        args.append(into)
        out_shape = jax.ShapeDtypeStruct(into.shape, into.dtype)
    return pl.pallas_call(
        kern,
        grid=(B, nq),
        in_specs=in_specs,
        out_specs=pl.BlockSpec((tq, QW), lambda b, j: (b * nq + j, out_cb)),
        out_shape=out_shape,
        input_output_aliases=aliases,
        compiler_params=_params("parallel", "parallel"),
        name="swa_attention",
    )(*args)


def _mem_attn_kernel(q_ref, g_ref, mk_ref, mv_ref, o_ref, *, n_heads, hd):
    q = q_ref[...]
    scale = hd ** -0.5
    for h in range(n_heads):
        sl = slice(h * hd, (h + 1) * hd)
        qh = q[:, sl]
        qn = (qh * lax.rsqrt(jnp.mean(qh * qh, axis=-1, keepdims=True) + NORM_EPS) * g_ref[...]).astype(BF16)
        s = _nt(qn, mk_ref[0, :, sl].astype(BF16)) * scale
        m = jnp.max(s, axis=-1, keepdims=True)
        p = jnp.exp(s - m)
        p = p / jnp.sum(p, axis=-1, keepdims=True)
        o_ref[:, sl] = _dot(p.astype(BF16), mv_ref[0, :, sl].astype(BF16)).astype(o_ref.dtype)


def mem_attention(q, qn_g, mk, mv, *, B, T, n_heads, tt=256, out_rows=None):
    MW = q.shape[1]
    ML = mk.shape[1]
    hd = MW // n_heads
    tt = min(tt, T)
    nt = T // tt
    out_rows = B * T if out_rows is None else out_rows
    return pl.pallas_call(
        functools.partial(_mem_attn_kernel, n_heads=n_heads, hd=hd),
        grid=(B, nt),
        in_specs=[pl.BlockSpec((tt, MW), lambda b, t: (b * nt + t, 0)),
                  pl.BlockSpec((1, hd), lambda b, t: (0, 0)),
                  pl.BlockSpec((1, ML, MW), lambda b, t: (b, 0, 0)),
                  pl.BlockSpec((1, ML, MW), lambda b, t: (b, 0, 0))],
        out_specs=pl.BlockSpec((tt, MW), lambda b, t: (b * nt + t, 0)),
        out_shape=jax.ShapeDtypeStruct((out_rows, MW), BF16),
        compiler_params=_params("parallel", "parallel"),
        name="mem_attention",
    )(q, qn_g.reshape(1, hd).astype(F32), mk, mv)


def _ffn_act_kernel(g_ref, halo_ref, up_ref, cw_ref, cb_ref, o_ref):
    t = pl.program_id(2)
    g = g_ref[0]
    prev = halo_ref[0] * jnp.where(t > 0, 1.0, 0.0)
    gc = cb_ref[...] + cw_ref[2:3, :] * g
    for s in (1, 2):
        gc = gc + cw_ref[2 - s:3 - s, :] * _shift_rows(g, prev, s)
    o_ref[0] = (jax.nn.gelu(gc, approximate=True) * up_ref[0]).astype(o_ref.dtype)


def ffn_act(gate, gate_cb0, up, up_cb0, conv_w, conv_b, *, tt=256, tc=512):
    B, T, _ = gate.shape
    C = conv_w.shape[1]
    tt = min(tt, T)
    hb = tt // SUBLANES
    return pl.pallas_call(
        _ffn_act_kernel,
        grid=(B, C // tc, T // tt),
        in_specs=[pl.BlockSpec((1, tt, tc), lambda b, c, t: (b, t, gate_cb0 + c)),
                  pl.BlockSpec((1, SUBLANES, tc), lambda b, c, t: (b, jnp.maximum(t * hb - 1, 0), gate_cb0 + c)),
                  pl.BlockSpec((1, tt, tc), lambda b, c, t: (b, t, up_cb0 + c)),
                  pl.BlockSpec((3, tc), lambda b, c, t: (0, c)),
                  pl.BlockSpec((1, tc), lambda b, c, t: (0, c))],
        out_specs=pl.BlockSpec((1, tt, tc), lambda b, c, t: (b, t, c)),
        out_shape=jax.ShapeDtypeStruct((B, T, C), BF16),
        compiler_params=_params("parallel", "parallel", "parallel"),
        name="ffn_act",
    )(gate, gate, up, conv_w.astype(F32), conv_b.reshape(1, C).astype(F32))


def _ffn_in_kernel(*refs, blocks_per_seq, sub, has_into, emit_w):
    x_ref, wg_ref, wu_ref, cw_ref, cb_ref = refs[:5]
    pos = 6 if has_into else 5
    o_ref, tail_ref = refs[pos], refs[pos + 1]
    carry_scr = refs[-1]
    i = pl.program_id(0)
    j = pl.program_id(1)
    tm = x_ref.shape[0]

    @pl.when(i % blocks_per_seq == 0)
    def _():
        carry_scr[j] = jnp.zeros(carry_scr.shape[1:], F32)

    prev = carry_scr[j]
    wg = wg_ref[...].astype(BF16)
    wu = wu_ref[...].astype(BF16)
    if emit_w:
        refs[pos + 2][...] = wg
        refs[pos + 3][...] = wu
    for c in range(tm // sub):
        rows = slice(c * sub, (c + 1) * sub)
        x = x_ref[rows, :]
        gate = _dot(x, wg)
        up = _dot(x, wu)
        gc = cb_ref[...] + cw_ref[2:3, :] * gate
        for s in (1, 2):
            gc = gc + cw_ref[2 - s:3 - s, :] * _shift_rows(gate, prev, s)
        o_ref[rows, :] = (jax.nn.gelu(gc, approximate=True) * up).astype(o_ref.dtype)
        prev = gate[sub - SUBLANES:, :]
    carry_scr[j] = prev
    tail_ref[0] = prev


def _ffn_in_call(x, wg, wu, layer, up_off, conv_w, conv_b, *, seq_len, row0, rows, tm, tn, sub, into, emit_w):
    M, D = x.shape
    FF = conv_w.shape[1]
    tm = min(tm, seq_len)
    bps = seq_len // tm
    nj = FF // tn
    i0 = row0 // tm
    uo = up_off // tn
    wspec = lambda off: (pl.BlockSpec((D, tn), lambda i, j: (0, j + off)) if layer is None else
                         pl.BlockSpec((None, D, tn), lambda i, j: (layer, 0, j + off)))
    in_specs = [pl.BlockSpec((tm, D), lambda i, j: (i + i0, 0), pipeline_mode=pl.Buffered(1)),
                wspec(0), wspec(uo),
                pl.BlockSpec((3, tn), lambda i, j: (0, j)),
                pl.BlockSpec((1, tn), lambda i, j: (0, j))]
    args = [x, wg, wu, conv_w.astype(F32), conv_b.reshape(1, FF).astype(F32)]
    aliases = {}
    if into is not None:
        aliases = {len(args): 0}
        in_specs.append(pl.BlockSpec(memory_space=pl.ANY))
        args.append(into)
    out_specs = [pl.BlockSpec((tm, tn), lambda i, j: (i + i0, j)),
                 pl.BlockSpec((1, SUBLANES, tn), lambda i, j: (i, 0, j))]
    out_shape = [jax.ShapeDtypeStruct((M, FF), BF16),
                 jax.ShapeDtypeStruct((rows // tm, SUBLANES, FF), F32)]
    if emit_w:
        out_specs += [pl.BlockSpec((D, tn), lambda i, j: (0, j))] * 2
        out_shape += [jax.ShapeDtypeStruct((D, FF), BF16)] * 2
    outs = pl.pallas_call(
        functools.partial(_ffn_in_kernel, blocks_per_seq=bps, sub=min(sub, tm), has_into=into is not None,
                          emit_w=emit_w),
        grid=(rows // tm, nj),
        in_specs=in_specs,
        out_specs=out_specs,
        out_shape=out_shape,
        input_output_aliases=aliases,
        scratch_shapes=[pltpu.VMEM((nj, SUBLANES, tn), F32)],
        compiler_params=_params("arbitrary", "arbitrary"),
        name="ffn_in_fused",
    )(*args)
    return (outs[0], outs[1][bps - 1::bps]) + tuple(outs[2:])


def ffn_in_fused(x, w_in, layer, conv_w, conv_b, wcache, key, *, seq_len, rows=None, sub=256):
    M = x.shape[0] if rows is None else rows
    FF = conv_w.shape[1]
    act, tail0, wg, wu = _ffn_in_call(x, w_in, w_in, layer, FF, conv_w, conv_b, seq_len=seq_len, row0=0,
                                      rows=seq_len, tm=2048, tn=256, sub=sub, into=None, emit_w=True)
    wcache[key] = (wg, wu)
    if M == seq_len:
        return act, tail0
    act, tails = _ffn_in_call(x, wg, wu, None, 0, conv_w, conv_b, seq_len=seq_len, row0=seq_len,
                              rows=M - seq_len, tm=1024, tn=512, sub=sub, into=act, emit_w=False)
    return act, jnp.concatenate([tail0, tails], axis=0)


def _rwkv_mix_kernel(x_ref, halo_ref, s0_ref, g_ref, mu_ref, w1_ref, a1_ref, g1_ref, *out_refs, n_valid_last):
    t = pl.program_id(1)
    nt = pl.num_programs(1)
    xr_ref, xk_ref, xv_ref, lw_ref, la_ref, lg_ref, hl_ref = out_refs

    def norm(v):
        return v * lax.rsqrt(jnp.mean(v * v, axis=-1, keepdims=True) + NORM_EPS) * g_ref[...]

    h = norm(x_ref[...])
    hp = norm(halo_ref[...])
    first = jnp.where(t > 0, 1.0, 0.0)
    hp = hp * first + jnp.broadcast_to(s0_ref[0], hp.shape) * (1.0 - first)
    xx = _shift_rows(h, hp, 1) - h
    mix = lambda j: (h + xx * mu_ref[j:j + 1, :]).astype(BF16)
    xr_ref[...] = mix(0)
    xk_ref[...] = mix(2)
    xv_ref[...] = mix(3)
    lw_ref[...] = jnp.tanh(_dot(mix(1), w1_ref[...])).astype(BF16)
    la_ref[...] = _dot(mix(4), a1_ref[...]).astype(BF16)
    lg_ref[...] = jax.nn.sigmoid(_dot(mix(5), g1_ref[...])).astype(BF16)

    @pl.when(t == nt - 1)
    def _():
        hl_ref[0] = h[n_valid_last - 1:n_valid_last, :]


def rwkv_mix(x, shift0, g, mu, w1, a1, g1, *, B, T, n_valid, tt=256, out_rows=None):
    D = x.shape[1]
    tt = min(tt, T)
    nt = T // tt
    hb = tt // SUBLANES
    out_rows = B * T if out_rows is None else out_rows
    n_valid_last = n_valid - (nt - 1) * tt
    blk = pl.BlockSpec((tt, D), lambda b, t: (b * nt + t, 0))
    one = pl.BlockSpec((1, 1, D), lambda b, t: (b, 0, 0))
    return pl.pallas_call(
        functools.partial(_rwkv_mix_kernel, n_valid_last=n_valid_last),
        grid=(B, nt),
        in_specs=[blk,
                  pl.BlockSpec((SUBLANES, D), lambda b, t: (jnp.maximum((b * nt + t) * hb - 1, 0), 0)),
                  one,
                  pl.BlockSpec((1, D), lambda b, t: (0, 0)),
                  pl.BlockSpec((6, D), lambda b, t: (0, 0))] +
                 [pl.BlockSpec(w.shape, lambda b, t: (0, 0)) for w in (w1, a1, g1)],
        out_specs=[blk] * 3 + [pl.BlockSpec((tt, w.shape[1]), lambda b, t: (b * nt + t, 0)) for w in (w1, a1, g1)]
                  + [one],
        out_shape=[jax.ShapeDtypeStruct((out_rows, D), BF16)] * 3 +
                  [jax.ShapeDtypeStruct((out_rows, w.shape[1]), BF16) for w in (w1, a1, g1)] +
                  [jax.ShapeDtypeStruct((B, 1, D), F32)],
        compiler_params=_params("parallel", "arbitrary"),
        name="rwkv_mix",
    )(x, x, shift0, g.reshape(1, D).astype(F32), mu.astype(F32), w1, a1, g1)


def _rwkv_chunk_kernel(r_ref, k_ref, v_ref, lw_ref, la_ref, lg_ref, w2_ref, a2_ref, g2_ref, w0_ref, a0_ref,
                       kk_ref, ka_ref, rk_ref, lng_ref, lnb_ref, s0_ref, o_ref, so_ref, s_scr, *, hs, C, n_valid):
    c = pl.program_id(2)
    nc = pl.num_programs(2)
    R, L = r_ref.shape
    n_pairs = L // LANES

    @pl.when(c == 0)
    def _():
        z = jnp.zeros((hs, hs), F32)
        for p in range(n_pairs):
            s_scr[p] = jnp.concatenate([jnp.concatenate([s0_ref[0, 2 * p], z], axis=1),
                                        jnp.concatenate([z, s0_ref[0, 2 * p + 1]], axis=1)], axis=0)

    ti = lax.broadcasted_iota(jnp.int32, (C, C), 0)
    si = lax.broadcasted_iota(jnp.int32, (C, C), 1)
    tri = jnp.where(ti >= si, 1.0, 0.0).astype(BF16)
    lane = lax.broadcasted_iota(jnp.int32, (1, LANES), 1)
    m0 = lane < hs
    C2 = 2 * C
    ri = lax.broadcasted_iota(jnp.int32, (C2, C2), 0)
    ci = lax.broadcasted_iota(jnp.int32, (C2, C2), 1)
    same = (ri // C) == (ci // C)
    strict = same & ((ri % C) > (ci % C))
    lower = (ri % C) >= (ci % C)
    vi = lax.broadcasted_iota(jnp.int32, (LANES, LANES), 0) // hs
    vj = lax.broadcasted_iota(jnp.int32, (LANES, LANES), 1) // hs
    blockdiag = vi == vj
    n_steps = int(math.log2(C))
    wide = C2 % LANES == 0
    P = range(n_pairs)
    sls = [slice(p * LANES, (p + 1) * LANES) for p in P]
    inv = 1.0 / hs

    def stack_heads(x):
        return jnp.concatenate([jnp.where(m0, x, 0.0), jnp.where(m0, 0.0, x)], axis=0).astype(BF16)

    S = [s_scr[p] for p in P]
    for sc in range(R // C):
        rows = slice(sc * C, (sc + 1) * C)
        r = r_ref[rows, :]
        k = k_ref[rows, :]
        v = v_ref[rows, :]
        wp = w0_ref[...] + _dot(lw_ref[rows, :], w2_ref[...])
        logw = -jnp.exp(-jax.nn.softplus(-wp) - 0.5)
        if n_valid is not None:
            logw = jnp.where(lax.broadcasted_iota(jnp.int32, (C, 1), 0) + sc * C < n_valid, logw, 0.0)
        a = jax.nn.sigmoid(a0_ref[...] + _dot(la_ref[rows, :], a2_ref[...]))
        kk = k * kk_ref[...]
        kk = kk / jnp.maximum(jnp.sqrt(_group_sum_bcast(kk * kk, hs, terms=1)), 1e-12)
        k2 = k * (1.0 + (a - 1.0) * ka_ref[...])
        bm = kk * a
        cum = _dot_exact_rhs_lhs(tri, logw)
        e_in = jnp.exp(cum)
        e_out = jnp.exp(-cum)
        e_end = jnp.exp(cum[C - 1:C, :] - cum)
        g_end = jnp.exp(cum[C - 1:C, :])
        rt = r * e_in
        kkt = kk * jnp.exp(cum - logw)
        bh = bm * e_out
        kh = k2 * e_out
        bbar = bm * e_end
        kbar = k2 * e_end

        S16 = [s_.astype(BF16) for s_ in S]
        V16 = [v[:, sl].astype(BF16) for sl in sls]
        lhk = [stack_heads(kkt[:, sl]) for sl in sls]
        if wide:
            gbk = [_nt(lhk[p], jnp.concatenate([stack_heads(bh[:, sls[p]]), stack_heads(kh[:, sls[p]])], axis=0))
                   for p in P]
            nmat = [jnp.where(strict, -g_[:, :C2], 0.0) for g_ in gbk]
            auk = [jnp.where(strict, -g_[:, C2:], 0.0).astype(BF16) for g_ in gbk]
        else:
            nmat = [jnp.where(strict, -_nt(lhk[p], stack_heads(bh[:, sls[p]])), 0.0) for p in P]
            auk = [jnp.where(strict, -_nt(lhk[p], stack_heads(kh[:, sls[p]])), 0.0).astype(BF16) for p in P]
        sprod = [_nt(jnp.concatenate([lhk[p], rt[:, sls[p]].astype(BF16)], axis=0), S16[p]) for p in P]
        u = [_dot(auk[p], jnp.concatenate([V16[p], V16[p]], axis=0)) - sprod[p][:C2] for p in P]
        npow = [n_.astype(BF16) for n_ in nmat]
        for it in range(n_steps):
            last = it + 1 == n_steps
            if wide and not last:
                res = [_dot(npow[p], jnp.concatenate([u[p].astype(BF16), npow[p]], axis=1)) for p in P]
                u = [u[p] + res[p][:, :LANES] for p in P]
                npow = [res[p][:, LANES:].astype(BF16) for p in P]
            else:
                u = [u[p] + _dot(npow[p], u[p].astype(BF16)) for p in P]
                if not last:
                    npow = [_dot(npow[p], npow[p]).astype(BF16) for p in P]
        uv = [jnp.concatenate([jnp.where(m0, u[p][:C], u[p][C:]).astype(BF16), V16[p]], axis=0) for p in P]
        ar = [jnp.where(lower, _nt(stack_heads(rt[:, sls[p]]),
                                   jnp.concatenate([bh[:, sls[p]], kh[:, sls[p]]], axis=0).astype(BF16)),
                        0.0).astype(BF16) for p in P]
        tmat = [_dot(ar[p], uv[p]) for p in P]
        ys = [sprod[p][C2:] + jnp.where(m0, tmat[p][:C], tmat[p][C:]) for p in P]
        upd = [_tn(uv[p], jnp.concatenate([bbar[:, sls[p]], kbar[:, sls[p]]], axis=0).astype(BF16)) for p in P]
        S = [S[p] * g_end[:, sls[p]] + jnp.where(blockdiag, upd[p], 0.0) for p in P]

        y = ys[0] if n_pairs == 1 else jnp.concatenate(ys, axis=-1)
        mean = _group_sum_bcast(y, hs, terms=1) * inv
        yc = y - mean
        var = _group_sum_bcast(yc * yc, hs, terms=1) * inv
        yn = yc * lax.rsqrt(var + RW_GN_EPS) * lng_ref[...] + lnb_ref[...]
        bonus = _group_sum_bcast(r * k2 * rk_ref[...], hs, terms=1) * v
        o_ref[rows, :] = ((yn + bonus) * _dot(lg_ref[rows, :], g2_ref[...])).astype(o_ref.dtype)

    for p in P:
        s_scr[p] = S[p]

    @pl.when(c == nc - 1)
    def _():
        for p in P:
            so_ref[0, 2 * p] = S[p][:hs, :hs]
            so_ref[0, 2 * p + 1] = S[p][hs:, hs:]


def _dot_exact_rhs_lhs(m_bf16, x):
    hi = x.astype(BF16)
    lo = (x - hi.astype(F32)).astype(BF16)
    return _dot(m_bf16, hi) + _dot(m_bf16, lo)


def rwkv_chunked(r, k, v, lw, la, lg, w2, a2, g2, w0, a0, k_k, k_a, r_k, ln_g, ln_b, s0, *, B, T, hs, chunk,
                 chunks_per_step=1, lanes_per_step=1024, out_rows=None, n_valid=None):
    D = r.shape[1]
    L = min(lanes_per_step, D)
    npg = L // LANES
    rows = chunk * chunks_per_step
    nc = T // rows
    out_rows = B * T if out_rows is None else out_rows
    seq = pl.BlockSpec((rows, L), lambda b, hg, c: (b * nc + c, hg))
    low = lambda a_: pl.BlockSpec((rows, a_.shape[1]), lambda b, hg, c: (b * nc + c, 0))
    wgt = lambda w_: pl.BlockSpec((w_.shape[0], L), lambda b, hg, c: (0, hg))
    par = pl.BlockSpec((1, L), lambda b, hg, c: (0, hg))
    st = pl.BlockSpec((1, 2 * npg, hs, hs), lambda b, hg, c: (b, hg, 0, 0))
    row = lambda x: x.reshape(1, D).astype(F32)
    return pl.pallas_call(
        functools.partial(_rwkv_chunk_kernel, hs=hs, C=chunk, n_valid=n_valid),
        grid=(B, D // L, nc),
        in_specs=[seq] * 3 + [low(lw), low(la), low(lg), wgt(w2), wgt(a2), wgt(g2)] + [par] * 7 + [st],
        out_specs=[seq, st],
        out_shape=[jax.ShapeDtypeStruct((out_rows, D), BF16),
                   jax.ShapeDtypeStruct(s0.shape, F32)],
        scratch_shapes=[pltpu.VMEM((npg, LANES, LANES), F32)],
        compiler_params=_params("parallel", "parallel", "arbitrary"),
        name="rwkv_chunked",
    )(r, k, v, lw, la, lg, w2, a2, g2, row(w0), row(a0), row(k_k), row(k_a), row(r_k), row(ln_g), row(ln_b), s0)


def _pad_rows(x, front, back):
    return jnp.pad(x, ((0, 0), (front, back), (0, 0)))


def kernel(x_prompt, x_sample, mem_prompt, state_lru_conv, state_lru_h, cache_swa_k, cache_swa_v,
           state_rwkv_shift, state_rwkv_wkv, cache_mem_k, cache_mem_v, state_ffn_conv,
           a_norm_g, a_w_in, a_conv_w, a_conv_b, a_gate_a_w, a_gate_a_b, a_gate_x_w, a_gate_x_b,
           a_lambda, b_q_norm_g, b_k_norm_g, b_sink, a_w_out,
           c_norm_g, c_mu, c_w_r, c_w_k, c_w_v, c_w_o, c_w0, c_w1, c_w2, c_a0, c_a1, c_a2,
           c_g1, c_g2, c_k_k, c_k_a, c_r_k, c_ln_g, c_ln_b,
           m_norm_g, m_mem_norm_g, m_w_q, m_w_kv, m_q_norm_g, m_k_norm_g, m_w_o,
           f_norm_g, f_w_in, f_conv_w, f_conv_b, f_w_out):
    D = x_prompt.shape[-1]
    depth = m_norm_g.shape[0]
    W = a_conv_w.shape[-1]
    KA = a_conv_w.shape[1]
    hd = b_q_norm_g.shape[-1]
    n_q = b_sink.shape[-1]
    n_kv = cache_swa_k.shape[3]
    win = cache_swa_k.shape[2]
    QW, KW = n_q * hd, n_kv * hd
    hs = c_r_k.shape[-1]
    n_rw = c_r_k.shape[1]
    m_heads, m_hd = cache_mem_k.shape[3], cache_mem_k.shape[4]
    MW = m_heads * m_hd
    FF = f_conv_w.shape[-1]
    KF = f_conv_w.shape[1]
    bf = lambda w: w.astype(BF16)
    wcache = {}

    Bp, S = x_prompt.shape[:2]
    Bs, Ts = x_sample.shape[:2]
    Mp, Ms = Bp * S, Bs * Ts
    Mt = Mp + Ms
    tma = _row_tile(Mt, 1280)
    n_a = a_norm_g.shape[0]
    n_c = c_norm_g.shape[0]
    ML = mem_prompt.shape[1]
    PAD8, PAD16 = SUBLANES, 2 * SUBLANES

    def mm(xin, w, layer, key, **kw):
        return dense(xin, w, layer, wcache, (key, layer), tm=tma, **kw)

    def seq_tail(arr, n, cols=slice(None)):
        return jnp.stack([arr[(b + 1) * S - n:(b + 1) * S, cols] for b in range(Bp)])

    def put_sample(buf, rows_s):
        return lax.dynamic_update_slice(buf, rows_s.astype(buf.dtype), (Mp, 0))

    mem_flat = mem_prompt.reshape(Bp * ML, D)
    mks, mvs = [], []
    for l in range(depth):
        mn = rmsnorm_rows(mem_flat, m_mem_norm_g[l])
        kv = matmul(mn, m_w_kv, layer=l)
        mk = headnorm(kv, 0, MW, m_k_norm_g[l], m_hd, F32)
        mks.append(mk.reshape(Bp, ML, m_heads, m_hd))
        mvs.append(kv[:, MW:].reshape(Bp, ML, m_heads, m_hd))
    p_mem_k = jnp.stack(mks)
    p_mem_v = jnp.stack(mvs)

    xp2, xs2 = x_prompt.reshape(Mp, D), x_sample.reshape(Ms, D)
    xf = None

    def first_norm(g):
        return put_sample(rmsnorm_rows(xp2, g, out_rows=Mt), rmsnorm_rows(xs2, g))

    def first_residual(xin, w, layer, key):
        y = dense(xin, w, layer, wcache, (key, layer), rows=Mp, out_rows=Mt, res=xp2)
        return matmul(xin, wcache[(key, layer)], row0=Mp, rows=Ms, res=xs2, res_row0=0, into=y)

    po ={k_: [] for k_ in ("lc", "lh", "sk", "sv", "rs", "rw", "fc")}
    so = {k_: [] for k_ in ("lc", "lh", "sk", "sv", "rs", "rw", "fc")}
    ia = ic = 0
    y_p = y_s = None
    for l in range(depth):
        if l % 2 == 0:
            i = ia
            ia += 1
            assert W == QW
            h = first_norm(a_norm_g[i]) if xf is None else rmsnorm_rows(xf, a_norm_g[i])
            zz = mm(h, a_w_in, i, "a_w_in_rg", n=2 * W)
            qkv = mm(h, a_w_in, i, "a_w_in_qkv", n0=2 * W)
            qn = headnorm(qkv, 0, QW, b_q_norm_g[i], hd, BF16)
            kn = headnorm(qkv, QW // KW, KW, b_k_norm_g[i], hd, F32)
            vv = qkv[:, QW + KW:]
            lru_w = (a_conv_w[i], a_conv_b[i], a_gate_a_w[i], a_gate_a_b[i], a_gate_x_w[i], a_gate_x_b[i],
                     a_lambda[i])
            mix, hl_p = lru_mixer(zz, 0, zz, 1, jnp.zeros((Bp, 1, W), F32), *lru_w, B=Bp, T=S, n_pad=0, tt=256,
                                  out_rows=Mt, out_cols=W + QW)
            mix = swa_attention(qn, bf(kn), bf(vv), b_sink[i], B=Bp, nq=S // win, nkb=S // win, j0=0, win=win,
                                hd=hd, into=mix, out_cb=W // QW)
            po["lc"].append(seq_tail(zz, KA - 1, slice(0, W)))
            po["lh"].append(hl_p.reshape(Bp, W))
            po["sk"].append(seq_tail(kn, win).reshape(Bp, win, n_kv, hd))
            po["sv"].append(seq_tail(vv, win).reshape(Bp, win, n_kv, hd))
            zz_s = zz[Mp:].reshape(Bs, Ts, 2 * W)
            n_pad = PAD8 - Ts
            xr_hist = jnp.concatenate([state_lru_conv[i].astype(F32), zz_s[:, :, :W]], axis=1)
            xr_p = _pad_rows(xr_hist, PAD8 - xr_hist.shape[1], 0).reshape(Bs * PAD8, W)
            yg_p = _pad_rows(zz_s[:, :, W:], n_pad, 0).reshape(Bs * PAD8, W)
            oa_s, hl_s = lru_mixer(xr_p, 0, yg_p, 0, state_lru_h[i].reshape(Bs, 1, W).astype(F32), *lru_w,
                                   B=Bs, T=PAD8, n_pad=n_pad, tt=PAD8)
            oa_s = oa_s.reshape(Bs, PAD8, W)[:, n_pad:].reshape(Ms, W)
            kc = cache_swa_k[i].reshape(Bs, win, KW)
            vc = cache_swa_v[i].reshape(Bs, win, KW)
            k_all = jnp.concatenate([kc, kn[Mp:].reshape(Bs, Ts, KW)], axis=1)
            v_all = jnp.concatenate([vc, vv[Mp:].reshape(Bs, Ts, KW)], axis=1)
            q_s = _pad_rows(qn[Mp:].reshape(Bs, Ts, QW), 0, PAD16 - Ts).reshape(Bs * PAD16, QW)
            o_s = swa_attention(q_s, bf(_pad_rows(k_all, 0, win - Ts)).reshape(Bs * 2 * win, KW),
                                bf(_pad_rows(v_all, 0, win - Ts)).reshape(Bs * 2 * win, KW), b_sink[i],
                                B=Bs, nq=1, nkb=2, j0=1, win=win, hd=hd, tq=PAD16)
            o_s = o_s.reshape(Bs, PAD16, QW)[:, :Ts].reshape(Ms, QW)
            mix = put_sample(mix, jnp.concatenate([oa_s, o_s], axis=-1))
            so["lc"].append(xr_hist[:, Ts:])
            so["lh"].append(hl_s.reshape(Bs, W))
            so["sk"].append(k_all[:, Ts:].reshape(Bs, win, n_kv, hd))
            so["sv"].append(v_all[:, Ts:].reshape(Bs, win, n_kv, hd))
            xf = (first_residual(mix, a_w_out, i, "a_w_out") if xf is None
                  else mm(mix, a_w_out, i, "a_w_out", res=xf))
        else:
            i = ic
            ic += 1
            gl = c_g1[i].shape[1]
            glp = -(-gl // LANES) * LANES
            lora1 = (bf(c_w1[i]), bf(c_a1[i]), bf(jnp.pad(c_g1[i], ((0, 0), (0, glp - gl)))))
            lora2 = (bf(c_w2[i]), bf(c_a2[i]), bf(jnp.pad(c_g2[i], ((0, glp - gl), (0, 0)))), c_w0[i], c_a0[i])
            mixes = rwkv_mix(xf, jnp.zeros((Bp, 1, D), F32), c_norm_g[i], c_mu[i], *lora1, B=Bp, T=S, n_valid=S,
                             out_rows=Mt)
            x_s = _pad_rows(xf[Mp:].reshape(Bs, Ts, D), 0, PAD8 - Ts).reshape(Bs * PAD8, D)
            mixes_s = rwkv_mix(x_s, state_rwkv_shift[i].reshape(Bs, 1, D).astype(F32), c_norm_g[i], c_mu[i], *lora1,
                               B=Bs, T=PAD8, n_valid=Ts, tt=PAD8)
            take = lambda m: m.reshape(Bs, PAD8, m.shape[-1])[:, :Ts].reshape(Ms, m.shape[-1])
            xr, xk, xv, lw, la, lg = [put_sample(mixes[j], take(mixes_s[j])) for j in range(6)]
            po["rs"].append(mixes[6].reshape(Bp, D))
            so["rs"].append(mixes_s[6].reshape(Bs, D))
            r = mm(xr, c_w_r, i, "c_w_r")
            k = mm(xk, c_w_k, i, "c_w_k")
            v = mm(xv, c_w_v, i, "c_w_v")
            rw_par = (c_k_k[i], c_k_a[i], c_r_k[i], c_ln_g[i], c_ln_b[i])
            yo, s_end = rwkv_chunked(r, k, v, lw, la, lg, *lora2, *rw_par, jnp.zeros((Bp, n_rw, hs, hs), F32),
                                     B=Bp, T=S, hs=hs, chunk=64, chunks_per_step=4, out_rows=Mt)
            po["rw"].append(s_end)
            cs = PAD16
            sq = lambda t: jnp.pad(t[Mp:].reshape(Bs, Ts, t.shape[-1]),
                                   ((0, 0), (0, cs - Ts), (0, 0))).reshape(Bs * cs, t.shape[-1])
            yo_s, s_end_s = rwkv_chunked(sq(r), sq(k), sq(v), sq(lw), sq(la), sq(lg), *lora2, *rw_par,
                                         state_rwkv_wkv[i].astype(F32), B=Bs, T=cs, hs=hs, chunk=cs,
                                         lanes_per_step=2048, n_valid=Ts)
            yo = put_sample(yo, yo_s.reshape(Bs, cs, D)[:, :Ts].reshape(Ms, D))
            so["rw"].append(s_end_s)
            xf = mm(yo, c_w_o, i, "c_w_o", res=xf)
        q = norm_proj(xf, m_norm_g[l], m_w_q, l)
        om = mem_attention(q, m_q_norm_g[l], p_mem_k[l].reshape(Bp, ML, MW), p_mem_v[l].reshape(Bp, ML, MW),
                           B=Bp, T=S, n_heads=m_heads, out_rows=Mt)
        q_s = _pad_rows(q[Mp:].reshape(Bs, Ts, MW), 0, PAD8 - Ts).reshape(Bs * PAD8, MW)
        om_s = mem_attention(q_s, m_q_norm_g[l], cache_mem_k[l].reshape(Bs, ML, MW),
                             cache_mem_v[l].reshape(Bs, ML, MW), B=Bs, T=PAD8, n_heads=m_heads, tt=PAD8)
        om = put_sample(om, om_s.reshape(Bs, PAD8, MW)[:, :Ts].reshape(Ms, MW))
        xf, hf = proj_res_norm(om, m_w_o, l, xf, f_norm_g[l])
        act, tail = ffn_in_fused(hf, f_w_in, l, f_conv_w[l], f_conv_b[l], wcache, ("f_w_in", l), seq_len=S, rows=Mp)
        po["fc"].append(tail[:, SUBLANES - (KF - 1):, :])
        wg, wu = wcache[("f_w_in", l)]
        hf_s = hf[Mp:]
        gate_s = matmul(hf_s, wg, tn=1024).reshape(Bs, Ts, FF)
        up_s = matmul(hf_s, wu, tn=1024).reshape(Bs, Ts, FF)
        g_hist = jnp.concatenate([state_ffn_conv[l].astype(F32), gate_s], axis=1)
        g_p = _pad_rows(g_hist, PAD8 - g_hist.shape[1], 0)
        up_p = _pad_rows(up_s, PAD8 - Ts, 0)
        act_s = ffn_act(g_p, 0, up_p, 0, f_conv_w[l], f_conv_b[l], tt=PAD8, tc=FF)[:, PAD8 - Ts:]
        act = put_sample(act, act_s.reshape(Ms, FF))
        so["fc"].append(g_hist[:, Ts:])
        if l + 1 < depth:
            xf = mm(act, f_w_out, l, "f_w_out", res=xf, tn=1024, tk=2048)
        else:
            y_p = dense(act, f_w_out, l, wcache, ("f_w_out", l), rows=Mp, out_rows=Mp, res=xf, tn=1024, tk=2048)
            y_s = matmul(act, wcache[("f_w_out", l)], row0=Mp, rows=Ms, out_rows=Ms, out_row0=0, res=xf,
                         tn=1024, tk=2048)
    st = jnp.stack
    return (y_p.reshape(Bp, S, D), y_s.reshape(Bs, Ts, D),
            st(po["lc"]), st(po["lh"]), st(po["sk"]), st(po["sv"]), st(po["rs"]), st(po["rw"]),
            p_mem_k, p_mem_v, st(po["fc"]),
            st(so["lc"]), st(so["lh"]), st(so["sk"]), st(so["sv"]), st(so["rs"]), st(so["rw"]), st(so["fc"]))
```

```python
import functools
import math

import jax
import jax.numpy as jnp
from jax import lax
from jax.experimental import pallas as pl
from jax.experimental.pallas import tpu as pltpu

F32 = jnp.float32
BF16 = jnp.bfloat16

NORM_EPS = 1e-6
RW_GN_EPS = 64e-5
LRU_C = 8.0
LANES = 128
SUBLANES = 8
VMEM_LIMIT_BYTES = 56 * 1024 * 1024


def _params(*sem):
    return pltpu.CompilerParams(dimension_semantics=sem, vmem_limit_bytes=VMEM_LIMIT_BYTES)


def _nt(a, b):
    return lax.dot_general(a, b, (((1,), (1,)), ((), ())), preferred_element_type=F32)


def _tn(a, b):
    return lax.dot_general(a, b, (((0,), (0,)), ((), ())), preferred_element_type=F32)


def _dot(a, b):
    return jnp.dot(a, b, preferred_element_type=F32)


def _group_sum_bcast(x, width, terms=2):
    m, L = x.shape
    n = L // LANES
    if width == LANES:
        parts = []
        for c in range(n):
            s = jnp.sum(x[:, c * LANES:(c + 1) * LANES], axis=-1, keepdims=True)
            parts.append(jnp.broadcast_to(s, (m, LANES)))
        return parts[0] if n == 1 else jnp.concatenate(parts, axis=-1)
    li = lax.broadcasted_iota(jnp.int32, (LANES, LANES), 0) // width
    lj = lax.broadcasted_iota(jnp.int32, (LANES, LANES), 1) // width
    e = jnp.where(li == lj, 1.0, 0.0).astype(BF16)
    xs = x if n == 1 else jnp.concatenate([x[:, c * LANES:(c + 1) * LANES] for c in range(n)], axis=0)
    hi = xs.astype(BF16)
    out = _dot(hi, e)
    if terms > 1:
        out = out + _dot((xs - hi.astype(F32)).astype(BF16), e)
    return out if n == 1 else jnp.concatenate([out[c * m:(c + 1) * m] for c in range(n)], axis=-1)


def _shift_rows(x, prev8, s):
    rolled = pltpu.roll(x, s, 0)
    top = jnp.where(lax.broadcasted_iota(jnp.int32, (SUBLANES, 1), 0) < s,
                    pltpu.roll(prev8, s, 0), rolled[0:SUBLANES])
    if x.shape[0] == SUBLANES:
        return top
    return jnp.concatenate([top, rolled[SUBLANES:]], axis=0)


def _rmsnorm_kernel(x_ref, g_ref, o_ref):
    x = x_ref[...]
    ms = jnp.mean(x * x, axis=-1, keepdims=True)
    o_ref[...] = (x * lax.rsqrt(ms + NORM_EPS) * g_ref[...]).astype(o_ref.dtype)


def _row_tile(m, target):
    best = None
    for d in range(16, min(m, target) + 1, 16):
        if m % d == 0:
            best = d
    return best or m


def rmsnorm_rows(x, g, out_dtype=BF16, tm=512, out_rows=None):
    M, D = x.shape
    tm = _row_tile(M, tm)
    return pl.pallas_call(
        _rmsnorm_kernel,
        grid=(M // tm,),
        in_specs=[pl.BlockSpec((tm, D), lambda i: (i, 0)), pl.BlockSpec((1, D), lambda i: (0, 0))],
        out_specs=pl.BlockSpec((tm, D), lambda i: (i, 0)),
        out_shape=jax.ShapeDtypeStruct((M if out_rows is None else out_rows, D), out_dtype),
        compiler_params=_params("parallel"),
        name="rmsnorm_rows",
    )(x, g.reshape(1, D))


def _mm_kernel(*refs, nk, has_bias, has_res, has_into, emit_w, in_place, act):
    x_ref, w_ref = refs[0], refs[1]
    pos = 2
    b_ref = r_ref = wb_ref = acc_ref = None
    if has_bias:
        b_ref = refs[pos]
        pos += 1
    if has_res:
        r_ref = refs[pos]
        pos += 1
    if has_into:
        pos += 1
    o_ref = refs[pos]
    pos += 1
    if emit_w:
        wb_ref = refs[pos]
        pos += 1
    if nk > 1 and not in_place:
        acc_ref = refs[pos]

    def weights(cs=slice(None)):
        w16 = w_ref[:, cs].astype(BF16)
        if emit_w:
            wb_ref[:, cs] = w16
        return w16

    def epilogue(y):
        if has_bias:
            y = y + b_ref[...]
        if act == "tanh":
            y = jnp.tanh(y)
        elif act == "sigmoid":
            y = jax.nn.sigmoid(y)
        if has_res:
            y = y + r_ref[...]
        o_ref[...] = y.astype(o_ref.dtype)

    if nk > 1 and in_place:
        k = pl.program_id(2)
        tn = o_ref.shape[1]
        cw = min(tn, 2 * LANES)

        def sweep(first):
            x = x_ref[...]
            for c in range(tn // cw):
                cs = slice(c * cw, (c + 1) * cw)
                part = _dot(x, weights(cs))
                if not first:
                    o_ref[:, cs] += part
                elif has_res:
                    o_ref[:, cs] = part + r_ref[:, cs]
                else:
                    o_ref[:, cs] = part

        pl.when(k == 0)(functools.partial(sweep, True))
        pl.when(k > 0)(functools.partial(sweep, False))
        return

    part = _dot(x_ref[...], weights())
    if nk == 1:
        epilogue(part)
        return
    k = pl.program_id(2)

    @pl.when(k == 0)
    def _():
        acc_ref[...] = part

    @pl.when(k > 0)
    def _():
        acc_ref[...] += part

    @pl.when(k == nk - 1)
    def _():
        epilogue(acc_ref[...])


def _pick(n, prefs):
    for p in prefs:
        if n % p == 0:
            return p
    return n


def matmul(x, w, *, layer=None, n0=0, n=None, bias=None, res=None, act=None, out_dtype=F32,
           tm=1024, tn=512, tk=4096, row0=0, rows=None, into=None, emit_w=False, out_rows=None, out_row0=None,
           res_row0=None):
    M, K = x.shape
    N = w.shape[-1] - n0 if n is None else n
    rows = M - row0 if rows is None else rows
    tm = min(tm, rows)
    out_rows = M if out_rows is None else out_rows
    o0 = (row0 if out_row0 is None else out_row0) // tm
    tn = _pick(math.gcd(N, n0) if n0 else N, (tn, 512, 256, 128))
    tk = _pick(K, (tk, 2048, 1024, 512))
    nk = K // tk
    j0 = n0 // tn
    i0 = row0 // tm
    assert row0 % tm == 0 and rows % tm == 0 and (not emit_w or rows == tm)
    in_specs = [pl.BlockSpec((tm, tk), lambda i, j, k: (i + i0, k)),
                pl.BlockSpec((tk, tn), lambda i, j, k: (k, j + j0)) if layer is None else
                pl.BlockSpec((None, tk, tn), lambda i, j, k: (layer, k, j + j0))]
    args = [x, w]
    if bias is not None:
        in_specs.append(pl.BlockSpec((1, tn), lambda i, j, k: (0, j)))
        args.append(bias.reshape(1, N).astype(F32))
    if res is not None:
        r0 = i0 if res_row0 is None else res_row0 // tm
        in_specs.append(pl.BlockSpec((tm, tn), lambda i, j, k: (i + r0, j)))
        args.append(res)
    aliases = {}
    if into is not None:
        aliases = {len(args): 0}
        in_specs.append(pl.BlockSpec(memory_space=pl.ANY))
        args.append(into)
    in_place = bias is None and act is None and out_dtype == F32
    kern = functools.partial(_mm_kernel, nk=nk, has_bias=bias is not None, has_res=res is not None,
                             has_into=into is not None, emit_w=emit_w, in_place=in_place, act=act)
    out_specs = [pl.BlockSpec((tm, tn), lambda i, j, k: (i + o0, j))]
    out_shape = [jax.ShapeDtypeStruct((out_rows, N), out_dtype)]
    if emit_w:
        out_specs.append(pl.BlockSpec((tk, tn), lambda i, j, k: (k, j)))
        out_shape.append(jax.ShapeDtypeStruct((K, N), BF16))
    outs = pl.pallas_call(
        kern,
        grid=(rows // tm, N // tn, nk),
        in_specs=in_specs,
        out_specs=out_specs,
        out_shape=out_shape,
        input_output_aliases=aliases,
        scratch_shapes=[pltpu.VMEM((tm, tn), F32)] if (nk > 1 and not in_place) else [],
        compiler_params=_params("parallel", "parallel", "arbitrary"),
        name="matmul",
    )(*args)
    return outs if emit_w else outs[0]


def dense(x, w, layer, wcache, key, **kw):
    rows = kw.pop("rows", x.shape[0])
    tm = min(kw.get("tm", 1024), rows)
    first, wb = matmul(x, w, layer=layer, rows=tm, emit_w=True, **kw)
    wcache[key] = wb
    if tm == rows:
        return first
    kw = {a: b for a, b in kw.items() if a not in ("n0", "n")}
    return matmul(x, wb, row0=tm, rows=rows - tm, into=first, **kw)


def _proj_res_norm_kernel(x_ref, w_ref, r_ref, g_ref, y_ref, h_ref, w_scr):
    @pl.when(pl.program_id(0) == 0)
    def _():
        w_scr[...] = w_ref[...].astype(BF16)

    y = _dot(x_ref[...], w_scr[...]) + r_ref[...]
    y_ref[...] = y
    ms = jnp.mean(y * y, axis=-1, keepdims=True)
    h_ref[...] = (y * lax.rsqrt(ms + NORM_EPS) * g_ref[...]).astype(h_ref.dtype)


def proj_res_norm(x, w, layer, res, g, tm=320):
    M, K = x.shape
    N = w.shape[-1]
    tm = _row_tile(M, tm)
    row_blk = lambda c: pl.BlockSpec((tm, c), lambda i: (i, 0))
    return pl.pallas_call(
        _proj_res_norm_kernel,
        grid=(M // tm,),
        in_specs=[row_blk(K), pl.BlockSpec((None, K, N), lambda i: (layer, 0, 0)), row_blk(N),
                  pl.BlockSpec((1, N), lambda i: (0, 0))],
        out_specs=[row_blk(N), row_blk(N)],
        out_shape=[jax.ShapeDtypeStruct((M, N), F32), jax.ShapeDtypeStruct((M, N), BF16)],
        scratch_shapes=[pltpu.VMEM((K, N), BF16)],
        compiler_params=_params("arbitrary"),
        name="proj_res_norm",
    )(x, w, res, g.reshape(1, N).astype(F32))


def _norm_proj_kernel(x_ref, g_ref, w_ref, y_ref, w_scr):
    @pl.when(pl.program_id(0) == 0)
    def _():
        w_scr[...] = w_ref[...].astype(BF16)

    x = x_ref[...]
    ms = jnp.mean(x * x, axis=-1, keepdims=True)
    h = (x * lax.rsqrt(ms + NORM_EPS) * g_ref[...]).astype(BF16)
    y_ref[...] = _dot(h, w_scr[...])


def norm_proj(x, g, w, layer, tm=320):
    M, D = x.shape
    N = w.shape[-1]
    tm = _row_tile(M, tm)
    return pl.pallas_call(
        _norm_proj_kernel,
        grid=(M // tm,),
        in_specs=[pl.BlockSpec((tm, D), lambda i: (i, 0)), pl.BlockSpec((1, D), lambda i: (0, 0)),
                  pl.BlockSpec((None, D, N), lambda i: (layer, 0, 0))],
        out_specs=pl.BlockSpec((tm, N), lambda i: (i, 0)),
        out_shape=jax.ShapeDtypeStruct((M, N), F32),
        scratch_shapes=[pltpu.VMEM((D, N), BF16)],
        compiler_params=_params("arbitrary"),
        name="norm_proj",
    )(x, g.reshape(1, D).astype(F32), w)


def _headnorm_kernel(x_ref, g_ref, o_ref, *, hd):
    x = x_ref[...]
    ms = _group_sum_bcast(x * x, hd) * (1.0 / hd)
    o_ref[...] = (x * lax.rsqrt(ms + NORM_EPS) * g_ref[...]).astype(o_ref.dtype)


def headnorm(x, col_block, width, g, hd, out_dtype, tm=512):
    M = x.shape[0]
    tm = _row_tile(M, tm)
    g_row = jnp.tile(g.astype(F32), width // hd).reshape(1, width)
    return pl.pallas_call(
        functools.partial(_headnorm_kernel, hd=hd),
        grid=(M // tm,),
        in_specs=[pl.BlockSpec((tm, width), lambda i: (i, col_block)),
                  pl.BlockSpec((1, width), lambda i: (0, 0))],
        out_specs=pl.BlockSpec((tm, width), lambda i: (i, 0)),
        out_shape=jax.ShapeDtypeStruct((M, width), out_dtype),
        compiler_params=_params("parallel"),
        name="headnorm",
    )(x, g_row)


def _lru_kernel(xr_ref, halo_ref, yg_ref, h0_ref, cw_ref, cb_ref, gaw_ref, gab_ref, gxw_ref, gxb_ref,
                lam_ref, o_ref, hl_ref, a_scr, u_scr, h_scr, *, n_pad, n_blocks, bw, scan_w):
    t = pl.program_id(1)
    tt, W = a_scr.shape

    @pl.when(t == 0)
    def _():
        h_scr[...] = h0_ref[0]

    x = xr_ref[...]
    prev = halo_ref[...] * jnp.where(t > 0, 1.0, 0.0)
    xc = cb_ref[...] + cw_ref[3:4, :] * x
    for s in (1, 2, 3):
        xc = xc + cw_ref[3 - s:4 - s, :] * _shift_rows(x, prev, s)

    nsp = -LRU_C * jax.nn.softplus(-lam_ref[...])
    if n_pad:
        live = (lax.broadcasted_iota(jnp.int32, (tt, 1), 0) >= n_pad) | (t > 0)
    for n in range(n_blocks):
        sl = slice(n * bw, (n + 1) * bw)
        xb = xc[:, sl]
        xb16 = xb.astype(BF16)
        r = jax.nn.sigmoid(_dot(xb16, gaw_ref[n]) + gab_ref[:, sl])
        i = jax.nn.sigmoid(_dot(xb16, gxw_ref[n]) + gxb_ref[:, sl])
        log_a = r * nsp[:, sl]
        a = jnp.exp(log_a)
        u = jnp.sqrt(jnp.maximum(-jnp.tanh(log_a) * (a * a + 1.0), 0.0)) * (i * xb)
        if n_pad:
            a = jnp.where(live, a, 1.0)
            u = jnp.where(live, u, 0.0)
        a_scr[:, sl] = a
        u_scr[:, sl] = u

    row8 = lax.broadcasted_iota(jnp.int32, (SUBLANES, 1), 0)
    for c in range(W // scan_w):
        cs = slice(c * scan_w, (c + 1) * scan_w)

        def body(gi, h, cs=cs):
            r0 = pl.multiple_of(gi * SUBLANES, SUBLANES)
            A = a_scr[pl.ds(r0, SUBLANES), cs]
            U = u_scr[pl.ds(r0, SUBLANES), cs]
            for s in (1, 2, 4):
                As = pltpu.roll(A, s, 0)
                Us = pltpu.roll(U, s, 0)
                m = row8 >= s
                U = jnp.where(m, A * Us + U, U)
                A = jnp.where(m, A * As, A)
            H = A * h + U
            u_scr[pl.ds(r0, SUBLANES), cs] = H
            return H[SUBLANES - 1:SUBLANES, :]

        h_end = lax.fori_loop(0, tt // SUBLANES, body, h_scr[:, cs])
        h_scr[:, cs] = h_end

    o_ref[...] = (u_scr[...] * jax.nn.gelu(yg_ref[...], approximate=True)).astype(o_ref.dtype)
    hl_ref[0] = h_scr[...]


def lru_mixer(xr, xr_cb, yg, yg_cb, h0, conv_w, conv_b, ga_w, ga_b, gx_w, gx_b, lam, *, B, T, n_pad, tt,
              out_rows=None, out_cols=None):
    W = conv_w.shape[1]
    nb, bw = ga_w.shape[0], ga_w.shape[1]
    tt = min(tt, T)
    nt = T // tt
    hb = tt // SUBLANES
    out_rows = B * T if out_rows is None else out_rows
    out_cols = W if out_cols is None else out_cols
    row = lambda v: v.reshape(1, W).astype(F32)
    kern = functools.partial(_lru_kernel, n_pad=n_pad, n_blocks=nb, bw=bw, scan_w=min(W, 512))
    return pl.pallas_call(
        kern,
        grid=(B, nt),
        in_specs=[
            pl.BlockSpec((tt, W), lambda b, t: (b * nt + t, xr_cb)),
            pl.BlockSpec((SUBLANES, W), lambda b, t: (jnp.maximum((b * nt + t) * hb - 1, 0), xr_cb)),
            pl.BlockSpec((tt, W), lambda b, t: (b * nt + t, yg_cb)),
            pl.BlockSpec((1, 1, W), lambda b, t: (b, 0, 0)),
            pl.BlockSpec((4, W), lambda b, t: (0, 0)),
            pl.BlockSpec((1, W), lambda b, t: (0, 0)),
            pl.BlockSpec((nb, bw, bw), lambda b, t: (0, 0, 0)),
            pl.BlockSpec((1, W), lambda b, t: (0, 0)),
            pl.BlockSpec((nb, bw, bw), lambda b, t: (0, 0, 0)),
            pl.BlockSpec((1, W), lambda b, t: (0, 0)),
            pl.BlockSpec((1, W), lambda b, t: (0, 0)),
        ],
        out_specs=[pl.BlockSpec((tt, W), lambda b, t: (b * nt + t, 0)),
                   pl.BlockSpec((1, 1, W), lambda b, t: (b, 0, 0))],
        out_shape=[jax.ShapeDtypeStruct((out_rows, out_cols), BF16), jax.ShapeDtypeStruct((B, 1, W), F32)],
        scratch_shapes=[pltpu.VMEM((tt, W), F32), pltpu.VMEM((tt, W), F32), pltpu.VMEM((1, W), F32)],
        compiler_params=_params("parallel", "arbitrary"),
        name="lru_mixer",
    )(xr, xr, yg, h0, conv_w.astype(F32), row(conv_b), ga_w.astype(BF16), row(ga_b),
      gx_w.astype(BF16), row(gx_b), row(lam))


def _swa_kernel(sink_ref, q_ref, kp_ref, kc_ref, vp_ref, vc_ref, *rest, j0, n_kv, group, hd, win):
    o_ref = rest[-1]
    j = pl.program_id(1) + j0
    q = q_ref[...]
    k2 = jnp.concatenate([kp_ref[...], kc_ref[...]], axis=0)
    v2 = jnp.concatenate([vp_ref[...], vc_ref[...]], axis=0)
    tq = q.shape[0]
    qi = lax.broadcasted_iota(jnp.int32, (tq, 2 * win), 0)
    kj = lax.broadcasted_iota(jnp.int32, (tq, 2 * win), 1)
    dist = qi + win - kj
    mask = (dist >= 0) & (dist < win) & ((j > 0) | (kj >= win))
    scale = hd ** -0.5
    for kvh in range(n_kv):
        kh = k2[:, kvh * hd:(kvh + 1) * hd]
        vh = v2[:, kvh * hd:(kvh + 1) * hd]
        hs_ = [kvh * group + g for g in range(group)]
        s = [jnp.where(mask, _nt(q[:, h * hd:(h + 1) * hd], kh) * scale, -jnp.inf) for h in hs_]
        m = [jnp.maximum(jnp.max(s[g], axis=-1, keepdims=True), sink_ref[hs_[g]]) for g in range(group)]
        p = [jnp.exp(s[g] - m[g]) for g in range(group)]
        den = [jnp.sum(p[g], axis=-1, keepdims=True) + jnp.exp(sink_ref[hs_[g]] - m[g]) for g in range(group)]
        outs = [_dot(p[g].astype(BF16), vh) / den[g] for g in range(group)]
        o_ref[:, hs_[0] * hd:(hs_[-1] + 1) * hd] = jnp.concatenate(outs, axis=-1).astype(o_ref.dtype)


def swa_attention(q, k, v, sink, *, B, nq, nkb, j0, win, hd, tq=None, into=None, out_cb=0):
    QW = q.shape[1]
    KW = k.shape[1]
    n_kv = KW // hd
    group = QW // KW
    tq = win if tq is None else tq
    kern = functools.partial(_swa_kernel, j0=j0, n_kv=n_kv, group=group, hd=hd, win=win)
    prev = lambda b, j: (b * nkb + jnp.maximum(j + j0 - 1, 0), 0)
    cur = lambda b, j: (b * nkb + j + j0, 0)
    in_specs = [pl.BlockSpec(memory_space=pltpu.SMEM),
                pl.BlockSpec((tq, QW), lambda b, j: (b * nq + j, 0)),
                pl.BlockSpec((win, KW), prev), pl.BlockSpec((win, KW), cur),
                pl.BlockSpec((win, KW), prev), pl.BlockSpec((win, KW), cur)]
    args = [sink.astype(F32), q, k, k, v, v]
    aliases = {}
    out_shape = jax.ShapeDtypeStruct((B * nq * tq, QW), BF16)
    if into is not None:
        aliases = {len(args): 0}
        in_specs.append(pl.BlockSpec(memory_space=pl.ANY))
        args.append(into)
        out_shape = jax.ShapeDtypeStruct(into.shape, into.dtype)
    return pl.pallas_call(
        kern,
        grid=(B, nq),
        in_specs=in_specs,
        out_specs=pl.BlockSpec((tq, QW), lambda b, j: (b * nq + j, out_cb)),
        out_shape=out_shape,
        input_output_aliases=aliases,
        compiler_params=_params("parallel", "parallel"),
        name="swa_attention",
    )(*args)


def _mem_attn_kernel(q_ref, g_ref, mk_ref, mv_ref, o_ref, *, n_heads, hd):
    q = q_ref[...]
    scale = hd ** -0.5
    for h in range(n_heads):
        sl = slice(h * hd, (h + 1) * hd)
        qh = q[:, sl]
        qn = (qh * lax.rsqrt(jnp.mean(qh * qh, axis=-1, keepdims=True) + NORM_EPS) * g_ref[...]).astype(BF16)
        s = _nt(qn, mk_ref[0, :, sl].astype(BF16)) * scale
        m = jnp.max(s, axis=-1, keepdims=True)
        p = jnp.exp(s - m)
        p = p / jnp.sum(p, axis=-1, keepdims=True)
        o_ref[:, sl] = _dot(p.astype(BF16), mv_ref[0, :, sl].astype(BF16)).astype(o_ref.dtype)


def mem_attention(q, qn_g, mk, mv, layer, *, B, T, n_heads, tt=256, out_rows=None):
    MW = q.shape[1]
    ML = mk.shape[2]
    hd = MW // n_heads
    tt = min(tt, T)
    nt = T // tt
    out_rows = B * T if out_rows is None else out_rows
    mem = pl.BlockSpec((None, 1, ML, MW), lambda b, t: (layer, b, 0, 0))
    return pl.pallas_call(
        functools.partial(_mem_attn_kernel, n_heads=n_heads, hd=hd),
        grid=(B, nt),
        in_specs=[pl.BlockSpec((tt, MW), lambda b, t: (b * nt + t, 0)),
                  pl.BlockSpec((1, hd), lambda b, t: (0, 0)),
                  mem, mem],
        out_specs=pl.BlockSpec((tt, MW), lambda b, t: (b * nt + t, 0)),
        out_shape=jax.ShapeDtypeStruct((out_rows, MW), BF16),
        compiler_params=_params("parallel", "parallel"),
        name="mem_attention",
    )(q, qn_g.reshape(1, hd).astype(F32), mk, mv)


def _ffn_in_kernel(*refs, blocks_per_seq, sub, has_into, emit_w, step_len):
    x_ref, wg_ref, wu_ref, cw_ref, cb_ref = refs[:5]
    pos = 6 if has_into else 5
    if step_len:
        xs_ref, h1_ref, h2_ref = refs[pos:pos + 3]
        pos += 3
    o_ref, tail_ref = refs[pos], refs[pos + 1]
    carry_scr = refs[-1]
    i = pl.program_id(0)
    j = pl.program_id(1)
    tm = x_ref.shape[0]

    @pl.when(i % blocks_per_seq == 0)
    def _():
        carry_scr[j] = jnp.zeros(carry_scr.shape[1:], F32)

    prev = carry_scr[j]
    wg = wg_ref[...].astype(BF16)
    wu = wu_ref[...].astype(BF16)
    if emit_w:
        refs[pos + 2][...] = wg
        refs[pos + 3][...] = wu
    if step_len:
        acts_ref, gates_ref = refs[pos + 4], refs[pos + 5]
        xs = xs_ref[...]
        gs = _dot(xs, wg)
        tpos = lax.broadcasted_iota(jnp.int32, (gs.shape[0], 1), 0) % step_len
        g1 = jnp.where(tpos < 1, h1_ref[...], pltpu.roll(gs, 1, 0))
        g2 = jnp.where(tpos < 2, h2_ref[...], pltpu.roll(gs, 2, 0))
        gcs = cb_ref[...] + cw_ref[2:3, :] * gs + cw_ref[1:2, :] * g1 + cw_ref[0:1, :] * g2
        acts_ref[...] = (jax.nn.gelu(gcs, approximate=True) * _dot(xs, wu)).astype(acts_ref.dtype)
        gates_ref[...] = gs
    for c in range(tm // sub):
        rows = slice(c * sub, (c + 1) * sub)
        x = x_ref[rows, :]
        gate = _dot(x, wg)
        up = _dot(x, wu)
        gc = cb_ref[...] + cw_ref[2:3, :] * gate
        for s in (1, 2):
            gc = gc + cw_ref[2 - s:3 - s, :] * _shift_rows(gate, prev, s)
        o_ref[rows, :] = (jax.nn.gelu(gc, approximate=True) * up).astype(o_ref.dtype)
        prev = gate[sub - SUBLANES:, :]
    carry_scr[j] = prev
    tail_ref[0] = prev


def _ffn_in_call(x, wg, wu, layer, up_off, conv_w, conv_b, *, seq_len, row0, rows, tm, tn, sub, into, emit_w,
                 steps=None):
    M, D = x.shape
    FF = conv_w.shape[1]
    tm = min(tm, seq_len)
    bps = seq_len // tm
    nj = FF // tn
    i0 = row0 // tm
    uo = up_off // tn
    wspec = lambda off: (pl.BlockSpec((D, tn), lambda i, j: (0, j + off)) if layer is None else
                         pl.BlockSpec((None, D, tn), lambda i, j: (layer, 0, j + off)))
    in_specs = [pl.BlockSpec((tm, D), lambda i, j: (i + i0, 0), pipeline_mode=pl.Buffered(1)),
                wspec(0), wspec(uo),
                pl.BlockSpec((3, tn), lambda i, j: (0, j)),
                pl.BlockSpec((1, tn), lambda i, j: (0, j))]
    args = [x, wg, wu, conv_w.astype(F32), conv_b.reshape(1, FF).astype(F32)]
    aliases = {}
    if into is not None:
        aliases = {len(args): 0}
        in_specs.append(pl.BlockSpec(memory_space=pl.ANY))
        args.append(into)
    step_len = 0
    if steps is not None:
        xs, h1, h2, step_len = steps
        ms = xs.shape[0]
        assert emit_w and rows == tm
        in_specs += [pl.BlockSpec((ms, D), lambda i, j: (0, 0)),
                     pl.BlockSpec((ms, tn), lambda i, j: (0, j)), pl.BlockSpec((ms, tn), lambda i, j: (0, j))]
        args += [xs, h1, h2]
    out_specs = [pl.BlockSpec((tm, tn), lambda i, j: (i + i0, j)),
                 pl.BlockSpec((1, SUBLANES, tn), lambda i, j: (i, 0, j))]
    out_shape = [jax.ShapeDtypeStruct((M, FF), BF16),
                 jax.ShapeDtypeStruct((rows // tm, SUBLANES, FF), F32)]
    if emit_w:
        out_specs += [pl.BlockSpec((D, tn), lambda i, j: (0, j))] * 2
        out_shape += [jax.ShapeDtypeStruct((D, FF), BF16)] * 2
    if step_len:
        out_specs += [pl.BlockSpec((ms, tn), lambda i, j: (0, j))] * 2
        out_shape += [jax.ShapeDtypeStruct((ms, FF), BF16), jax.ShapeDtypeStruct((ms, FF), F32)]
    outs = pl.pallas_call(
        functools.partial(_ffn_in_kernel, blocks_per_seq=bps, sub=min(sub, tm), has_into=into is not None,
                          emit_w=emit_w, step_len=step_len),
        grid=(rows // tm, nj),
        in_specs=in_specs,
        out_specs=out_specs,
        out_shape=out_shape,
        input_output_aliases=aliases,
        scratch_shapes=[pltpu.VMEM((nj, SUBLANES, tn), F32)],
        compiler_params=_params("arbitrary", "arbitrary"),
        name="ffn_in_fused",
    )(*args)
    return (outs[0], outs[1][bps - 1::bps]) + tuple(outs[2:])


def ffn_in_fused(x, w_in, layer, conv_w, conv_b, steps, *, seq_len, rows=None, sub=256):
    M = x.shape[0] if rows is None else rows
    FF = conv_w.shape[1]
    act, tail0, wg, wu, act_s, gate_s = _ffn_in_call(
        x, w_in, w_in, layer, FF, conv_w, conv_b, seq_len=seq_len, row0=0, rows=seq_len, tm=2048, tn=256,
        sub=sub, into=None, emit_w=True, steps=steps)
    if M == seq_len:
        return act, tail0, act_s, gate_s
    act, tails = _ffn_in_call(x, wg, wu, None, 0, conv_w, conv_b, seq_len=seq_len, row0=seq_len,
                              rows=M - seq_len, tm=1024, tn=512, sub=sub, into=act, emit_w=False)
    return act, jnp.concatenate([tail0, tails], axis=0), act_s, gate_s


def _rwkv_mix_kernel(x_ref, halo_ref, s0_ref, g_ref, mu_ref, w1_ref, a1_ref, g1_ref, *out_refs, n_valid_last):
    t = pl.program_id(1)
    nt = pl.num_programs(1)
    xr_ref, xk_ref, xv_ref, lw_ref, la_ref, lg_ref, hl_ref = out_refs

    def norm(v):
        return v * lax.rsqrt(jnp.mean(v * v, axis=-1, keepdims=True) + NORM_EPS) * g_ref[...]

    h = norm(x_ref[...])
    hp = norm(halo_ref[...])
    first = jnp.where(t > 0, 1.0, 0.0)
    hp = hp * first + jnp.broadcast_to(s0_ref[0], hp.shape) * (1.0 - first)
    xx = _shift_rows(h, hp, 1) - h
    mix = lambda j: (h + xx * mu_ref[j:j + 1, :]).astype(BF16)
    xr_ref[...] = mix(0)
    xk_ref[...] = mix(2)
    xv_ref[...] = mix(3)
    lw_ref[...] = jnp.tanh(_dot(mix(1), w1_ref[...])).astype(BF16)
    la_ref[...] = _dot(mix(4), a1_ref[...]).astype(BF16)
    lg_ref[...] = jax.nn.sigmoid(_dot(mix(5), g1_ref[...])).astype(BF16)

    @pl.when(t == nt - 1)
    def _():
        hl_ref[0] = h[n_valid_last - 1:n_valid_last, :]


def rwkv_mix(x, shift0, g, mu, w1, a1, g1, *, B, T, n_valid, tt=256, out_rows=None):
    D = x.shape[1]
    tt = min(tt, T)
    nt = T // tt
    hb = tt // SUBLANES
    out_rows = B * T if out_rows is None else out_rows
    n_valid_last = n_valid - (nt - 1) * tt
    blk = pl.BlockSpec((tt, D), lambda b, t: (b * nt + t, 0))
    one = pl.BlockSpec((1, 1, D), lambda b, t: (b, 0, 0))
    return pl.pallas_call(
        functools.partial(_rwkv_mix_kernel, n_valid_last=n_valid_last),
        grid=(B, nt),
        in_specs=[blk,
                  pl.BlockSpec((SUBLANES, D), lambda b, t: (jnp.maximum((b * nt + t) * hb - 1, 0), 0)),
                  one,
                  pl.BlockSpec((1, D), lambda b, t: (0, 0)),
                  pl.BlockSpec((6, D), lambda b, t: (0, 0))] +
                 [pl.BlockSpec(w.shape, lambda b, t: (0, 0)) for w in (w1, a1, g1)],
        out_specs=[blk] * 3 + [pl.BlockSpec((tt, w.shape[1]), lambda b, t: (b * nt + t, 0)) for w in (w1, a1, g1)]
                  + [one],
        out_shape=[jax.ShapeDtypeStruct((out_rows, D), BF16)] * 3 +
                  [jax.ShapeDtypeStruct((out_rows, w.shape[1]), BF16) for w in (w1, a1, g1)] +
                  [jax.ShapeDtypeStruct((B, 1, D), F32)],
        compiler_params=_params("parallel", "arbitrary"),
        name="rwkv_mix",
    )(x, x, shift0, g.reshape(1, D).astype(F32), mu.astype(F32), w1, a1, g1)


def _rwkv_chunk_kernel(r_ref, k_ref, v_ref, lw_ref, la_ref, lg_ref, w2_ref, a2_ref, g2_ref, w0_ref, a0_ref,
                       kk_ref, ka_ref, rk_ref, lng_ref, lnb_ref, s0_ref, o_ref, so_ref, s_scr, *, hs, C, n_valid):
    c = pl.program_id(2)
    nc = pl.num_programs(2)
    R, L = r_ref.shape
    n_pairs = L // LANES

    @pl.when(c == 0)
    def _():
        z = jnp.zeros((hs, hs), F32)
        for p in range(n_pairs):
            s_scr[p] = jnp.concatenate([jnp.concatenate([s0_ref[0, 2 * p], z], axis=1),
                                        jnp.concatenate([z, s0_ref[0, 2 * p + 1]], axis=1)], axis=0)

    ti = lax.broadcasted_iota(jnp.int32, (C, C), 0)
    si = lax.broadcasted_iota(jnp.int32, (C, C), 1)
    tri = jnp.where(ti >= si, 1.0, 0.0).astype(BF16)
    lane = lax.broadcasted_iota(jnp.int32, (1, LANES), 1)
    m0 = lane < hs
    C2 = 2 * C
    ri = lax.broadcasted_iota(jnp.int32, (C2, C2), 0)
    ci = lax.broadcasted_iota(jnp.int32, (C2, C2), 1)
    same = (ri // C) == (ci // C)
    strict = same & ((ri % C) > (ci % C))
    lower = (ri % C) >= (ci % C)
    vi = lax.broadcasted_iota(jnp.int32, (LANES, LANES), 0) // hs
    vj = lax.broadcasted_iota(jnp.int32, (LANES, LANES), 1) // hs
    blockdiag = vi == vj
    n_steps = int(math.log2(C))
    wide = C2 % LANES == 0
    P = range(n_pairs)
    sls = [slice(p * LANES, (p + 1) * LANES) for p in P]
    inv = 1.0 / hs

    def stack_heads(x):
        return jnp.concatenate([jnp.where(m0, x, 0.0), jnp.where(m0, 0.0, x)], axis=0).astype(BF16)

    S = [s_scr[p] for p in P]
    for sc in range(R // C):
        rows = slice(sc * C, (sc + 1) * C)
        r = r_ref[rows, :]
        k = k_ref[rows, :]
        v = v_ref[rows, :]
        wp = w0_ref[...] + _dot(lw_ref[rows, :], w2_ref[...])
        logw = -jnp.exp(-jax.nn.softplus(-wp) - 0.5)
        if n_valid is not None:
            logw = jnp.where(lax.broadcasted_iota(jnp.int32, (C, 1), 0) + sc * C < n_valid, logw, 0.0)
        a = jax.nn.sigmoid(a0_ref[...] + _dot(la_ref[rows, :], a2_ref[...]))
        kk = k * kk_ref[...]
        kk = kk / jnp.maximum(jnp.sqrt(_group_sum_bcast(kk * kk, hs, terms=1)), 1e-12)
        k2 = k * (1.0 + (a - 1.0) * ka_ref[...])
        bm = kk * a
        cum = _dot_exact_rhs_lhs(tri, logw)
        e_in = jnp.exp(cum)
        e_out = jnp.exp(-cum)
        e_end = jnp.exp(cum[C - 1:C, :] - cum)
        g_end = jnp.exp(cum[C - 1:C, :])
        rt = r * e_in
        kkt = kk * jnp.exp(cum - logw)
        bh = bm * e_out
        kh = k2 * e_out
        bbar = bm * e_end
        kbar = k2 * e_end

        S16 = [s_.astype(BF16) for s_ in S]
        V16 = [v[:, sl].astype(BF16) for sl in sls]
        lhk = [stack_heads(kkt[:, sl]) for sl in sls]
        if wide:
            gbk = [_nt(lhk[p], jnp.concatenate([stack_heads(bh[:, sls[p]]), stack_heads(kh[:, sls[p]])], axis=0))
                   for p in P]
            nmat = [jnp.where(strict, -g_[:, :C2], 0.0) for g_ in gbk]
            auk = [jnp.where(strict, -g_[:, C2:], 0.0).astype(BF16) for g_ in gbk]
        else:
            nmat = [jnp.where(strict, -_nt(lhk[p], stack_heads(bh[:, sls[p]])), 0.0) for p in P]
            auk = [jnp.where(strict, -_nt(lhk[p], stack_heads(kh[:, sls[p]])), 0.0).astype(BF16) for p in P]
        sprod = [_nt(jnp.concatenate([lhk[p], rt[:, sls[p]].astype(BF16)], axis=0), S16[p]) for p in P]
        u = [_dot(auk[p], jnp.concatenate([V16[p], V16[p]], axis=0)) - sprod[p][:C2] for p in P]
        npow = [n_.astype(BF16) for n_ in nmat]
        for it in range(n_steps):
            last = it + 1 == n_steps
            if wide and not last:
                res = [_dot(npow[p], jnp.concatenate([u[p].astype(BF16), npow[p]], axis=1)) for p in P]
                u = [u[p] + res[p][:, :LANES] for p in P]
                npow = [res[p][:, LANES:].astype(BF16) for p in P]
            else:
                u = [u[p] + _dot(npow[p], u[p].astype(BF16)) for p in P]
                if not last:
                    npow = [_dot(npow[p], npow[p]).astype(BF16) for p in P]
        uv = [jnp.concatenate([jnp.where(m0, u[p][:C], u[p][C:]).astype(BF16), V16[p]], axis=0) for p in P]
        ar = [jnp.where(lower, _nt(stack_heads(rt[:, sls[p]]),
                                   jnp.concatenate([bh[:, sls[p]], kh[:, sls[p]]], axis=0).astype(BF16)),
                        0.0).astype(BF16) for p in P]
        tmat = [_dot(ar[p], uv[p]) for p in P]
        ys = [sprod[p][C2:] + jnp.where(m0, tmat[p][:C], tmat[p][C:]) for p in P]
        upd = [_tn(uv[p], jnp.concatenate([bbar[:, sls[p]], kbar[:, sls[p]]], axis=0).astype(BF16)) for p in P]
        S = [S[p] * g_end[:, sls[p]] + jnp.where(blockdiag, upd[p], 0.0) for p in P]

        y = ys[0] if n_pairs == 1 else jnp.concatenate(ys, axis=-1)
        mean = _group_sum_bcast(y, hs, terms=1) * inv
        yc = y - mean
        var = _group_sum_bcast(yc * yc, hs, terms=1) * inv
        yn = yc * lax.rsqrt(var + RW_GN_EPS) * lng_ref[...] + lnb_ref[...]
        bonus = _group_sum_bcast(r * k2 * rk_ref[...], hs, terms=1) * v
        o_ref[rows, :] = ((yn + bonus) * _dot(lg_ref[rows, :], g2_ref[...])).astype(o_ref.dtype)

    for p in P:
        s_scr[p] = S[p]

    @pl.when(c == nc - 1)
    def _():
        for p in P:
            so_ref[0, 2 * p] = S[p][:hs, :hs]
            so_ref[0, 2 * p + 1] = S[p][hs:, hs:]


def _dot_exact_rhs_lhs(m_bf16, x):
    hi = x.astype(BF16)
    lo = (x - hi.astype(F32)).astype(BF16)
    return _dot(m_bf16, hi) + _dot(m_bf16, lo)


def rwkv_chunked(r, k, v, lw, la, lg, w2, a2, g2, w0, a0, k_k, k_a, r_k, ln_g, ln_b, s0, *, B, T, hs, chunk,
                 chunks_per_step=1, lanes_per_step=1024, out_rows=None, n_valid=None):
    D = r.shape[1]
    L = min(lanes_per_step, D)
    npg = L // LANES
    rows = chunk * chunks_per_step
    nc = T // rows
    out_rows = B * T if out_rows is None else out_rows
    seq = pl.BlockSpec((rows, L), lambda b, hg, c: (b * nc + c, hg))
    low = lambda a_: pl.BlockSpec((rows, a_.shape[1]), lambda b, hg, c: (b * nc + c, 0))
    wgt = lambda w_: pl.BlockSpec((w_.shape[0], L), lambda b, hg, c: (0, hg))
    par = pl.BlockSpec((1, L), lambda b, hg, c: (0, hg))
    st = pl.BlockSpec((1, 2 * npg, hs, hs), lambda b, hg, c: (b, hg, 0, 0))
    row = lambda x: x.reshape(1, D).astype(F32)
    return pl.pallas_call(
        functools.partial(_rwkv_chunk_kernel, hs=hs, C=chunk, n_valid=n_valid),
        grid=(B, D // L, nc),
        in_specs=[seq] * 3 + [low(lw), low(la), low(lg), wgt(w2), wgt(a2), wgt(g2)] + [par] * 7 + [st],
        out_specs=[seq, st],
        out_shape=[jax.ShapeDtypeStruct((out_rows, D), BF16),
                   jax.ShapeDtypeStruct(s0.shape, F32)],
        scratch_shapes=[pltpu.VMEM((npg, LANES, LANES), F32)],
        compiler_params=_params("parallel", "parallel", "arbitrary"),
        name="rwkv_chunked",
    )(r, k, v, lw, la, lg, w2, a2, g2, row(w0), row(a0), row(k_k), row(k_a), row(r_k), row(ln_g), row(ln_b), s0)


def _pad_rows(x, front, back):
    return jnp.pad(x, ((0, 0), (front, back), (0, 0)))


def kernel(x_prompt, x_sample, mem_prompt, state_lru_conv, state_lru_h, cache_swa_k, cache_swa_v,
           state_rwkv_shift, state_rwkv_wkv, cache_mem_k, cache_mem_v, state_ffn_conv,
           a_norm_g, a_w_in, a_conv_w, a_conv_b, a_gate_a_w, a_gate_a_b, a_gate_x_w, a_gate_x_b,
           a_lambda, b_q_norm_g, b_k_norm_g, b_sink, a_w_out,
           c_norm_g, c_mu, c_w_r, c_w_k, c_w_v, c_w_o, c_w0, c_w1, c_w2, c_a0, c_a1, c_a2,
           c_g1, c_g2, c_k_k, c_k_a, c_r_k, c_ln_g, c_ln_b,
           m_norm_g, m_mem_norm_g, m_w_q, m_w_kv, m_q_norm_g, m_k_norm_g, m_w_o,
           f_norm_g, f_w_in, f_conv_w, f_conv_b, f_w_out):
    D = x_prompt.shape[-1]
    depth = m_norm_g.shape[0]
    W = a_conv_w.shape[-1]
    KA = a_conv_w.shape[1]
    hd = b_q_norm_g.shape[-1]
    n_q = b_sink.shape[-1]
    n_kv = cache_swa_k.shape[3]
    win = cache_swa_k.shape[2]
    QW, KW = n_q * hd, n_kv * hd
    hs = c_r_k.shape[-1]
    n_rw = c_r_k.shape[1]
    m_heads, m_hd = cache_mem_k.shape[3], cache_mem_k.shape[4]
    MW = m_heads * m_hd
    FF = f_conv_w.shape[-1]
    KF = f_conv_w.shape[1]
    bf = lambda w: w.astype(BF16)
    wcache = {}

    Bp, S = x_prompt.shape[:2]
    Bs, Ts = x_sample.shape[:2]
    Mp, Ms = Bp * S, Bs * Ts
    Mt = Mp + Ms
    tma = _row_tile(Mt, 1280)
    n_a = a_norm_g.shape[0]
    n_c = c_norm_g.shape[0]
    ML = mem_prompt.shape[1]
    PAD8, PAD16 = SUBLANES, 2 * SUBLANES

    def mm(xin, w, layer, key, **kw):
        return dense(xin, w, layer, wcache, (key, layer), tm=tma, **kw)

    def seq_tail(arr, n, cols=slice(None)):
        return jnp.stack([arr[(b + 1) * S - n:(b + 1) * S, cols] for b in range(Bp)])

    def put_sample(buf, rows_s):
        return lax.dynamic_update_slice(buf, rows_s.astype(buf.dtype), (Mp, 0))

    mem_flat = mem_prompt.reshape(Bp * ML, D)
    mks, mvs = [], []
    for l in range(depth):
        mn = rmsnorm_rows(mem_flat, m_mem_norm_g[l])
        kv = matmul(mn, m_w_kv, layer=l)
        mk = headnorm(kv, 0, MW, m_k_norm_g[l], m_hd, F32)
        mks.append(mk.reshape(Bp, ML, m_heads, m_hd))
        mvs.append(kv[:, MW:].reshape(Bp, ML, m_heads, m_hd))
    p_mem_k = jnp.stack(mks)
    p_mem_v = jnp.stack(mvs)

    xp2, xs2 = x_prompt.reshape(Mp, D), x_sample.reshape(Ms, D)
    xf = None

    def first_norm(g):
        return put_sample(rmsnorm_rows(xp2, g, out_rows=Mt), rmsnorm_rows(xs2, g))

    def first_residual(xin, w, layer, key):
        y = dense(xin, w, layer, wcache, (key, layer), rows=Mp, out_rows=Mt, res=xp2)
        return matmul(xin, wcache[(key, layer)], row0=Mp, rows=Ms, res=xs2, res_row0=0, into=y)

    po ={k_: [] for k_ in ("lc", "lh", "sk", "sv", "rs", "rw", "fc")}
    so = {k_: [] for k_ in ("lc", "lh", "sk", "sv", "rs", "rw", "fc")}
    ia = ic = 0
    y_p = y_s = None
    for l in range(depth):
        if l % 2 == 0:
            i = ia
            ia += 1
            assert W == QW
            h = first_norm(a_norm_g[i]) if xf is None else rmsnorm_rows(xf, a_norm_g[i])
            zz = mm(h, a_w_in, i, "a_w_in_rg", n=2 * W)
            qkv = mm(h, a_w_in, i, "a_w_in_qkv", n0=2 * W)
            qn = headnorm(qkv, 0, QW, b_q_norm_g[i], hd, BF16)
            kn = headnorm(qkv, QW // KW, KW, b_k_norm_g[i], hd, F32)
            vv = qkv[:, QW + KW:]
            lru_w = (a_conv_w[i], a_conv_b[i], a_gate_a_w[i], a_gate_a_b[i], a_gate_x_w[i], a_gate_x_b[i],
                     a_lambda[i])
            mix, hl_p = lru_mixer(zz, 0, zz, 1, jnp.zeros((Bp, 1, W), F32), *lru_w, B=Bp, T=S, n_pad=0, tt=256,
                                  out_rows=Mt, out_cols=W + QW)
            mix = swa_attention(qn, bf(kn), bf(vv), b_sink[i], B=Bp, nq=S // win, nkb=S // win, j0=0, win=win,
                                hd=hd, into=mix, out_cb=W // QW)
            po["lc"].append(seq_tail(zz, KA - 1, slice(0, W)))
            po["lh"].append(hl_p.reshape(Bp, W))
            po["sk"].append(seq_tail(kn, win).reshape(Bp, win, n_kv, hd))
            po["sv"].append(seq_tail(vv, win).reshape(Bp, win, n_kv, hd))
            zz_s = zz[Mp:].reshape(Bs, Ts, 2 * W)
            n_pad = PAD8 - Ts
            xr_hist = jnp.concatenate([state_lru_conv[i].astype(F32), zz_s[:, :, :W]], axis=1)
            xr_p = _pad_rows(xr_hist, PAD8 - xr_hist.shape[1], 0).reshape(Bs * PAD8, W)
            yg_p = _pad_rows(zz_s[:, :, W:], n_pad, 0).reshape(Bs * PAD8, W)
            oa_s, hl_s = lru_mixer(xr_p, 0, yg_p, 0, state_lru_h[i].reshape(Bs, 1, W).astype(F32), *lru_w,
                                   B=Bs, T=PAD8, n_pad=n_pad, tt=PAD8)
            oa_s = oa_s.reshape(Bs, PAD8, W)[:, n_pad:].reshape(Ms, W)
            kc = cache_swa_k[i].reshape(Bs, win, KW)
            vc = cache_swa_v[i].reshape(Bs, win, KW)
            k_all = jnp.concatenate([kc, kn[Mp:].reshape(Bs, Ts, KW)], axis=1)
            v_all = jnp.concatenate([vc, vv[Mp:].reshape(Bs, Ts, KW)], axis=1)
            q_s = _pad_rows(qn[Mp:].reshape(Bs, Ts, QW), 0, PAD16 - Ts).reshape(Bs * PAD16, QW)
            o_s = swa_attention(q_s, bf(_pad_rows(k_all, 0, win - Ts)).reshape(Bs * 2 * win, KW),
                                bf(_pad_rows(v_all, 0, win - Ts)).reshape(Bs * 2 * win, KW), b_sink[i],
                                B=Bs, nq=1, nkb=2, j0=1, win=win, hd=hd, tq=PAD16)
            o_s = o_s.reshape(Bs, PAD16, QW)[:, :Ts].reshape(Ms, QW)
            mix = put_sample(mix, jnp.concatenate([oa_s, o_s], axis=-1))
            so["lc"].append(xr_hist[:, Ts:])
            so["lh"].append(hl_s.reshape(Bs, W))
            so["sk"].append(k_all[:, Ts:].reshape(Bs, win, n_kv, hd))
            so["sv"].append(v_all[:, Ts:].reshape(Bs, win, n_kv, hd))
            xf = (first_residual(mix, a_w_out, i, "a_w_out") if xf is None
                  else mm(mix, a_w_out, i, "a_w_out", res=xf))
        else:
            i = ic
            ic += 1
            gl = c_g1[i].shape[1]
            glp = -(-gl // LANES) * LANES
            lora1 = (bf(c_w1[i]), bf(c_a1[i]), bf(jnp.pad(c_g1[i], ((0, 0), (0, glp - gl)))))
            lora2 = (bf(c_w2[i]), bf(c_a2[i]), bf(jnp.pad(c_g2[i], ((0, glp - gl), (0, 0)))), c_w0[i], c_a0[i])
            mixes = rwkv_mix(xf, jnp.zeros((Bp, 1, D), F32), c_norm_g[i], c_mu[i], *lora1, B=Bp, T=S, n_valid=S,
                             out_rows=Mt)
            x_s = _pad_rows(xf[Mp:].reshape(Bs, Ts, D), 0, PAD8 - Ts).reshape(Bs * PAD8, D)
            mixes_s = rwkv_mix(x_s, state_rwkv_shift[i].reshape(Bs, 1, D).astype(F32), c_norm_g[i], c_mu[i], *lora1,
                               B=Bs, T=PAD8, n_valid=Ts, tt=PAD8)
            take = lambda m: m.reshape(Bs, PAD8, m.shape[-1])[:, :Ts].reshape(Ms, m.shape[-1])
            xr, xk, xv, lw, la, lg = [put_sample(mixes[j], take(mixes_s[j])) for j in range(6)]
            po["rs"].append(mixes[6].reshape(Bp, D))
            so["rs"].append(mixes_s[6].reshape(Bs, D))
            r = mm(xr, c_w_r, i, "c_w_r")
            k = mm(xk, c_w_k, i, "c_w_k")
            v = mm(xv, c_w_v, i, "c_w_v")
            rw_par = (c_k_k[i], c_k_a[i], c_r_k[i], c_ln_g[i], c_ln_b[i])
            yo, s_end = rwkv_chunked(r, k, v, lw, la, lg, *lora2, *rw_par, jnp.zeros((Bp, n_rw, hs, hs), F32),
                                     B=Bp, T=S, hs=hs, chunk=64, chunks_per_step=4, out_rows=Mt)
            po["rw"].append(s_end)
            cs = PAD16
            sq = lambda t: jnp.pad(t[Mp:].reshape(Bs, Ts, t.shape[-1]),
                                   ((0, 0), (0, cs - Ts), (0, 0))).reshape(Bs * cs, t.shape[-1])
            yo_s, s_end_s = rwkv_chunked(sq(r), sq(k), sq(v), sq(lw), sq(la), sq(lg), *lora2, *rw_par,
                                         state_rwkv_wkv[i].astype(F32), B=Bs, T=cs, hs=hs, chunk=cs,
                                         lanes_per_step=2048, n_valid=Ts)
            yo = put_sample(yo, yo_s.reshape(Bs, cs, D)[:, :Ts].reshape(Ms, D))
            so["rw"].append(s_end_s)
            xf = mm(yo, c_w_o, i, "c_w_o", res=xf)
        q = norm_proj(xf, m_norm_g[l], m_w_q, l)
        om = mem_attention(q, m_q_norm_g[l], p_mem_k.reshape(depth, Bp, ML, MW), p_mem_v.reshape(depth, Bp, ML, MW),
                           l, B=Bp, T=S, n_heads=m_heads, out_rows=Mt)
        q_s = _pad_rows(q[Mp:].reshape(Bs, Ts, MW), 0, PAD8 - Ts).reshape(Bs * PAD8, MW)
        om_s = mem_attention(q_s, m_q_norm_g[l], cache_mem_k.reshape(depth, Bs, ML, MW),
                             cache_mem_v.reshape(depth, Bs, ML, MW), l, B=Bs, T=PAD8, n_heads=m_heads, tt=PAD8)
        om = put_sample(om, om_s.reshape(Bs, PAD8, MW)[:, :Ts].reshape(Ms, MW))
        xf, hf = proj_res_norm(om, m_w_o, l, xf, f_norm_g[l])
        assert Ts >= KF - 1 == 2
        st_f = state_ffn_conv[l].astype(F32)
        zrow = jnp.zeros((Bs, Ts - 1, FF), F32)
        h1 = jnp.concatenate([st_f[:, 1:2], zrow], axis=1).reshape(Ms, FF)
        h2 = jnp.concatenate([st_f, zrow[:, 1:]], axis=1).reshape(Ms, FF)
        act, tail, act_s, gate_s = ffn_in_fused(hf, f_w_in, l, f_conv_w[l], f_conv_b[l], (hf[Mp:], h1, h2, Ts),
                                                seq_len=S, rows=Mp)
        po["fc"].append(tail[:, SUBLANES - (KF - 1):, :])
        act = put_sample(act, act_s)
        so["fc"].append(gate_s.reshape(Bs, Ts, FF)[:, Ts - (KF - 1):])
        if l + 1 < depth:
            xf = mm(act, f_w_out, l, "f_w_out", res=xf, tn=1024, tk=2048)
        else:
            y_p = dense(act, f_w_out, l, wcache, ("f_w_out", l), rows=Mp, out_rows=Mp, res=xf, tn=1024, tk=2048)
            y_s = matmul(act, wcache[("f_w_out", l)], row0=Mp, rows=Ms, out_rows=Ms, out_row0=0, res=xf,
                         tn=1024, tk=2048)
    st = jnp.stack
    return (y_p.reshape(Bp, S, D), y_s.reshape(Bs, Ts, D),
            st(po["lc"]), st(po["lh"]), st(po["sk"]), st(po["sv"]), st(po["rs"]), st(po["rw"]),
            p_mem_k, p_mem_v, st(po["fc"]),
            st(so["lc"]), st(so["lh"]), st(so["sk"]), st(so["sv"]), st(so["rs"]), st(so["rw"]), st(so["fc"]))
```

```python
import functools
import math

import jax
import jax.numpy as jnp
from jax import lax
from jax.experimental import pallas as pl
from jax.experimental.pallas import tpu as pltpu

F32 = jnp.float32
BF16 = jnp.bfloat16

NORM_EPS = 1e-6
RW_GN_EPS = 64e-5
LRU_C = 8.0
LANES = 128
SUBLANES = 8
VMEM_LIMIT_BYTES = 56 * 1024 * 1024


def _params(*sem):
    return pltpu.CompilerParams(dimension_semantics=sem, vmem_limit_bytes=VMEM_LIMIT_BYTES)


def _nt(a, b):
    return lax.dot_general(a, b, (((1,), (1,)), ((), ())), preferred_element_type=F32)


def _tn(a, b):
    return lax.dot_general(a, b, (((0,), (0,)), ((), ())), preferred_element_type=F32)


def _dot(a, b):
    return jnp.dot(a, b, preferred_element_type=F32)


def _group_sum_bcast(x, width, terms=2):
    m, L = x.shape
    n = L // LANES
    if width == LANES:
        parts = []
        for c in range(n):
            s = jnp.sum(x[:, c * LANES:(c + 1) * LANES], axis=-1, keepdims=True)
            parts.append(jnp.broadcast_to(s, (m, LANES)))
        return parts[0] if n == 1 else jnp.concatenate(parts, axis=-1)
    li = lax.broadcasted_iota(jnp.int32, (LANES, LANES), 0) // width
    lj = lax.broadcasted_iota(jnp.int32, (LANES, LANES), 1) // width
    e = jnp.where(li == lj, 1.0, 0.0).astype(BF16)
    xs = x if n == 1 else jnp.concatenate([x[:, c * LANES:(c + 1) * LANES] for c in range(n)], axis=0)
    hi = xs.astype(BF16)
    out = _dot(hi, e)
    if terms > 1:
        out = out + _dot((xs - hi.astype(F32)).astype(BF16), e)
    return out if n == 1 else jnp.concatenate([out[c * m:(c + 1) * m] for c in range(n)], axis=-1)


def _shift_rows(x, prev8, s):
    rolled = pltpu.roll(x, s, 0)
    top = jnp.where(lax.broadcasted_iota(jnp.int32, (SUBLANES, 1), 0) < s,
                    pltpu.roll(prev8, s, 0), rolled[0:SUBLANES])
    if x.shape[0] == SUBLANES:
        return top
    return jnp.concatenate([top, rolled[SUBLANES:]], axis=0)


def _rmsnorm_kernel(x_ref, g_ref, o_ref):
    x = x_ref[...]
    ms = jnp.mean(x * x, axis=-1, keepdims=True)
    o_ref[...] = (x * lax.rsqrt(ms + NORM_EPS) * g_ref[...]).astype(o_ref.dtype)


def _row_tile(m, target):
    best = None
    for d in range(16, min(m, target) + 1, 16):
        if m % d == 0:
            best = d
    return best or m


def rmsnorm_rows(x, g, out_dtype=BF16, tm=512, out_rows=None):
    M, D = x.shape
    tm = _row_tile(M, tm)
    return pl.pallas_call(
        _rmsnorm_kernel,
        grid=(M // tm,),
        in_specs=[pl.BlockSpec((tm, D), lambda i: (i, 0)), pl.BlockSpec((1, D), lambda i: (0, 0))],
        out_specs=pl.BlockSpec((tm, D), lambda i: (i, 0)),
        out_shape=jax.ShapeDtypeStruct((M if out_rows is None else out_rows, D), out_dtype),
        compiler_params=_params("parallel"),
        name="rmsnorm_rows",
    )(x, g.reshape(1, D))


def _mm_kernel(*refs, nk, has_bias, has_res, has_into, emit_w, in_place, act):
    x_ref, w_ref = refs[0], refs[1]
    pos = 2
    b_ref = r_ref = wb_ref = acc_ref = None
    if has_bias:
        b_ref = refs[pos]
        pos += 1
    if has_res:
        r_ref = refs[pos]
        pos += 1
    if has_into:
        pos += 1
    o_ref = refs[pos]
    pos += 1
    if emit_w:
        wb_ref = refs[pos]
        pos += 1
    if nk > 1 and not in_place:
        acc_ref = refs[pos]

    def weights(cs=slice(None)):
        w16 = w_ref[:, cs].astype(BF16)
        if emit_w:
            wb_ref[:, cs] = w16
        return w16

    def epilogue(y):
        if has_bias:
            y = y + b_ref[...]
        if act == "tanh":
            y = jnp.tanh(y)
        elif act == "sigmoid":
            y = jax.nn.sigmoid(y)
        if has_res:
            y = y + r_ref[...]
        o_ref[...] = y.astype(o_ref.dtype)

    if nk > 1 and in_place:
        k = pl.program_id(2)
        tn = o_ref.shape[1]
        cw = min(tn, 2 * LANES)

        def sweep(first):
            x = x_ref[...]
            for c in range(tn // cw):
                cs = slice(c * cw, (c + 1) * cw)
                part = _dot(x, weights(cs))
                if not first:
                    o_ref[:, cs] += part
                elif has_res:
                    o_ref[:, cs] = part + r_ref[:, cs]
                else:
                    o_ref[:, cs] = part

        pl.when(k == 0)(functools.partial(sweep, True))
        pl.when(k > 0)(functools.partial(sweep, False))
        return

    part = _dot(x_ref[...], weights())
    if nk == 1:
        epilogue(part)
        return
    k = pl.program_id(2)

    @pl.when(k == 0)
    def _():
        acc_ref[...] = part

    @pl.when(k > 0)
    def _():
        acc_ref[...] += part

    @pl.when(k == nk - 1)
    def _():
        epilogue(acc_ref[...])


def _pick(n, prefs):
    for p in prefs:
        if n % p == 0:
            return p
    return n


def matmul(x, w, *, layer=None, n0=0, n=None, bias=None, res=None, act=None, out_dtype=F32,
           tm=1024, tn=512, tk=4096, row0=0, rows=None, into=None, emit_w=False, out_rows=None, out_row0=None,
           res_row0=None):
    M, K = x.shape
    N = w.shape[-1] - n0 if n is None else n
    rows = M - row0 if rows is None else rows
    tm = min(tm, rows)
    out_rows = M if out_rows is None else out_rows
    o0 = (row0 if out_row0 is None else out_row0) // tm
    tn = _pick(math.gcd(N, n0) if n0 else N, (tn, 512, 256, 128))
    tk = _pick(K, (tk, 2048, 1024, 512))
    nk = K // tk
    j0 = n0 // tn
    i0 = row0 // tm
    assert row0 % tm == 0 and rows % tm == 0 and (not emit_w or rows == tm)
    in_specs = [pl.BlockSpec((tm, tk), lambda i, j, k: (i + i0, k)),
                pl.BlockSpec((tk, tn), lambda i, j, k: (k, j + j0)) if layer is None else
                pl.BlockSpec((None, tk, tn), lambda i, j, k: (layer, k, j + j0))]
    args = [x, w]
    if bias is not None:
        in_specs.append(pl.BlockSpec((1, tn), lambda i, j, k: (0, j)))
        args.append(bias.reshape(1, N).astype(F32))
    if res is not None:
        r0 = i0 if res_row0 is None else res_row0 // tm
        in_specs.append(pl.BlockSpec((tm, tn), lambda i, j, k: (i + r0, j)))
        args.append(res)
    aliases = {}
    if into is not None:
        aliases = {len(args): 0}
        in_specs.append(pl.BlockSpec(memory_space=pl.ANY))
        args.append(into)
    in_place = bias is None and act is None and out_dtype == F32
    kern = functools.partial(_mm_kernel, nk=nk, has_bias=bias is not None, has_res=res is not None,
                             has_into=into is not None, emit_w=emit_w, in_place=in_place, act=act)
    out_specs = [pl.BlockSpec((tm, tn), lambda i, j, k: (i + o0, j))]
    out_shape = [jax.ShapeDtypeStruct((out_rows, N), out_dtype)]
    if emit_w:
        out_specs.append(pl.BlockSpec((tk, tn), lambda i, j, k: (k, j)))
        out_shape.append(jax.ShapeDtypeStruct((K, N), BF16))
    outs = pl.pallas_call(
        kern,
        grid=(rows // tm, N // tn, nk),
        in_specs=in_specs,
        out_specs=out_specs,
        out_shape=out_shape,
        input_output_aliases=aliases,
        scratch_shapes=[pltpu.VMEM((tm, tn), F32)] if (nk > 1 and not in_place) else [],
        compiler_params=_params("parallel", "parallel", "arbitrary"),
        name="matmul",
    )(*args)
    return outs if emit_w else outs[0]


def dense(x, w, layer, wcache, key, **kw):
    rows = kw.pop("rows", x.shape[0])
    tm = min(kw.get("tm", 1024), rows)
    first, wb = matmul(x, w, layer=layer, rows=tm, emit_w=True, **kw)
    wcache[key] = wb
    if tm == rows:
        return first
    kw = {a: b for a, b in kw.items() if a not in ("n0", "n")}
    return matmul(x, wb, row0=tm, rows=rows - tm, into=first, **kw)


def _proj_res_norm_kernel(x_ref, w_ref, r_ref, g_ref, y_ref, h_ref, w_scr):
    @pl.when(pl.program_id(0) == 0)
    def _():
        w_scr[...] = w_ref[...].astype(BF16)

    y = _dot(x_ref[...], w_scr[...]) + r_ref[...]
    y_ref[...] = y
    ms = jnp.mean(y * y, axis=-1, keepdims=True)
    h_ref[...] = (y * lax.rsqrt(ms + NORM_EPS) * g_ref[...]).astype(h_ref.dtype)


def proj_res_norm(x, w, layer, res, g, tm=320):
    M, K = x.shape
    N = w.shape[-1]
    tm = _row_tile(M, tm)
    row_blk = lambda c: pl.BlockSpec((tm, c), lambda i: (i, 0))
    return pl.pallas_call(
        _proj_res_norm_kernel,
        grid=(M // tm,),
        in_specs=[row_blk(K), pl.BlockSpec((None, K, N), lambda i: (layer, 0, 0)), row_blk(N),
                  pl.BlockSpec((1, N), lambda i: (0, 0))],
        out_specs=[row_blk(N), row_blk(N)],
        out_shape=[jax.ShapeDtypeStruct((M, N), F32), jax.ShapeDtypeStruct((M, N), BF16)],
        scratch_shapes=[pltpu.VMEM((K, N), BF16)],
        compiler_params=_params("arbitrary"),
        name="proj_res_norm",
    )(x, w, res, g.reshape(1, N).astype(F32))


def _norm_proj_kernel(x_ref, g_ref, w_ref, y_ref, w_scr):
    @pl.when(pl.program_id(0) == 0)
    def _():
        w_scr[...] = w_ref[...].astype(BF16)

    x = x_ref[...]
    ms = jnp.mean(x * x, axis=-1, keepdims=True)
    h = (x * lax.rsqrt(ms + NORM_EPS) * g_ref[...]).astype(BF16)
    y_ref[...] = _dot(h, w_scr[...])


def norm_proj(x, g, w, layer, tm=320):
    M, D = x.shape
    N = w.shape[-1]
    tm = _row_tile(M, tm)
    return pl.pallas_call(
        _norm_proj_kernel,
        grid=(M // tm,),
        in_specs=[pl.BlockSpec((tm, D), lambda i: (i, 0)), pl.BlockSpec((1, D), lambda i: (0, 0)),
                  pl.BlockSpec((None, D, N), lambda i: (layer, 0, 0))],
        out_specs=pl.BlockSpec((tm, N), lambda i: (i, 0)),
        out_shape=jax.ShapeDtypeStruct((M, N), F32),
        scratch_shapes=[pltpu.VMEM((D, N), BF16)],
        compiler_params=_params("arbitrary"),
        name="norm_proj",
    )(x, g.reshape(1, D).astype(F32), w)


def _headnorm_kernel(x_ref, g_ref, o_ref, *, hd):
    x = x_ref[...]
    ms = _group_sum_bcast(x * x, hd) * (1.0 / hd)
    o_ref[...] = (x * lax.rsqrt(ms + NORM_EPS) * g_ref[...]).astype(o_ref.dtype)


def headnorm(x, col_block, width, g, hd, out_dtype, tm=512):
    M = x.shape[0]
    tm = _row_tile(M, tm)
    g_row = jnp.tile(g.astype(F32), width // hd).reshape(1, width)
    return pl.pallas_call(
        functools.partial(_headnorm_kernel, hd=hd),
        grid=(M // tm,),
        in_specs=[pl.BlockSpec((tm, width), lambda i: (i, col_block)),
                  pl.BlockSpec((1, width), lambda i: (0, 0))],
        out_specs=pl.BlockSpec((tm, width), lambda i: (i, 0)),
        out_shape=jax.ShapeDtypeStruct((M, width), out_dtype),
        compiler_params=_params("parallel"),
        name="headnorm",
    )(x, g_row)


def _lru_kernel(xr_ref, halo_ref, yg_ref, h0_ref, cw_ref, cb_ref, gaw_ref, gab_ref, gxw_ref, gxb_ref,
                lam_ref, o_ref, hl_ref, a_scr, u_scr, h_scr, *, n_pad, n_blocks, bw, scan_w):
    t = pl.program_id(1)
    tt, W = a_scr.shape

    @pl.when(t == 0)
    def _():
        h_scr[...] = h0_ref[0]

    x = xr_ref[...]
    prev = halo_ref[...] * jnp.where(t > 0, 1.0, 0.0)
    xc = cb_ref[...] + cw_ref[3:4, :] * x
    for s in (1, 2, 3):
        xc = xc + cw_ref[3 - s:4 - s, :] * _shift_rows(x, prev, s)

    nsp = -LRU_C * jax.nn.softplus(-lam_ref[...])
    if n_pad:
        live = (lax.broadcasted_iota(jnp.int32, (tt, 1), 0) >= n_pad) | (t > 0)
    for n in range(n_blocks):
        sl = slice(n * bw, (n + 1) * bw)
        xb = xc[:, sl]
        xb16 = xb.astype(BF16)
        r = jax.nn.sigmoid(_dot(xb16, gaw_ref[n]) + gab_ref[:, sl])
        i = jax.nn.sigmoid(_dot(xb16, gxw_ref[n]) + gxb_ref[:, sl])
        log_a = r * nsp[:, sl]
        a = jnp.exp(log_a)
        u = jnp.sqrt(jnp.maximum(-jnp.tanh(log_a) * (a * a + 1.0), 0.0)) * (i * xb)
        if n_pad:
            a = jnp.where(live, a, 1.0)
            u = jnp.where(live, u, 0.0)
        a_scr[:, sl] = a
        u_scr[:, sl] = u

    row8 = lax.broadcasted_iota(jnp.int32, (SUBLANES, 1), 0)
    for c in range(W // scan_w):
        cs = slice(c * scan_w, (c + 1) * scan_w)

        def body(gi, h, cs=cs):
            r0 = pl.multiple_of(gi * SUBLANES, SUBLANES)
            A = a_scr[pl.ds(r0, SUBLANES), cs]
            U = u_scr[pl.ds(r0, SUBLANES), cs]
            for s in (1, 2, 4):
                As = pltpu.roll(A, s, 0)
                Us = pltpu.roll(U, s, 0)
                m = row8 >= s
                U = jnp.where(m, A * Us + U, U)
                A = jnp.where(m, A * As, A)
            H = A * h + U
            u_scr[pl.ds(r0, SUBLANES), cs] = H
            return H[SUBLANES - 1:SUBLANES, :]

        h_end = lax.fori_loop(0, tt // SUBLANES, body, h_scr[:, cs])
        h_scr[:, cs] = h_end

    o_ref[...] = (u_scr[...] * jax.nn.gelu(yg_ref[...], approximate=True)).astype(o_ref.dtype)
    hl_ref[0] = h_scr[...]


def lru_mixer(xr, xr_cb, yg, yg_cb, h0, conv_w, conv_b, ga_w, ga_b, gx_w, gx_b, lam, *, B, T, n_pad, tt,
              out_rows=None, out_cols=None):
    W = conv_w.shape[1]
    nb, bw = ga_w.shape[0], ga_w.shape[1]
    tt = min(tt, T)
    nt = T // tt
    hb = tt // SUBLANES
    out_rows = B * T if out_rows is None else out_rows
    out_cols = W if out_cols is None else out_cols
    row = lambda v: v.reshape(1, W).astype(F32)
    kern = functools.partial(_lru_kernel, n_pad=n_pad, n_blocks=nb, bw=bw, scan_w=min(W, 512))
    return pl.pallas_call(
        kern,
        grid=(B, nt),
        in_specs=[
            pl.BlockSpec((tt, W), lambda b, t: (b * nt + t, xr_cb)),
            pl.BlockSpec((SUBLANES, W), lambda b, t: (jnp.maximum((b * nt + t) * hb - 1, 0), xr_cb)),
            pl.BlockSpec((tt, W), lambda b, t: (b * nt + t, yg_cb)),
            pl.BlockSpec((1, 1, W), lambda b, t: (b, 0, 0)),
            pl.BlockSpec((4, W), lambda b, t: (0, 0)),
            pl.BlockSpec((1, W), lambda b, t: (0, 0)),
            pl.BlockSpec((nb, bw, bw), lambda b, t: (0, 0, 0)),
            pl.BlockSpec((1, W), lambda b, t: (0, 0)),
            pl.BlockSpec((nb, bw, bw), lambda b, t: (0, 0, 0)),
            pl.BlockSpec((1, W), lambda b, t: (0, 0)),
            pl.BlockSpec((1, W), lambda b, t: (0, 0)),
        ],
        out_specs=[pl.BlockSpec((tt, W), lambda b, t: (b * nt + t, 0)),
                   pl.BlockSpec((1, 1, W), lambda b, t: (b, 0, 0))],
        out_shape=[jax.ShapeDtypeStruct((out_rows, out_cols), BF16), jax.ShapeDtypeStruct((B, 1, W), F32)],
        scratch_shapes=[pltpu.VMEM((tt, W), F32), pltpu.VMEM((tt, W), F32), pltpu.VMEM((1, W), F32)],
        compiler_params=_params("parallel", "arbitrary"),
        name="lru_mixer",
    )(xr, xr, yg, h0, conv_w.astype(F32), row(conv_b), ga_w.astype(BF16), row(ga_b),
      gx_w.astype(BF16), row(gx_b), row(lam))


def _swa_kernel(sink_ref, q_ref, kp_ref, kc_ref, vp_ref, vc_ref, *rest, j0, n_kv, group, hd, win):
    o_ref = rest[-1]
    j = pl.program_id(1) + j0
    q = q_ref[...]
    k2 = jnp.concatenate([kp_ref[...], kc_ref[...]], axis=0)
    v2 = jnp.concatenate([vp_ref[...], vc_ref[...]], axis=0)
    tq = q.shape[0]
    qi = lax.broadcasted_iota(jnp.int32, (tq, 2 * win), 0)
    kj = lax.broadcasted_iota(jnp.int32, (tq, 2 * win), 1)
    dist = qi + win - kj
    mask = (dist >= 0) & (dist < win) & ((j > 0) | (kj >= win))
    scale = hd ** -0.5
    for kvh in range(n_kv):
        kh = k2[:, kvh * hd:(kvh + 1) * hd]
        vh = v2[:, kvh * hd:(kvh + 1) * hd]
        hs_ = [kvh * group + g for g in range(group)]
        s = [jnp.where(mask, _nt(q[:, h * hd:(h + 1) * hd], kh) * scale, -jnp.inf) for h in hs_]
        m = [jnp.maximum(jnp.max(s[g], axis=-1, keepdims=True), sink_ref[hs_[g]]) for g in range(group)]
        p = [jnp.exp(s[g] - m[g]) for g in range(group)]
        den = [jnp.sum(p[g], axis=-1, keepdims=True) + jnp.exp(sink_ref[hs_[g]] - m[g]) for g in range(group)]
        outs = [_dot(p[g].astype(BF16), vh) / den[g] for g in range(group)]
        o_ref[:, hs_[0] * hd:(hs_[-1] + 1) * hd] = jnp.concatenate(outs, axis=-1).astype(o_ref.dtype)


def swa_attention(q, k, v, sink, *, B, nq, nkb, j0, win, hd, tq=None, into=None, out_cb=0):
    QW = q.shape[1]
    KW = k.shape[1]
    n_kv = KW // hd
    group = QW // KW
    tq = win if tq is None else tq
    kern = functools.partial(_swa_kernel, j0=j0, n_kv=n_kv, group=group, hd=hd, win=win)
    prev = lambda b, j: (b * nkb + jnp.maximum(j + j0 - 1, 0), 0)
    cur = lambda b, j: (b * nkb + j + j0, 0)
    in_specs = [pl.BlockSpec(memory_space=pltpu.SMEM),
                pl.BlockSpec((tq, QW), lambda b, j: (b * nq + j, 0)),
                pl.BlockSpec((win, KW), prev), pl.BlockSpec((win, KW), cur),
                pl.BlockSpec((win, KW), prev), pl.BlockSpec((win, KW), cur)]
    args = [sink.astype(F32), q, k, k, v, v]
    aliases = {}
    out_shape = jax.ShapeDtypeStruct((B * nq * tq, QW), BF16)
    if into is not None:
        aliases = {len(args): 0}
        in_specs.append(pl.BlockSpec(memory_space=pl.ANY))
        args.append(into)
        out_shape = jax.ShapeDtypeStruct(into.shape, into.dtype)
    return pl.pallas_call(
        kern,
        grid=(B, nq),
        in_specs=in_specs,
        out_specs=pl.BlockSpec((tq, QW), lambda b, j: (b * nq + j, out_cb)),
        out_shape=out_shape,
        input_output_aliases=aliases,
        compiler_params=_params("parallel", "parallel"),
        name="swa_attention",
    )(*args)


def _mem_attn_kernel(q_ref, g_ref, mk_ref, mv_ref, o_ref, *, n_heads, hd):
    q = q_ref[...]
    scale = hd ** -0.5
    head = (lambda ref, h: ref[0, :, h, :]) if len(mk_ref.shape) == 4 else \
           (lambda ref, h: ref[0, :, h * hd:(h + 1) * hd])
    for h in range(n_heads):
        sl = slice(h * hd, (h + 1) * hd)
        qh = q[:, sl]
        qn = (qh * lax.rsqrt(jnp.mean(qh * qh, axis=-1, keepdims=True) + NORM_EPS) * g_ref[...]).astype(BF16)
        s = _nt(qn, head(mk_ref, h).astype(BF16)) * scale
        m = jnp.max(s, axis=-1, keepdims=True)
        p = jnp.exp(s - m)
        p = p / jnp.sum(p, axis=-1, keepdims=True)
        o_ref[:, sl] = _dot(p.astype(BF16), head(mv_ref, h).astype(BF16)).astype(o_ref.dtype)


def mem_attention(q, qn_g, mk, mv, layer, *, B, T, n_heads, tt=256, out_rows=None):
    MW = q.shape[1]
    ML = mk.shape[2]
    hd = MW // n_heads
    tt = min(tt, T)
    nt = T // tt
    out_rows = B * T if out_rows is None else out_rows
    mem = pl.BlockSpec((None, 1) + mk.shape[2:], lambda b, t: (layer, b) + (0,) * (mk.ndim - 2))
    return pl.pallas_call(
        functools.partial(_mem_attn_kernel, n_heads=n_heads, hd=hd),
        grid=(B, nt),
        in_specs=[pl.BlockSpec((tt, MW), lambda b, t: (b * nt + t, 0)),
                  pl.BlockSpec((1, hd), lambda b, t: (0, 0)),
                  mem, mem],
        out_specs=pl.BlockSpec((tt, MW), lambda b, t: (b * nt + t, 0)),
        out_shape=jax.ShapeDtypeStruct((out_rows, MW), BF16),
        compiler_params=_params("parallel", "parallel"),
        name="mem_attention",
    )(q, qn_g.reshape(1, hd).astype(F32), mk, mv)


def _ffn_in_kernel(*refs, blocks_per_seq, sub, has_into, emit_w, step_len):
    x_ref, wg_ref, wu_ref, cw_ref, cb_ref = refs[:5]
    pos = 6 if has_into else 5
    if step_len:
        xs_ref, h1_ref, h2_ref = refs[pos:pos + 3]
        pos += 3
    o_ref, tail_ref = refs[pos], refs[pos + 1]
    carry_scr = refs[-1]
    i = pl.program_id(0)
    j = pl.program_id(1)
    tm = x_ref.shape[0]

    @pl.when(i % blocks_per_seq == 0)
    def _():
        carry_scr[j] = jnp.zeros(carry_scr.shape[1:], F32)

    prev = carry_scr[j]
    wg = wg_ref[...].astype(BF16)
    wu = wu_ref[...].astype(BF16)
    if emit_w:
        refs[pos + 2][...] = wg
        refs[pos + 3][...] = wu
    if step_len:
        acts_ref, gates_ref = refs[pos + 4], refs[pos + 5]
        xs = xs_ref[...]
        gs = _dot(xs, wg)
        tpos = lax.broadcasted_iota(jnp.int32, (gs.shape[0], 1), 0) % step_len
        g1 = jnp.where(tpos < 1, h1_ref[...], pltpu.roll(gs, 1, 0))
        g2 = jnp.where(tpos < 2, h2_ref[...], pltpu.roll(gs, 2, 0))
        gcs = cb_ref[...] + cw_ref[2:3, :] * gs + cw_ref[1:2, :] * g1 + cw_ref[0:1, :] * g2
        acts_ref[...] = (jax.nn.gelu(gcs, approximate=True) * _dot(xs, wu)).astype(acts_ref.dtype)
        gates_ref[...] = gs
    for c in range(tm // sub):
        rows = slice(c * sub, (c + 1) * sub)
        x = x_ref[rows, :]
        gate = _dot(x, wg)
        up = _dot(x, wu)
        gc = cb_ref[...] + cw_ref[2:3, :] * gate
        for s in (1, 2):
            gc = gc + cw_ref[2 - s:3 - s, :] * _shift_rows(gate, prev, s)
        o_ref[rows, :] = (jax.nn.gelu(gc, approximate=True) * up).astype(o_ref.dtype)
        prev = gate[sub - SUBLANES:, :]
    carry_scr[j] = prev
    tail_ref[0] = prev


def _ffn_in_call(x, wg, wu, layer, up_off, conv_w, conv_b, *, seq_len, row0, rows, tm, tn, sub, into, emit_w,
                 steps=None):
    M, D = x.shape
    FF = conv_w.shape[1]
    tm = min(tm, seq_len)
    bps = seq_len // tm
    nj = FF // tn
    i0 = row0 // tm
    uo = up_off // tn
    wspec = lambda off: (pl.BlockSpec((D, tn), lambda i, j: (0, j + off)) if layer is None else
                         pl.BlockSpec((None, D, tn), lambda i, j: (layer, 0, j + off)))
    in_specs = [pl.BlockSpec((tm, D), lambda i, j: (i + i0, 0), pipeline_mode=pl.Buffered(1)),
                wspec(0), wspec(uo),
                pl.BlockSpec((3, tn), lambda i, j: (0, j)),
                pl.BlockSpec((1, tn), lambda i, j: (0, j))]
    args = [x, wg, wu, conv_w.astype(F32), conv_b.reshape(1, FF).astype(F32)]
    aliases = {}
    if into is not None:
        aliases = {len(args): 0}
        in_specs.append(pl.BlockSpec(memory_space=pl.ANY))
        args.append(into)
    step_len = 0
    if steps is not None:
        xs, h1, h2, step_len = steps
        ms = xs.shape[0]
        assert emit_w and rows == tm
        in_specs += [pl.BlockSpec((ms, D), lambda i, j: (0, 0)),
                     pl.BlockSpec((ms, tn), lambda i, j: (0, j)), pl.BlockSpec((ms, tn), lambda i, j: (0, j))]
        args += [xs, h1, h2]
    out_specs = [pl.BlockSpec((tm, tn), lambda i, j: (i + i0, j)),
                 pl.BlockSpec((1, SUBLANES, tn), lambda i, j: (i, 0, j))]
    out_shape = [jax.ShapeDtypeStruct((M, FF), BF16),
                 jax.ShapeDtypeStruct((rows // tm, SUBLANES, FF), F32)]
    if emit_w:
        out_specs += [pl.BlockSpec((D, tn), lambda i, j: (0, j))] * 2
        out_shape += [jax.ShapeDtypeStruct((D, FF), BF16)] * 2
    if step_len:
        out_specs += [pl.BlockSpec((ms, tn), lambda i, j: (0, j))] * 2
        out_shape += [jax.ShapeDtypeStruct((ms, FF), BF16), jax.ShapeDtypeStruct((ms, FF), F32)]
    outs = pl.pallas_call(
        functools.partial(_ffn_in_kernel, blocks_per_seq=bps, sub=min(sub, tm), has_into=into is not None,
                          emit_w=emit_w, step_len=step_len),
        grid=(rows // tm, nj),
        in_specs=in_specs,
        out_specs=out_specs,
        out_shape=out_shape,
        input_output_aliases=aliases,
        scratch_shapes=[pltpu.VMEM((nj, SUBLANES, tn), F32)],
        compiler_params=_params("arbitrary", "arbitrary"),
        name="ffn_in_fused",
    )(*args)
    return (outs[0], outs[1][bps - 1::bps]) + tuple(outs[2:])


def ffn_in_fused(x, w_in, layer, conv_w, conv_b, steps, *, seq_len, rows=None, sub=256):
    M = x.shape[0] if rows is None else rows
    FF = conv_w.shape[1]
    act, tail0, wg, wu, act_s, gate_s = _ffn_in_call(
        x, w_in, w_in, layer, FF, conv_w, conv_b, seq_len=seq_len, row0=0, rows=seq_len, tm=2048, tn=256,
        sub=sub, into=None, emit_w=True, steps=steps)
    if M == seq_len:
        return act, tail0, act_s, gate_s
    act, tails = _ffn_in_call(x, wg, wu, None, 0, conv_w, conv_b, seq_len=seq_len, row0=seq_len,
                              rows=M - seq_len, tm=1024, tn=512, sub=sub, into=act, emit_w=False)
    return act, jnp.concatenate([tail0, tails], axis=0), act_s, gate_s


def _rwkv_mix_kernel(x_ref, halo_ref, s0_ref, g_ref, mu_ref, w1_ref, a1_ref, g1_ref, *out_refs, n_valid_last):
    t = pl.program_id(1)
    nt = pl.num_programs(1)
    xr_ref, xk_ref, xv_ref, lw_ref, la_ref, lg_ref, hl_ref = out_refs

    def norm(v):
        return v * lax.rsqrt(jnp.mean(v * v, axis=-1, keepdims=True) + NORM_EPS) * g_ref[...]

    h = norm(x_ref[...])
    hp = norm(halo_ref[...])
    first = jnp.where(t > 0, 1.0, 0.0)
    hp = hp * first + jnp.broadcast_to(s0_ref[0], hp.shape) * (1.0 - first)
    xx = _shift_rows(h, hp, 1) - h
    mix = lambda j: (h + xx * mu_ref[j:j + 1, :]).astype(BF16)
    xr_ref[...] = mix(0)
    xk_ref[...] = mix(2)
    xv_ref[...] = mix(3)
    lw_ref[...] = jnp.tanh(_dot(mix(1), w1_ref[...])).astype(BF16)
    la_ref[...] = _dot(mix(4), a1_ref[...]).astype(BF16)
    lg_ref[...] = jax.nn.sigmoid(_dot(mix(5), g1_ref[...])).astype(BF16)

    @pl.when(t == nt - 1)
    def _():
        hl_ref[0] = h[n_valid_last - 1:n_valid_last, :]


def rwkv_mix(x, shift0, g, mu, w1, a1, g1, *, B, T, n_valid, tt=256, out_rows=None):
    D = x.shape[1]
    tt = min(tt, T)
    nt = T // tt
    hb = tt // SUBLANES
    out_rows = B * T if out_rows is None else out_rows
    n_valid_last = n_valid - (nt - 1) * tt
    blk = pl.BlockSpec((tt, D), lambda b, t: (b * nt + t, 0))
    one = pl.BlockSpec((1, 1, D), lambda b, t: (b, 0, 0))
    return pl.pallas_call(
        functools.partial(_rwkv_mix_kernel, n_valid_last=n_valid_last),
        grid=(B, nt),
        in_specs=[blk,
                  pl.BlockSpec((SUBLANES, D), lambda b, t: (jnp.maximum((b * nt + t) * hb - 1, 0), 0)),
                  one,
                  pl.BlockSpec((1, D), lambda b, t: (0, 0)),
                  pl.BlockSpec((6, D), lambda b, t: (0, 0))] +
                 [pl.BlockSpec(w.shape, lambda b, t: (0, 0)) for w in (w1, a1, g1)],
        out_specs=[blk] * 3 + [pl.BlockSpec((tt, w.shape[1]), lambda b, t: (b * nt + t, 0)) for w in (w1, a1, g1)]
                  + [one],
        out_shape=[jax.ShapeDtypeStruct((out_rows, D), BF16)] * 3 +
                  [jax.ShapeDtypeStruct((out_rows, w.shape[1]), BF16) for w in (w1, a1, g1)] +
                  [jax.ShapeDtypeStruct((B, 1, D), F32)],
        compiler_params=_params("parallel", "arbitrary"),
        name="rwkv_mix",
    )(x, x, shift0, g.reshape(1, D).astype(F32), mu.astype(F32), w1, a1, g1)


def _rwkv_chunk_kernel(r_ref, k_ref, v_ref, lw_ref, la_ref, lg_ref, w2_ref, a2_ref, g2_ref, w0_ref, a0_ref,
                       kk_ref, ka_ref, rk_ref, lng_ref, lnb_ref, s0_ref, o_ref, so_ref, s_scr, *, hs, C, n_valid):
    c = pl.program_id(2)
    nc = pl.num_programs(2)
    R, L = r_ref.shape
    n_pairs = L // LANES

    @pl.when(c == 0)
    def _():
        z = jnp.zeros((hs, hs), F32)
        for p in range(n_pairs):
            s_scr[p] = jnp.concatenate([jnp.concatenate([s0_ref[0, 2 * p], z], axis=1),
                                        jnp.concatenate([z, s0_ref[0, 2 * p + 1]], axis=1)], axis=0)

    ti = lax.broadcasted_iota(jnp.int32, (C, C), 0)
    si = lax.broadcasted_iota(jnp.int32, (C, C), 1)
    tri = jnp.where(ti >= si, 1.0, 0.0).astype(BF16)
    lane = lax.broadcasted_iota(jnp.int32, (1, LANES), 1)
    m0 = lane < hs
    C2 = 2 * C
    ri = lax.broadcasted_iota(jnp.int32, (C2, C2), 0)
    ci = lax.broadcasted_iota(jnp.int32, (C2, C2), 1)
    same = (ri // C) == (ci // C)
    strict = same & ((ri % C) > (ci % C))
    lower = (ri % C) >= (ci % C)
    vi = lax.broadcasted_iota(jnp.int32, (LANES, LANES), 0) // hs
    vj = lax.broadcasted_iota(jnp.int32, (LANES, LANES), 1) // hs
    blockdiag = vi == vj
    n_steps = int(math.log2(C))
    wide = C2 % LANES == 0
    P = range(n_pairs)
    sls = [slice(p * LANES, (p + 1) * LANES) for p in P]
    inv = 1.0 / hs

    def stack_heads(x):
        return jnp.concatenate([jnp.where(m0, x, 0.0), jnp.where(m0, 0.0, x)], axis=0).astype(BF16)

    S = [s_scr[p] for p in P]
    for sc in range(R // C):
        rows = slice(sc * C, (sc + 1) * C)
        r = r_ref[rows, :]
        k = k_ref[rows, :]
        v = v_ref[rows, :]
        wp = w0_ref[...] + _dot(lw_ref[rows, :], w2_ref[...])
        logw = -jnp.exp(-jax.nn.softplus(-wp) - 0.5)
        if n_valid is not None:
            logw = jnp.where(lax.broadcasted_iota(jnp.int32, (C, 1), 0) + sc * C < n_valid, logw, 0.0)
        a = jax.nn.sigmoid(a0_ref[...] + _dot(la_ref[rows, :], a2_ref[...]))
        kk = k * kk_ref[...]
        kk = kk / jnp.maximum(jnp.sqrt(_group_sum_bcast(kk * kk, hs, terms=1)), 1e-12)
        k2 = k * (1.0 + (a - 1.0) * ka_ref[...])
        bm = kk * a
        cum = _dot_exact_rhs_lhs(tri, logw)
        e_in = jnp.exp(cum)
        e_out = jnp.exp(-cum)
        e_end = jnp.exp(cum[C - 1:C, :] - cum)
        g_end = jnp.exp(cum[C - 1:C, :])
        rt = r * e_in
        kkt = kk * jnp.exp(cum - logw)
        bh = bm * e_out
        kh = k2 * e_out
        bbar = bm * e_end
        kbar = k2 * e_end

        S16 = [s_.astype(BF16) for s_ in S]
        V16 = [v[:, sl].astype(BF16) for sl in sls]
        lhk = [stack_heads(kkt[:, sl]) for sl in sls]
        if wide:
            gbk = [_nt(lhk[p], jnp.concatenate([stack_heads(bh[:, sls[p]]), stack_heads(kh[:, sls[p]])], axis=0))
                   for p in P]
            nmat = [jnp.where(strict, -g_[:, :C2], 0.0) for g_ in gbk]
            auk = [jnp.where(strict, -g_[:, C2:], 0.0).astype(BF16) for g_ in gbk]
        else:
            nmat = [jnp.where(strict, -_nt(lhk[p], stack_heads(bh[:, sls[p]])), 0.0) for p in P]
            auk = [jnp.where(strict, -_nt(lhk[p], stack_heads(kh[:, sls[p]])), 0.0).astype(BF16) for p in P]
        sprod = [_nt(jnp.concatenate([lhk[p], rt[:, sls[p]].astype(BF16)], axis=0), S16[p]) for p in P]
        u = [_dot(auk[p], jnp.concatenate([V16[p], V16[p]], axis=0)) - sprod[p][:C2] for p in P]
        npow = [n_.astype(BF16) for n_ in nmat]
        for it in range(n_steps):
            last = it + 1 == n_steps
            if wide and not last:
                res = [_dot(npow[p], jnp.concatenate([u[p].astype(BF16), npow[p]], axis=1)) for p in P]
                u = [u[p] + res[p][:, :LANES] for p in P]
                npow = [res[p][:, LANES:].astype(BF16) for p in P]
            else:
                u = [u[p] + _dot(npow[p], u[p].astype(BF16)) for p in P]
                if not last:
                    npow = [_dot(npow[p], npow[p]).astype(BF16) for p in P]
        uv = [jnp.concatenate([jnp.where(m0, u[p][:C], u[p][C:]).astype(BF16), V16[p]], axis=0) for p in P]
        ar = [jnp.where(lower, _nt(stack_heads(rt[:, sls[p]]),
                                   jnp.concatenate([bh[:, sls[p]], kh[:, sls[p]]], axis=0).astype(BF16)),
                        0.0).astype(BF16) for p in P]
        tmat = [_dot(ar[p], uv[p]) for p in P]
        ys = [sprod[p][C2:] + jnp.where(m0, tmat[p][:C], tmat[p][C:]) for p in P]
        upd = [_tn(uv[p], jnp.concatenate([bbar[:, sls[p]], kbar[:, sls[p]]], axis=0).astype(BF16)) for p in P]
        S = [S[p] * g_end[:, sls[p]] + jnp.where(blockdiag, upd[p], 0.0) for p in P]

        y = ys[0] if n_pairs == 1 else jnp.concatenate(ys, axis=-1)
        mean = _group_sum_bcast(y, hs, terms=1) * inv
        yc = y - mean
        var = _group_sum_bcast(yc * yc, hs, terms=1) * inv
        yn = yc * lax.rsqrt(var + RW_GN_EPS) * lng_ref[...] + lnb_ref[...]
        bonus = _group_sum_bcast(r * k2 * rk_ref[...], hs, terms=1) * v
        o_ref[rows, :] = ((yn + bonus) * _dot(lg_ref[rows, :], g2_ref[...])).astype(o_ref.dtype)

    for p in P:
        s_scr[p] = S[p]

    @pl.when(c == nc - 1)
    def _():
        for p in P:
            so_ref[0, 2 * p] = S[p][:hs, :hs]
            so_ref[0, 2 * p + 1] = S[p][hs:, hs:]


def _dot_exact_rhs_lhs(m_bf16, x):
    hi = x.astype(BF16)
    lo = (x - hi.astype(F32)).astype(BF16)
    return _dot(m_bf16, hi) + _dot(m_bf16, lo)


def rwkv_chunked(r, k, v, lw, la, lg, w2, a2, g2, w0, a0, k_k, k_a, r_k, ln_g, ln_b, s0, *, B, T, hs, chunk,
                 chunks_per_step=1, lanes_per_step=1024, out_rows=None, n_valid=None):
    D = r.shape[1]
    L = min(lanes_per_step, D)
    npg = L // LANES
    rows = chunk * chunks_per_step
    nc = T // rows
    out_rows = B * T if out_rows is None else out_rows
    seq = pl.BlockSpec((rows, L), lambda b, hg, c: (b * nc + c, hg))
    low = lambda a_: pl.BlockSpec((rows, a_.shape[1]), lambda b, hg, c: (b * nc + c, 0))
    wgt = lambda w_: pl.BlockSpec((w_.shape[0], L), lambda b, hg, c: (0, hg))
    par = pl.BlockSpec((1, L), lambda b, hg, c: (0, hg))
    st = pl.BlockSpec((1, 2 * npg, hs, hs), lambda b, hg, c: (b, hg, 0, 0))
    row = lambda x: x.reshape(1, D).astype(F32)
    return pl.pallas_call(
        functools.partial(_rwkv_chunk_kernel, hs=hs, C=chunk, n_valid=n_valid),
        grid=(B, D // L, nc),
        in_specs=[seq] * 3 + [low(lw), low(la), low(lg), wgt(w2), wgt(a2), wgt(g2)] + [par] * 7 + [st],
        out_specs=[seq, st],
        out_shape=[jax.ShapeDtypeStruct((out_rows, D), BF16),
                   jax.ShapeDtypeStruct(s0.shape, F32)],
        scratch_shapes=[pltpu.VMEM((npg, LANES, LANES), F32)],
        compiler_params=_params("parallel", "parallel", "arbitrary"),
        name="rwkv_chunked",
    )(r, k, v, lw, la, lg, w2, a2, g2, row(w0), row(a0), row(k_k), row(k_a), row(r_k), row(ln_g), row(ln_b), s0)


def _pad_rows(x, front, back):
    return jnp.pad(x, ((0, 0), (front, back), (0, 0)))


def kernel(x_prompt, x_sample, mem_prompt, state_lru_conv, state_lru_h, cache_swa_k, cache_swa_v,
           state_rwkv_shift, state_rwkv_wkv, cache_mem_k, cache_mem_v, state_ffn_conv,
           a_norm_g, a_w_in, a_conv_w, a_conv_b, a_gate_a_w, a_gate_a_b, a_gate_x_w, a_gate_x_b,
           a_lambda, b_q_norm_g, b_k_norm_g, b_sink, a_w_out,
           c_norm_g, c_mu, c_w_r, c_w_k, c_w_v, c_w_o, c_w0, c_w1, c_w2, c_a0, c_a1, c_a2,
           c_g1, c_g2, c_k_k, c_k_a, c_r_k, c_ln_g, c_ln_b,
           m_norm_g, m_mem_norm_g, m_w_q, m_w_kv, m_q_norm_g, m_k_norm_g, m_w_o,
           f_norm_g, f_w_in, f_conv_w, f_conv_b, f_w_out):
    D = x_prompt.shape[-1]
    depth = m_norm_g.shape[0]
    W = a_conv_w.shape[-1]
    KA = a_conv_w.shape[1]
    hd = b_q_norm_g.shape[-1]
    n_q = b_sink.shape[-1]
    n_kv = cache_swa_k.shape[3]
    win = cache_swa_k.shape[2]
    QW, KW = n_q * hd, n_kv * hd
    hs = c_r_k.shape[-1]
    n_rw = c_r_k.shape[1]
    m_heads, m_hd = cache_mem_k.shape[3], cache_mem_k.shape[4]
    MW = m_heads * m_hd
    FF = f_conv_w.shape[-1]
    KF = f_conv_w.shape[1]
    bf = lambda w: w.astype(BF16)
    wcache = {}

    Bp, S = x_prompt.shape[:2]
    Bs, Ts = x_sample.shape[:2]
    Mp, Ms = Bp * S, Bs * Ts
    Mt = Mp + Ms
    tma = _row_tile(Mt, 1280)
    n_a = a_norm_g.shape[0]
    n_c = c_norm_g.shape[0]
    ML = mem_prompt.shape[1]
    PAD8, PAD16 = SUBLANES, 2 * SUBLANES

    def mm(xin, w, layer, key, **kw):
        return dense(xin, w, layer, wcache, (key, layer), tm=tma, **kw)

    def seq_tail(arr, n, cols=slice(None)):
        return jnp.stack([arr[(b + 1) * S - n:(b + 1) * S, cols] for b in range(Bp)])

    def put_sample(buf, rows_s):
        return lax.dynamic_update_slice(buf, rows_s.astype(buf.dtype), (Mp, 0))

    mem_flat = mem_prompt.reshape(Bp * ML, D)
    mks, mvs = [], []
    for l in range(depth):
        mn = rmsnorm_rows(mem_flat, m_mem_norm_g[l])
        kv = matmul(mn, m_w_kv, layer=l)
        mk = headnorm(kv, 0, MW, m_k_norm_g[l], m_hd, F32)
        mks.append(mk.reshape(Bp, ML, MW))
        mvs.append(kv[:, MW:].reshape(Bp, ML, MW))
    mem_k_p, mem_v_p = jnp.stack(mks), jnp.stack(mvs)
    p_mem_k = mem_k_p.reshape(depth, Bp, ML, m_heads, m_hd)
    p_mem_v = mem_v_p.reshape(depth, Bp, ML, m_heads, m_hd)

    xp2, xs2 = x_prompt.reshape(Mp, D), x_sample.reshape(Ms, D)
    xf = None

    def first_norm(g):
        return put_sample(rmsnorm_rows(xp2, g, out_rows=Mt), rmsnorm_rows(xs2, g))

    def first_residual(xin, w, layer, key):
        y = dense(xin, w, layer, wcache, (key, layer), rows=Mp, out_rows=Mt, res=xp2)
        return matmul(xin, wcache[(key, layer)], row0=Mp, rows=Ms, res=xs2, res_row0=0, into=y)

    po ={k_: [] for k_ in ("lc", "lh", "sk", "sv", "rs", "rw", "fc")}
    so = {k_: [] for k_ in ("lc", "lh", "sk", "sv", "rs", "rw", "fc")}
    ia = ic = 0
    y_p = y_s = None
    for l in range(depth):
        if l % 2 == 0:
            i = ia
            ia += 1
            assert W == QW
            h = first_norm(a_norm_g[i]) if xf is None else rmsnorm_rows(xf, a_norm_g[i])
            zz = mm(h, a_w_in, i, "a_w_in_rg", n=2 * W)
            qkv = mm(h, a_w_in, i, "a_w_in_qkv", n0=2 * W)
            qn = headnorm(qkv, 0, QW, b_q_norm_g[i], hd, BF16)
            kn = headnorm(qkv, QW // KW, KW, b_k_norm_g[i], hd, F32)
            vv = qkv[:, QW + KW:]
            lru_w = (a_conv_w[i], a_conv_b[i], a_gate_a_w[i], a_gate_a_b[i], a_gate_x_w[i], a_gate_x_b[i],
                     a_lambda[i])
            mix, hl_p = lru_mixer(zz, 0, zz, 1, jnp.zeros((Bp, 1, W), F32), *lru_w, B=Bp, T=S, n_pad=0, tt=256,
                                  out_rows=Mt, out_cols=W + QW)
            mix = swa_attention(qn, bf(kn), bf(vv), b_sink[i], B=Bp, nq=S // win, nkb=S // win, j0=0, win=win,
                                hd=hd, into=mix, out_cb=W // QW)
            po["lc"].append(seq_tail(zz, KA - 1, slice(0, W)))
            po["lh"].append(hl_p.reshape(Bp, W))
            po["sk"].append(seq_tail(kn, win).reshape(Bp, win, n_kv, hd))
            po["sv"].append(seq_tail(vv, win).reshape(Bp, win, n_kv, hd))
            zz_s = zz[Mp:].reshape(Bs, Ts, 2 * W)
            n_pad = PAD8 - Ts
            xr_hist = jnp.concatenate([state_lru_conv[i].astype(F32), zz_s[:, :, :W]], axis=1)
            xr_p = _pad_rows(xr_hist, PAD8 - xr_hist.shape[1], 0).reshape(Bs * PAD8, W)
            yg_p = _pad_rows(zz_s[:, :, W:], n_pad, 0).reshape(Bs * PAD8, W)
            oa_s, hl_s = lru_mixer(xr_p, 0, yg_p, 0, state_lru_h[i].reshape(Bs, 1, W).astype(F32), *lru_w,
                                   B=Bs, T=PAD8, n_pad=n_pad, tt=PAD8)
            oa_s = oa_s.reshape(Bs, PAD8, W)[:, n_pad:].reshape(Ms, W)
            kc = cache_swa_k[i].reshape(Bs, win, KW)
            vc = cache_swa_v[i].reshape(Bs, win, KW)
            k_all = jnp.concatenate([kc, kn[Mp:].reshape(Bs, Ts, KW)], axis=1)
            v_all = jnp.concatenate([vc, vv[Mp:].reshape(Bs, Ts, KW)], axis=1)
            q_s = _pad_rows(qn[Mp:].reshape(Bs, Ts, QW), 0, PAD16 - Ts).reshape(Bs * PAD16, QW)
            o_s = swa_attention(q_s, bf(_pad_rows(k_all, 0, win - Ts)).reshape(Bs * 2 * win, KW),
                                bf(_pad_rows(v_all, 0, win - Ts)).reshape(Bs * 2 * win, KW), b_sink[i],
                                B=Bs, nq=1, nkb=2, j0=1, win=win, hd=hd, tq=PAD16)
            o_s = o_s.reshape(Bs, PAD16, QW)[:, :Ts].reshape(Ms, QW)
            mix = put_sample(mix, jnp.concatenate([oa_s, o_s], axis=-1))
            so["lc"].append(xr_hist[:, Ts:])
            so["lh"].append(hl_s.reshape(Bs, W))
            so["sk"].append(k_all[:, Ts:].reshape(Bs, win, n_kv, hd))
            so["sv"].append(v_all[:, Ts:].reshape(Bs, win, n_kv, hd))
            xf = (first_residual(mix, a_w_out, i, "a_w_out") if xf is None
                  else mm(mix, a_w_out, i, "a_w_out", res=xf))
        else:
            i = ic
            ic += 1
            gl = c_g1[i].shape[1]
            glp = -(-gl // LANES) * LANES
            lora1 = (bf(c_w1[i]), bf(c_a1[i]), bf(jnp.pad(c_g1[i], ((0, 0), (0, glp - gl)))))
            lora2 = (bf(c_w2[i]), bf(c_a2[i]), bf(jnp.pad(c_g2[i], ((0, glp - gl), (0, 0)))), c_w0[i], c_a0[i])
            mixes = rwkv_mix(xf, jnp.zeros((Bp, 1, D), F32), c_norm_g[i], c_mu[i], *lora1, B=Bp, T=S, n_valid=S,
                             out_rows=Mt)
            x_s = _pad_rows(xf[Mp:].reshape(Bs, Ts, D), 0, PAD8 - Ts).reshape(Bs * PAD8, D)
            mixes_s = rwkv_mix(x_s, state_rwkv_shift[i].reshape(Bs, 1, D).astype(F32), c_norm_g[i], c_mu[i], *lora1,
                               B=Bs, T=PAD8, n_valid=Ts, tt=PAD8)
            take = lambda m: m.reshape(Bs, PAD8, m.shape[-1])[:, :Ts].reshape(Ms, m.shape[-1])
            xr, xk, xv, lw, la, lg = [put_sample(mixes[j], take(mixes_s[j])) for j in range(6)]
            po["rs"].append(mixes[6].reshape(Bp, D))
            so["rs"].append(mixes_s[6].reshape(Bs, D))
            r = mm(xr, c_w_r, i, "c_w_r")
            k = mm(xk, c_w_k, i, "c_w_k")
            v = mm(xv, c_w_v, i, "c_w_v")
            rw_par = (c_k_k[i], c_k_a[i], c_r_k[i], c_ln_g[i], c_ln_b[i])
            yo, s_end = rwkv_chunked(r, k, v, lw, la, lg, *lora2, *rw_par, jnp.zeros((Bp, n_rw, hs, hs), F32),
                                     B=Bp, T=S, hs=hs, chunk=64, chunks_per_step=4, out_rows=Mt)
            po["rw"].append(s_end)
            cs = PAD16
            sq = lambda t: jnp.pad(t[Mp:].reshape(Bs, Ts, t.shape[-1]),
                                   ((0, 0), (0, cs - Ts), (0, 0))).reshape(Bs * cs, t.shape[-1])
            yo_s, s_end_s = rwkv_chunked(sq(r), sq(k), sq(v), sq(lw), sq(la), sq(lg), *lora2, *rw_par,
                                         state_rwkv_wkv[i].astype(F32), B=Bs, T=cs, hs=hs, chunk=cs,
                                         lanes_per_step=2048, n_valid=Ts)
            yo = put_sample(yo, yo_s.reshape(Bs, cs, D)[:, :Ts].reshape(Ms, D))
            so["rw"].append(s_end_s)
            xf = mm(yo, c_w_o, i, "c_w_o", res=xf)
        q = norm_proj(xf, m_norm_g[l], m_w_q, l)
        om = mem_attention(q, m_q_norm_g[l], mem_k_p, mem_v_p, l, B=Bp, T=S, n_heads=m_heads, out_rows=Mt)
        q_s = _pad_rows(q[Mp:].reshape(Bs, Ts, MW), 0, PAD8 - Ts).reshape(Bs * PAD8, MW)
        om_s = mem_attention(q_s, m_q_norm_g[l], cache_mem_k, cache_mem_v, l, B=Bs, T=PAD8, n_heads=m_heads, tt=PAD8)
        om = put_sample(om, om_s.reshape(Bs, PAD8, MW)[:, :Ts].reshape(Ms, MW))
        xf, hf = proj_res_norm(om, m_w_o, l, xf, f_norm_g[l])
        assert Ts >= KF - 1 == 2
        st_f = state_ffn_conv[l].astype(F32)
        zrow = jnp.zeros((Bs, Ts - 1, FF), F32)
        h1 = jnp.concatenate([st_f[:, 1:2], zrow], axis=1).reshape(Ms, FF)
        h2 = jnp.concatenate([st_f, zrow[:, 1:]], axis=1).reshape(Ms, FF)
        act, tail, act_s, gate_s = ffn_in_fused(hf, f_w_in, l, f_conv_w[l], f_conv_b[l], (hf[Mp:], h1, h2, Ts),
                                                seq_len=S, rows=Mp)
        po["fc"].append(tail[:, SUBLANES - (KF - 1):, :])
        act = put_sample(act, act_s)
        so["fc"].append(gate_s.reshape(Bs, Ts, FF)[:, Ts - (KF - 1):])
        if l + 1 < depth:
            xf = mm(act, f_w_out, l, "f_w_out", res=xf, tn=1024, tk=2048)
        else:
            y_p = dense(act, f_w_out, l, wcache, ("f_w_out", l), rows=Mp, out_rows=Mp, res=xf, tn=1024, tk=2048)
            y_s = matmul(act, wcache[("f_w_out", l)], row0=Mp, rows=Ms, out_rows=Ms, out_row0=0, res=xf,
                         tn=1024, tk=2048)
    st = jnp.stack
    return (y_p.reshape(Bp, S, D), y_s.reshape(Bs, Ts, D),
            st(po["lc"]), st(po["lh"]), st(po["sk"]), st(po["sv"]), st(po["rs"]), st(po["rw"]),
            p_mem_k, p_mem_v, st(po["fc"]),
            st(so["lc"]), st(so["lh"]), st(so["sk"]), st(so["sv"]), st(so["rs"]), st(so["rw"]), st(so["fc"]))
```

```python
import functools
import math

import jax
import jax.numpy as jnp
from jax import lax
from jax.experimental import pallas as pl
from jax.experimental.pallas import tpu as pltpu

F32 = jnp.float32
BF16 = jnp.bfloat16

NORM_EPS = 1e-6
RW_GN_EPS = 64e-5
LRU_C = 8.0
LANES = 128
SUBLANES = 8
VMEM_LIMIT_BYTES = 56 * 1024 * 1024


def _params(*sem):
    return pltpu.CompilerParams(dimension_semantics=sem, vmem_limit_bytes=VMEM_LIMIT_BYTES)


def _nt(a, b):
    return lax.dot_general(a, b, (((1,), (1,)), ((), ())), preferred_element_type=F32)


def _tn(a, b):
    return lax.dot_general(a, b, (((0,), (0,)), ((), ())), preferred_element_type=F32)


def _dot(a, b):
    return jnp.dot(a, b, preferred_element_type=F32)


def _group_sum_bcast(x, width, terms=2):
    m, L = x.shape
    n = L // LANES
    if width == LANES:
        parts = []
        for c in range(n):
            s = jnp.sum(x[:, c * LANES:(c + 1) * LANES], axis=-1, keepdims=True)
            parts.append(jnp.broadcast_to(s, (m, LANES)))
        return parts[0] if n == 1 else jnp.concatenate(parts, axis=-1)
    li = lax.broadcasted_iota(jnp.int32, (LANES, LANES), 0) // width
    lj = lax.broadcasted_iota(jnp.int32, (LANES, LANES), 1) // width
    e = jnp.where(li == lj, 1.0, 0.0).astype(BF16)
    xs = x if n == 1 else jnp.concatenate([x[:, c * LANES:(c + 1) * LANES] for c in range(n)], axis=0)
    hi = xs.astype(BF16)
    out = _dot(hi, e)
    if terms > 1:
        out = out + _dot((xs - hi.astype(F32)).astype(BF16), e)
    return out if n == 1 else jnp.concatenate([out[c * m:(c + 1) * m] for c in range(n)], axis=-1)


def _shift_rows(x, prev8, s):
    rolled = pltpu.roll(x, s, 0)
    top = jnp.where(lax.broadcasted_iota(jnp.int32, (SUBLANES, 1), 0) < s,
                    pltpu.roll(prev8, s, 0), rolled[0:SUBLANES])
    if x.shape[0] == SUBLANES:
        return top
    return jnp.concatenate([top, rolled[SUBLANES:]], axis=0)


def _rmsnorm_kernel(x_ref, g_ref, o_ref):
    x = x_ref[...]
    ms = jnp.mean(x * x, axis=-1, keepdims=True)
    o_ref[...] = (x * lax.rsqrt(ms + NORM_EPS) * g_ref[...]).astype(o_ref.dtype)


def _row_tile(m, target):
    best = None
    for d in range(16, min(m, target) + 1, 16):
        if m % d == 0:
            best = d
    return best or m


def rmsnorm_rows(x, g, out_dtype=BF16, tm=512, out_rows=None):
    M, D = x.shape
    tm = _row_tile(M, tm)
    return pl.pallas_call(
        _rmsnorm_kernel,
        grid=(M // tm,),
        in_specs=[pl.BlockSpec((tm, D), lambda i: (i, 0)), pl.BlockSpec((1, D), lambda i: (0, 0))],
        out_specs=pl.BlockSpec((tm, D), lambda i: (i, 0)),
        out_shape=jax.ShapeDtypeStruct((M if out_rows is None else out_rows, D), out_dtype),
        compiler_params=_params("parallel"),
        name="rmsnorm_rows",
    )(x, g.reshape(1, D))


def _mm_kernel(*refs, nk, has_bias, has_res, has_into, emit_w, in_place, act):
    x_ref, w_ref = refs[0], refs[1]
    pos = 2
    b_ref = r_ref = wb_ref = acc_ref = None
    if has_bias:
        b_ref = refs[pos]
        pos += 1
    if has_res:
        r_ref = refs[pos]
        pos += 1
    if has_into:
        pos += 1
    o_ref = refs[pos]
    pos += 1
    if emit_w:
        wb_ref = refs[pos]
        pos += 1
    if nk > 1 and not in_place:
        acc_ref = refs[pos]

    def weights(cs=slice(None)):
        w16 = w_ref[:, cs].astype(BF16)
        if emit_w:
            wb_ref[:, cs] = w16
        return w16

    def epilogue(y):
        if has_bias:
            y = y + b_ref[...]
        if act == "tanh":
            y = jnp.tanh(y)
        elif act == "sigmoid":
            y = jax.nn.sigmoid(y)
        if has_res:
            y = y + r_ref[...]
        o_ref[...] = y.astype(o_ref.dtype)

    if nk > 1 and in_place:
        k = pl.program_id(2)
        tn = o_ref.shape[1]
        cw = min(tn, 2 * LANES)

        def sweep(first):
            x = x_ref[...]
            for c in range(tn // cw):
                cs = slice(c * cw, (c + 1) * cw)
                part = _dot(x, weights(cs))
                if not first:
                    o_ref[:, cs] += part
                elif has_res:
                    o_ref[:, cs] = part + r_ref[:, cs]
                else:
                    o_ref[:, cs] = part

        pl.when(k == 0)(functools.partial(sweep, True))
        pl.when(k > 0)(functools.partial(sweep, False))
        return

    part = _dot(x_ref[...], weights())
    if nk == 1:
        epilogue(part)
        return
    k = pl.program_id(2)

    @pl.when(k == 0)
    def _():
        acc_ref[...] = part

    @pl.when(k > 0)
    def _():
        acc_ref[...] += part

    @pl.when(k == nk - 1)
    def _():
        epilogue(acc_ref[...])


def _pick(n, prefs):
    for p in prefs:
        if n % p == 0:
            return p
    return n


def matmul(x, w, *, layer=None, n0=0, n=None, bias=None, res=None, act=None, out_dtype=F32,
           tm=1024, tn=512, tk=4096, row0=0, rows=None, into=None, emit_w=False, out_rows=None, out_row0=None,
           res_row0=None):
    M, K = x.shape
    N = w.shape[-1] - n0 if n is None else n
    rows = M - row0 if rows is None else rows
    tm = min(tm, rows)
    out_rows = M if out_rows is None else out_rows
    o0 = (row0 if out_row0 is None else out_row0) // tm
    tn = _pick(math.gcd(N, n0) if n0 else N, (tn, 512, 256, 128))
    tk = _pick(K, (tk, 2048, 1024, 512))
    nk = K // tk
    j0 = n0 // tn
    i0 = row0 // tm
    assert row0 % tm == 0 and rows % tm == 0 and (not emit_w or rows == tm)
    in_specs = [pl.BlockSpec((tm, tk), lambda i, j, k: (i + i0, k)),
                pl.BlockSpec((tk, tn), lambda i, j, k: (k, j + j0)) if layer is None else
                pl.BlockSpec((None, tk, tn), lambda i, j, k: (layer, k, j + j0))]
    args = [x, w]
    if bias is not None:
        in_specs.append(pl.BlockSpec((1, tn), lambda i, j, k: (0, j)))
        args.append(bias.reshape(1, N).astype(F32))
    if res is not None:
        r0 = i0 if res_row0 is None else res_row0 // tm
        in_specs.append(pl.BlockSpec((tm, tn), lambda i, j, k: (i + r0, j)))
        args.append(res)
    aliases = {}
    if into is not None:
        aliases = {len(args): 0}
        in_specs.append(pl.BlockSpec(memory_space=pl.ANY))
        args.append(into)
    in_place = bias is None and act is None and out_dtype == F32
    kern = functools.partial(_mm_kernel, nk=nk, has_bias=bias is not None, has_res=res is not None,
                             has_into=into is not None, emit_w=emit_w, in_place=in_place, act=act)
    out_specs = [pl.BlockSpec((tm, tn), lambda i, j, k: (i + o0, j))]
    out_shape = [jax.ShapeDtypeStruct((out_rows, N), out_dtype)]
    if emit_w:
        out_specs.append(pl.BlockSpec((tk, tn), lambda i, j, k: (k, j)))
        out_shape.append(jax.ShapeDtypeStruct((K, N), BF16))
    outs = pl.pallas_call(
        kern,
        grid=(rows // tm, N // tn, nk),
        in_specs=in_specs,
        out_specs=out_specs,
        out_shape=out_shape,
        input_output_aliases=aliases,
        scratch_shapes=[pltpu.VMEM((tm, tn), F32)] if (nk > 1 and not in_place) else [],
        compiler_params=_params("parallel", "parallel", "arbitrary"),
        name="matmul",
    )(*args)
    return outs if emit_w else outs[0]


def dense(x, w, layer, wcache, key, **kw):
    rows = kw.pop("rows", x.shape[0])
    tm = min(kw.get("tm", 1024), rows)
    first, wb = matmul(x, w, layer=layer, rows=tm, emit_w=True, **kw)
    wcache[key] = wb
    if tm == rows:
        return first
    kw = {a: b for a, b in kw.items() if a not in ("n0", "n")}
    return matmul(x, wb, row0=tm, rows=rows - tm, into=first, **kw)


def _proj_res_norm_kernel(x_ref, w_ref, r_ref, g_ref, y_ref, h_ref, w_scr):
    @pl.when(pl.program_id(0) == 0)
    def _():
        w_scr[...] = w_ref[...].astype(BF16)

    y = _dot(x_ref[...], w_scr[...]) + r_ref[...]
    y_ref[...] = y
    ms = jnp.mean(y * y, axis=-1, keepdims=True)
    h_ref[...] = (y * lax.rsqrt(ms + NORM_EPS) * g_ref[...]).astype(h_ref.dtype)


def proj_res_norm(x, w, layer, res, g, tm=320):
    M, K = x.shape
    N = w.shape[-1]
    tm = _row_tile(M, tm)
    row_blk = lambda c: pl.BlockSpec((tm, c), lambda i: (i, 0))
    return pl.pallas_call(
        _proj_res_norm_kernel,
        grid=(M // tm,),
        in_specs=[row_blk(K), pl.BlockSpec((None, K, N), lambda i: (layer, 0, 0)), row_blk(N),
                  pl.BlockSpec((1, N), lambda i: (0, 0))],
        out_specs=[row_blk(N), row_blk(N)],
        out_shape=[jax.ShapeDtypeStruct((M, N), F32), jax.ShapeDtypeStruct((M, N), BF16)],
        scratch_shapes=[pltpu.VMEM((K, N), BF16)],
        compiler_params=_params("arbitrary"),
        name="proj_res_norm",
    )(x, w, res, g.reshape(1, N).astype(F32))


def _norm_proj_kernel(x_ref, g_ref, w_ref, y_ref, w_scr):
    @pl.when(pl.program_id(0) == 0)
    def _():
        w_scr[...] = w_ref[...].astype(BF16)

    x = x_ref[...]
    ms = jnp.mean(x * x, axis=-1, keepdims=True)
    h = (x * lax.rsqrt(ms + NORM_EPS) * g_ref[...]).astype(BF16)
    y_ref[...] = _dot(h, w_scr[...])


def norm_proj(x, g, w, layer, tm=320):
    M, D = x.shape
    N = w.shape[-1]
    tm = _row_tile(M, tm)
    return pl.pallas_call(
        _norm_proj_kernel,
        grid=(M // tm,),
        in_specs=[pl.BlockSpec((tm, D), lambda i: (i, 0)), pl.BlockSpec((1, D), lambda i: (0, 0)),
                  pl.BlockSpec((None, D, N), lambda i: (layer, 0, 0))],
        out_specs=pl.BlockSpec((tm, N), lambda i: (i, 0)),
        out_shape=jax.ShapeDtypeStruct((M, N), F32),
        scratch_shapes=[pltpu.VMEM((D, N), BF16)],
        compiler_params=_params("arbitrary"),
        name="norm_proj",
    )(x, g.reshape(1, D).astype(F32), w)


def _headnorm_kernel(x_ref, g_ref, o_ref, *, hd):
    x = x_ref[...]
    ms = _group_sum_bcast(x * x, hd) * (1.0 / hd)
    o_ref[...] = (x * lax.rsqrt(ms + NORM_EPS) * g_ref[...]).astype(o_ref.dtype)


def headnorm(x, col_block, width, g, hd, out_dtype, tm=512):
    M = x.shape[0]
    tm = _row_tile(M, tm)
    g_row = jnp.tile(g.astype(F32), width // hd).reshape(1, width)
    return pl.pallas_call(
        functools.partial(_headnorm_kernel, hd=hd),
        grid=(M // tm,),
        in_specs=[pl.BlockSpec((tm, width), lambda i: (i, col_block)),
                  pl.BlockSpec((1, width), lambda i: (0, 0))],
        out_specs=pl.BlockSpec((tm, width), lambda i: (i, 0)),
        out_shape=jax.ShapeDtypeStruct((M, width), out_dtype),
        compiler_params=_params("parallel"),
        name="headnorm",
    )(x, g_row)


def _lru_kernel(xr_ref, halo_ref, yg_ref, h0_ref, cw_ref, cb_ref, gaw_ref, gab_ref, gxw_ref, gxb_ref,
                lam_ref, o_ref, hl_ref, a_scr, u_scr, h_scr, *, n_pad, n_blocks, bw, scan_w):
    t = pl.program_id(1)
    tt, W = a_scr.shape

    @pl.when(t == 0)
    def _():
        h_scr[...] = h0_ref[0]

    x = xr_ref[...]
    prev = halo_ref[...] * jnp.where(t > 0, 1.0, 0.0)
    xc = cb_ref[...] + cw_ref[3:4, :] * x
    for s in (1, 2, 3):
        xc = xc + cw_ref[3 - s:4 - s, :] * _shift_rows(x, prev, s)

    nsp = -LRU_C * jax.nn.softplus(-lam_ref[...])
    if n_pad:
        live = (lax.broadcasted_iota(jnp.int32, (tt, 1), 0) >= n_pad) | (t > 0)
    for n in range(n_blocks):
        sl = slice(n * bw, (n + 1) * bw)
        xb = xc[:, sl]
        xb16 = xb.astype(BF16)
        r = jax.nn.sigmoid(_dot(xb16, gaw_ref[n]) + gab_ref[:, sl])
        i = jax.nn.sigmoid(_dot(xb16, gxw_ref[n]) + gxb_ref[:, sl])
        log_a = r * nsp[:, sl]
        a = jnp.exp(log_a)
        u = jnp.sqrt(jnp.maximum(-jnp.tanh(log_a) * (a * a + 1.0), 0.0)) * (i * xb)
        if n_pad:
            a = jnp.where(live, a, 1.0)
            u = jnp.where(live, u, 0.0)
        a_scr[:, sl] = a
        u_scr[:, sl] = u

    row8 = lax.broadcasted_iota(jnp.int32, (SUBLANES, 1), 0)
    for c in range(W // scan_w):
        cs = slice(c * scan_w, (c + 1) * scan_w)

        def body(gi, h, cs=cs):
            r0 = pl.multiple_of(gi * SUBLANES, SUBLANES)
            A = a_scr[pl.ds(r0, SUBLANES), cs]
            U = u_scr[pl.ds(r0, SUBLANES), cs]
            for s in (1, 2, 4):
                As = pltpu.roll(A, s, 0)
                Us = pltpu.roll(U, s, 0)
                m = row8 >= s
                U = jnp.where(m, A * Us + U, U)
                A = jnp.where(m, A * As, A)
            H = A * h + U
            u_scr[pl.ds(r0, SUBLANES), cs] = H
            return H[SUBLANES - 1:SUBLANES, :]

        h_end = lax.fori_loop(0, tt // SUBLANES, body, h_scr[:, cs])
        h_scr[:, cs] = h_end

    o_ref[...] = (u_scr[...] * jax.nn.gelu(yg_ref[...], approximate=True)).astype(o_ref.dtype)
    hl_ref[0] = h_scr[...]


def lru_mixer(xr, xr_cb, yg, yg_cb, h0, conv_w, conv_b, ga_w, ga_b, gx_w, gx_b, lam, *, B, T, n_pad, tt,
              out_rows=None, out_cols=None):
    W = conv_w.shape[1]
    nb, bw = ga_w.shape[0], ga_w.shape[1]
    tt = min(tt, T)
    nt = T // tt
    hb = tt // SUBLANES
    out_rows = B * T if out_rows is None else out_rows
    out_cols = W if out_cols is None else out_cols
    row = lambda v: v.reshape(1, W).astype(F32)
    kern = functools.partial(_lru_kernel, n_pad=n_pad, n_blocks=nb, bw=bw, scan_w=min(W, 512))
    return pl.pallas_call(
        kern,
        grid=(B, nt),
        in_specs=[
            pl.BlockSpec((tt, W), lambda b, t: (b * nt + t, xr_cb)),
            pl.BlockSpec((SUBLANES, W), lambda b, t: (jnp.maximum((b * nt + t) * hb - 1, 0), xr_cb)),
            pl.BlockSpec((tt, W), lambda b, t: (b * nt + t, yg_cb)),
            pl.BlockSpec((1, 1, W), lambda b, t: (b, 0, 0)),
            pl.BlockSpec((4, W), lambda b, t: (0, 0)),
            pl.BlockSpec((1, W), lambda b, t: (0, 0)),
            pl.BlockSpec((nb, bw, bw), lambda b, t: (0, 0, 0)),
            pl.BlockSpec((1, W), lambda b, t: (0, 0)),
            pl.BlockSpec((nb, bw, bw), lambda b, t: (0, 0, 0)),
            pl.BlockSpec((1, W), lambda b, t: (0, 0)),
            pl.BlockSpec((1, W), lambda b, t: (0, 0)),
        ],
        out_specs=[pl.BlockSpec((tt, W), lambda b, t: (b * nt + t, 0)),
                   pl.BlockSpec((1, 1, W), lambda b, t: (b, 0, 0))],
        out_shape=[jax.ShapeDtypeStruct((out_rows, out_cols), BF16), jax.ShapeDtypeStruct((B, 1, W), F32)],
        scratch_shapes=[pltpu.VMEM((tt, W), F32), pltpu.VMEM((tt, W), F32), pltpu.VMEM((1, W), F32)],
        compiler_params=_params("parallel", "arbitrary"),
        name="lru_mixer",
    )(xr, xr, yg, h0, conv_w.astype(F32), row(conv_b), ga_w.astype(BF16), row(ga_b),
      gx_w.astype(BF16), row(gx_b), row(lam))


def _swa_kernel(sink_ref, q_ref, kp_ref, kc_ref, vp_ref, vc_ref, *rest, j0, n_kv, group, hd, win):
    o_ref = rest[-1]
    j = pl.program_id(1) + j0
    q = q_ref[...]
    k2 = jnp.concatenate([kp_ref[...], kc_ref[...]], axis=0)
    v2 = jnp.concatenate([vp_ref[...], vc_ref[...]], axis=0)
    tq = q.shape[0]
    qi = lax.broadcasted_iota(jnp.int32, (tq, 2 * win), 0)
    kj = lax.broadcasted_iota(jnp.int32, (tq, 2 * win), 1)
    dist = qi + win - kj
    mask = (dist >= 0) & (dist < win) & ((j > 0) | (kj >= win))
    scale = hd ** -0.5
    for kvh in range(n_kv):
        kh = k2[:, kvh * hd:(kvh + 1) * hd]
        vh = v2[:, kvh * hd:(kvh + 1) * hd]
        hs_ = [kvh * group + g for g in range(group)]
        s = [jnp.where(mask, _nt(q[:, h * hd:(h + 1) * hd], kh) * scale, -jnp.inf) for h in hs_]
        m = [jnp.maximum(jnp.max(s[g], axis=-1, keepdims=True), sink_ref[hs_[g]]) for g in range(group)]
        p = [jnp.exp(s[g] - m[g]) for g in range(group)]
        den = [jnp.sum(p[g], axis=-1, keepdims=True) + jnp.exp(sink_ref[hs_[g]] - m[g]) for g in range(group)]
        outs = [_dot(p[g].astype(BF16), vh) / den[g] for g in range(group)]
        o_ref[:, hs_[0] * hd:(hs_[-1] + 1) * hd] = jnp.concatenate(outs, axis=-1).astype(o_ref.dtype)


def swa_attention(q, k, v, sink, *, B, nq, nkb, j0, win, hd, tq=None, into=None, out_cb=0):
    QW = q.shape[1]
    KW = k.shape[1]
    n_kv = KW // hd
    group = QW // KW
    tq = win if tq is None else tq
    kern = functools.partial(_swa_kernel, j0=j0, n_kv=n_kv, group=group, hd=hd, win=win)
    prev = lambda b, j: (b * nkb + jnp.maximum(j + j0 - 1, 0), 0)
    cur = lambda b, j: (b * nkb + j + j0, 0)
    in_specs = [pl.BlockSpec(memory_space=pltpu.SMEM),
                pl.BlockSpec((tq, QW), lambda b, j: (b * nq + j, 0)),
                pl.BlockSpec((win, KW), prev), pl.BlockSpec((win, KW), cur),
                pl.BlockSpec((win, KW), prev), pl.BlockSpec((win, KW), cur)]
    args = [sink.astype(F32), q, k, k, v, v]
    aliases = {}
    out_shape = jax.ShapeDtypeStruct((B * nq * tq, QW), BF16)
    if into is not None:
        aliases = {len(args): 0}
        in_specs.append(pl.BlockSpec(memory_space=pl.ANY))
        args.append(into)
        out_shape = jax.ShapeDtypeStruct(into.shape, into.dtype)
    return pl.pallas_call(
        kern,
        grid=(B, nq),
        in_specs=in_specs,
        out_specs=pl.BlockSpec((tq, QW), lambda b, j: (b * nq + j, out_cb)),
        out_shape=out_shape,
        input_output_aliases=aliases,
        compiler_params=_params("parallel", "parallel"),
        name="swa_attention",
    )(*args)


def _mem_attn_kernel(q_ref, g_ref, mk_ref, mv_ref, o_ref, *, n_heads, hd):
    q = q_ref[...]
    scale = hd ** -0.5
    head = (lambda ref, h: ref[0, :, h, :]) if len(mk_ref.shape) == 4 else \
           (lambda ref, h: ref[0, :, h * hd:(h + 1) * hd])
    for h in range(n_heads):
        sl = slice(h * hd, (h + 1) * hd)
        qh = q[:, sl]
        qn = (qh * lax.rsqrt(jnp.mean(qh * qh, axis=-1, keepdims=True) + NORM_EPS) * g_ref[...]).astype(BF16)
        s = _nt(qn, head(mk_ref, h).astype(BF16)) * scale
        m = jnp.max(s, axis=-1, keepdims=True)
        p = jnp.exp(s - m)
        p = p / jnp.sum(p, axis=-1, keepdims=True)
        o_ref[:, sl] = _dot(p.astype(BF16), head(mv_ref, h).astype(BF16)).astype(o_ref.dtype)


def mem_attention(q, qn_g, mk, mv, layer, *, B, T, n_heads, tt=256, out_rows=None):
    MW = q.shape[1]
    ML = mk.shape[2]
    hd = MW // n_heads
    tt = min(tt, T)
    nt = T // tt
    out_rows = B * T if out_rows is None else out_rows
    mem = pl.BlockSpec((None, 1) + mk.shape[2:], lambda b, t: (layer, b) + (0,) * (mk.ndim - 2))
    return pl.pallas_call(
        functools.partial(_mem_attn_kernel, n_heads=n_heads, hd=hd),
        grid=(B, nt),
        in_specs=[pl.BlockSpec((tt, MW), lambda b, t: (b * nt + t, 0)),
                  pl.BlockSpec((1, hd), lambda b, t: (0, 0)),
                  mem, mem],
        out_specs=pl.BlockSpec((tt, MW), lambda b, t: (b * nt + t, 0)),
        out_shape=jax.ShapeDtypeStruct((out_rows, MW), BF16),
        compiler_params=_params("parallel", "parallel"),
        name="mem_attention",
    )(q, qn_g.reshape(1, hd).astype(F32), mk, mv)


def _ffn_in_kernel(*refs, blocks_per_seq, sub, has_into, emit_w, step_len):
    x_ref, wg_ref, wu_ref, cw_ref, cb_ref = refs[:5]
    pos = 6 if has_into else 5
    if step_len:
        xs_ref, h1_ref, h2_ref = refs[pos:pos + 3]
        pos += 3
    o_ref, tail_ref = refs[pos], refs[pos + 1]
    carry_scr = refs[-1]
    i = pl.program_id(0)
    j = pl.program_id(1)
    tm = x_ref.shape[0]

    @pl.when(i % blocks_per_seq == 0)
    def _():
        carry_scr[j] = jnp.zeros(carry_scr.shape[1:], F32)

    prev = carry_scr[j]
    wg = wg_ref[...].astype(BF16)
    wu = wu_ref[...].astype(BF16)
    if emit_w:
        refs[pos + 2][...] = wg
        refs[pos + 3][...] = wu
    if step_len:
        acts_ref, gates_ref = refs[pos + 4], refs[pos + 5]
        xs = xs_ref[...]
        gs = _dot(xs, wg)
        tpos = lax.broadcasted_iota(jnp.int32, (gs.shape[0], 1), 0) % step_len
        g1 = jnp.where(tpos < 1, h1_ref[...], pltpu.roll(gs, 1, 0))
        g2 = jnp.where(tpos < 2, h2_ref[...], pltpu.roll(gs, 2, 0))
        gcs = cb_ref[...] + cw_ref[2:3, :] * gs + cw_ref[1:2, :] * g1 + cw_ref[0:1, :] * g2
        acts_ref[...] = (jax.nn.gelu(gcs, approximate=True) * _dot(xs, wu)).astype(acts_ref.dtype)
        gates_ref[...] = gs
    for c in range(tm // sub):
        rows = slice(c * sub, (c + 1) * sub)
        x = x_ref[rows, :]
        gate = _dot(x, wg)
        up = _dot(x, wu)
        gc = cb_ref[...] + cw_ref[2:3, :] * gate
        for s in (1, 2):
            gc = gc + cw_ref[2 - s:3 - s, :] * _shift_rows(gate, prev, s)
        o_ref[rows, :] = (jax.nn.gelu(gc, approximate=True) * up).astype(o_ref.dtype)
        prev = gate[sub - SUBLANES:, :]
    carry_scr[j] = prev
    tail_ref[0] = prev


def _ffn_in_call(x, wg, wu, layer, up_off, conv_w, conv_b, *, seq_len, row0, rows, tm, tn, sub, into, emit_w,
                 steps=None):
    M, D = x.shape
    FF = conv_w.shape[1]
    tm = min(tm, seq_len)
    bps = seq_len // tm
    nj = FF // tn
    i0 = row0 // tm
    uo = up_off // tn
    wspec = lambda off: (pl.BlockSpec((D, tn), lambda i, j: (0, j + off)) if layer is None else
                         pl.BlockSpec((None, D, tn), lambda i, j: (layer, 0, j + off)))
    in_specs = [pl.BlockSpec((tm, D), lambda i, j: (i + i0, 0), pipeline_mode=pl.Buffered(1)),
                wspec(0), wspec(uo),
                pl.BlockSpec((3, tn), lambda i, j: (0, j)),
                pl.BlockSpec((1, tn), lambda i, j: (0, j))]
    args = [x, wg, wu, conv_w.astype(F32), conv_b.reshape(1, FF).astype(F32)]
    aliases = {}
    if into is not None:
        aliases = {len(args): 0}
        in_specs.append(pl.BlockSpec(memory_space=pl.ANY))
        args.append(into)
    step_len = 0
    if steps is not None:
        xs, h1, h2, step_len = steps
        ms = xs.shape[0]
        assert emit_w and rows == tm
        in_specs += [pl.BlockSpec((ms, D), lambda i, j: (0, 0)),
                     pl.BlockSpec((ms, tn), lambda i, j: (0, j)), pl.BlockSpec((ms, tn), lambda i, j: (0, j))]
        args += [xs, h1, h2]
    out_specs = [pl.BlockSpec((tm, tn), lambda i, j: (i + i0, j)),
                 pl.BlockSpec((1, SUBLANES, tn), lambda i, j: (i, 0, j))]
    out_shape = [jax.ShapeDtypeStruct((M, FF), BF16),
                 jax.ShapeDtypeStruct((rows // tm, SUBLANES, FF), F32)]
    if emit_w:
        out_specs += [pl.BlockSpec((D, tn), lambda i, j: (0, j))] * 2
        out_shape += [jax.ShapeDtypeStruct((D, FF), BF16)] * 2
    if step_len:
        out_specs += [pl.BlockSpec((ms, tn), lambda i, j: (0, j))] * 2
        out_shape += [jax.ShapeDtypeStruct((ms, FF), BF16), jax.ShapeDtypeStruct((ms, FF), F32)]
    outs = pl.pallas_call(
        functools.partial(_ffn_in_kernel, blocks_per_seq=bps, sub=min(sub, tm), has_into=into is not None,
                          emit_w=emit_w, step_len=step_len),
        grid=(rows // tm, nj),
        in_specs=in_specs,
        out_specs=out_specs,
        out_shape=out_shape,
        input_output_aliases=aliases,
        scratch_shapes=[pltpu.VMEM((nj, SUBLANES, tn), F32)],
        compiler_params=_params("arbitrary", "arbitrary"),
        name="ffn_in_fused",
    )(*args)
    return (outs[0], outs[1][bps - 1::bps]) + tuple(outs[2:])


def ffn_in_fused(x, w_in, layer, conv_w, conv_b, steps, *, seq_len, rows=None, sub=256):
    M = x.shape[0] if rows is None else rows
    FF = conv_w.shape[1]
    act, tail0, wg, wu, act_s, gate_s = _ffn_in_call(
        x, w_in, w_in, layer, FF, conv_w, conv_b, seq_len=seq_len, row0=0, rows=seq_len, tm=2048, tn=256,
        sub=sub, into=None, emit_w=True, steps=steps)
    if M == seq_len:
        return act, tail0, act_s, gate_s
    act, tails = _ffn_in_call(x, wg, wu, None, 0, conv_w, conv_b, seq_len=seq_len, row0=seq_len,
                              rows=M - seq_len, tm=1024, tn=512, sub=sub, into=act, emit_w=False)
    return act, jnp.concatenate([tail0, tails], axis=0), act_s, gate_s


def _rwkv_mix_kernel(x_ref, halo_ref, s0_ref, g_ref, mu_ref, w1_ref, a1_ref, g1_ref, *out_refs, n_valid_last):
    t = pl.program_id(1)
    nt = pl.num_programs(1)
    xr_ref, xk_ref, xv_ref, lw_ref, la_ref, lg_ref, hl_ref = out_refs

    def norm(v):
        return v * lax.rsqrt(jnp.mean(v * v, axis=-1, keepdims=True) + NORM_EPS) * g_ref[...]

    h = norm(x_ref[...])
    hp = norm(halo_ref[...])
    first = jnp.where(t > 0, 1.0, 0.0)
    hp = hp * first + jnp.broadcast_to(s0_ref[0], hp.shape) * (1.0 - first)
    xx = _shift_rows(h, hp, 1) - h
    mix = lambda j: (h + xx * mu_ref[j:j + 1, :]).astype(BF16)
    xr_ref[...] = mix(0)
    xk_ref[...] = mix(2)
    xv_ref[...] = mix(3)
    lw_ref[...] = jnp.tanh(_dot(mix(1), w1_ref[...])).astype(BF16)
    la_ref[...] = _dot(mix(4), a1_ref[...]).astype(BF16)
    lg_ref[...] = jax.nn.sigmoid(_dot(mix(5), g1_ref[...])).astype(BF16)

    @pl.when(t == nt - 1)
    def _():
        hl_ref[0] = h[n_valid_last - 1:n_valid_last, :]


def rwkv_mix(x, shift0, g, mu, w1, a1, g1, *, B, T, n_valid, tt=256, out_rows=None):
    D = x.shape[1]
    tt = min(tt, T)
    nt = T // tt
    hb = tt // SUBLANES
    out_rows = B * T if out_rows is None else out_rows
    n_valid_last = n_valid - (nt - 1) * tt
    blk = pl.BlockSpec((tt, D), lambda b, t: (b * nt + t, 0))
    one = pl.BlockSpec((1, 1, D), lambda b, t: (b, 0, 0))
    return pl.pallas_call(
        functools.partial(_rwkv_mix_kernel, n_valid_last=n_valid_last),
        grid=(B, nt),
        in_specs=[blk,
                  pl.BlockSpec((SUBLANES, D), lambda b, t: (jnp.maximum((b * nt + t) * hb - 1, 0), 0)),
                  one,
                  pl.BlockSpec((1, D), lambda b, t: (0, 0)),
                  pl.BlockSpec((6, D), lambda b, t: (0, 0))] +
                 [pl.BlockSpec(w.shape, lambda b, t: (0, 0)) for w in (w1, a1, g1)],
        out_specs=[blk] * 3 + [pl.BlockSpec((tt, w.shape[1]), lambda b, t: (b * nt + t, 0)) for w in (w1, a1, g1)]
                  + [one],
        out_shape=[jax.ShapeDtypeStruct((out_rows, D), BF16)] * 3 +
                  [jax.ShapeDtypeStruct((out_rows, w.shape[1]), BF16) for w in (w1, a1, g1)] +
                  [jax.ShapeDtypeStruct((B, 1, D), F32)],
        compiler_params=_params("parallel", "arbitrary"),
        name="rwkv_mix",
    )(x, x, shift0, g.reshape(1, D).astype(F32), mu.astype(F32), w1, a1, g1)


def _rwkv_chunk_kernel(r_ref, k_ref, v_ref, lw_ref, la_ref, lg_ref, w2_ref, a2_ref, g2_ref, w0_ref, a0_ref,
                       kk_ref, ka_ref, rk_ref, lng_ref, lnb_ref, s0_ref, o_ref, so_ref, s_scr, *, hs, C, n_valid):
    c = pl.program_id(2)
    nc = pl.num_programs(2)
    R, L = r_ref.shape
    n_pairs = L // LANES

    @pl.when(c == 0)
    def _():
        z = jnp.zeros((hs, hs), F32)
        for p in range(n_pairs):
            s_scr[p] = jnp.concatenate([jnp.concatenate([s0_ref[0, 2 * p], z], axis=1),
                                        jnp.concatenate([z, s0_ref[0, 2 * p + 1]], axis=1)], axis=0)

    ti = lax.broadcasted_iota(jnp.int32, (C, C), 0)
    si = lax.broadcasted_iota(jnp.int32, (C, C), 1)
    tri = jnp.where(ti >= si, 1.0, 0.0).astype(BF16)
    lane = lax.broadcasted_iota(jnp.int32, (1, LANES), 1)
    m0 = lane < hs
    C2 = 2 * C
    ri = lax.broadcasted_iota(jnp.int32, (C2, C2), 0)
    ci = lax.broadcasted_iota(jnp.int32, (C2, C2), 1)
    same = (ri // C) == (ci // C)
    strict = same & ((ri % C) > (ci % C))
    lower = (ri % C) >= (ci % C)
    vi = lax.broadcasted_iota(jnp.int32, (LANES, LANES), 0) // hs
    vj = lax.broadcasted_iota(jnp.int32, (LANES, LANES), 1) // hs
    blockdiag = vi == vj
    n_steps = int(math.log2(C))
    wide = C2 % LANES == 0
    P = range(n_pairs)
    sls = [slice(p * LANES, (p + 1) * LANES) for p in P]
    inv = 1.0 / hs

    def stack_heads(x):
        return jnp.concatenate([jnp.where(m0, x, 0.0), jnp.where(m0, 0.0, x)], axis=0).astype(BF16)

    def prep(sc):
        rows = slice(sc * C, (sc + 1) * C)
        r = r_ref[rows, :]
        k = k_ref[rows, :]
        v = v_ref[rows, :]
        wp = w0_ref[...] + _dot(lw_ref[rows, :], w2_ref[...])
        logw = -jnp.exp(-jax.nn.softplus(-wp) - 0.5)
        if n_valid is not None:
            logw = jnp.where(lax.broadcasted_iota(jnp.int32, (C, 1), 0) + sc * C < n_valid, logw, 0.0)
        a = jax.nn.sigmoid(a0_ref[...] + _dot(la_ref[rows, :], a2_ref[...]))
        kk = k * kk_ref[...]
        kk = kk / jnp.maximum(jnp.sqrt(_group_sum_bcast(kk * kk, hs, terms=1)), 1e-12)
        k2 = k * (1.0 + (a - 1.0) * ka_ref[...])
        bm = kk * a
        cum = _dot_exact_rhs_lhs(tri, logw)
        e_in = jnp.exp(cum)
        e_out = jnp.exp(-cum)
        e_end = jnp.exp(cum[C - 1:C, :] - cum)
        g_end = jnp.exp(cum[C - 1:C, :])
        rt = r * e_in
        kkt = kk * jnp.exp(cum - logw)
        bh = bm * e_out
        kh = k2 * e_out
        bbar = bm * e_end
        kbar = k2 * e_end
        return rows, r, v, k2, g_end, rt, kkt, bh, kh, bbar, kbar

    S = [s_scr[p] for p in P]
    nxt = prep(0)
    for sc in range(R // C):
        rows, r, v, k2, g_end, rt, kkt, bh, kh, bbar, kbar = nxt
        if sc + 1 < R // C:
            nxt = prep(sc + 1)
        S16 = [s_.astype(BF16) for s_ in S]
        V16 = [v[:, sl].astype(BF16) for sl in sls]
        lhk = [stack_heads(kkt[:, sl]) for sl in sls]
        if wide:
            gbk = [_nt(lhk[p], jnp.concatenate([stack_heads(bh[:, sls[p]]), stack_heads(kh[:, sls[p]])], axis=0))
                   for p in P]
            nmat = [jnp.where(strict, -g_[:, :C2], 0.0) for g_ in gbk]
            auk = [jnp.where(strict, -g_[:, C2:], 0.0).astype(BF16) for g_ in gbk]
        else:
            nmat = [jnp.where(strict, -_nt(lhk[p], stack_heads(bh[:, sls[p]])), 0.0) for p in P]
            auk = [jnp.where(strict, -_nt(lhk[p], stack_heads(kh[:, sls[p]])), 0.0).astype(BF16) for p in P]
        sprod = [_nt(jnp.concatenate([lhk[p], rt[:, sls[p]].astype(BF16)], axis=0), S16[p]) for p in P]
        u = [_dot(auk[p], jnp.concatenate([V16[p], V16[p]], axis=0)) - sprod[p][:C2] for p in P]
        npow = [n_.astype(BF16) for n_ in nmat]
        for it in range(n_steps):
            last = it + 1 == n_steps
            if wide and not last:
                res = [_dot(npow[p], jnp.concatenate([u[p].astype(BF16), npow[p]], axis=1)) for p in P]
                u = [u[p] + res[p][:, :LANES] for p in P]
                npow = [res[p][:, LANES:].astype(BF16) for p in P]
            else:
                u = [u[p] + _dot(npow[p], u[p].astype(BF16)) for p in P]
                if not last:
                    npow = [_dot(npow[p], npow[p]).astype(BF16) for p in P]
        uv = [jnp.concatenate([jnp.where(m0, u[p][:C], u[p][C:]).astype(BF16), V16[p]], axis=0) for p in P]
        ar = [jnp.where(lower, _nt(stack_heads(rt[:, sls[p]]),
                                   jnp.concatenate([bh[:, sls[p]], kh[:, sls[p]]], axis=0).astype(BF16)),
                        0.0).astype(BF16) for p in P]
        tmat = [_dot(ar[p], uv[p]) for p in P]
        ys = [sprod[p][C2:] + jnp.where(m0, tmat[p][:C], tmat[p][C:]) for p in P]
        upd = [_tn(uv[p], jnp.concatenate([bbar[:, sls[p]], kbar[:, sls[p]]], axis=0).astype(BF16)) for p in P]
        S = [S[p] * g_end[:, sls[p]] + jnp.where(blockdiag, upd[p], 0.0) for p in P]

        y = ys[0] if n_pairs == 1 else jnp.concatenate(ys, axis=-1)
        mean = _group_sum_bcast(y, hs, terms=1) * inv
        yc = y - mean
        var = _group_sum_bcast(yc * yc, hs, terms=1) * inv
        yn = yc * lax.rsqrt(var + RW_GN_EPS) * lng_ref[...] + lnb_ref[...]
        bonus = _group_sum_bcast(r * k2 * rk_ref[...], hs, terms=1) * v
        o_ref[rows, :] = ((yn + bonus) * _dot(lg_ref[rows, :], g2_ref[...])).astype(o_ref.dtype)

    for p in P:
        s_scr[p] = S[p]

    @pl.when(c == nc - 1)
    def _():
        for p in P:
            so_ref[0, 2 * p] = S[p][:hs, :hs]
            so_ref[0, 2 * p + 1] = S[p][hs:, hs:]


def _dot_exact_rhs_lhs(m_bf16, x):
    hi = x.astype(BF16)
    lo = (x - hi.astype(F32)).astype(BF16)
    return _dot(m_bf16, hi) + _dot(m_bf16, lo)


def rwkv_chunked(r, k, v, lw, la, lg, w2, a2, g2, w0, a0, k_k, k_a, r_k, ln_g, ln_b, s0, *, B, T, hs, chunk,
                 chunks_per_step=1, lanes_per_step=1024, out_rows=None, n_valid=None):
    D = r.shape[1]
    L = min(lanes_per_step, D)
    npg = L // LANES
    rows = chunk * chunks_per_step
    nc = T // rows
    out_rows = B * T if out_rows is None else out_rows
    seq = pl.BlockSpec((rows, L), lambda b, hg, c: (b * nc + c, hg))
    low = lambda a_: pl.BlockSpec((rows, a_.shape[1]), lambda b, hg, c: (b * nc + c, 0))
    wgt = lambda w_: pl.BlockSpec((w_.shape[0], L), lambda b, hg, c: (0, hg))
    par = pl.BlockSpec((1, L), lambda b, hg, c: (0, hg))
    st = pl.BlockSpec((1, 2 * npg, hs, hs), lambda b, hg, c: (b, hg, 0, 0))
    row = lambda x: x.reshape(1, D).astype(F32)
    return pl.pallas_call(
        functools.partial(_rwkv_chunk_kernel, hs=hs, C=chunk, n_valid=n_valid),
        grid=(B, D // L, nc),
        in_specs=[seq] * 3 + [low(lw), low(la), low(lg), wgt(w2), wgt(a2), wgt(g2)] + [par] * 7 + [st],
        out_specs=[seq, st],
        out_shape=[jax.ShapeDtypeStruct((out_rows, D), BF16),
                   jax.ShapeDtypeStruct(s0.shape, F32)],
        scratch_shapes=[pltpu.VMEM((npg, LANES, LANES), F32)],
        compiler_params=_params("parallel", "parallel", "arbitrary"),
        name="rwkv_chunked",
    )(r, k, v, lw, la, lg, w2, a2, g2, row(w0), row(a0), row(k_k), row(k_a), row(r_k), row(ln_g), row(ln_b), s0)


def _pad_rows(x, front, back):
    return jnp.pad(x, ((0, 0), (front, back), (0, 0)))


def kernel(x_prompt, x_sample, mem_prompt, state_lru_conv, state_lru_h, cache_swa_k, cache_swa_v,
           state_rwkv_shift, state_rwkv_wkv, cache_mem_k, cache_mem_v, state_ffn_conv,
           a_norm_g, a_w_in, a_conv_w, a_conv_b, a_gate_a_w, a_gate_a_b, a_gate_x_w, a_gate_x_b,
           a_lambda, b_q_norm_g, b_k_norm_g, b_sink, a_w_out,
           c_norm_g, c_mu, c_w_r, c_w_k, c_w_v, c_w_o, c_w0, c_w1, c_w2, c_a0, c_a1, c_a2,
           c_g1, c_g2, c_k_k, c_k_a, c_r_k, c_ln_g, c_ln_b,
           m_norm_g, m_mem_norm_g, m_w_q, m_w_kv, m_q_norm_g, m_k_norm_g, m_w_o,
           f_norm_g, f_w_in, f_conv_w, f_conv_b, f_w_out):
    D = x_prompt.shape[-1]
    depth = m_norm_g.shape[0]
    W = a_conv_w.shape[-1]
    KA = a_conv_w.shape[1]
    hd = b_q_norm_g.shape[-1]
    n_q = b_sink.shape[-1]
    n_kv = cache_swa_k.shape[3]
    win = cache_swa_k.shape[2]
    QW, KW = n_q * hd, n_kv * hd
    hs = c_r_k.shape[-1]
    n_rw = c_r_k.shape[1]
    m_heads, m_hd = cache_mem_k.shape[3], cache_mem_k.shape[4]
    MW = m_heads * m_hd
    FF = f_conv_w.shape[-1]
    KF = f_conv_w.shape[1]
    bf = lambda w: w.astype(BF16)
    wcache = {}

    Bp, S = x_prompt.shape[:2]
    Bs, Ts = x_sample.shape[:2]
    Mp, Ms = Bp * S, Bs * Ts
    Mt = Mp + Ms
    tma = _row_tile(Mt, 1280)
    n_a = a_norm_g.shape[0]
    n_c = c_norm_g.shape[0]
    ML = mem_prompt.shape[1]
    PAD8, PAD16 = SUBLANES, 2 * SUBLANES

    def mm(xin, w, layer, key, **kw):
        return dense(xin, w, layer, wcache, (key, layer), tm=tma, **kw)

    def seq_tail(arr, n, cols=slice(None)):
        return jnp.stack([arr[(b + 1) * S - n:(b + 1) * S, cols] for b in range(Bp)])

    def put_sample(buf, rows_s):
        return lax.dynamic_update_slice(buf, rows_s.astype(buf.dtype), (Mp, 0))

    mem_flat = mem_prompt.reshape(Bp * ML, D)
    mks, mvs = [], []
    for l in range(depth):
        mn = rmsnorm_rows(mem_flat, m_mem_norm_g[l])
        kv = matmul(mn, m_w_kv, layer=l)
        mk = headnorm(kv, 0, MW, m_k_norm_g[l], m_hd, F32)
        mks.append(mk.reshape(Bp, ML, MW))
        mvs.append(kv[:, MW:].reshape(Bp, ML, MW))
    mem_k_p, mem_v_p = jnp.stack(mks), jnp.stack(mvs)
    p_mem_k = mem_k_p.reshape(depth, Bp, ML, m_heads, m_hd)
    p_mem_v = mem_v_p.reshape(depth, Bp, ML, m_heads, m_hd)

    xp2, xs2 = x_prompt.reshape(Mp, D), x_sample.reshape(Ms, D)
    xf = None

    def first_norm(g):
        return put_sample(rmsnorm_rows(xp2, g, out_rows=Mt), rmsnorm_rows(xs2, g))

    def first_residual(xin, w, layer, key):
        y = dense(xin, w, layer, wcache, (key, layer), rows=Mp, out_rows=Mt, res=xp2)
        return matmul(xin, wcache[(key, layer)], row0=Mp, rows=Ms, res=xs2, res_row0=0, into=y)

    po ={k_: [] for k_ in ("lc", "lh", "sk", "sv", "rs", "rw", "fc")}
    so = {k_: [] for k_ in ("lc", "lh", "sk", "sv", "rs", "rw", "fc")}
    ia = ic = 0
    y_p = y_s = None
    for l in range(depth):
        if l % 2 == 0:
            i = ia
            ia += 1
            assert W == QW
            h = first_norm(a_norm_g[i]) if xf is None else rmsnorm_rows(xf, a_norm_g[i])
            zz = mm(h, a_w_in, i, "a_w_in_rg", n=2 * W)
            qkv = mm(h, a_w_in, i, "a_w_in_qkv", n0=2 * W)
            qn = headnorm(qkv, 0, QW, b_q_norm_g[i], hd, BF16)
            kn = headnorm(qkv, QW // KW, KW, b_k_norm_g[i], hd, F32)
            vv = qkv[:, QW + KW:]
            lru_w = (a_conv_w[i], a_conv_b[i], a_gate_a_w[i], a_gate_a_b[i], a_gate_x_w[i], a_gate_x_b[i],
                     a_lambda[i])
            mix, hl_p = lru_mixer(zz, 0, zz, 1, jnp.zeros((Bp, 1, W), F32), *lru_w, B=Bp, T=S, n_pad=0, tt=256,
                                  out_rows=Mt, out_cols=W + QW)
            mix = swa_attention(qn, bf(kn), bf(vv), b_sink[i], B=Bp, nq=S // win, nkb=S // win, j0=0, win=win,
                                hd=hd, into=mix, out_cb=W // QW)
            po["lc"].append(seq_tail(zz, KA - 1, slice(0, W)))
            po["lh"].append(hl_p.reshape(Bp, W))
            po["sk"].append(seq_tail(kn, win).reshape(Bp, win, n_kv, hd))
            po["sv"].append(seq_tail(vv, win).reshape(Bp, win, n_kv, hd))
            zz_s = zz[Mp:].reshape(Bs, Ts, 2 * W)
            n_pad = PAD8 - Ts
            xr_hist = jnp.concatenate([state_lru_conv[i].astype(F32), zz_s[:, :, :W]], axis=1)
            xr_p = _pad_rows(xr_hist, PAD8 - xr_hist.shape[1], 0).reshape(Bs * PAD8, W)
            yg_p = _pad_rows(zz_s[:, :, W:], n_pad, 0).reshape(Bs * PAD8, W)
            oa_s, hl_s = lru_mixer(xr_p, 0, yg_p, 0, state_lru_h[i].reshape(Bs, 1, W).astype(F32), *lru_w,
                                   B=Bs, T=PAD8, n_pad=n_pad, tt=PAD8)
            oa_s = oa_s.reshape(Bs, PAD8, W)[:, n_pad:].reshape(Ms, W)
            kc = cache_swa_k[i].reshape(Bs, win, KW)
            vc = cache_swa_v[i].reshape(Bs, win, KW)
            k_all = jnp.concatenate([kc, kn[Mp:].reshape(Bs, Ts, KW)], axis=1)
            v_all = jnp.concatenate([vc, vv[Mp:].reshape(Bs, Ts, KW)], axis=1)
            q_s = _pad_rows(qn[Mp:].reshape(Bs, Ts, QW), 0, PAD16 - Ts).reshape(Bs * PAD16, QW)
            o_s = swa_attention(q_s, bf(_pad_rows(k_all, 0, win - Ts)).reshape(Bs * 2 * win, KW),
                                bf(_pad_rows(v_all, 0, win - Ts)).reshape(Bs * 2 * win, KW), b_sink[i],
                                B=Bs, nq=1, nkb=2, j0=1, win=win, hd=hd, tq=PAD16)
            o_s = o_s.reshape(Bs, PAD16, QW)[:, :Ts].reshape(Ms, QW)
            mix = put_sample(mix, jnp.concatenate([oa_s, o_s], axis=-1))
            so["lc"].append(xr_hist[:, Ts:])
            so["lh"].append(hl_s.reshape(Bs, W))
            so["sk"].append(k_all[:, Ts:].reshape(Bs, win, n_kv, hd))
            so["sv"].append(v_all[:, Ts:].reshape(Bs, win, n_kv, hd))
            xf = (first_residual(mix, a_w_out, i, "a_w_out") if xf is None
                  else mm(mix, a_w_out, i, "a_w_out", res=xf))
        else:
            i = ic
            ic += 1
            gl = c_g1[i].shape[1]
            glp = -(-gl // LANES) * LANES
            lora1 = (bf(c_w1[i]), bf(c_a1[i]), bf(jnp.pad(c_g1[i], ((0, 0), (0, glp - gl)))))
            lora2 = (bf(c_w2[i]), bf(c_a2[i]), bf(jnp.pad(c_g2[i], ((0, glp - gl), (0, 0)))), c_w0[i], c_a0[i])
            mixes = rwkv_mix(xf, jnp.zeros((Bp, 1, D), F32), c_norm_g[i], c_mu[i], *lora1, B=Bp, T=S, n_valid=S,
                             out_rows=Mt)
            x_s = _pad_rows(xf[Mp:].reshape(Bs, Ts, D), 0, PAD8 - Ts).reshape(Bs * PAD8, D)
            mixes_s = rwkv_mix(x_s, state_rwkv_shift[i].reshape(Bs, 1, D).astype(F32), c_norm_g[i], c_mu[i], *lora1,
                               B=Bs, T=PAD8, n_valid=Ts, tt=PAD8)
            take = lambda m: m.reshape(Bs, PAD8, m.shape[-1])[:, :Ts].reshape(Ms, m.shape[-1])
            xr, xk, xv, lw, la, lg = [put_sample(mixes[j], take(mixes_s[j])) for j in range(6)]
            po["rs"].append(mixes[6].reshape(Bp, D))
            so["rs"].append(mixes_s[6].reshape(Bs, D))
            r = mm(xr, c_w_r, i, "c_w_r")
            k = mm(xk, c_w_k, i, "c_w_k")
            v = mm(xv, c_w_v, i, "c_w_v")
            rw_par = (c_k_k[i], c_k_a[i], c_r_k[i], c_ln_g[i], c_ln_b[i])
            yo, s_end = rwkv_chunked(r, k, v, lw, la, lg, *lora2, *rw_par, jnp.zeros((Bp, n_rw, hs, hs), F32),
                                     B=Bp, T=S, hs=hs, chunk=64, chunks_per_step=4, out_rows=Mt)
            po["rw"].append(s_end)
            cs = PAD16
            sq = lambda t: jnp.pad(t[Mp:].reshape(Bs, Ts, t.shape[-1]),
                                   ((0, 0), (0, cs - Ts), (0, 0))).reshape(Bs * cs, t.shape[-1])
            yo_s, s_end_s = rwkv_chunked(sq(r), sq(k), sq(v), sq(lw), sq(la), sq(lg), *lora2, *rw_par,
                                         state_rwkv_wkv[i].astype(F32), B=Bs, T=cs, hs=hs, chunk=cs,
                                         lanes_per_step=2048, n_valid=Ts)
            yo = put_sample(yo, yo_s.reshape(Bs, cs, D)[:, :Ts].reshape(Ms, D))
            so["rw"].append(s_end_s)
            xf = mm(yo, c_w_o, i, "c_w_o", res=xf)
        q = norm_proj(xf, m_norm_g[l], m_w_q, l)
        om = mem_attention(q, m_q_norm_g[l], mem_k_p, mem_v_p, l, B=Bp, T=S, n_heads=m_heads, out_rows=Mt)
        q_s = _pad_rows(q[Mp:].reshape(Bs, Ts, MW), 0, PAD8 - Ts).reshape(Bs * PAD8, MW)
        om_s = mem_attention(q_s, m_q_norm_g[l], cache_mem_k, cache_mem_v, l, B=Bs, T=PAD8, n_heads=m_heads, tt=PAD8)
        om = put_sample(om, om_s.reshape(Bs, PAD8, MW)[:, :Ts].reshape(Ms, MW))
        xf, hf = proj_res_norm(om, m_w_o, l, xf, f_norm_g[l])
        assert Ts >= KF - 1 == 2
        st_f = state_ffn_conv[l].astype(F32)
        zrow = jnp.zeros((Bs, Ts - 1, FF), F32)
        h1 = jnp.concatenate([st_f[:, 1:2], zrow], axis=1).reshape(Ms, FF)
        h2 = jnp.concatenate([st_f, zrow[:, 1:]], axis=1).reshape(Ms, FF)
        act, tail, act_s, gate_s = ffn_in_fused(hf, f_w_in, l, f_conv_w[l], f_conv_b[l], (hf[Mp:], h1, h2, Ts),
                                                seq_len=S, rows=Mp)
        po["fc"].append(tail[:, SUBLANES - (KF - 1):, :])
        act = put_sample(act, act_s)
        so["fc"].append(gate_s.reshape(Bs, Ts, FF)[:, Ts - (KF - 1):])
        if l + 1 < depth:
            xf = mm(act, f_w_out, l, "f_w_out", res=xf, tn=1024, tk=2048)
        else:
            y_p = dense(act, f_w_out, l, wcache, ("f_w_out", l), rows=Mp, out_rows=Mp, res=xf, tn=1024, tk=2048)
            y_s = matmul(act, wcache[("f_w_out", l)], row0=Mp, rows=Ms, out_rows=Ms, out_row0=0, res=xf,
                         tn=1024, tk=2048)
    st = jnp.stack
    return (y_p.reshape(Bp, S, D), y_s.reshape(Bs, Ts, D),
            st(po["lc"]), st(po["lh"]), st(po["sk"]), st(po["sv"]), st(po["rs"]), st(po["rw"]),
            p_mem_k, p_mem_v, st(po["fc"]),
            st(so["lc"]), st(so["lh"]), st(so["sk"]), st(so["sv"]), st(so["rs"]), st(so["rw"]), st(so["fc"]))
```

```python
import functools
import math

import jax
import jax.numpy as jnp
from jax import lax
from jax.experimental import pallas as pl
from jax.experimental.pallas import tpu as pltpu

F32 = jnp.float32
BF16 = jnp.bfloat16

NORM_EPS = 1e-6
RW_GN_EPS = 64e-5
LRU_C = 8.0
LANES = 128
SUBLANES = 8
VMEM_LIMIT_BYTES = 56 * 1024 * 1024


def _params(*sem):
    return pltpu.CompilerParams(dimension_semantics=sem, vmem_limit_bytes=VMEM_LIMIT_BYTES)


def _nt(a, b):
    return lax.dot_general(a, b, (((1,), (1,)), ((), ())), preferred_element_type=F32)


def _tn(a, b):
    return lax.dot_general(a, b, (((0,), (0,)), ((), ())), preferred_element_type=F32)


def _dot(a, b):
    return jnp.dot(a, b, preferred_element_type=F32)


def _group_sum_bcast(x, width, terms=2):
    m, L = x.shape
    n = L // LANES
    if width == LANES:
        parts = []
        for c in range(n):
            s = jnp.sum(x[:, c * LANES:(c + 1) * LANES], axis=-1, keepdims=True)
            parts.append(jnp.broadcast_to(s, (m, LANES)))
        return parts[0] if n == 1 else jnp.concatenate(parts, axis=-1)
    li = lax.broadcasted_iota(jnp.int32, (LANES, LANES), 0) // width
    lj = lax.broadcasted_iota(jnp.int32, (LANES, LANES), 1) // width
    e = jnp.where(li == lj, 1.0, 0.0).astype(BF16)
    xs = x if n == 1 else jnp.concatenate([x[:, c * LANES:(c + 1) * LANES] for c in range(n)], axis=0)
    hi = xs.astype(BF16)
    out = _dot(hi, e)
    if terms > 1:
        out = out + _dot((xs - hi.astype(F32)).astype(BF16), e)
    return out if n == 1 else jnp.concatenate([out[c * m:(c + 1) * m] for c in range(n)], axis=-1)


def _shift_rows(x, prev8, s):
    rolled = pltpu.roll(x, s, 0)
    top = jnp.where(lax.broadcasted_iota(jnp.int32, (SUBLANES, 1), 0) < s,
                    pltpu.roll(prev8, s, 0), rolled[0:SUBLANES])
    if x.shape[0] == SUBLANES:
        return top
    return jnp.concatenate([top, rolled[SUBLANES:]], axis=0)


def _rmsnorm_kernel(x_ref, g_ref, o_ref):
    x = x_ref[...]
    ms = jnp.mean(x * x, axis=-1, keepdims=True)
    o_ref[...] = (x * lax.rsqrt(ms + NORM_EPS) * g_ref[...]).astype(o_ref.dtype)


def _row_tile(m, target):
    best = None
    for d in range(16, min(m, target) + 1, 16):
        if m % d == 0:
            best = d
    return best or m


def rmsnorm_rows(x, g, out_dtype=BF16, tm=512, out_rows=None):
    M, D = x.shape
    tm = _row_tile(M, tm)
    return pl.pallas_call(
        _rmsnorm_kernel,
        grid=(M // tm,),
        in_specs=[pl.BlockSpec((tm, D), lambda i: (i, 0)), pl.BlockSpec((1, D), lambda i: (0, 0))],
        out_specs=pl.BlockSpec((tm, D), lambda i: (i, 0)),
        out_shape=jax.ShapeDtypeStruct((M if out_rows is None else out_rows, D), out_dtype),
        compiler_params=_params("parallel"),
        name="rmsnorm_rows",
    )(x, g.reshape(1, D))


def _mm_kernel(*refs, nk, has_bias, has_res, has_into, emit_w, in_place, act):
    x_ref, w_ref = refs[0], refs[1]
    pos = 2
    b_ref = r_ref = wb_ref = acc_ref = None
    if has_bias:
        b_ref = refs[pos]
        pos += 1
    if has_res:
        r_ref = refs[pos]
        pos += 1
    if has_into:
        pos += 1
    o_ref = refs[pos]
    pos += 1
    if emit_w:
        wb_ref = refs[pos]
        pos += 1
    if nk > 1 and not in_place:
        acc_ref = refs[pos]

    def weights(cs=slice(None)):
        w16 = w_ref[:, cs].astype(BF16)
        if emit_w:
            wb_ref[:, cs] = w16
        return w16

    def epilogue(y):
        if has_bias:
            y = y + b_ref[...]
        if act == "tanh":
            y = jnp.tanh(y)
        elif act == "sigmoid":
            y = jax.nn.sigmoid(y)
        if has_res:
            y = y + r_ref[...]
        o_ref[...] = y.astype(o_ref.dtype)

    if nk > 1 and in_place:
        k = pl.program_id(2)
        tn = o_ref.shape[1]
        cw = min(tn, 2 * LANES)

        def sweep(first):
            x = x_ref[...]
            for c in range(tn // cw):
                cs = slice(c * cw, (c + 1) * cw)
                part = _dot(x, weights(cs))
                if not first:
                    o_ref[:, cs] += part
                elif has_res:
                    o_ref[:, cs] = part + r_ref[:, cs]
                else:
                    o_ref[:, cs] = part

        pl.when(k == 0)(functools.partial(sweep, True))
        pl.when(k > 0)(functools.partial(sweep, False))
        return

    part = _dot(x_ref[...], weights())
    if nk == 1:
        epilogue(part)
        return
    k = pl.program_id(2)

    @pl.when(k == 0)
    def _():
        acc_ref[...] = part

    @pl.when(k > 0)
    def _():
        acc_ref[...] += part

    @pl.when(k == nk - 1)
    def _():
        epilogue(acc_ref[...])


def _pick(n, prefs):
    for p in prefs:
        if n % p == 0:
            return p
    return n


def matmul(x, w, *, layer=None, n0=0, n=None, bias=None, res=None, act=None, out_dtype=F32,
           tm=1024, tn=512, tk=4096, row0=0, rows=None, into=None, emit_w=False, out_rows=None, out_row0=None,
           res_row0=None):
    M, K = x.shape
    N = w.shape[-1] - n0 if n is None else n
    rows = M - row0 if rows is None else rows
    tm = min(tm, rows)
    out_rows = M if out_rows is None else out_rows
    o0 = (row0 if out_row0 is None else out_row0) // tm
    tn = _pick(math.gcd(N, n0) if n0 else N, (tn, 512, 256, 128))
    tk = _pick(K, (tk, 2048, 1024, 512))
    nk = K // tk
    j0 = n0 // tn
    i0 = row0 // tm
    assert row0 % tm == 0 and rows % tm == 0 and (not emit_w or rows == tm)
    in_specs = [pl.BlockSpec((tm, tk), lambda i, j, k: (i + i0, k)),
                pl.BlockSpec((tk, tn), lambda i, j, k: (k, j + j0)) if layer is None else
                pl.BlockSpec((None, tk, tn), lambda i, j, k: (layer, k, j + j0))]
    args = [x, w]
    if bias is not None:
        in_specs.append(pl.BlockSpec((1, tn), lambda i, j, k: (0, j)))
        args.append(bias.reshape(1, N).astype(F32))
    if res is not None:
        r0 = i0 if res_row0 is None else res_row0 // tm
        in_specs.append(pl.BlockSpec((tm, tn), lambda i, j, k: (i + r0, j)))
        args.append(res)
    aliases = {}
    if into is not None:
        aliases = {len(args): 0}
        in_specs.append(pl.BlockSpec(memory_space=pl.ANY))
        args.append(into)
    in_place = bias is None and act is None and out_dtype == F32
    kern = functools.partial(_mm_kernel, nk=nk, has_bias=bias is not None, has_res=res is not None,
                             has_into=into is not None, emit_w=emit_w, in_place=in_place, act=act)
    out_specs = [pl.BlockSpec((tm, tn), lambda i, j, k: (i + o0, j))]
    out_shape = [jax.ShapeDtypeStruct((out_rows, N), out_dtype)]
    if emit_w:
        out_specs.append(pl.BlockSpec((tk, tn), lambda i, j, k: (k, j)))
        out_shape.append(jax.ShapeDtypeStruct((K, N), BF16))
    outs = pl.pallas_call(
        kern,
        grid=(rows // tm, N // tn, nk),
        in_specs=in_specs,
        out_specs=out_specs,
        out_shape=out_shape,
        input_output_aliases=aliases,
        scratch_shapes=[pltpu.VMEM((tm, tn), F32)] if (nk > 1 and not in_place) else [],
        compiler_params=_params("parallel", "parallel", "arbitrary"),
        name="matmul",
    )(*args)
    return outs if emit_w else outs[0]


def dense(x, w, layer, wcache, key, **kw):
    rows = kw.pop("rows", x.shape[0])
    tm = min(kw.get("tm", 1024), rows)
    first, wb = matmul(x, w, layer=layer, rows=tm, emit_w=True, **kw)
    wcache[key] = wb
    if tm == rows:
        return first
    kw = {a: b for a, b in kw.items() if a not in ("n0", "n")}
    return matmul(x, wb, row0=tm, rows=rows - tm, into=first, **kw)


def _proj_res_norm_kernel(x_ref, w_ref, r_ref, g_ref, y_ref, h_ref, w_scr):
    @pl.when(pl.program_id(0) == 0)
    def _():
        w_scr[...] = w_ref[...].astype(BF16)

    y = _dot(x_ref[...], w_scr[...]) + r_ref[...]
    y_ref[...] = y
    ms = jnp.mean(y * y, axis=-1, keepdims=True)
    h_ref[...] = (y * lax.rsqrt(ms + NORM_EPS) * g_ref[...]).astype(h_ref.dtype)


def proj_res_norm(x, w, layer, res, g, tm=320):
    M, K = x.shape
    N = w.shape[-1]
    tm = _row_tile(M, tm)
    row_blk = lambda c: pl.BlockSpec((tm, c), lambda i: (i, 0))
    return pl.pallas_call(
        _proj_res_norm_kernel,
        grid=(M // tm,),
        in_specs=[row_blk(K), pl.BlockSpec((None, K, N), lambda i: (layer, 0, 0)), row_blk(N),
                  pl.BlockSpec((1, N), lambda i: (0, 0))],
        out_specs=[row_blk(N), row_blk(N)],
        out_shape=[jax.ShapeDtypeStruct((M, N), F32), jax.ShapeDtypeStruct((M, N), BF16)],
        scratch_shapes=[pltpu.VMEM((K, N), BF16)],
        compiler_params=_params("arbitrary"),
        name="proj_res_norm",
    )(x, w, res, g.reshape(1, N).astype(F32))


def _norm_proj_kernel(x_ref, g_ref, w_ref, y_ref, w_scr):
    @pl.when(pl.program_id(0) == 0)
    def _():
        w_scr[...] = w_ref[...].astype(BF16)

    x = x_ref[...]
    ms = jnp.mean(x * x, axis=-1, keepdims=True)
    h = (x * lax.rsqrt(ms + NORM_EPS) * g_ref[...]).astype(BF16)
    y_ref[...] = _dot(h, w_scr[...])


def norm_proj(x, g, w, layer, tm=320):
    M, D = x.shape
    N = w.shape[-1]
    tm = _row_tile(M, tm)
    return pl.pallas_call(
        _norm_proj_kernel,
        grid=(M // tm,),
        in_specs=[pl.BlockSpec((tm, D), lambda i: (i, 0)), pl.BlockSpec((1, D), lambda i: (0, 0)),
                  pl.BlockSpec((None, D, N), lambda i: (layer, 0, 0))],
        out_specs=pl.BlockSpec((tm, N), lambda i: (i, 0)),
        out_shape=jax.ShapeDtypeStruct((M, N), F32),
        scratch_shapes=[pltpu.VMEM((D, N), BF16)],
        compiler_params=_params("arbitrary"),
        name="norm_proj",
    )(x, g.reshape(1, D).astype(F32), w)


def _headnorm_kernel(x_ref, g_ref, o_ref, *, hd):
    x = x_ref[...]
    ms = _group_sum_bcast(x * x, hd) * (1.0 / hd)
    o_ref[...] = (x * lax.rsqrt(ms + NORM_EPS) * g_ref[...]).astype(o_ref.dtype)


def headnorm(x, col_block, width, g, hd, out_dtype, tm=512):
    M = x.shape[0]
    tm = _row_tile(M, tm)
    g_row = jnp.tile(g.astype(F32), width // hd).reshape(1, width)
    return pl.pallas_call(
        functools.partial(_headnorm_kernel, hd=hd),
        grid=(M // tm,),
        in_specs=[pl.BlockSpec((tm, width), lambda i: (i, col_block)),
                  pl.BlockSpec((1, width), lambda i: (0, 0))],
        out_specs=pl.BlockSpec((tm, width), lambda i: (i, 0)),
        out_shape=jax.ShapeDtypeStruct((M, width), out_dtype),
        compiler_params=_params("parallel"),
        name="headnorm",
    )(x, g_row)


def _lru_kernel(xr_ref, halo_ref, yg_ref, h0_ref, cw_ref, cb_ref, gaw_ref, gab_ref, gxw_ref, gxb_ref,
                lam_ref, o_ref, hl_ref, a_scr, u_scr, h_scr, *, n_pad, n_blocks, bw, scan_w):
    t = pl.program_id(1)
    tt, W = a_scr.shape

    @pl.when(t == 0)
    def _():
        h_scr[...] = h0_ref[0]

    x = xr_ref[...]
    prev = halo_ref[...] * jnp.where(t > 0, 1.0, 0.0)
    xc = cb_ref[...] + cw_ref[3:4, :] * x
    for s in (1, 2, 3):
        xc = xc + cw_ref[3 - s:4 - s, :] * _shift_rows(x, prev, s)

    nsp = -LRU_C * jax.nn.softplus(-lam_ref[...])
    if n_pad:
        live = (lax.broadcasted_iota(jnp.int32, (tt, 1), 0) >= n_pad) | (t > 0)
    for n in range(n_blocks):
        sl = slice(n * bw, (n + 1) * bw)
        xb = xc[:, sl]
        xb16 = xb.astype(BF16)
        r = jax.nn.sigmoid(_dot(xb16, gaw_ref[n]) + gab_ref[:, sl])
        i = jax.nn.sigmoid(_dot(xb16, gxw_ref[n]) + gxb_ref[:, sl])
        log_a = r * nsp[:, sl]
        a = jnp.exp(log_a)
        u = jnp.sqrt(jnp.maximum(-jnp.tanh(log_a) * (a * a + 1.0), 0.0)) * (i * xb)
        if n_pad:
            a = jnp.where(live, a, 1.0)
            u = jnp.where(live, u, 0.0)
        a_scr[:, sl] = a
        u_scr[:, sl] = u

    row8 = lax.broadcasted_iota(jnp.int32, (SUBLANES, 1), 0)
    for c in range(W // scan_w):
        cs = slice(c * scan_w, (c + 1) * scan_w)

        def body(gi, h, cs=cs):
            r0 = pl.multiple_of(gi * SUBLANES, SUBLANES)
            A = a_scr[pl.ds(r0, SUBLANES), cs]
            U = u_scr[pl.ds(r0, SUBLANES), cs]
            for s in (1, 2, 4):
                As = pltpu.roll(A, s, 0)
                Us = pltpu.roll(U, s, 0)
                m = row8 >= s
                U = jnp.where(m, A * Us + U, U)
                A = jnp.where(m, A * As, A)
            H = A * h + U
            u_scr[pl.ds(r0, SUBLANES), cs] = H
            return H[SUBLANES - 1:SUBLANES, :]

        h_end = lax.fori_loop(0, tt // SUBLANES, body, h_scr[:, cs])
        h_scr[:, cs] = h_end

    o_ref[...] = (u_scr[...] * jax.nn.gelu(yg_ref[...], approximate=True)).astype(o_ref.dtype)
    hl_ref[0] = h_scr[...]


def lru_mixer(xr, xr_cb, yg, yg_cb, h0, conv_w, conv_b, ga_w, ga_b, gx_w, gx_b, lam, *, B, T, n_pad, tt,
              out_rows=None, out_cols=None):
    W = conv_w.shape[1]
    nb, bw = ga_w.shape[0], ga_w.shape[1]
    tt = min(tt, T)
    nt = T // tt
    hb = tt // SUBLANES
    out_rows = B * T if out_rows is None else out_rows
    out_cols = W if out_cols is None else out_cols
    row = lambda v: v.reshape(1, W).astype(F32)
    kern = functools.partial(_lru_kernel, n_pad=n_pad, n_blocks=nb, bw=bw, scan_w=min(W, 512))
    return pl.pallas_call(
        kern,
        grid=(B, nt),
        in_specs=[
            pl.BlockSpec((tt, W), lambda b, t: (b * nt + t, xr_cb)),
            pl.BlockSpec((SUBLANES, W), lambda b, t: (jnp.maximum((b * nt + t) * hb - 1, 0), xr_cb)),
            pl.BlockSpec((tt, W), lambda b, t: (b * nt + t, yg_cb)),
            pl.BlockSpec((1, 1, W), lambda b, t: (b, 0, 0)),
            pl.BlockSpec((4, W), lambda b, t: (0, 0)),
            pl.BlockSpec((1, W), lambda b, t: (0, 0)),
            pl.BlockSpec((nb, bw, bw), lambda b, t: (0, 0, 0)),
            pl.BlockSpec((1, W), lambda b, t: (0, 0)),
            pl.BlockSpec((nb, bw, bw), lambda b, t: (0, 0, 0)),
            pl.BlockSpec((1, W), lambda b, t: (0, 0)),
            pl.BlockSpec((1, W), lambda b, t: (0, 0)),
        ],
        out_specs=[pl.BlockSpec((tt, W), lambda b, t: (b * nt + t, 0)),
                   pl.BlockSpec((1, 1, W), lambda b, t: (b, 0, 0))],
        out_shape=[jax.ShapeDtypeStruct((out_rows, out_cols), BF16), jax.ShapeDtypeStruct((B, 1, W), F32)],
        scratch_shapes=[pltpu.VMEM((tt, W), F32), pltpu.VMEM((tt, W), F32), pltpu.VMEM((1, W), F32)],
        compiler_params=_params("parallel", "arbitrary"),
        name="lru_mixer",
    )(xr, xr, yg, h0, conv_w.astype(F32), row(conv_b), ga_w.astype(BF16), row(ga_b),
      gx_w.astype(BF16), row(gx_b), row(lam))


def _swa_kernel(sink_ref, q_ref, kp_ref, kc_ref, vp_ref, vc_ref, *rest, j0, n_kv, group, hd, win):
    o_ref = rest[-1]
    j = pl.program_id(1) + j0
    q = q_ref[...]
    k2 = jnp.concatenate([kp_ref[...], kc_ref[...]], axis=0)
    v2 = jnp.concatenate([vp_ref[...], vc_ref[...]], axis=0)
    tq = q.shape[0]
    qi = lax.broadcasted_iota(jnp.int32, (tq, 2 * win), 0)
    kj = lax.broadcasted_iota(jnp.int32, (tq, 2 * win), 1)
    dist = qi + win - kj
    mask = (dist >= 0) & (dist < win) & ((j > 0) | (kj >= win))
    scale = hd ** -0.5
    for kvh in range(n_kv):
        kh = k2[:, kvh * hd:(kvh + 1) * hd]
        vh = v2[:, kvh * hd:(kvh + 1) * hd]
        hs_ = [kvh * group + g for g in range(group)]
        s = [jnp.where(mask, _nt(q[:, h * hd:(h + 1) * hd], kh) * scale, -jnp.inf) for h in hs_]
        m = [jnp.maximum(jnp.max(s[g], axis=-1, keepdims=True), sink_ref[hs_[g]]) for g in range(group)]
        p = [jnp.exp(s[g] - m[g]) for g in range(group)]
        den = [jnp.sum(p[g], axis=-1, keepdims=True) + jnp.exp(sink_ref[hs_[g]] - m[g]) for g in range(group)]
        outs = [_dot(p[g].astype(BF16), vh) / den[g] for g in range(group)]
        o_ref[:, hs_[0] * hd:(hs_[-1] + 1) * hd] = jnp.concatenate(outs, axis=-1).astype(o_ref.dtype)


def swa_attention(q, k, v, sink, *, B, nq, nkb, j0, win, hd, tq=None, into=None, out_cb=0):
    QW = q.shape[1]
    KW = k.shape[1]
    n_kv = KW // hd
    group = QW // KW
    tq = win if tq is None else tq
    kern = functools.partial(_swa_kernel, j0=j0, n_kv=n_kv, group=group, hd=hd, win=win)
    prev = lambda b, j: (b * nkb + jnp.maximum(j + j0 - 1, 0), 0)
    cur = lambda b, j: (b * nkb + j + j0, 0)
    in_specs = [pl.BlockSpec(memory_space=pltpu.SMEM),
                pl.BlockSpec((tq, QW), lambda b, j: (b * nq + j, 0)),
                pl.BlockSpec((win, KW), prev), pl.BlockSpec((win, KW), cur),
                pl.BlockSpec((win, KW), prev), pl.BlockSpec((win, KW), cur)]
    args = [sink.astype(F32), q, k, k, v, v]
    aliases = {}
    out_shape = jax.ShapeDtypeStruct((B * nq * tq, QW), BF16)
    if into is not None:
        aliases = {len(args): 0}
        in_specs.append(pl.BlockSpec(memory_space=pl.ANY))
        args.append(into)
        out_shape = jax.ShapeDtypeStruct(into.shape, into.dtype)
    return pl.pallas_call(
        kern,
        grid=(B, nq),
        in_specs=in_specs,
        out_specs=pl.BlockSpec((tq, QW), lambda b, j: (b * nq + j, out_cb)),
        out_shape=out_shape,
        input_output_aliases=aliases,
        compiler_params=_params("parallel", "parallel"),
        name="swa_attention",
    )(*args)


def _mem_attn_kernel(q_ref, g_ref, mk_ref, mv_ref, o_ref, *, n_heads, hd):
    q = q_ref[...]
    scale = hd ** -0.5
    head = (lambda ref, h: ref[0, :, h, :]) if len(mk_ref.shape) == 4 else \
           (lambda ref, h: ref[0, :, h * hd:(h + 1) * hd])
    for h in range(n_heads):
        sl = slice(h * hd, (h + 1) * hd)
        qh = q[:, sl]
        qn = (qh * lax.rsqrt(jnp.mean(qh * qh, axis=-1, keepdims=True) + NORM_EPS) * g_ref[...]).astype(BF16)
        s = _nt(qn, head(mk_ref, h).astype(BF16)) * scale
        m = jnp.max(s, axis=-1, keepdims=True)
        p = jnp.exp(s - m)
        p = p / jnp.sum(p, axis=-1, keepdims=True)
        o_ref[:, sl] = _dot(p.astype(BF16), head(mv_ref, h).astype(BF16)).astype(o_ref.dtype)


def mem_attention(q, qn_g, mk, mv, layer, *, B, T, n_heads, tt=256, out_rows=None):
    MW = q.shape[1]
    ML = mk.shape[2]
    hd = MW // n_heads
    tt = min(tt, T)
    nt = T // tt
    out_rows = B * T if out_rows is None else out_rows
    mem = pl.BlockSpec((None, 1) + mk.shape[2:], lambda b, t: (layer, b) + (0,) * (mk.ndim - 2))
    return pl.pallas_call(
        functools.partial(_mem_attn_kernel, n_heads=n_heads, hd=hd),
        grid=(B, nt),
        in_specs=[pl.BlockSpec((tt, MW), lambda b, t: (b * nt + t, 0)),
                  pl.BlockSpec((1, hd), lambda b, t: (0, 0)),
                  mem, mem],
        out_specs=pl.BlockSpec((tt, MW), lambda b, t: (b * nt + t, 0)),
        out_shape=jax.ShapeDtypeStruct((out_rows, MW), BF16),
        compiler_params=_params("parallel", "parallel"),
        name="mem_attention",
    )(q, qn_g.reshape(1, hd).astype(F32), mk, mv)


def _ffn_in_kernel(*refs, blocks_per_seq, sub, has_into, emit_w, step_len):
    x_ref, wg_ref, wu_ref, cw_ref, cb_ref = refs[:5]
    pos = 6 if has_into else 5
    if step_len:
        xs_ref, h1_ref, h2_ref = refs[pos:pos + 3]
        pos += 3
    o_ref, tail_ref = refs[pos], refs[pos + 1]
    carry_scr = refs[-1]
    i = pl.program_id(0)
    j = pl.program_id(1)
    tm = x_ref.shape[0]

    @pl.when(i % blocks_per_seq == 0)
    def _():
        carry_scr[j] = jnp.zeros(carry_scr.shape[1:], F32)

    prev = carry_scr[j]
    wg = wg_ref[...].astype(BF16)
    wu = wu_ref[...].astype(BF16)
    if emit_w:
        refs[pos + 2][...] = wg
        refs[pos + 3][...] = wu
    if step_len:
        acts_ref, gates_ref = refs[pos + 4], refs[pos + 5]
        xs = xs_ref[...]
        gs = _dot(xs, wg)
        tpos = lax.broadcasted_iota(jnp.int32, (gs.shape[0], 1), 0) % step_len
        g1 = jnp.where(tpos < 1, h1_ref[...], pltpu.roll(gs, 1, 0))
        g2 = jnp.where(tpos < 2, h2_ref[...], pltpu.roll(gs, 2, 0))
        gcs = cb_ref[...] + cw_ref[2:3, :] * gs + cw_ref[1:2, :] * g1 + cw_ref[0:1, :] * g2
        acts_ref[...] = (jax.nn.gelu(gcs, approximate=True) * _dot(xs, wu)).astype(acts_ref.dtype)
        gates_ref[...] = gs
    for c in range(tm // sub):
        rows = slice(c * sub, (c + 1) * sub)
        x = x_ref[rows, :]
        gate = _dot(x, wg)
        up = _dot(x, wu)
        gc = cb_ref[...] + cw_ref[2:3, :] * gate
        for s in (1, 2):
            gc = gc + cw_ref[2 - s:3 - s, :] * _shift_rows(gate, prev, s)
        o_ref[rows, :] = (jax.nn.gelu(gc, approximate=True) * up).astype(o_ref.dtype)
        prev = gate[sub - SUBLANES:, :]
    carry_scr[j] = prev
    tail_ref[0] = prev


def _ffn_in_call(x, wg, wu, layer, up_off, conv_w, conv_b, *, seq_len, row0, rows, tm, tn, sub, into, emit_w,
                 steps=None):
    M, D = x.shape
    FF = conv_w.shape[1]
    tm = min(tm, seq_len)
    bps = seq_len // tm
    nj = FF // tn
    i0 = row0 // tm
    uo = up_off // tn
    wspec = lambda off: (pl.BlockSpec((D, tn), lambda i, j: (0, j + off)) if layer is None else
                         pl.BlockSpec((None, D, tn), lambda i, j: (layer, 0, j + off)))
    in_specs = [pl.BlockSpec((tm, D), lambda i, j: (i + i0, 0), pipeline_mode=pl.Buffered(1)),
                wspec(0), wspec(uo),
                pl.BlockSpec((3, tn), lambda i, j: (0, j)),
                pl.BlockSpec((1, tn), lambda i, j: (0, j))]
    args = [x, wg, wu, conv_w.astype(F32), conv_b.reshape(1, FF).astype(F32)]
    aliases = {}
    if into is not None:
        aliases = {len(args): 0}
        in_specs.append(pl.BlockSpec(memory_space=pl.ANY))
        args.append(into)
    step_len = 0
    if steps is not None:
        xs, h1, h2, step_len = steps
        ms = xs.shape[0]
        assert emit_w and rows == tm
        in_specs += [pl.BlockSpec((ms, D), lambda i, j: (0, 0)),
                     pl.BlockSpec((ms, tn), lambda i, j: (0, j)), pl.BlockSpec((ms, tn), lambda i, j: (0, j))]
        args += [xs, h1, h2]
    out_specs = [pl.BlockSpec((tm, tn), lambda i, j: (i + i0, j)),
                 pl.BlockSpec((1, SUBLANES, tn), lambda i, j: (i, 0, j))]
    out_shape = [jax.ShapeDtypeStruct((M, FF), BF16),
                 jax.ShapeDtypeStruct((rows // tm, SUBLANES, FF), F32)]
    if emit_w:
        out_specs += [pl.BlockSpec((D, tn), lambda i, j: (0, j))] * 2
        out_shape += [jax.ShapeDtypeStruct((D, FF), BF16)] * 2
    if step_len:
        out_specs += [pl.BlockSpec((ms, tn), lambda i, j: (0, j))] * 2
        out_shape += [jax.ShapeDtypeStruct((ms, FF), BF16), jax.ShapeDtypeStruct((ms, FF), F32)]
    outs = pl.pallas_call(
        functools.partial(_ffn_in_kernel, blocks_per_seq=bps, sub=min(sub, tm), has_into=into is not None,
                          emit_w=emit_w, step_len=step_len),
        grid=(rows // tm, nj),
        in_specs=in_specs,
        out_specs=out_specs,
        out_shape=out_shape,
        input_output_aliases=aliases,
        scratch_shapes=[pltpu.VMEM((nj, SUBLANES, tn), F32)],
        compiler_params=_params("arbitrary", "arbitrary"),
        name="ffn_in_fused",
    )(*args)
    return (outs[0], outs[1][bps - 1::bps]) + tuple(outs[2:])


def ffn_in_fused(x, w_in, layer, conv_w, conv_b, steps, *, seq_len, rows=None, sub=256):
    M = x.shape[0] if rows is None else rows
    FF = conv_w.shape[1]
    act, tail0, wg, wu, act_s, gate_s = _ffn_in_call(
        x, w_in, w_in, layer, FF, conv_w, conv_b, seq_len=seq_len, row0=0, rows=seq_len, tm=2048, tn=256,
        sub=sub, into=None, emit_w=True, steps=steps)
    if M == seq_len:
        return act, tail0, act_s, gate_s
    act, tails = _ffn_in_call(x, wg, wu, None, 0, conv_w, conv_b, seq_len=seq_len, row0=seq_len,
                              rows=M - seq_len, tm=1024, tn=512, sub=sub, into=act, emit_w=False)
    return act, jnp.concatenate([tail0, tails], axis=0), act_s, gate_s


def _rwkv_mix_kernel(x_ref, halo_ref, s0_ref, g_ref, mu_ref, w1_ref, a1_ref, g1_ref, *out_refs, n_valid_last):
    t = pl.program_id(1)
    nt = pl.num_programs(1)
    xr_ref, xk_ref, xv_ref, lw_ref, la_ref, lg_ref, hl_ref = out_refs

    def norm(v):
        return v * lax.rsqrt(jnp.mean(v * v, axis=-1, keepdims=True) + NORM_EPS) * g_ref[...]

    h = norm(x_ref[...])
    hp = norm(halo_ref[...])
    first = jnp.where(t > 0, 1.0, 0.0)
    hp = hp * first + jnp.broadcast_to(s0_ref[0], hp.shape) * (1.0 - first)
    xx = _shift_rows(h, hp, 1) - h
    mix = lambda j: (h + xx * mu_ref[j:j + 1, :]).astype(BF16)
    xr_ref[...] = mix(0)
    xk_ref[...] = mix(2)
    xv_ref[...] = mix(3)
    lw_ref[...] = jnp.tanh(_dot(mix(1), w1_ref[...])).astype(BF16)
    la_ref[...] = _dot(mix(4), a1_ref[...]).astype(BF16)
    lg_ref[...] = jax.nn.sigmoid(_dot(mix(5), g1_ref[...])).astype(BF16)

    @pl.when(t == nt - 1)
    def _():
        hl_ref[0] = h[n_valid_last - 1:n_valid_last, :]


def rwkv_mix(x, shift0, g, mu, w1, a1, g1, *, B, T, n_valid, tt=256, out_rows=None):
    D = x.shape[1]
    tt = min(tt, T)
    nt = T // tt
    hb = tt // SUBLANES
    out_rows = B * T if out_rows is None else out_rows
    n_valid_last = n_valid - (nt - 1) * tt
    blk = pl.BlockSpec((tt, D), lambda b, t: (b * nt + t, 0))
    one = pl.BlockSpec((1, 1, D), lambda b, t: (b, 0, 0))
    return pl.pallas_call(
        functools.partial(_rwkv_mix_kernel, n_valid_last=n_valid_last),
        grid=(B, nt),
        in_specs=[blk,
                  pl.BlockSpec((SUBLANES, D), lambda b, t: (jnp.maximum((b * nt + t) * hb - 1, 0), 0)),
                  one,
                  pl.BlockSpec((1, D), lambda b, t: (0, 0)),
                  pl.BlockSpec((6, D), lambda b, t: (0, 0))] +
                 [pl.BlockSpec(w.shape, lambda b, t: (0, 0)) for w in (w1, a1, g1)],
        out_specs=[blk] * 3 + [pl.BlockSpec((tt, w.shape[1]), lambda b, t: (b * nt + t, 0)) for w in (w1, a1, g1)]
                  + [one],
        out_shape=[jax.ShapeDtypeStruct((out_rows, D), BF16)] * 3 +
                  [jax.ShapeDtypeStruct((out_rows, w.shape[1]), BF16) for w in (w1, a1, g1)] +
                  [jax.ShapeDtypeStruct((B, 1, D), F32)],
        compiler_params=_params("parallel", "arbitrary"),
        name="rwkv_mix",
    )(x, x, shift0, g.reshape(1, D).astype(F32), mu.astype(F32), w1, a1, g1)


def _rwkv_chunk_kernel(r_ref, k_ref, v_ref, lw_ref, la_ref, lg_ref, w2_ref, a2_ref, g2_ref, w0_ref, a0_ref,
                       kk_ref, ka_ref, rk_ref, lng_ref, lnb_ref, s0_ref, o_ref, so_ref, s_scr, *, hs, C, n_valid):
    c = pl.program_id(2)
    nc = pl.num_programs(2)
    R, L = r_ref.shape
    n_pairs = L // LANES

    @pl.when(c == 0)
    def _():
        z = jnp.zeros((hs, hs), F32)
        for p in range(n_pairs):
            s_scr[p] = jnp.concatenate([jnp.concatenate([s0_ref[0, 2 * p], z], axis=1),
                                        jnp.concatenate([z, s0_ref[0, 2 * p + 1]], axis=1)], axis=0)

    ti = lax.broadcasted_iota(jnp.int32, (C, C), 0)
    si = lax.broadcasted_iota(jnp.int32, (C, C), 1)
    tri = jnp.where(ti >= si, 1.0, 0.0).astype(BF16)
    lane = lax.broadcasted_iota(jnp.int32, (1, LANES), 1)
    m0 = lane < hs
    C2 = 2 * C
    ri = lax.broadcasted_iota(jnp.int32, (C2, C2), 0)
    ci = lax.broadcasted_iota(jnp.int32, (C2, C2), 1)
    same = (ri // C) == (ci // C)
    strict = same & ((ri % C) > (ci % C))
    lower = (ri % C) >= (ci % C)
    vi = lax.broadcasted_iota(jnp.int32, (LANES, LANES), 0) // hs
    vj = lax.broadcasted_iota(jnp.int32, (LANES, LANES), 1) // hs
    blockdiag = vi == vj
    n_steps = int(math.log2(C))
    wide = C2 % LANES == 0
    P = range(n_pairs)
    sls = [slice(p * LANES, (p + 1) * LANES) for p in P]
    inv = 1.0 / hs

    def stack_heads(x):
        return jnp.concatenate([jnp.where(m0, x, 0.0), jnp.where(m0, 0.0, x)], axis=0).astype(BF16)

    def prep(sc):
        rows = slice(sc * C, (sc + 1) * C)
        r = r_ref[rows, :]
        k = k_ref[rows, :]
        v = v_ref[rows, :]
        wp = w0_ref[...] + _dot(lw_ref[rows, :], w2_ref[...])
        logw = -jnp.exp(-jax.nn.softplus(-wp) - 0.5)
        if n_valid is not None:
            logw = jnp.where(lax.broadcasted_iota(jnp.int32, (C, 1), 0) + sc * C < n_valid, logw, 0.0)
        a = jax.nn.sigmoid(a0_ref[...] + _dot(la_ref[rows, :], a2_ref[...]))
        kk = k * kk_ref[...]
        kk = kk / jnp.maximum(jnp.sqrt(_group_sum_bcast(kk * kk, hs, terms=1)), 1e-12)
        k2 = k * (1.0 + (a - 1.0) * ka_ref[...])
        bm = kk * a
        cum = _dot_exact_rhs_lhs(tri, logw)
        e_in = jnp.exp(cum)
        e_out = jnp.exp(-cum)
        e_end = jnp.exp(cum[C - 1:C, :] - cum)
        g_end = jnp.exp(cum[C - 1:C, :])
        rt = r * e_in
        kkt = kk * jnp.exp(cum - logw)
        bh = bm * e_out
        kh = k2 * e_out
        bbar = bm * e_end
        kbar = k2 * e_end
        return rows, r, v, k2, g_end, rt, kkt, bh, kh, bbar, kbar

    S = [s_scr[p] for p in P]
    nxt = prep(0)
    for sc in range(R // C):
        rows, r, v, k2, g_end, rt, kkt, bh, kh, bbar, kbar = nxt
        if sc + 1 < R // C:
            nxt = prep(sc + 1)
        S16 = [s_.astype(BF16) for s_ in S]
        V16 = [v[:, sl].astype(BF16) for sl in sls]
        lhk = [stack_heads(kkt[:, sl]) for sl in sls]
        if wide:
            gbk = [_nt(lhk[p], jnp.concatenate([stack_heads(bh[:, sls[p]]), stack_heads(kh[:, sls[p]])], axis=0))
                   for p in P]
            nmat = [jnp.where(strict, -g_[:, :C2], 0.0) for g_ in gbk]
            auk = [jnp.where(strict, -g_[:, C2:], 0.0).astype(BF16) for g_ in gbk]
        else:
            nmat = [jnp.where(strict, -_nt(lhk[p], stack_heads(bh[:, sls[p]])), 0.0) for p in P]
            auk = [jnp.where(strict, -_nt(lhk[p], stack_heads(kh[:, sls[p]])), 0.0).astype(BF16) for p in P]
        sprod = [_nt(jnp.concatenate([lhk[p], rt[:, sls[p]].astype(BF16)], axis=0), S16[p]) for p in P]
        u = [_dot(auk[p], jnp.concatenate([V16[p], V16[p]], axis=0)) - sprod[p][:C2] for p in P]
        npow = [n_.astype(BF16) for n_ in nmat]
        for it in range(n_steps):
            last = it + 1 == n_steps
            if wide and not last:
                res = [_dot(npow[p], jnp.concatenate([u[p].astype(BF16), npow[p]], axis=1)) for p in P]
                u = [u[p] + res[p][:, :LANES] for p in P]
                npow = [res[p][:, LANES:].astype(BF16) for p in P]
            else:
                u = [u[p] + _dot(npow[p], u[p].astype(BF16)) for p in P]
                if not last:
                    npow = [_dot(npow[p], npow[p]).astype(BF16) for p in P]
        uv = [jnp.concatenate([jnp.where(m0, u[p][:C], u[p][C:]).astype(BF16), V16[p]], axis=0) for p in P]
        ar = [jnp.where(lower, _nt(stack_heads(rt[:, sls[p]]),
                                   jnp.concatenate([bh[:, sls[p]], kh[:, sls[p]]], axis=0).astype(BF16)),
                        0.0).astype(BF16) for p in P]
        tmat = [_dot(ar[p], uv[p]) for p in P]
        ys = [sprod[p][C2:] + jnp.where(m0, tmat[p][:C], tmat[p][C:]) for p in P]
        upd = [_tn(uv[p], jnp.concatenate([bbar[:, sls[p]], kbar[:, sls[p]]], axis=0).astype(BF16)) for p in P]
        S = [S[p] * g_end[:, sls[p]] + jnp.where(blockdiag, upd[p], 0.0) for p in P]

        y = ys[0] if n_pairs == 1 else jnp.concatenate(ys, axis=-1)
        mean = _group_sum_bcast(y, hs, terms=1) * inv
        yc = y - mean
        var = _group_sum_bcast(yc * yc, hs, terms=1) * inv
        yn = yc * lax.rsqrt(var + RW_GN_EPS) * lng_ref[...] + lnb_ref[...]
        bonus = _group_sum_bcast(r * k2 * rk_ref[...], hs, terms=1) * v
        o_ref[rows, :] = ((yn + bonus) * _dot(lg_ref[rows, :], g2_ref[...])).astype(o_ref.dtype)

    for p in P:
        s_scr[p] = S[p]

    @pl.when(c == nc - 1)
    def _():
        for p in P:
            so_ref[0, 2 * p] = S[p][:hs, :hs]
            so_ref[0, 2 * p + 1] = S[p][hs:, hs:]


def _dot_exact_rhs_lhs(m_bf16, x):
    hi = x.astype(BF16)
    lo = (x - hi.astype(F32)).astype(BF16)
    return _dot(m_bf16, hi) + _dot(m_bf16, lo)


def rwkv_chunked(r, k, v, lw, la, lg, w2, a2, g2, w0, a0, k_k, k_a, r_k, ln_g, ln_b, s0, *, B, T, hs, chunk,
                 chunks_per_step=1, lanes_per_step=1024, out_rows=None, n_valid=None):
    D = r.shape[1]
    L = min(lanes_per_step, D)
    npg = L // LANES
    rows = chunk * chunks_per_step
    nc = T // rows
    out_rows = B * T if out_rows is None else out_rows
    seq = pl.BlockSpec((rows, L), lambda b, hg, c: (b * nc + c, hg))
    low = lambda a_: pl.BlockSpec((rows, a_.shape[1]), lambda b, hg, c: (b * nc + c, 0))
    wgt = lambda w_: pl.BlockSpec((w_.shape[0], L), lambda b, hg, c: (0, hg))
    par = pl.BlockSpec((1, L), lambda b, hg, c: (0, hg))
    st = pl.BlockSpec((1, 2 * npg, hs, hs), lambda b, hg, c: (b, hg, 0, 0))
    row = lambda x: x.reshape(1, D).astype(F32)
    return pl.pallas_call(
        functools.partial(_rwkv_chunk_kernel, hs=hs, C=chunk, n_valid=n_valid),
        grid=(B, D // L, nc),
        in_specs=[seq] * 3 + [low(lw), low(la), low(lg), wgt(w2), wgt(a2), wgt(g2)] + [par] * 7 + [st],
        out_specs=[seq, st],
        out_shape=[jax.ShapeDtypeStruct((out_rows, D), BF16),
                   jax.ShapeDtypeStruct(s0.shape, F32)],
        scratch_shapes=[pltpu.VMEM((npg, LANES, LANES), F32)],
        compiler_params=_params("parallel", "parallel", "arbitrary"),
        name="rwkv_chunked",
    )(r, k, v, lw, la, lg, w2, a2, g2, row(w0), row(a0), row(k_k), row(k_a), row(r_k), row(ln_g), row(ln_b), s0)


def _pad_rows(x, front, back):
    return jnp.pad(x, ((0, 0), (front, back), (0, 0)))


def kernel(x_prompt, x_sample, mem_prompt, state_lru_conv, state_lru_h, cache_swa_k, cache_swa_v,
           state_rwkv_shift, state_rwkv_wkv, cache_mem_k, cache_mem_v, state_ffn_conv,
           a_norm_g, a_w_in, a_conv_w, a_conv_b, a_gate_a_w, a_gate_a_b, a_gate_x_w, a_gate_x_b,
           a_lambda, b_q_norm_g, b_k_norm_g, b_sink, a_w_out,
           c_norm_g, c_mu, c_w_r, c_w_k, c_w_v, c_w_o, c_w0, c_w1, c_w2, c_a0, c_a1, c_a2,
           c_g1, c_g2, c_k_k, c_k_a, c_r_k, c_ln_g, c_ln_b,
           m_norm_g, m_mem_norm_g, m_w_q, m_w_kv, m_q_norm_g, m_k_norm_g, m_w_o,
           f_norm_g, f_w_in, f_conv_w, f_conv_b, f_w_out):
    D = x_prompt.shape[-1]
    depth = m_norm_g.shape[0]
    W = a_conv_w.shape[-1]
    KA = a_conv_w.shape[1]
    hd = b_q_norm_g.shape[-1]
    n_q = b_sink.shape[-1]
    n_kv = cache_swa_k.shape[3]
    win = cache_swa_k.shape[2]
    QW, KW = n_q * hd, n_kv * hd
    hs = c_r_k.shape[-1]
    n_rw = c_r_k.shape[1]
    m_heads, m_hd = cache_mem_k.shape[3], cache_mem_k.shape[4]
    MW = m_heads * m_hd
    FF = f_conv_w.shape[-1]
    KF = f_conv_w.shape[1]
    bf = lambda w: w.astype(BF16)
    wcache = {}

    Bp, S = x_prompt.shape[:2]
    Bs, Ts = x_sample.shape[:2]
    Mp, Ms = Bp * S, Bs * Ts
    Mt = Mp + Ms
    tma = _row_tile(Mt, 1280)
    ML = mem_prompt.shape[1]
    PAD8, PAD16 = SUBLANES, 2 * SUBLANES

    def mm(xin, w, layer, key, **kw):
        return dense(xin, w, layer, wcache, (key, layer), tm=tma, **kw)

    def seq_tail(arr, n, cols=slice(None)):
        return jnp.stack([arr[(b + 1) * S - n:(b + 1) * S, cols] for b in range(Bp)])

    def put_sample(buf, rows_s):
        return lax.dynamic_update_slice(buf, rows_s.astype(buf.dtype), (Mp, 0))

    mem_flat = mem_prompt.reshape(Bp * ML, D)
    mks, mvs = [], []
    for l in range(depth):
        mn = rmsnorm_rows(mem_flat, m_mem_norm_g[l])
        kv = matmul(mn, m_w_kv, layer=l)
        mk = headnorm(kv, 0, MW, m_k_norm_g[l], m_hd, F32)
        mks.append(mk.reshape(Bp, ML, MW))
        mvs.append(kv[:, MW:].reshape(Bp, ML, MW))
    mem_k_p, mem_v_p = jnp.stack(mks), jnp.stack(mvs)
    p_mem_k = mem_k_p.reshape(depth, Bp, ML, m_heads, m_hd)
    p_mem_v = mem_v_p.reshape(depth, Bp, ML, m_heads, m_hd)

    xp2, xs2 = x_prompt.reshape(Mp, D), x_sample.reshape(Ms, D)
    xf = None

    def first_norm(g):
        return put_sample(rmsnorm_rows(xp2, g, out_rows=Mt), rmsnorm_rows(xs2, g))

    def first_residual(xin, w, layer, key):
        y = dense(xin, w, layer, wcache, (key, layer), rows=Mp, out_rows=Mt, res=xp2)
        return matmul(xin, wcache[(key, layer)], row0=Mp, rows=Ms, res=xs2, res_row0=0, into=y)

    po ={k_: [] for k_ in ("lc", "lh", "sk", "sv", "rs", "rw", "fc")}
    so = {k_: [] for k_ in ("lc", "lh", "sk", "sv", "rs", "rw", "fc")}
    ia = ic = 0
    y_p = y_s = None
    for l in range(depth):
        if l % 2 == 0:
            i = ia
            ia += 1
            assert W == QW
            h = first_norm(a_norm_g[i]) if xf is None else rmsnorm_rows(xf, a_norm_g[i])
            zz = mm(h, a_w_in, i, "a_w_in_rg", n=2 * W)
            qkv = mm(h, a_w_in, i, "a_w_in_qkv", n0=2 * W)
            qn = headnorm(qkv, 0, QW, b_q_norm_g[i], hd, BF16)
            kn = headnorm(qkv, QW // KW, KW, b_k_norm_g[i], hd, F32)
            vv = qkv[:, QW + KW:]
            lru_w = (a_conv_w[i], a_conv_b[i], a_gate_a_w[i], a_gate_a_b[i], a_gate_x_w[i], a_gate_x_b[i],
                     a_lambda[i])
            mix, hl_p = lru_mixer(zz, 0, zz, 1, jnp.zeros((Bp, 1, W), F32), *lru_w, B=Bp, T=S, n_pad=0, tt=256,
                                  out_rows=Mt, out_cols=W + QW)
            mix = swa_attention(qn, bf(kn), bf(vv), b_sink[i], B=Bp, nq=S // win, nkb=S // win, j0=0, win=win,
                                hd=hd, into=mix, out_cb=W // QW)
            po["lc"].append(seq_tail(zz, KA - 1, slice(0, W)))
            po["lh"].append(hl_p.reshape(Bp, W))
            po["sk"].append(seq_tail(kn, win).reshape(Bp, win, n_kv, hd))
            po["sv"].append(seq_tail(vv, win).reshape(Bp, win, n_kv, hd))
            zz_s = zz[Mp:].reshape(Bs, Ts, 2 * W)
            n_pad = PAD8 - Ts
            xr_hist = jnp.concatenate([state_lru_conv[i].astype(F32), zz_s[:, :, :W]], axis=1)
            xr_p = _pad_rows(xr_hist, PAD8 - xr_hist.shape[1], 0).reshape(Bs * PAD8, W)
            yg_p = _pad_rows(zz_s[:, :, W:], n_pad, 0).reshape(Bs * PAD8, W)
            oa_s, hl_s = lru_mixer(xr_p, 0, yg_p, 0, state_lru_h[i].reshape(Bs, 1, W).astype(F32), *lru_w,
                                   B=Bs, T=PAD8, n_pad=n_pad, tt=PAD8)
            oa_s = oa_s.reshape(Bs, PAD8, W)[:, n_pad:].reshape(Ms, W)
            kc = cache_swa_k[i].reshape(Bs, win, KW)
            vc = cache_swa_v[i].reshape(Bs, win, KW)
            k_all = jnp.concatenate([kc, kn[Mp:].reshape(Bs, Ts, KW)], axis=1)
            v_all = jnp.concatenate([vc, vv[Mp:].reshape(Bs, Ts, KW)], axis=1)
            q_s = _pad_rows(qn[Mp:].reshape(Bs, Ts, QW), 0, PAD16 - Ts).reshape(Bs * PAD16, QW)
            o_s = swa_attention(q_s, bf(_pad_rows(k_all, 0, win - Ts)).reshape(Bs * 2 * win, KW),
                                bf(_pad_rows(v_all, 0, win - Ts)).reshape(Bs * 2 * win, KW), b_sink[i],
                                B=Bs, nq=1, nkb=2, j0=1, win=win, hd=hd, tq=PAD16)
            o_s = o_s.reshape(Bs, PAD16, QW)[:, :Ts].reshape(Ms, QW)
            mix = put_sample(mix, jnp.concatenate([oa_s, o_s], axis=-1))
            so["lc"].append(xr_hist[:, Ts:])
            so["lh"].append(hl_s.reshape(Bs, W))
            so["sk"].append(k_all[:, Ts:].reshape(Bs, win, n_kv, hd))
            so["sv"].append(v_all[:, Ts:].reshape(Bs, win, n_kv, hd))
            xf = (first_residual(mix, a_w_out, i, "a_w_out") if xf is None
                  else mm(mix, a_w_out, i, "a_w_out", res=xf))
        else:
            i = ic
            ic += 1
            gl = c_g1[i].shape[1]
            glp = -(-gl // LANES) * LANES
            lora1 = (bf(c_w1[i]), bf(c_a1[i]), bf(jnp.pad(c_g1[i], ((0, 0), (0, glp - gl)))))
            lora2 = (bf(c_w2[i]), bf(c_a2[i]), bf(jnp.pad(c_g2[i], ((0, glp - gl), (0, 0)))), c_w0[i], c_a0[i])
            mixes = rwkv_mix(xf, jnp.zeros((Bp, 1, D), F32), c_norm_g[i], c_mu[i], *lora1, B=Bp, T=S, n_valid=S,
                             out_rows=Mt)
            x_s = _pad_rows(xf[Mp:].reshape(Bs, Ts, D), 0, PAD8 - Ts).reshape(Bs * PAD8, D)
            mixes_s = rwkv_mix(x_s, state_rwkv_shift[i].reshape(Bs, 1, D).astype(F32), c_norm_g[i], c_mu[i], *lora1,
                               B=Bs, T=PAD8, n_valid=Ts, tt=PAD8)
            take = lambda m: m.reshape(Bs, PAD8, m.shape[-1])[:, :Ts].reshape(Ms, m.shape[-1])
            xr, xk, xv, lw, la, lg = [put_sample(mixes[j], take(mixes_s[j])) for j in range(6)]
            po["rs"].append(mixes[6].reshape(Bp, D))
            so["rs"].append(mixes_s[6].reshape(Bs, D))
            r = mm(xr, c_w_r, i, "c_w_r")
            k = mm(xk, c_w_k, i, "c_w_k")
            v = mm(xv, c_w_v, i, "c_w_v")
            rw_par = (c_k_k[i], c_k_a[i], c_r_k[i], c_ln_g[i], c_ln_b[i])
            yo, s_end = rwkv_chunked(r, k, v, lw, la, lg, *lora2, *rw_par, jnp.zeros((Bp, n_rw, hs, hs), F32),
                                     B=Bp, T=S, hs=hs, chunk=64, chunks_per_step=8, out_rows=Mt)
            po["rw"].append(s_end)
            cs = PAD16
            sq = lambda t: jnp.pad(t[Mp:].reshape(Bs, Ts, t.shape[-1]),
                                   ((0, 0), (0, cs - Ts), (0, 0))).reshape(Bs * cs, t.shape[-1])
            yo_s, s_end_s = rwkv_chunked(sq(r), sq(k), sq(v), sq(lw), sq(la), sq(lg), *lora2, *rw_par,
                                         state_rwkv_wkv[i].astype(F32), B=Bs, T=cs, hs=hs, chunk=cs,
                                         lanes_per_step=2048, n_valid=Ts)
            yo = put_sample(yo, yo_s.reshape(Bs, cs, D)[:, :Ts].reshape(Ms, D))
            so["rw"].append(s_end_s)
            xf = mm(yo, c_w_o, i, "c_w_o", res=xf)
        q = norm_proj(xf, m_norm_g[l], m_w_q, l)
        om = mem_attention(q, m_q_norm_g[l], mem_k_p, mem_v_p, l, B=Bp, T=S, n_heads=m_heads, out_rows=Mt)
        q_s = _pad_rows(q[Mp:].reshape(Bs, Ts, MW), 0, PAD8 - Ts).reshape(Bs * PAD8, MW)
        om_s = mem_attention(q_s, m_q_norm_g[l], cache_mem_k, cache_mem_v, l, B=Bs, T=PAD8, n_heads=m_heads, tt=PAD8)
        om = put_sample(om, om_s.reshape(Bs, PAD8, MW)[:, :Ts].reshape(Ms, MW))
        xf, hf = proj_res_norm(om, m_w_o, l, xf, f_norm_g[l])
        assert Ts >= KF - 1 == 2
        st_f = state_ffn_conv[l].astype(F32)
        zrow = jnp.zeros((Bs, Ts - 1, FF), F32)
        h1 = jnp.concatenate([st_f[:, 1:2], zrow], axis=1).reshape(Ms, FF)
        h2 = jnp.concatenate([st_f, zrow[:, 1:]], axis=1).reshape(Ms, FF)
        act, tail, act_s, gate_s = ffn_in_fused(hf, f_w_in, l, f_conv_w[l], f_conv_b[l], (hf[Mp:], h1, h2, Ts),
                                                seq_len=S, rows=Mp)
        po["fc"].append(tail[:, SUBLANES - (KF - 1):, :])
        act = put_sample(act, act_s)
        so["fc"].append(gate_s.reshape(Bs, Ts, FF)[:, Ts - (KF - 1):])
        if l + 1 < depth:
            xf = mm(act, f_w_out, l, "f_w_out", res=xf, tn=1024, tk=2048)
        else:
            y_p = dense(act, f_w_out, l, wcache, ("f_w_out", l), rows=Mp, out_rows=Mp, res=xf, tn=1024, tk=2048)
            y_s = matmul(act, wcache[("f_w_out", l)], row0=Mp, rows=Ms, out_rows=Ms, out_row0=0, res=xf,
                         tn=1024, tk=2048)
    st = jnp.stack
    return (y_p.reshape(Bp, S, D), y_s.reshape(Bs, Ts, D),
            st(po["lc"]), st(po["lh"]), st(po["sk"]), st(po["sv"]), st(po["rs"]), st(po["rw"]),
            p_mem_k, p_mem_v, st(po["fc"]),
            st(so["lc"]), st(so["lh"]), st(so["sk"]), st(so["sv"]), st(so["rs"]), st(so["rw"]), st(so["fc"]))
```

```python
import functools
import math

import jax
import jax.numpy as jnp
from jax import lax
from jax.experimental import pallas as pl
from jax.experimental.pallas import tpu as pltpu

F32 = jnp.float32
BF16 = jnp.bfloat16

NORM_EPS = 1e-6
RW_GN_EPS = 64e-5
LRU_C = 8.0
LANES = 128
SUBLANES = 8
VMEM_LIMIT_BYTES = 56 * 1024 * 1024


def _params(*sem):
    return pltpu.CompilerParams(dimension_semantics=sem, vmem_limit_bytes=VMEM_LIMIT_BYTES)


def _nt(a, b):
    return lax.dot_general(a, b, (((1,), (1,)), ((), ())), preferred_element_type=F32)


def _tn(a, b):
    return lax.dot_general(a, b, (((0,), (0,)), ((), ())), preferred_element_type=F32)


def _dot(a, b):
    return jnp.dot(a, b, preferred_element_type=F32)


def _group_sum_bcast(x, width, terms=2):
    m, L = x.shape
    n = L // LANES
    if width == LANES:
        parts = []
        for c in range(n):
            s = jnp.sum(x[:, c * LANES:(c + 1) * LANES], axis=-1, keepdims=True)
            parts.append(jnp.broadcast_to(s, (m, LANES)))
        return parts[0] if n == 1 else jnp.concatenate(parts, axis=-1)
    if terms == 0:
        assert width * 2 == LANES
        m0 = lax.broadcasted_iota(jnp.int32, (1, LANES), 1) < width
        parts = []
        for c in range(n):
            xt = x[:, c * LANES:(c + 1) * LANES]
            s0 = jnp.sum(jnp.where(m0, xt, 0.0), axis=-1, keepdims=True)
            s1 = jnp.sum(jnp.where(m0, 0.0, xt), axis=-1, keepdims=True)
            parts.append(jnp.where(m0, s0, s1))
        return parts[0] if n == 1 else jnp.concatenate(parts, axis=-1)
    li = lax.broadcasted_iota(jnp.int32, (LANES, LANES), 0) // width
    lj = lax.broadcasted_iota(jnp.int32, (LANES, LANES), 1) // width
    e = jnp.where(li == lj, 1.0, 0.0).astype(BF16)
    xs = x if n == 1 else jnp.concatenate([x[:, c * LANES:(c + 1) * LANES] for c in range(n)], axis=0)
    hi = xs.astype(BF16)
    out = _dot(hi, e)
    if terms > 1:
        out = out + _dot((xs - hi.astype(F32)).astype(BF16), e)
    return out if n == 1 else jnp.concatenate([out[c * m:(c + 1) * m] for c in range(n)], axis=-1)


def _shift_rows(x, prev8, s):
    rolled = pltpu.roll(x, s, 0)
    top = jnp.where(lax.broadcasted_iota(jnp.int32, (SUBLANES, 1), 0) < s,
                    pltpu.roll(prev8, s, 0), rolled[0:SUBLANES])
    if x.shape[0] == SUBLANES:
        return top
    return jnp.concatenate([top, rolled[SUBLANES:]], axis=0)


def _rmsnorm_kernel(x_ref, g_ref, o_ref):
    x = x_ref[...]
    ms = jnp.mean(x * x, axis=-1, keepdims=True)
    o_ref[...] = (x * lax.rsqrt(ms + NORM_EPS) * g_ref[...]).astype(o_ref.dtype)


def _row_tile(m, target):
    best = None
    for d in range(16, min(m, target) + 1, 16):
        if m % d == 0:
            best = d
    return best or m


def rmsnorm_rows(x, g, out_dtype=BF16, tm=512, out_rows=None):
    M, D = x.shape
    tm = _row_tile(M, tm)
    return pl.pallas_call(
        _rmsnorm_kernel,
        grid=(M // tm,),
        in_specs=[pl.BlockSpec((tm, D), lambda i: (i, 0)), pl.BlockSpec((1, D), lambda i: (0, 0))],
        out_specs=pl.BlockSpec((tm, D), lambda i: (i, 0)),
        out_shape=jax.ShapeDtypeStruct((M if out_rows is None else out_rows, D), out_dtype),
        compiler_params=_params("parallel"),
        name="rmsnorm_rows",
    )(x, g.reshape(1, D))


def _mm_kernel(*refs, nk, has_bias, has_res, has_into, emit_w, in_place, act):
    x_ref, w_ref = refs[0], refs[1]
    pos = 2
    b_ref = r_ref = wb_ref = acc_ref = None
    if has_bias:
        b_ref = refs[pos]
        pos += 1
    if has_res:
        r_ref = refs[pos]
        pos += 1
    if has_into:
        pos += 1
    o_ref = refs[pos]
    pos += 1
    if emit_w:
        wb_ref = refs[pos]
        pos += 1
    if nk > 1 and not in_place:
        acc_ref = refs[pos]

    def weights(cs=slice(None)):
        w16 = w_ref[:, cs].astype(BF16)
        if emit_w:
            wb_ref[:, cs] = w16
        return w16

    def epilogue(y):
        if has_bias:
            y = y + b_ref[...]
        if act == "tanh":
            y = jnp.tanh(y)
        elif act == "sigmoid":
            y = jax.nn.sigmoid(y)
        if has_res:
            y = y + r_ref[...]
        o_ref[...] = y.astype(o_ref.dtype)

    if nk > 1 and in_place:
        k = pl.program_id(2)
        tn = o_ref.shape[1]
        cw = min(tn, 2 * LANES)

        def sweep(first):
            x = x_ref[...]
            for c in range(tn // cw):
                cs = slice(c * cw, (c + 1) * cw)
                part = _dot(x, weights(cs))
                if not first:
                    o_ref[:, cs] += part
                elif has_res:
                    o_ref[:, cs] = part + r_ref[:, cs]
                else:
                    o_ref[:, cs] = part

        pl.when(k == 0)(functools.partial(sweep, True))
        pl.when(k > 0)(functools.partial(sweep, False))
        return

    part = _dot(x_ref[...], weights())
    if nk == 1:
        epilogue(part)
        return
    k = pl.program_id(2)

    @pl.when(k == 0)
    def _():
        acc_ref[...] = part

    @pl.when(k > 0)
    def _():
        acc_ref[...] += part

    @pl.when(k == nk - 1)
    def _():
        epilogue(acc_ref[...])


def _pick(n, prefs):
    for p in prefs:
        if n % p == 0:
            return p
    return n


def matmul(x, w, *, layer=None, n0=0, n=None, bias=None, res=None, act=None, out_dtype=F32,
           tm=1024, tn=512, tk=4096, row0=0, rows=None, into=None, emit_w=False, out_rows=None, out_row0=None,
           res_row0=None):
    M, K = x.shape
    N = w.shape[-1] - n0 if n is None else n
    rows = M - row0 if rows is None else rows
    tm = min(tm, rows)
    out_rows = M if out_rows is None else out_rows
    o0 = (row0 if out_row0 is None else out_row0) // tm
    tn = _pick(math.gcd(N, n0) if n0 else N, (tn, 512, 256, 128))
    tk = _pick(K, (tk, 2048, 1024, 512))
    nk = K // tk
    j0 = n0 // tn
    i0 = row0 // tm
    assert row0 % tm == 0 and rows % tm == 0 and (not emit_w or rows == tm)
    in_specs = [pl.BlockSpec((tm, tk), lambda i, j, k: (i + i0, k)),
                pl.BlockSpec((tk, tn), lambda i, j, k: (k, j + j0)) if layer is None else
                pl.BlockSpec((None, tk, tn), lambda i, j, k: (layer, k, j + j0))]
    args = [x, w]
    if bias is not None:
        in_specs.append(pl.BlockSpec((1, tn), lambda i, j, k: (0, j)))
        args.append(bias.reshape(1, N).astype(F32))
    if res is not None:
        r0 = i0 if res_row0 is None else res_row0 // tm
        in_specs.append(pl.BlockSpec((tm, tn), lambda i, j, k: (i + r0, j)))
        args.append(res)
    aliases = {}
    if into is not None:
        aliases = {len(args): 0}
        in_specs.append(pl.BlockSpec(memory_space=pl.ANY))
        args.append(into)
    in_place = bias is None and act is None and out_dtype == F32
    kern = functools.partial(_mm_kernel, nk=nk, has_bias=bias is not None, has_res=res is not None,
                             has_into=into is not None, emit_w=emit_w, in_place=in_place, act=act)
    out_specs = [pl.BlockSpec((tm, tn), lambda i, j, k: (i + o0, j))]
    out_shape = [jax.ShapeDtypeStruct((out_rows, N), out_dtype)]
    if emit_w:
        out_specs.append(pl.BlockSpec((tk, tn), lambda i, j, k: (k, j)))
        out_shape.append(jax.ShapeDtypeStruct((K, N), BF16))
    outs = pl.pallas_call(
        kern,
        grid=(rows // tm, N // tn, nk),
        in_specs=in_specs,
        out_specs=out_specs,
        out_shape=out_shape,
        input_output_aliases=aliases,
        scratch_shapes=[pltpu.VMEM((tm, tn), F32)] if (nk > 1 and not in_place) else [],
        compiler_params=_params("parallel", "parallel", "arbitrary"),
        name="matmul",
    )(*args)
    return outs if emit_w else outs[0]


def dense(x, w, layer, wcache, key, **kw):
    rows = kw.pop("rows", x.shape[0])
    tm = min(kw.get("tm", 1024), rows)
    first, wb = matmul(x, w, layer=layer, rows=tm, emit_w=True, **kw)
    wcache[key] = wb
    if tm == rows:
        return first
    kw = {a: b for a, b in kw.items() if a not in ("n0", "n")}
    return matmul(x, wb, row0=tm, rows=rows - tm, into=first, **kw)


def _proj_res_norm_kernel(x_ref, w_ref, r_ref, g_ref, y_ref, h_ref, w_scr):
    @pl.when(pl.program_id(0) == 0)
    def _():
        w_scr[...] = w_ref[...].astype(BF16)

    y = _dot(x_ref[...], w_scr[...]) + r_ref[...]
    y_ref[...] = y
    ms = jnp.mean(y * y, axis=-1, keepdims=True)
    h_ref[...] = (y * lax.rsqrt(ms + NORM_EPS) * g_ref[...]).astype(h_ref.dtype)


def proj_res_norm(x, w, layer, res, g, tm=320):
    M, K = x.shape
    N = w.shape[-1]
    tm = _row_tile(M, tm)
    row_blk = lambda c: pl.BlockSpec((tm, c), lambda i: (i, 0))
    return pl.pallas_call(
        _proj_res_norm_kernel,
        grid=(M // tm,),
        in_specs=[row_blk(K), pl.BlockSpec((None, K, N), lambda i: (layer, 0, 0)), row_blk(N),
                  pl.BlockSpec((1, N), lambda i: (0, 0))],
        out_specs=[row_blk(N), row_blk(N)],
        out_shape=[jax.ShapeDtypeStruct((M, N), F32), jax.ShapeDtypeStruct((M, N), BF16)],
        scratch_shapes=[pltpu.VMEM((K, N), BF16)],
        compiler_params=_params("arbitrary"),
        name="proj_res_norm",
    )(x, w, res, g.reshape(1, N).astype(F32))


def _norm_proj_kernel(x_ref, g_ref, w_ref, y_ref, w_scr):
    @pl.when(pl.program_id(0) == 0)
    def _():
        w_scr[...] = w_ref[...].astype(BF16)

    x = x_ref[...]
    ms = jnp.mean(x * x, axis=-1, keepdims=True)
    h = (x * lax.rsqrt(ms + NORM_EPS) * g_ref[...]).astype(BF16)
    y_ref[...] = _dot(h, w_scr[...])


def norm_proj(x, g, w, layer, tm=320):
    M, D = x.shape
    N = w.shape[-1]
    tm = _row_tile(M, tm)
    return pl.pallas_call(
        _norm_proj_kernel,
        grid=(M // tm,),
        in_specs=[pl.BlockSpec((tm, D), lambda i: (i, 0)), pl.BlockSpec((1, D), lambda i: (0, 0)),
                  pl.BlockSpec((None, D, N), lambda i: (layer, 0, 0))],
        out_specs=pl.BlockSpec((tm, N), lambda i: (i, 0)),
        out_shape=jax.ShapeDtypeStruct((M, N), F32),
        scratch_shapes=[pltpu.VMEM((D, N), BF16)],
        compiler_params=_params("arbitrary"),
        name="norm_proj",
    )(x, g.reshape(1, D).astype(F32), w)


def _headnorm_kernel(x_ref, g_ref, o_ref, *, hd):
    x = x_ref[...]
    ms = _group_sum_bcast(x * x, hd) * (1.0 / hd)
    o_ref[...] = (x * lax.rsqrt(ms + NORM_EPS) * g_ref[...]).astype(o_ref.dtype)


def headnorm(x, col_block, width, g, hd, out_dtype, tm=512):
    M = x.shape[0]
    tm = _row_tile(M, tm)
    g_row = jnp.tile(g.astype(F32), width // hd).reshape(1, width)
    return pl.pallas_call(
        functools.partial(_headnorm_kernel, hd=hd),
        grid=(M // tm,),
        in_specs=[pl.BlockSpec((tm, width), lambda i: (i, col_block)),
                  pl.BlockSpec((1, width), lambda i: (0, 0))],
        out_specs=pl.BlockSpec((tm, width), lambda i: (i, 0)),
        out_shape=jax.ShapeDtypeStruct((M, width), out_dtype),
        compiler_params=_params("parallel"),
        name="headnorm",
    )(x, g_row)


def _lru_kernel(xr_ref, halo_ref, yg_ref, h0_ref, cw_ref, cb_ref, gaw_ref, gab_ref, gxw_ref, gxb_ref,
                lam_ref, o_ref, hl_ref, a_scr, u_scr, h_scr, *, n_pad, n_blocks, bw, scan_w):
    t = pl.program_id(1)
    tt, W = a_scr.shape

    @pl.when(t == 0)
    def _():
        h_scr[...] = h0_ref[0]

    x = xr_ref[...]
    prev = halo_ref[...] * jnp.where(t > 0, 1.0, 0.0)
    xc = cb_ref[...] + cw_ref[3:4, :] * x
    for s in (1, 2, 3):
        xc = xc + cw_ref[3 - s:4 - s, :] * _shift_rows(x, prev, s)

    nsp = -LRU_C * jax.nn.softplus(-lam_ref[...])
    if n_pad:
        live = (lax.broadcasted_iota(jnp.int32, (tt, 1), 0) >= n_pad) | (t > 0)
    for n in range(n_blocks):
        sl = slice(n * bw, (n + 1) * bw)
        xb = xc[:, sl]
        xb16 = xb.astype(BF16)
        r = jax.nn.sigmoid(_dot(xb16, gaw_ref[n]) + gab_ref[:, sl])
        i = jax.nn.sigmoid(_dot(xb16, gxw_ref[n]) + gxb_ref[:, sl])
        log_a = r * nsp[:, sl]
        a = jnp.exp(log_a)
        u = jnp.sqrt(jnp.maximum(-jnp.tanh(log_a) * (a * a + 1.0), 0.0)) * (i * xb)
        if n_pad:
            a = jnp.where(live, a, 1.0)
            u = jnp.where(live, u, 0.0)
        a_scr[:, sl] = a
        u_scr[:, sl] = u

    row8 = lax.broadcasted_iota(jnp.int32, (SUBLANES, 1), 0)
    for c in range(W // scan_w):
        cs = slice(c * scan_w, (c + 1) * scan_w)

        def body(gi, h, cs=cs):
            r0 = pl.multiple_of(gi * SUBLANES, SUBLANES)
            A = a_scr[pl.ds(r0, SUBLANES), cs]
            U = u_scr[pl.ds(r0, SUBLANES), cs]
            for s in (1, 2, 4):
                As = pltpu.roll(A, s, 0)
                Us = pltpu.roll(U, s, 0)
                m = row8 >= s
                U = jnp.where(m, A * Us + U, U)
                A = jnp.where(m, A * As, A)
            H = A * h + U
            u_scr[pl.ds(r0, SUBLANES), cs] = H
            return H[SUBLANES - 1:SUBLANES, :]

        h_end = lax.fori_loop(0, tt // SUBLANES, body, h_scr[:, cs])
        h_scr[:, cs] = h_end

    o_ref[...] = (u_scr[...] * jax.nn.gelu(yg_ref[...], approximate=True)).astype(o_ref.dtype)
    hl_ref[0] = h_scr[...]


def lru_mixer(xr, xr_cb, yg, yg_cb, h0, conv_w, conv_b, ga_w, ga_b, gx_w, gx_b, lam, *, B, T, n_pad, tt,
              out_rows=None, out_cols=None):
    W = conv_w.shape[1]
    nb, bw = ga_w.shape[0], ga_w.shape[1]
    tt = min(tt, T)
    nt = T // tt
    hb = tt // SUBLANES
    out_rows = B * T if out_rows is None else out_rows
    out_cols = W if out_cols is None else out_cols
    row = lambda v: v.reshape(1, W).astype(F32)
    kern = functools.partial(_lru_kernel, n_pad=n_pad, n_blocks=nb, bw=bw, scan_w=min(W, 512))
    return pl.pallas_call(
        kern,
        grid=(B, nt),
        in_specs=[
            pl.BlockSpec((tt, W), lambda b, t: (b * nt + t, xr_cb)),
            pl.BlockSpec((SUBLANES, W), lambda b, t: (jnp.maximum((b * nt + t) * hb - 1, 0), xr_cb)),
            pl.BlockSpec((tt, W), lambda b, t: (b * nt + t, yg_cb)),
            pl.BlockSpec((1, 1, W), lambda b, t: (b, 0, 0)),
            pl.BlockSpec((4, W), lambda b, t: (0, 0)),
            pl.BlockSpec((1, W), lambda b, t: (0, 0)),
            pl.BlockSpec((nb, bw, bw), lambda b, t: (0, 0, 0)),
            pl.BlockSpec((1, W), lambda b, t: (0, 0)),
            pl.BlockSpec((nb, bw, bw), lambda b, t: (0, 0, 0)),
            pl.BlockSpec((1, W), lambda b, t: (0, 0)),
            pl.BlockSpec((1, W), lambda b, t: (0, 0)),
        ],
        out_specs=[pl.BlockSpec((tt, W), lambda b, t: (b * nt + t, 0)),
                   pl.BlockSpec((1, 1, W), lambda b, t: (b, 0, 0))],
        out_shape=[jax.ShapeDtypeStruct((out_rows, out_cols), BF16), jax.ShapeDtypeStruct((B, 1, W), F32)],
        scratch_shapes=[pltpu.VMEM((tt, W), F32), pltpu.VMEM((tt, W), F32), pltpu.VMEM((1, W), F32)],
        compiler_params=_params("parallel", "arbitrary"),
        name="lru_mixer",
    )(xr, xr, yg, h0, conv_w.astype(F32), row(conv_b), ga_w.astype(BF16), row(ga_b),
      gx_w.astype(BF16), row(gx_b), row(lam))


def _swa_kernel(sink_ref, q_ref, kp_ref, kc_ref, vp_ref, vc_ref, *rest, j0, n_kv, group, hd, win):
    o_ref = rest[-1]
    j = pl.program_id(1) + j0
    q = q_ref[...]
    k2 = jnp.concatenate([kp_ref[...], kc_ref[...]], axis=0)
    v2 = jnp.concatenate([vp_ref[...], vc_ref[...]], axis=0)
    tq = q.shape[0]
    qi = lax.broadcasted_iota(jnp.int32, (tq, 2 * win), 0)
    kj = lax.broadcasted_iota(jnp.int32, (tq, 2 * win), 1)
    dist = qi + win - kj
    mask = (dist >= 0) & (dist < win) & ((j > 0) | (kj >= win))
    scale = hd ** -0.5
    for kvh in range(n_kv):
        kh = k2[:, kvh * hd:(kvh + 1) * hd]
        vh = v2[:, kvh * hd:(kvh + 1) * hd]
        hs_ = [kvh * group + g for g in range(group)]
        s = [jnp.where(mask, _nt(q[:, h * hd:(h + 1) * hd], kh) * scale, -jnp.inf) for h in hs_]
        m = [jnp.maximum(jnp.max(s[g], axis=-1, keepdims=True), sink_ref[hs_[g]]) for g in range(group)]
        p = [jnp.exp(s[g] - m[g]) for g in range(group)]
        den = [jnp.sum(p[g], axis=-1, keepdims=True) + jnp.exp(sink_ref[hs_[g]] - m[g]) for g in range(group)]
        outs = [_dot(p[g].astype(BF16), vh) / den[g] for g in range(group)]
        o_ref[:, hs_[0] * hd:(hs_[-1] + 1) * hd] = jnp.concatenate(outs, axis=-1).astype(o_ref.dtype)


def swa_attention(q, k, v, sink, *, B, nq, nkb, j0, win, hd, tq=None, into=None, out_cb=0):
    QW = q.shape[1]
    KW = k.shape[1]
    n_kv = KW // hd
    group = QW // KW
    tq = win if tq is None else tq
    kern = functools.partial(_swa_kernel, j0=j0, n_kv=n_kv, group=group, hd=hd, win=win)
    prev = lambda b, j: (b * nkb + jnp.maximum(j + j0 - 1, 0), 0)
    cur = lambda b, j: (b * nkb + j + j0, 0)
    in_specs = [pl.BlockSpec(memory_space=pltpu.SMEM),
                pl.BlockSpec((tq, QW), lambda b, j: (b * nq + j, 0)),
                pl.BlockSpec((win, KW), prev), pl.BlockSpec((win, KW), cur),
                pl.BlockSpec((win, KW), prev), pl.BlockSpec((win, KW), cur)]
    args = [sink.astype(F32), q, k, k, v, v]
    aliases = {}
    out_shape = jax.ShapeDtypeStruct((B * nq * tq, QW), BF16)
    if into is not None:
        aliases = {len(args): 0}
        in_specs.append(pl.BlockSpec(memory_space=pl.ANY))
        args.append(into)
        out_shape = jax.ShapeDtypeStruct(into.shape, into.dtype)
    return pl.pallas_call(
        kern,
        grid=(B, nq),
        in_specs=in_specs,
        out_specs=pl.BlockSpec((tq, QW), lambda b, j: (b * nq + j, out_cb)),
        out_shape=out_shape,
        input_output_aliases=aliases,
        compiler_params=_params("parallel", "parallel"),
        name="swa_attention",
    )(*args)


def _mem_attn_kernel(q_ref, g_ref, mk_ref, mv_ref, o_ref, *, n_heads, hd):
    q = q_ref[...]
    scale = hd ** -0.5
    head = (lambda ref, h: ref[0, :, h, :]) if len(mk_ref.shape) == 4 else \
           (lambda ref, h: ref[0, :, h * hd:(h + 1) * hd])
    for h in range(n_heads):
        sl = slice(h * hd, (h + 1) * hd)
        qh = q[:, sl]
        qn = (qh * lax.rsqrt(jnp.mean(qh * qh, axis=-1, keepdims=True) + NORM_EPS) * g_ref[...]).astype(BF16)
        s = _nt(qn, head(mk_ref, h).astype(BF16)) * scale
        m = jnp.max(s, axis=-1, keepdims=True)
        p = jnp.exp(s - m)
        p = p / jnp.sum(p, axis=-1, keepdims=True)
        o_ref[:, sl] = _dot(p.astype(BF16), head(mv_ref, h).astype(BF16)).astype(o_ref.dtype)


def mem_attention(q, qn_g, mk, mv, layer, *, B, T, n_heads, tt=256, out_rows=None):
    MW = q.shape[1]
    ML = mk.shape[2]
    hd = MW // n_heads
    tt = min(tt, T)
    nt = T // tt
    out_rows = B * T if out_rows is None else out_rows
    mem = pl.BlockSpec((None, 1) + mk.shape[2:], lambda b, t: (layer, b) + (0,) * (mk.ndim - 2))
    return pl.pallas_call(
        functools.partial(_mem_attn_kernel, n_heads=n_heads, hd=hd),
        grid=(B, nt),
        in_specs=[pl.BlockSpec((tt, MW), lambda b, t: (b * nt + t, 0)),
                  pl.BlockSpec((1, hd), lambda b, t: (0, 0)),
                  mem, mem],
        out_specs=pl.BlockSpec((tt, MW), lambda b, t: (b * nt + t, 0)),
        out_shape=jax.ShapeDtypeStruct((out_rows, MW), BF16),
        compiler_params=_params("parallel", "parallel"),
        name="mem_attention",
    )(q, qn_g.reshape(1, hd).astype(F32), mk, mv)


def _ffn_in_kernel(*refs, blocks_per_seq, sub, has_into, emit_w, step_len):
    x_ref, wg_ref, wu_ref, cw_ref, cb_ref = refs[:5]
    pos = 6 if has_into else 5
    if step_len:
        xs_ref, h1_ref, h2_ref = refs[pos:pos + 3]
        pos += 3
    o_ref, tail_ref = refs[pos], refs[pos + 1]
    carry_scr = refs[-1]
    i = pl.program_id(0)
    j = pl.program_id(1)
    tm = x_ref.shape[0]

    @pl.when(i % blocks_per_seq == 0)
    def _():
        carry_scr[j] = jnp.zeros(carry_scr.shape[1:], F32)

    prev = carry_scr[j]
    wg = wg_ref[...].astype(BF16)
    wu = wu_ref[...].astype(BF16)
    if emit_w:
        refs[pos + 2][...] = wg
        refs[pos + 3][...] = wu
    if step_len:
        acts_ref, gates_ref = refs[pos + 4], refs[pos + 5]
        xs = xs_ref[...]
        gs = _dot(xs, wg)
        tpos = lax.broadcasted_iota(jnp.int32, (gs.shape[0], 1), 0) % step_len
        g1 = jnp.where(tpos < 1, h1_ref[...], pltpu.roll(gs, 1, 0))
        g2 = jnp.where(tpos < 2, h2_ref[...], pltpu.roll(gs, 2, 0))
        gcs = cb_ref[...] + cw_ref[2:3, :] * gs + cw_ref[1:2, :] * g1 + cw_ref[0:1, :] * g2
        acts_ref[...] = (jax.nn.gelu(gcs, approximate=True) * _dot(xs, wu)).astype(acts_ref.dtype)
        gates_ref[...] = gs
    for c in range(tm // sub):
        rows = slice(c * sub, (c + 1) * sub)
        x = x_ref[rows, :]
        gate = _dot(x, wg)
        up = _dot(x, wu)
        gc = cb_ref[...] + cw_ref[2:3, :] * gate
        for s in (1, 2):
            gc = gc + cw_ref[2 - s:3 - s, :] * _shift_rows(gate, prev, s)
        o_ref[rows, :] = (jax.nn.gelu(gc, approximate=True) * up).astype(o_ref.dtype)
        prev = gate[sub - SUBLANES:, :]
    carry_scr[j] = prev
    tail_ref[0] = prev


def _ffn_in_call(x, wg, wu, layer, up_off, conv_w, conv_b, *, seq_len, row0, rows, tm, tn, sub, into, emit_w,
                 steps=None):
    M, D = x.shape
    FF = conv_w.shape[1]
    tm = min(tm, seq_len)
    bps = seq_len // tm
    nj = FF // tn
    i0 = row0 // tm
    uo = up_off // tn
    wspec = lambda off: (pl.BlockSpec((D, tn), lambda i, j: (0, j + off)) if layer is None else
                         pl.BlockSpec((None, D, tn), lambda i, j: (layer, 0, j + off)))
    in_specs = [pl.BlockSpec((tm, D), lambda i, j: (i + i0, 0), pipeline_mode=pl.Buffered(1)),
                wspec(0), wspec(uo),
                pl.BlockSpec((3, tn), lambda i, j: (0, j)),
                pl.BlockSpec((1, tn), lambda i, j: (0, j))]
    args = [x, wg, wu, conv_w.astype(F32), conv_b.reshape(1, FF).astype(F32)]
    aliases = {}
    if into is not None:
        aliases = {len(args): 0}
        in_specs.append(pl.BlockSpec(memory_space=pl.ANY))
        args.append(into)
    step_len = 0
    if steps is not None:
        xs, h1, h2, step_len = steps
        ms = xs.shape[0]
        assert emit_w and rows == tm
        in_specs += [pl.BlockSpec((ms, D), lambda i, j: (0, 0)),
                     pl.BlockSpec((ms, tn), lambda i, j: (0, j)), pl.BlockSpec((ms, tn), lambda i, j: (0, j))]
        args += [xs, h1, h2]
    out_specs = [pl.BlockSpec((tm, tn), lambda i, j: (i + i0, j)),
                 pl.BlockSpec((1, SUBLANES, tn), lambda i, j: (i, 0, j))]
    out_shape = [jax.ShapeDtypeStruct((M, FF), BF16),
                 jax.ShapeDtypeStruct((rows // tm, SUBLANES, FF), F32)]
    if emit_w:
        out_specs += [pl.BlockSpec((D, tn), lambda i, j: (0, j))] * 2
        out_shape += [jax.ShapeDtypeStruct((D, FF), BF16)] * 2
    if step_len:
        out_specs += [pl.BlockSpec((ms, tn), lambda i, j: (0, j))] * 2
        out_shape += [jax.ShapeDtypeStruct((ms, FF), BF16), jax.ShapeDtypeStruct((ms, FF), F32)]
    outs = pl.pallas_call(
        functools.partial(_ffn_in_kernel, blocks_per_seq=bps, sub=min(sub, tm), has_into=into is not None,
                          emit_w=emit_w, step_len=step_len),
        grid=(rows // tm, nj),
        in_specs=in_specs,
        out_specs=out_specs,
        out_shape=out_shape,
        input_output_aliases=aliases,
        scratch_shapes=[pltpu.VMEM((nj, SUBLANES, tn), F32)],
        compiler_params=_params("arbitrary", "arbitrary"),
        name="ffn_in_fused",
    )(*args)
    return (outs[0], outs[1][bps - 1::bps]) + tuple(outs[2:])


def ffn_in_fused(x, w_in, layer, conv_w, conv_b, steps, *, seq_len, rows=None, sub=256):
    M = x.shape[0] if rows is None else rows
    FF = conv_w.shape[1]
    act, tail0, wg, wu, act_s, gate_s = _ffn_in_call(
        x, w_in, w_in, layer, FF, conv_w, conv_b, seq_len=seq_len, row0=0, rows=seq_len, tm=2048, tn=256,
        sub=sub, into=None, emit_w=True, steps=steps)
    if M == seq_len:
        return act, tail0, act_s, gate_s
    act, tails = _ffn_in_call(x, wg, wu, None, 0, conv_w, conv_b, seq_len=seq_len, row0=seq_len,
                              rows=M - seq_len, tm=1024, tn=512, sub=sub, into=act, emit_w=False)
    return act, jnp.concatenate([tail0, tails], axis=0), act_s, gate_s


def _rwkv_mix_kernel(x_ref, halo_ref, s0_ref, g_ref, mu_ref, w1_ref, a1_ref, g1_ref, *out_refs, n_valid_last):
    t = pl.program_id(1)
    nt = pl.num_programs(1)
    xr_ref, xk_ref, xv_ref, lw_ref, la_ref, lg_ref, hl_ref = out_refs

    def norm(v):
        return v * lax.rsqrt(jnp.mean(v * v, axis=-1, keepdims=True) + NORM_EPS) * g_ref[...]

    h = norm(x_ref[...])
    hp = norm(halo_ref[...])
    first = jnp.where(t > 0, 1.0, 0.0)
    hp = hp * first + jnp.broadcast_to(s0_ref[0], hp.shape) * (1.0 - first)
    xx = _shift_rows(h, hp, 1) - h
    mix = lambda j: (h + xx * mu_ref[j:j + 1, :]).astype(BF16)
    xr_ref[...] = mix(0)
    xk_ref[...] = mix(2)
    xv_ref[...] = mix(3)
    lw_ref[...] = jnp.tanh(_dot(mix(1), w1_ref[...])).astype(BF16)
    la_ref[...] = _dot(mix(4), a1_ref[...]).astype(BF16)
    lg_ref[...] = jax.nn.sigmoid(_dot(mix(5), g1_ref[...])).astype(BF16)

    @pl.when(t == nt - 1)
    def _():
        hl_ref[0] = h[n_valid_last - 1:n_valid_last, :]


def rwkv_mix(x, shift0, g, mu, w1, a1, g1, *, B, T, n_valid, tt=256, out_rows=None):
    D = x.shape[1]
    tt = min(tt, T)
    nt = T // tt
    hb = tt // SUBLANES
    out_rows = B * T if out_rows is None else out_rows
    n_valid_last = n_valid - (nt - 1) * tt
    blk = pl.BlockSpec((tt, D), lambda b, t: (b * nt + t, 0))
    one = pl.BlockSpec((1, 1, D), lambda b, t: (b, 0, 0))
    return pl.pallas_call(
        functools.partial(_rwkv_mix_kernel, n_valid_last=n_valid_last),
        grid=(B, nt),
        in_specs=[blk,
                  pl.BlockSpec((SUBLANES, D), lambda b, t: (jnp.maximum((b * nt + t) * hb - 1, 0), 0)),
                  one,
                  pl.BlockSpec((1, D), lambda b, t: (0, 0)),
                  pl.BlockSpec((6, D), lambda b, t: (0, 0))] +
                 [pl.BlockSpec(w.shape, lambda b, t: (0, 0)) for w in (w1, a1, g1)],
        out_specs=[blk] * 3 + [pl.BlockSpec((tt, w.shape[1]), lambda b, t: (b * nt + t, 0)) for w in (w1, a1, g1)]
                  + [one],
        out_shape=[jax.ShapeDtypeStruct((out_rows, D), BF16)] * 3 +
                  [jax.ShapeDtypeStruct((out_rows, w.shape[1]), BF16) for w in (w1, a1, g1)] +
                  [jax.ShapeDtypeStruct((B, 1, D), F32)],
        compiler_params=_params("parallel", "arbitrary"),
        name="rwkv_mix",
    )(x, x, shift0, g.reshape(1, D).astype(F32), mu.astype(F32), w1, a1, g1)


def _rwkv_chunk_kernel(r_ref, k_ref, v_ref, lw_ref, la_ref, lg_ref, w2_ref, a2_ref, g2_ref, w0_ref, a0_ref,
                       kk_ref, ka_ref, rk_ref, lng_ref, lnb_ref, s0_ref, o_ref, so_ref, s_scr, *, hs, C, n_valid):
    c = pl.program_id(2)
    nc = pl.num_programs(2)
    R, L = r_ref.shape
    n_pairs = L // LANES

    @pl.when(c == 0)
    def _():
        z = jnp.zeros((hs, hs), F32)
        for p in range(n_pairs):
            s_scr[p] = jnp.concatenate([jnp.concatenate([s0_ref[0, 2 * p], z], axis=1),
                                        jnp.concatenate([z, s0_ref[0, 2 * p + 1]], axis=1)], axis=0)

    ti = lax.broadcasted_iota(jnp.int32, (C, C), 0)
    si = lax.broadcasted_iota(jnp.int32, (C, C), 1)
    tri = jnp.where(ti >= si, 1.0, 0.0).astype(BF16)
    lane = lax.broadcasted_iota(jnp.int32, (1, LANES), 1)
    m0 = lane < hs
    C2 = 2 * C
    ri = lax.broadcasted_iota(jnp.int32, (C2, C2), 0)
    ci = lax.broadcasted_iota(jnp.int32, (C2, C2), 1)
    same = (ri // C) == (ci // C)
    strict = same & ((ri % C) > (ci % C))
    lower = (ri % C) >= (ci % C)
    vi = lax.broadcasted_iota(jnp.int32, (LANES, LANES), 0) // hs
    vj = lax.broadcasted_iota(jnp.int32, (LANES, LANES), 1) // hs
    blockdiag = vi == vj
    n_steps = int(math.log2(C))
    wide = C2 % LANES == 0
    P = range(n_pairs)
    sls = [slice(p * LANES, (p + 1) * LANES) for p in P]
    inv = 1.0 / hs

    def stack_heads(x):
        return jnp.concatenate([jnp.where(m0, x, 0.0), jnp.where(m0, 0.0, x)], axis=0).astype(BF16)

    def prep(sc):
        rows = slice(sc * C, (sc + 1) * C)
        r = r_ref[rows, :]
        k = k_ref[rows, :]
        v = v_ref[rows, :]
        wp = w0_ref[...] + _dot(lw_ref[rows, :], w2_ref[...])
        logw = -jnp.exp(-jax.nn.softplus(-wp) - 0.5)
        if n_valid is not None:
            logw = jnp.where(lax.broadcasted_iota(jnp.int32, (C, 1), 0) + sc * C < n_valid, logw, 0.0)
        a = jax.nn.sigmoid(a0_ref[...] + _dot(la_ref[rows, :], a2_ref[...]))
        kk = k * kk_ref[...]
        kk = kk / jnp.maximum(jnp.sqrt(_group_sum_bcast(kk * kk, hs, terms=0)), 1e-12)
        k2 = k * (1.0 + (a - 1.0) * ka_ref[...])
        bm = kk * a
        cum = _dot_exact_rhs_lhs(tri, logw)
        e_in = jnp.exp(cum)
        e_out = jnp.exp(-cum)
        e_end = jnp.exp(cum[C - 1:C, :] - cum)
        g_end = jnp.exp(cum[C - 1:C, :])
        rt = r * e_in
        kkt = kk * jnp.exp(cum - logw)
        bh = bm * e_out
        kh = k2 * e_out
        bbar = bm * e_end
        kbar = k2 * e_end
        return rows, r, v, k2, g_end, rt, kkt, bh, kh, bbar, kbar

    S = [s_scr[p] for p in P]
    nxt = prep(0)
    for sc in range(R // C):
        rows, r, v, k2, g_end, rt, kkt, bh, kh, bbar, kbar = nxt
        if sc + 1 < R // C:
            nxt = prep(sc + 1)
        S16 = [s_.astype(BF16) for s_ in S]
        V16 = [v[:, sl].astype(BF16) for sl in sls]
        lhk = [stack_heads(kkt[:, sl]) for sl in sls]
        if wide:
            gbk = [_nt(lhk[p], jnp.concatenate([stack_heads(bh[:, sls[p]]), stack_heads(kh[:, sls[p]])], axis=0))
                   for p in P]
            nmat = [jnp.where(strict, -g_[:, :C2], 0.0) for g_ in gbk]
            auk = [jnp.where(strict, -g_[:, C2:], 0.0).astype(BF16) for g_ in gbk]
        else:
            nmat = [jnp.where(strict, -_nt(lhk[p], stack_heads(bh[:, sls[p]])), 0.0) for p in P]
            auk = [jnp.where(strict, -_nt(lhk[p], stack_heads(kh[:, sls[p]])), 0.0).astype(BF16) for p in P]
        sprod = [_nt(jnp.concatenate([lhk[p], rt[:, sls[p]].astype(BF16)], axis=0), S16[p]) for p in P]
        u = [_dot(auk[p], jnp.concatenate([V16[p], V16[p]], axis=0)) - sprod[p][:C2] for p in P]
        npow = [n_.astype(BF16) for n_ in nmat]
        for it in range(n_steps):
            last = it + 1 == n_steps
            if wide and not last:
                res = [_dot(npow[p], jnp.concatenate([u[p].astype(BF16), npow[p]], axis=1)) for p in P]
                u = [u[p] + res[p][:, :LANES] for p in P]
                npow = [res[p][:, LANES:].astype(BF16) for p in P]
            else:
                u = [u[p] + _dot(npow[p], u[p].astype(BF16)) for p in P]
                if not last:
                    npow = [_dot(npow[p], npow[p]).astype(BF16) for p in P]
        uv = [jnp.concatenate([jnp.where(m0, u[p][:C], u[p][C:]).astype(BF16), V16[p]], axis=0) for p in P]
        ar = [jnp.where(lower, _nt(stack_heads(rt[:, sls[p]]),
                                   jnp.concatenate([bh[:, sls[p]], kh[:, sls[p]]], axis=0).astype(BF16)),
                        0.0).astype(BF16) for p in P]
        tmat = [_dot(ar[p], uv[p]) for p in P]
        ys = [sprod[p][C2:] + jnp.where(m0, tmat[p][:C], tmat[p][C:]) for p in P]
        upd = [_tn(uv[p], jnp.concatenate([bbar[:, sls[p]], kbar[:, sls[p]]], axis=0).astype(BF16)) for p in P]
        S = [S[p] * g_end[:, sls[p]] + jnp.where(blockdiag, upd[p], 0.0) for p in P]

        y = ys[0] if n_pairs == 1 else jnp.concatenate(ys, axis=-1)
        mean = _group_sum_bcast(y, hs, terms=0) * inv
        yc = y - mean
        var = _group_sum_bcast(yc * yc, hs, terms=0) * inv
        yn = yc * lax.rsqrt(var + RW_GN_EPS) * lng_ref[...] + lnb_ref[...]
        bonus = _group_sum_bcast(r * k2 * rk_ref[...], hs, terms=0) * v
        o_ref[rows, :] = ((yn + bonus) * _dot(lg_ref[rows, :], g2_ref[...])).astype(o_ref.dtype)

    for p in P:
        s_scr[p] = S[p]

    @pl.when(c == nc - 1)
    def _():
        for p in P:
            so_ref[0, 2 * p] = S[p][:hs, :hs]
            so_ref[0, 2 * p + 1] = S[p][hs:, hs:]


def _dot_exact_rhs_lhs(m_bf16, x):
    hi = x.astype(BF16)
    lo = (x - hi.astype(F32)).astype(BF16)
    return _dot(m_bf16, hi) + _dot(m_bf16, lo)


def rwkv_chunked(r, k, v, lw, la, lg, w2, a2, g2, w0, a0, k_k, k_a, r_k, ln_g, ln_b, s0, *, B, T, hs, chunk,
                 chunks_per_step=1, lanes_per_step=1024, out_rows=None, n_valid=None):
    D = r.shape[1]
    L = min(lanes_per_step, D)
    npg = L // LANES
    rows = chunk * chunks_per_step
    nc = T // rows
    out_rows = B * T if out_rows is None else out_rows
    seq = pl.BlockSpec((rows, L), lambda b, hg, c: (b * nc + c, hg))
    low = lambda a_: pl.BlockSpec((rows, a_.shape[1]), lambda b, hg, c: (b * nc + c, 0))
    wgt = lambda w_: pl.BlockSpec((w_.shape[0], L), lambda b, hg, c: (0, hg))
    par = pl.BlockSpec((1, L), lambda b, hg, c: (0, hg))
    st = pl.BlockSpec((1, 2 * npg, hs, hs), lambda b, hg, c: (b, hg, 0, 0))
    row = lambda x: x.reshape(1, D).astype(F32)
    return pl.pallas_call(
        functools.partial(_rwkv_chunk_kernel, hs=hs, C=chunk, n_valid=n_valid),
        grid=(B, D // L, nc),
        in_specs=[seq] * 3 + [low(lw), low(la), low(lg), wgt(w2), wgt(a2), wgt(g2)] + [par] * 7 + [st],
        out_specs=[seq, st],
        out_shape=[jax.ShapeDtypeStruct((out_rows, D), BF16),
                   jax.ShapeDtypeStruct(s0.shape, F32)],
        scratch_shapes=[pltpu.VMEM((npg, LANES, LANES), F32)],
        compiler_params=_params("parallel", "parallel", "arbitrary"),
        name="rwkv_chunked",
    )(r, k, v, lw, la, lg, w2, a2, g2, row(w0), row(a0), row(k_k), row(k_a), row(r_k), row(ln_g), row(ln_b), s0)


def _pad_rows(x, front, back):
    return jnp.pad(x, ((0, 0), (front, back), (0, 0)))


def kernel(x_prompt, x_sample, mem_prompt, state_lru_conv, state_lru_h, cache_swa_k, cache_swa_v,
           state_rwkv_shift, state_rwkv_wkv, cache_mem_k, cache_mem_v, state_ffn_conv,
           a_norm_g, a_w_in, a_conv_w, a_conv_b, a_gate_a_w, a_gate_a_b, a_gate_x_w, a_gate_x_b,
           a_lambda, b_q_norm_g, b_k_norm_g, b_sink, a_w_out,
           c_norm_g, c_mu, c_w_r, c_w_k, c_w_v, c_w_o, c_w0, c_w1, c_w2, c_a0, c_a1, c_a2,
           c_g1, c_g2, c_k_k, c_k_a, c_r_k, c_ln_g, c_ln_b,
           m_norm_g, m_mem_norm_g, m_w_q, m_w_kv, m_q_norm_g, m_k_norm_g, m_w_o,
           f_norm_g, f_w_in, f_conv_w, f_conv_b, f_w_out):
    D = x_prompt.shape[-1]
    depth = m_norm_g.shape[0]
    W = a_conv_w.shape[-1]
    KA = a_conv_w.shape[1]
    hd = b_q_norm_g.shape[-1]
    n_q = b_sink.shape[-1]
    n_kv = cache_swa_k.shape[3]
    win = cache_swa_k.shape[2]
    QW, KW = n_q * hd, n_kv * hd
    hs = c_r_k.shape[-1]
    n_rw = c_r_k.shape[1]
    m_heads, m_hd = cache_mem_k.shape[3], cache_mem_k.shape[4]
    MW = m_heads * m_hd
    FF = f_conv_w.shape[-1]
    KF = f_conv_w.shape[1]
    bf = lambda w: w.astype(BF16)
    wcache = {}

    Bp, S = x_prompt.shape[:2]
    Bs, Ts = x_sample.shape[:2]
    Mp, Ms = Bp * S, Bs * Ts
    Mt = Mp + Ms
    tma = _row_tile(Mt, 1280)
    ML = mem_prompt.shape[1]
    PAD8, PAD16 = SUBLANES, 2 * SUBLANES

    def mm(xin, w, layer, key, **kw):
        return dense(xin, w, layer, wcache, (key, layer), tm=tma, **kw)

    def seq_tail(arr, n, cols=slice(None)):
        return jnp.stack([arr[(b + 1) * S - n:(b + 1) * S, cols] for b in range(Bp)])

    def put_sample(buf, rows_s):
        return lax.dynamic_update_slice(buf, rows_s.astype(buf.dtype), (Mp, 0))

    mem_flat = mem_prompt.reshape(Bp * ML, D)
    mks, mvs = [], []
    for l in range(depth):
        mn = rmsnorm_rows(mem_flat, m_mem_norm_g[l])
        kv = matmul(mn, m_w_kv, layer=l)
        mk = headnorm(kv, 0, MW, m_k_norm_g[l], m_hd, F32)
        mks.append(mk.reshape(Bp, ML, MW))
        mvs.append(kv[:, MW:].reshape(Bp, ML, MW))
    mem_k_p, mem_v_p = jnp.stack(mks), jnp.stack(mvs)
    p_mem_k = mem_k_p.reshape(depth, Bp, ML, m_heads, m_hd)
    p_mem_v = mem_v_p.reshape(depth, Bp, ML, m_heads, m_hd)

    xp2, xs2 = x_prompt.reshape(Mp, D), x_sample.reshape(Ms, D)
    xf = None

    def first_norm(g):
        return put_sample(rmsnorm_rows(xp2, g, out_rows=Mt), rmsnorm_rows(xs2, g))

    def first_residual(xin, w, layer, key):
        y = dense(xin, w, layer, wcache, (key, layer), rows=Mp, out_rows=Mt, res=xp2)
        return matmul(xin, wcache[(key, layer)], row0=Mp, rows=Ms, res=xs2, res_row0=0, into=y)

    po ={k_: [] for k_ in ("lc", "lh", "sk", "sv", "rs", "rw", "fc")}
    so = {k_: [] for k_ in ("lc", "lh", "sk", "sv", "rs", "rw", "fc")}
    ia = ic = 0
    y_p = y_s = None
    for l in range(depth):
        if l % 2 == 0:
            i = ia
            ia += 1
            assert W == QW
            h = first_norm(a_norm_g[i]) if xf is None else rmsnorm_rows(xf, a_norm_g[i])
            zz = mm(h, a_w_in, i, "a_w_in_rg", n=2 * W)
            qkv = mm(h, a_w_in, i, "a_w_in_qkv", n0=2 * W)
            qn = headnorm(qkv, 0, QW, b_q_norm_g[i], hd, BF16)
            kn = headnorm(qkv, QW // KW, KW, b_k_norm_g[i], hd, F32)
            vv = qkv[:, QW + KW:]
            lru_w = (a_conv_w[i], a_conv_b[i], a_gate_a_w[i], a_gate_a_b[i], a_gate_x_w[i], a_gate_x_b[i],
                     a_lambda[i])
            mix, hl_p = lru_mixer(zz, 0, zz, 1, jnp.zeros((Bp, 1, W), F32), *lru_w, B=Bp, T=S, n_pad=0, tt=256,
                                  out_rows=Mt, out_cols=W + QW)
            mix = swa_attention(qn, bf(kn), bf(vv), b_sink[i], B=Bp, nq=S // win, nkb=S // win, j0=0, win=win,
                                hd=hd, into=mix, out_cb=W // QW)
            po["lc"].append(seq_tail(zz, KA - 1, slice(0, W)))
            po["lh"].append(hl_p.reshape(Bp, W))
            po["sk"].append(seq_tail(kn, win).reshape(Bp, win, n_kv, hd))
            po["sv"].append(seq_tail(vv, win).reshape(Bp, win, n_kv, hd))
            zz_s = zz[Mp:].reshape(Bs, Ts, 2 * W)
            n_pad = PAD8 - Ts
            xr_hist = jnp.concatenate([state_lru_conv[i].astype(F32), zz_s[:, :, :W]], axis=1)
            xr_p = _pad_rows(xr_hist, PAD8 - xr_hist.shape[1], 0).reshape(Bs * PAD8, W)
            yg_p = _pad_rows(zz_s[:, :, W:], n_pad, 0).reshape(Bs * PAD8, W)
            oa_s, hl_s = lru_mixer(xr_p, 0, yg_p, 0, state_lru_h[i].reshape(Bs, 1, W).astype(F32), *lru_w,
                                   B=Bs, T=PAD8, n_pad=n_pad, tt=PAD8)
            oa_s = oa_s.reshape(Bs, PAD8, W)[:, n_pad:].reshape(Ms, W)
            kc = cache_swa_k[i].reshape(Bs, win, KW)
            vc = cache_swa_v[i].reshape(Bs, win, KW)
            k_all = jnp.concatenate([kc, kn[Mp:].reshape(Bs, Ts, KW)], axis=1)
            v_all = jnp.concatenate([vc, vv[Mp:].reshape(Bs, Ts, KW)], axis=1)
            q_s = _pad_rows(qn[Mp:].reshape(Bs, Ts, QW), 0, PAD16 - Ts).reshape(Bs * PAD16, QW)
            o_s = swa_attention(q_s, bf(_pad_rows(k_all, 0, win - Ts)).reshape(Bs * 2 * win, KW),
                                bf(_pad_rows(v_all, 0, win - Ts)).reshape(Bs * 2 * win, KW), b_sink[i],
                                B=Bs, nq=1, nkb=2, j0=1, win=win, hd=hd, tq=PAD16)
            o_s = o_s.reshape(Bs, PAD16, QW)[:, :Ts].reshape(Ms, QW)
            mix = put_sample(mix, jnp.concatenate([oa_s, o_s], axis=-1))
            so["lc"].append(xr_hist[:, Ts:])
            so["lh"].append(hl_s.reshape(Bs, W))
            so["sk"].append(k_all[:, Ts:].reshape(Bs, win, n_kv, hd))
            so["sv"].append(v_all[:, Ts:].reshape(Bs, win, n_kv, hd))
            xf = (first_residual(mix, a_w_out, i, "a_w_out") if xf is None
                  else mm(mix, a_w_out, i, "a_w_out", res=xf))
        else:
            i = ic
            ic += 1
            gl = c_g1[i].shape[1]
            glp = -(-gl // LANES) * LANES
            lora1 = (bf(c_w1[i]), bf(c_a1[i]), bf(jnp.pad(c_g1[i], ((0, 0), (0, glp - gl)))))
            lora2 = (bf(c_w2[i]), bf(c_a2[i]), bf(jnp.pad(c_g2[i], ((0, glp - gl), (0, 0)))), c_w0[i], c_a0[i])
            mixes = rwkv_mix(xf, jnp.zeros((Bp, 1, D), F32), c_norm_g[i], c_mu[i], *lora1, B=Bp, T=S, n_valid=S,
                             out_rows=Mt)
            x_s = _pad_rows(xf[Mp:].reshape(Bs, Ts, D), 0, PAD8 - Ts).reshape(Bs * PAD8, D)
            mixes_s = rwkv_mix(x_s, state_rwkv_shift[i].reshape(Bs, 1, D).astype(F32), c_norm_g[i], c_mu[i], *lora1,
                               B=Bs, T=PAD8, n_valid=Ts, tt=PAD8)
            take = lambda m: m.reshape(Bs, PAD8, m.shape[-1])[:, :Ts].reshape(Ms, m.shape[-1])
            xr, xk, xv, lw, la, lg = [put_sample(mixes[j], take(mixes_s[j])) for j in range(6)]
            po["rs"].append(mixes[6].reshape(Bp, D))
            so["rs"].append(mixes_s[6].reshape(Bs, D))
            r = mm(xr, c_w_r, i, "c_w_r")
            k = mm(xk, c_w_k, i, "c_w_k")
            v = mm(xv, c_w_v, i, "c_w_v")
            rw_par = (c_k_k[i], c_k_a[i], c_r_k[i], c_ln_g[i], c_ln_b[i])
            yo, s_end = rwkv_chunked(r, k, v, lw, la, lg, *lora2, *rw_par, jnp.zeros((Bp, n_rw, hs, hs), F32),
                                     B=Bp, T=S, hs=hs, chunk=64, chunks_per_step=8, out_rows=Mt)
            po["rw"].append(s_end)
            cs = PAD16
            sq = lambda t: jnp.pad(t[Mp:].reshape(Bs, Ts, t.shape[-1]),
                                   ((0, 0), (0, cs - Ts), (0, 0))).reshape(Bs * cs, t.shape[-1])
            yo_s, s_end_s = rwkv_chunked(sq(r), sq(k), sq(v), sq(lw), sq(la), sq(lg), *lora2, *rw_par,
                                         state_rwkv_wkv[i].astype(F32), B=Bs, T=cs, hs=hs, chunk=cs,
                                         lanes_per_step=2048, n_valid=Ts)
            yo = put_sample(yo, yo_s.reshape(Bs, cs, D)[:, :Ts].reshape(Ms, D))
            so["rw"].append(s_end_s)
            xf = mm(yo, c_w_o, i, "c_w_o", res=xf)
        q = norm_proj(xf, m_norm_g[l], m_w_q, l)
        om = mem_attention(q, m_q_norm_g[l], mem_k_p, mem_v_p, l, B=Bp, T=S, n_heads=m_heads, out_rows=Mt)
        q_s = _pad_rows(q[Mp:].reshape(Bs, Ts, MW), 0, PAD8 - Ts).reshape(Bs * PAD8, MW)
        om_s = mem_attention(q_s, m_q_norm_g[l], cache_mem_k, cache_mem_v, l, B=Bs, T=PAD8, n_heads=m_heads, tt=PAD8)
        om = put_sample(om, om_s.reshape(Bs, PAD8, MW)[:, :Ts].reshape(Ms, MW))
        xf, hf = proj_res_norm(om, m_w_o, l, xf, f_norm_g[l])
        assert Ts >= KF - 1 == 2
        st_f = state_ffn_conv[l].astype(F32)
        zrow = jnp.zeros((Bs, Ts - 1, FF), F32)
        h1 = jnp.concatenate([st_f[:, 1:2], zrow], axis=1).reshape(Ms, FF)
        h2 = jnp.concatenate([st_f, zrow[:, 1:]], axis=1).reshape(Ms, FF)
        act, tail, act_s, gate_s = ffn_in_fused(hf, f_w_in, l, f_conv_w[l], f_conv_b[l], (hf[Mp:], h1, h2, Ts),
                                                seq_len=S, rows=Mp)
        po["fc"].append(tail[:, SUBLANES - (KF - 1):, :])
        act = put_sample(act, act_s)
        so["fc"].append(gate_s.reshape(Bs, Ts, FF)[:, Ts - (KF - 1):])
        if l + 1 < depth:
            xf = mm(act, f_w_out, l, "f_w_out", res=xf, tn=1024, tk=2048)
        else:
            y_p = dense(act, f_w_out, l, wcache, ("f_w_out", l), rows=Mp, out_rows=Mp, res=xf, tn=1024, tk=2048)
            y_s = matmul(act, wcache[("f_w_out", l)], row0=Mp, rows=Ms, out_rows=Ms, out_row0=0, res=xf,
                         tn=1024, tk=2048)
    st = jnp.stack
    return (y_p.reshape(Bp, S, D), y_s.reshape(Bs, Ts, D),
            st(po["lc"]), st(po["lh"]), st(po["sk"]), st(po["sv"]), st(po["rs"]), st(po["rw"]),
            p_mem_k, p_mem_v, st(po["fc"]),
            st(so["lc"]), st(so["lh"]), st(so["sk"]), st(so["sv"]), st(so["rs"]), st(so["rw"]), st(so["fc"]))
```
